```python
import jax, jax.numpy as jnp
from jax import lax
import numpy as np

D_MODEL = 1024
BATCH = 8
SEQ = 2048
DEPTH = 1
DEC_BATCH = 128
DEC_SEQ = 1
PAST_LEN = 16384
PAGE_SIZE = 128

D_MIX = D_MODEL
LRU_WIDTH = D_MIX // 2
LRU_HEADS = 8
LRU_BLOCK = LRU_WIDTH // LRU_HEADS
CONV_WIDTH = 4
LRU_C = 8.0
HG_WIDTH = D_MIX - LRU_WIDTH
HG_HEAD_DIM = 128
HG_HEADS = HG_WIDTH // HG_HEAD_DIM
HG_CHUNK = 64
D_FF = 2816
N_MOD = 9
EPS = 1e-6

kernel_name = 'hymba_hawk_hgrn2_macaron_step'


def rms_norm(x, w):
    x32 = x.astype(jnp.float32)
    y = x32 * lax.rsqrt(jnp.mean(x32 * x32, axis=-1, keepdims=True) + EPS)
    return (y * w.astype(jnp.float32)).astype(x.dtype)


def swiglu(h, w_gate, w_up, w_down):
    return (jax.nn.silu(h @ w_gate) * (h @ w_up)) @ w_down


def causal_conv(u, buf, w, b):
    T = u.shape[1]
    full = jnp.concatenate([buf.astype(u.dtype), u], axis=1)
    out = b
    for k in range(CONV_WIDTH):
        out = out + full[:, k:k + T] * w[k]
    return out, full[:, T:]


def rg_lru(u, h0, w_a, b_a, w_x, b_x, lam, starts_sequence):
    B, T, C = u.shape
    f32 = jnp.float32
    u32 = u.astype(f32)
    ub = u32.reshape(B, T, LRU_HEADS, LRU_BLOCK)
    r = jax.nn.sigmoid(jnp.einsum('bthi,hij->bthj', ub, w_a.astype(f32)).reshape(B, T, C) + b_a.astype(f32))
    ig = jax.nn.sigmoid(jnp.einsum('bthi,hij->bthj', ub, w_x.astype(f32)).reshape(B, T, C) + b_x.astype(f32))
    log_a = -LRU_C * r * jax.nn.softplus(-lam.astype(f32))
    a = jnp.exp(log_a)
    mult = jnp.sqrt(-jnp.expm1(2.0 * log_a))
    if starts_sequence:
        mult = mult.at[:, 0].set(1.0)
    bterm = mult * ig * u32
    bterm = bterm.at[:, 0].add(a[:, 0] * h0.astype(f32))

    def combine(e1, e2):
        a1, b1 = e1
        a2, b2 = e2
        return a1 * a2, a2 * b1 + b2

    _, h = lax.associative_scan(combine, (a, bterm), axis=1)
    return h, h[:, -1]


def hgrn2(q, f_raw, v, g, lb, S0, norm_w):
    B, T, _ = q.shape
    f32 = jnp.float32

    def heads(z):
        return z.astype(f32).reshape(B, T, HG_HEADS, HG_HEAD_DIM)

    f = lb + (1.0 - lb) * jax.nn.sigmoid(f_raw.astype(f32))
    log_f = heads(jnp.log(f))
    k = heads(1.0 - f)
    qh = heads(q) * (HG_HEAD_DIM ** -0.5)
    vh = heads(v)
    C = min(HG_CHUNK, T)
    n = -(-T // C)
    pad = n * C - T

    def blocks(z):
        z = jnp.pad(z, ((0, 0), (0, pad), (0, 0), (0, 0)))
        return z.reshape(B, n, C, HG_HEADS, HG_HEAD_DIM).transpose(1, 0, 3, 2, 4)

    causal = jnp.tril(jnp.ones((C, C), dtype=bool))[:, :, None]

    def step(S, blk):
        qc, lfc, kc, vc = blk
        bc = jnp.cumsum(lfc, axis=2)
        diff = bc[:, :, :, None, :] - bc[:, :, None, :, :]
        decay = jnp.where(causal, jnp.exp(jnp.where(causal, diff, 0.0)), 0.0)
        scores = jnp.einsum('bhtk,bhtsk,bhsk->bhts', qc, decay, kc)
        o = (jnp.einsum('bhtk,bhkv->bhtv', qc * jnp.exp(bc), S)
             + jnp.einsum('bhts,bhsv->bhtv', scores, vc))
        b_last = bc[:, :, -1:]
        S_new = (jnp.exp(b_last[:, :, 0])[..., None] * S
                 + jnp.einsum('bhsk,bhsv->bhkv', kc * jnp.exp(b_last - bc), vc))
        return S_new, o

    S_T, o = lax.scan(step, S0.astype(f32), (blocks(qh), blocks(log_f), blocks(k), blocks(vh)))
    o = o.transpose(1, 0, 3, 2, 4).reshape(B, n * C, HG_HEADS, HG_HEAD_DIM)[:, :T]
    o = rms_norm(o, norm_w)
    o = o.reshape(B, T, HG_WIDTH) * jax.nn.silu(g.astype(f32))
    return o, S_T


def modulate(h, shift, scale):
    return h * (1.0 + scale) + shift


def layer(x, c, h0, conv0, S0, starts_sequence, lb, p):
    (w_ada, b_ada, ln_ffn1_pre, ln_ffn1_post, ffn1_w_gate, ffn1_w_up, ffn1_w_down,
     ln_mix_pre, ln_mix_post, w_in, lru_conv_w, lru_conv_b, lru_w_a, lru_b_a, lru_w_x, lru_b_x,
     lru_lambda, hg_norm_w, w_out, ln_ffn2_pre, ln_ffn2_post, ffn2_w_gate, ffn2_w_up, ffn2_w_down) = p
    mod = (jax.nn.silu(c) @ w_ada + b_ada).reshape(c.shape[0], N_MOD, 1, D_MODEL)
    sh1, sc1, g1, shm, scm, gm, sh2, sc2, g2 = [mod[:, i] for i in range(N_MOD)]
    h = modulate(rms_norm(x, ln_ffn1_pre), sh1, sc1)
    x = x + 0.5 * g1 * rms_norm(swiglu(h, ffn1_w_gate, ffn1_w_up, ffn1_w_down), ln_ffn1_post)
    h = modulate(rms_norm(x, ln_mix_pre), shm, scm)
    proj = h @ w_in
    splits = [LRU_WIDTH, 2 * LRU_WIDTH, 2 * LRU_WIDTH + HG_WIDTH, 2 * LRU_WIDTH + 2 * HG_WIDTH,
              2 * LRU_WIDTH + 3 * HG_WIDTH]
    u_lru, y_lru, q, f_raw, v, g = jnp.split(proj, splits, axis=-1)
    u_conv, conv_new = causal_conv(u_lru, conv0, lru_conv_w, lru_conv_b)
    hs, h_new = rg_lru(u_conv, h0, lru_w_a, lru_b_a, lru_w_x, lru_b_x, lru_lambda, starts_sequence)
    o_lru = hs.astype(x.dtype) * jax.nn.gelu(y_lru, approximate=True)
    o_hg, S_new = hgrn2(q, f_raw, v, g, lb, S0, hg_norm_w)
    mix = jnp.concatenate([o_lru, o_hg.astype(x.dtype)], axis=-1) @ w_out
    x = x + gm * rms_norm(mix, ln_mix_post)
    h = modulate(rms_norm(x, ln_ffn2_pre), sh2, sc2)
    x = x + 0.5 * g2 * rms_norm(swiglu(h, ffn2_w_gate, ffn2_w_up, ffn2_w_down), ln_ffn2_post)
    return x, h_new.astype(x.dtype), conv_new.astype(x.dtype), S_new.astype(x.dtype)


def setup_inputs(seed: int = 0) -> dict:
    key = jax.random.key(seed)
    ks = iter(jax.random.split(key, 48))
    f32 = jnp.float32
    L = DEPTH

    def normal(shape, scale):
        return scale * jax.random.normal(next(ks), shape, f32)

    def gain(shape):
        return 1.0 + 0.05 * jax.random.normal(next(ks), shape, f32)

    s = jax.random.uniform(next(ks), (L, LRU_WIDTH), f32, 0.9, 0.999) ** (1.0 / LRU_C)
    lru_lambda = jnp.log(s) - jnp.log1p(-s)
    return {
        'x_prompt': normal((BATCH, SEQ, D_MODEL), 1.0),
        'x_sample': normal((DEC_BATCH, DEC_SEQ, D_MODEL), 1.0),
        'c_prompt': normal((BATCH, D_MODEL), 1.0),
        'c_sample': normal((DEC_BATCH, D_MODEL), 1.0),
        'state_lru_h': normal((L, DEC_BATCH, LRU_WIDTH), 0.5),
        'state_lru_conv': normal((L, DEC_BATCH, CONV_WIDTH - 1, LRU_WIDTH), 1.0),
        'state_hgrn_S': normal((L, DEC_BATCH, HG_HEADS, HG_HEAD_DIM, HG_HEAD_DIM), 0.5),
        'w_ada': normal((L, D_MODEL, N_MOD * D_MODEL), 0.5 * D_MODEL ** -0.5),
        'b_ada': normal((L, N_MOD * D_MODEL), 0.02),
        'ln_ffn1_pre': gain((L, D_MODEL)),
        'ln_ffn1_post': gain((L, D_MODEL)),
        'ffn1_w_gate': normal((L, D_MODEL, D_FF), D_MODEL ** -0.5),
        'ffn1_w_up': normal((L, D_MODEL, D_FF), D_MODEL ** -0.5),
        'ffn1_w_down': normal((L, D_FF, D_MODEL), D_FF ** -0.5),
        'ln_mix_pre': gain((L, D_MODEL)),
        'ln_mix_post': gain((L, D_MODEL)),
        'w_in': normal((L, D_MODEL, 2 * LRU_WIDTH + 4 * HG_WIDTH), D_MODEL ** -0.5),
        'lru_conv_w': normal((L, CONV_WIDTH, LRU_WIDTH), CONV_WIDTH ** -0.5),
        'lru_conv_b': normal((L, LRU_WIDTH), 0.02),
        'lru_w_a': normal((L, LRU_HEADS, LRU_BLOCK, LRU_BLOCK), LRU_BLOCK ** -0.5),
        'lru_b_a': normal((L, LRU_WIDTH), 0.02),
        'lru_w_x': normal((L, LRU_HEADS, LRU_BLOCK, LRU_BLOCK), LRU_BLOCK ** -0.5),
        'lru_b_x': normal((L, LRU_WIDTH), 0.02),
        'lru_lambda': lru_lambda,
        'hg_lb_logits': normal((L + 1, HG_WIDTH), 0.5),
        'hg_norm_w': gain((L, HG_HEAD_DIM)),
        'w_out': normal((L, D_MIX, D_MODEL), D_MIX ** -0.5),
        'ln_ffn2_pre': gain((L, D_MODEL)),
        'ln_ffn2_post': gain((L, D_MODEL)),
        'ffn2_w_gate': normal((L, D_MODEL, D_FF), D_MODEL ** -0.5),
        'ffn2_w_up': normal((L, D_MODEL, D_FF), D_MODEL ** -0.5),
        'ffn2_w_down': normal((L, D_FF, D_MODEL), D_FF ** -0.5),
    }


def reference(x_prompt, x_sample, c_prompt, c_sample, state_lru_h, state_lru_conv, state_hgrn_S,
              w_ada, b_ada, ln_ffn1_pre, ln_ffn1_post, ffn1_w_gate, ffn1_w_up, ffn1_w_down,
              ln_mix_pre, ln_mix_post, w_in, lru_conv_w, lru_conv_b, lru_w_a, lru_b_a, lru_w_x, lru_b_x,
              lru_lambda, hg_lb_logits, hg_norm_w, w_out, ln_ffn2_pre, ln_ffn2_post,
              ffn2_w_gate, ffn2_w_up, ffn2_w_down):
    dt = x_prompt.dtype
    lbs = jnp.cumsum(jax.nn.softmax(hg_lb_logits.astype(jnp.float32), axis=0), axis=0)[:DEPTH]
    stacked = (w_ada, b_ada, ln_ffn1_pre, ln_ffn1_post, ffn1_w_gate, ffn1_w_up, ffn1_w_down,
               ln_mix_pre, ln_mix_post, w_in, lru_conv_w, lru_conv_b, lru_w_a, lru_b_a, lru_w_x, lru_b_x,
               lru_lambda, hg_norm_w, w_out, ln_ffn2_pre, ln_ffn2_post, ffn2_w_gate, ffn2_w_up, ffn2_w_down)
    B = x_prompt.shape[0]
    xp, xs = x_prompt, x_sample
    ph, pc, pS, sh, sc, sS = [], [], [], [], [], []
    for l in range(DEPTH):
        p = tuple(w[l] for w in stacked)
        h0 = jnp.zeros((B, LRU_WIDTH), dt)
        conv0 = jnp.zeros((B, CONV_WIDTH - 1, LRU_WIDTH), dt)
        S0 = jnp.zeros((B, HG_HEADS, HG_HEAD_DIM, HG_HEAD_DIM), dt)
        xp, h_p, c_p, S_p = layer(xp, c_prompt, h0, conv0, S0, True, lbs[l], p)
        xs, h_s, c_s, S_s = layer(xs, c_sample, state_lru_h[l], state_lru_conv[l], state_hgrn_S[l],
                                  False, lbs[l], p)
        ph.append(h_p); pc.append(c_p); pS.append(S_p)
        sh.append(h_s); sc.append(c_s); sS.append(S_s)
    return (xp, xs, jnp.stack(ph), jnp.stack(pc), jnp.stack(pS), jnp.stack(sh), jnp.stack(sc), jnp.stack(sS))
```

```python
import functools

import jax
import jax.numpy as jnp
from jax import lax
from jax.experimental import pallas as pl
from jax.experimental.pallas import tpu as pltpu

F32 = jnp.float32
BF16 = jnp.bfloat16

D_MODEL = 1024
D_FF = 2816
LRU_WIDTH = 512
LRU_HEADS = 8
LRU_BLOCK = LRU_WIDTH // LRU_HEADS
CONV_WIDTH = 4
LRU_C = 8.0
HG_WIDTH = 512
HG_HEAD_DIM = 128
HG_HEADS = HG_WIDTH // HG_HEAD_DIM
HG_CHUNK = 64
N_MOD = 9
D_PROJ = 2 * LRU_WIDTH + 4 * HG_WIDTH
EPS = 1e-6

V7X_SUBLANES = 8
V7X_VMEM_BYTES = 64 * 1024 * 1024
VMEM_LIMIT = 56 * 1024 * 1024

FFN_TILE = 512
MIX_TILE = 256
ADA_TILE = 1152
SAMPLE_SEQ_BLOCK = V7X_SUBLANES

HG_LEVELS = (1, 2, 4, 8, 16, 32)


def _rms(x, w):
    return (x * lax.rsqrt(jnp.mean(x * x, axis=-1, keepdims=True) + EPS)) * w


def _sigmoid(x):
    return 1.0 / (1.0 + jnp.exp(-x))


def _silu(x):
    return x * _sigmoid(x)


def _gelu_tanh(x):
    c = 0.7978845608028654
    return x * (0.5 * (1.0 + jnp.tanh(c * (x + 0.044715 * (x * x * x)))))


def _softplus(z):
    return jnp.maximum(z, 0.0) + jnp.log1p(jnp.exp(-jnp.abs(z)))


def _dot(a, b):
    return jnp.dot(a, b, preferred_element_type=F32)


def _dot_nt(a, b):
    return lax.dot_general(a, b, (((1,), (1,)), ((), ())), preferred_element_type=F32)


def _dot_tn(a, b):
    return lax.dot_general(a, b, (((0,), (0,)), ((), ())), preferred_element_type=F32)


def _const_spec(shape):
    nd = len(shape)
    return pl.BlockSpec(shape, lambda *_: (0,) * nd, pipeline_mode=pl.Buffered(1))


def _ada_kernel(c_ref, w_ref, b_ref, o_ref):
    s = _silu(c_ref[...])
    o_ref[...] = _dot(s.astype(BF16), w_ref[...].astype(BF16)) + b_ref[...]


def _ada(c_all, w_ada, b_ada):
    rows = c_all.shape[0]
    n = w_ada.shape[1]
    return pl.pallas_call(
        _ada_kernel,
        grid=(n // ADA_TILE,),
        in_specs=[
            pl.BlockSpec((rows, D_MODEL), lambda j: (0, 0)),
            pl.BlockSpec((D_MODEL, ADA_TILE), lambda j: (0, j)),
            pl.BlockSpec((1, ADA_TILE), lambda j: (0, j)),
        ],
        out_specs=pl.BlockSpec((rows, ADA_TILE), lambda j: (0, j)),
        out_shape=jax.ShapeDtypeStruct((rows, n), F32),
        compiler_params=pltpu.CompilerParams(
            dimension_semantics=("arbitrary",), vmem_limit_bytes=VMEM_LIMIT),
        name="ada_mod",
    )(c_all, w_ada, b_ada.reshape(1, n))


def _ffn_body(x, shift, scale, gate, pre_w, post_w, wg_ref, wu_ref, wd_ref):
    h = _rms(x, pre_w) * (1.0 + scale) + shift
    hb = h.astype(BF16)
    a = _dot(hb, wg_ref[...])
    u = _dot(hb, wu_ref[...])
    act = (_silu(a) * u).astype(BF16)
    y = _dot(act, wd_ref[...])
    return x + (0.5 * gate) * _rms(y, post_w)


def _ffn_prompt_kernel(x_ref, m_ref, pre_ref, post_ref, wg_ref, wu_ref, wd_ref, o_ref):
    m = m_ref[0]
    o_ref[...] = _ffn_body(x_ref[...], m[0:1], m[1:2], m[2:3], pre_ref[...], post_ref[...],
                           wg_ref, wu_ref, wd_ref)


def _ffn_sample_kernel(x_ref, m_ref, pre_ref, post_ref, wg_ref, wu_ref, wd_ref, o_ref):
    o_ref[...] = _ffn_body(x_ref[...], m_ref[0], m_ref[1], m_ref[2], pre_ref[...], post_ref[...],
                           wg_ref, wu_ref, wd_ref)


def _ffn_prompt(x, mod3, pre_w, post_w, wg, wu, wd, seq_len):
    n_tok = x.shape[0]
    steps_per_seq = seq_len // FFN_TILE
    return pl.pallas_call(
        _ffn_prompt_kernel,
        grid=(n_tok // FFN_TILE,),
        in_specs=[
            pl.BlockSpec((FFN_TILE, D_MODEL), lambda i: (i, 0)),
            pl.BlockSpec((1, 3, D_MODEL), lambda i: (i // steps_per_seq, 0, 0)),
            _const_spec((1, D_MODEL)),
            _const_spec((1, D_MODEL)),
            _const_spec((D_MODEL, D_FF)),
            _const_spec((D_MODEL, D_FF)),
            _const_spec((D_FF, D_MODEL)),
        ],
        out_specs=pl.BlockSpec((FFN_TILE, D_MODEL), lambda i: (i, 0)),
        out_shape=jax.ShapeDtypeStruct((n_tok, D_MODEL), F32),
        compiler_params=pltpu.CompilerParams(
            dimension_semantics=("arbitrary",), vmem_limit_bytes=VMEM_LIMIT),
        name="ffn_prompt",
    )(x, mod3, pre_w, post_w, wg, wu, wd)


def _ffn_sample(x, mod3, pre_w, post_w, wg, wu, wd):
    n_seq = x.shape[0]
    return pl.pallas_call(
        _ffn_sample_kernel,
        grid=(1,),
        in_specs=[
            _const_spec((n_seq, D_MODEL)),
            _const_spec((3, n_seq, D_MODEL)),
            _const_spec((1, D_MODEL)),
            _const_spec((1, D_MODEL)),
            _const_spec((D_MODEL, D_FF)),
            _const_spec((D_MODEL, D_FF)),
            _const_spec((D_FF, D_MODEL)),
        ],
        out_specs=pl.BlockSpec((n_seq, D_MODEL), lambda i: (0, 0)),
        out_shape=jax.ShapeDtypeStruct((n_seq, D_MODEL), F32),
        compiler_params=pltpu.CompilerParams(
            dimension_semantics=("arbitrary",), vmem_limit_bytes=VMEM_LIMIT),
        name="ffn_sample",
    )(x, mod3, pre_w, post_w, wg, wu, wd)


def _lower_bound(lb_logits):
    z = lb_logits - jnp.max(lb_logits, axis=0, keepdims=True)
    e = jnp.exp(z)
    return e[0:1] / jnp.sum(e, axis=0, keepdims=True)


def _lru_gates(u_conv, wa_ref, ba, wx_ref, bx, lam):
    ub = u_conv.astype(BF16)
    r = _sigmoid(_dot(ub, wa_ref[...]) + ba)
    ig = _sigmoid(_dot(ub, wx_ref[...]) + bx)
    log_a = (-LRU_C * r) * _softplus(-lam)
    a = jnp.exp(log_a)
    th = jnp.tanh(log_a)
    mult = jnp.sqrt((-2.0 * th) / (1.0 - th))
    return a, mult, ig


def _head_rms_gate(o, norm_w, g):
    return _rms(o, norm_w) * _silu(g)


def _mix_prompt_kernel(x_ref, m_ref, pre_ref, post_ref, win_ref, cw_ref, cb_ref,
                       wa_ref, ba_ref, wx_ref, bx_ref, lam_ref, lbl_ref, hgw_ref, wout_ref,
                       o_ref, h_out_ref, conv_out_ref, s_out_ref,
                       uext_ref, hcar_ref, st_ref, ohg_ref):
    t = pl.program_id(1)
    n_t = pl.num_programs(1)
    tb = MIX_TILE

    @pl.when(t == 0)
    def _():
        uext_ref[0:V7X_SUBLANES, :] = jnp.zeros((V7X_SUBLANES, LRU_WIDTH), F32)
        hcar_ref[...] = jnp.zeros_like(hcar_ref)
        st_ref[...] = jnp.zeros_like(st_ref)

    x = x_ref[...]
    m = m_ref[0]
    h = _rms(x, pre_ref[...]) * (1.0 + m[1:2]) + m[0:1]
    proj = _dot(h.astype(BF16), win_ref[...])
    u = proj[:, 0:LRU_WIDTH]
    y_lru = proj[:, LRU_WIDTH:2 * LRU_WIDTH]
    o0 = 2 * LRU_WIDTH
    q = proj[:, o0:o0 + HG_WIDTH]
    f_raw = proj[:, o0 + HG_WIDTH:o0 + 2 * HG_WIDTH]
    v = proj[:, o0 + 2 * HG_WIDTH:o0 + 3 * HG_WIDTH]
    g = proj[:, o0 + 3 * HG_WIDTH:o0 + 4 * HG_WIDTH]

    uext_ref[V7X_SUBLANES:V7X_SUBLANES + tb, :] = u
    cw = cw_ref[...]
    u_conv = cb_ref[...]
    for k in range(CONV_WIDTH):
        off = V7X_SUBLANES - (CONV_WIDTH - 1) + k
        u_conv = u_conv + uext_ref[off:off + tb, :] * cw[k:k + 1]
    uext_ref[0:V7X_SUBLANES, :] = u[tb - V7X_SUBLANES:tb, :]

    a, mult, ig = _lru_gates(u_conv, wa_ref, ba_ref[...], wx_ref, bx_ref[...], lam_ref[...])
    row = lax.broadcasted_iota(jnp.int32, (tb, LRU_WIDTH), 0)
    first_row = jnp.where(t == 0, 0, -1)
    mult = jnp.where(row == first_row, 1.0, mult)
    bt = (mult * ig) * u_conv
    s = 1
    while s < tb:
        keep = row >= s
        a_sh = pltpu.roll(a, s, 0)
        b_sh = pltpu.roll(bt, s, 0)
        bt = jnp.where(keep, a * b_sh + bt, bt)
        a = jnp.where(keep, a * a_sh, a)
        s *= 2
    hs = a * hcar_ref[...] + bt
    hcar_ref[...] = hs[tb - 1:tb, :]
    o_lru = hs * _gelu_tanh(y_lru)

    lb = _lower_bound(lbl_ref[...])
    f = lb + (1.0 - lb) * _sigmoid(f_raw)
    logf = jnp.log(f)
    kk = 1.0 - f
    qs = q * (HG_HEAD_DIM ** -0.5)
    crow = row & (HG_CHUNK - 1)
    bc = logf
    s = 1
    while s < HG_CHUNK:
        bc = jnp.where(crow >= s, bc + pltpu.roll(bc, s, 0), bc)
        s *= 2

    q_lv = [qs.astype(BF16)]
    k_lv = [kk.astype(BF16)]
    e_m = bc
    for lvl in HG_LEVELS:
        r_m = pltpu.roll(e_m, lvl, 0)
        q_lv.append((qs * jnp.exp(jnp.minimum(bc - r_m, 0.0))).astype(BF16))
        k_lv.append((kk * jnp.exp(e_m - bc)).astype(BF16))
        e_m = jnp.where((crow & lvl) == 0, pltpu.roll(e_m, tb - lvl, 0), e_m)
    q_in = (qs * jnp.exp(bc)).astype(BF16)
    k_out = (kk * jnp.exp(e_m - bc)).astype(BF16)
    s_decay = jnp.exp(e_m)
    vb = v.astype(BF16)

    ti = lax.broadcasted_iota(jnp.int32, (HG_CHUNK, HG_CHUNK), 0)
    si = lax.broadcasted_iota(jnp.int32, (HG_CHUNK, HG_CHUNK), 1)
    masks = [ti == si]
    tx = ti ^ si
    for lvl in HG_LEVELS:
        masks.append((tx >= lvl) & (tx < 2 * lvl) & ((ti & lvl) != 0))

    hgw = hgw_ref[...]
    for c in range(tb // HG_CHUNK):
        r0 = c * HG_CHUNK
        for hd in range(HG_HEADS):
            l0 = hd * HG_HEAD_DIM
            blk = lambda z: z[r0:r0 + HG_CHUNK, l0:l0 + HG_HEAD_DIM]
            scores = jnp.zeros((HG_CHUNK, HG_CHUNK), F32)
            for ql, kl, msk in zip(q_lv, k_lv, masks):
                scores = scores + jnp.where(msk, _dot_nt(blk(ql), blk(kl)), 0.0)
            st = st_ref[hd]
            o = _dot_nt(blk(q_in), st.astype(BF16)) + _dot(scores.astype(BF16), blk(vb))
            dec = s_decay[r0 + HG_CHUNK - 1:r0 + HG_CHUNK, l0:l0 + HG_HEAD_DIM]
            st_ref[hd] = st * dec + _dot_tn(blk(vb), blk(k_out))
            ohg_ref[r0:r0 + HG_CHUNK, l0:l0 + HG_HEAD_DIM] = _head_rms_gate(o, hgw, blk(g))

    mix_in = jnp.concatenate([o_lru, ohg_ref[...]], axis=-1).astype(BF16)
    mix = _dot(mix_in, wout_ref[...])
    o_ref[...] = x + m[2:3] * _rms(mix, post_ref[...])

    @pl.when(t == n_t - 1)
    def _():
        h_out_ref[0] = hcar_ref[...]
        conv_out_ref[0] = uext_ref[V7X_SUBLANES - (CONV_WIDTH - 1):V7X_SUBLANES, :]
        for hd in range(HG_HEADS):
            s_out_ref[0, hd] = st_ref[hd].T


def _mix_prompt(x, mod3, p, batch, seq_len):
    n_tok = x.shape[0]
    n_t = seq_len // MIX_TILE
    tok_spec = pl.BlockSpec((MIX_TILE, D_MODEL), lambda b, t: (b * n_t + t, 0))
    return pl.pallas_call(
        _mix_prompt_kernel,
        grid=(batch, n_t),
        in_specs=[
            tok_spec,
            pl.BlockSpec((1, 3, D_MODEL), lambda b, t: (b, 0, 0)),
            _const_spec((1, D_MODEL)),
            _const_spec((1, D_MODEL)),
            _const_spec((D_MODEL, D_PROJ)),
            _const_spec((CONV_WIDTH, LRU_WIDTH)),
            _const_spec((1, LRU_WIDTH)),
            _const_spec((LRU_WIDTH, LRU_WIDTH)),
            _const_spec((1, LRU_WIDTH)),
            _const_spec((LRU_WIDTH, LRU_WIDTH)),
            _const_spec((1, LRU_WIDTH)),
            _const_spec((1, LRU_WIDTH)),
            _const_spec(p["lb_logits"].shape),
            _const_spec((1, HG_HEAD_DIM)),
            _const_spec((D_MODEL, D_MODEL)),
        ],
        out_specs=[
            tok_spec,
            pl.BlockSpec((1, 1, LRU_WIDTH), lambda b, t: (b, 0, 0)),
            pl.BlockSpec((1, CONV_WIDTH - 1, LRU_WIDTH), lambda b, t: (b, 0, 0)),
            pl.BlockSpec((1, HG_HEADS, HG_HEAD_DIM, HG_HEAD_DIM), lambda b, t: (b, 0, 0, 0)),
        ],
        out_shape=[
            jax.ShapeDtypeStruct((n_tok, D_MODEL), F32),
            jax.ShapeDtypeStruct((batch, 1, LRU_WIDTH), F32),
            jax.ShapeDtypeStruct((batch, CONV_WIDTH - 1, LRU_WIDTH), F32),
            jax.ShapeDtypeStruct((batch, HG_HEADS, HG_HEAD_DIM, HG_HEAD_DIM), F32),
        ],
        scratch_shapes=[
            pltpu.VMEM((MIX_TILE + V7X_SUBLANES, LRU_WIDTH), F32),
            pltpu.VMEM((1, LRU_WIDTH), F32),
            pltpu.VMEM((HG_HEADS, HG_HEAD_DIM, HG_HEAD_DIM), F32),
            pltpu.VMEM((MIX_TILE, HG_WIDTH), F32),
        ],
        compiler_params=pltpu.CompilerParams(
            dimension_semantics=("arbitrary", "arbitrary"), vmem_limit_bytes=VMEM_LIMIT),
        name="mix_prompt",
    )(x, mod3, p["ln_mix_pre"], p["ln_mix_post"], p["w_in"], p["conv_w"], p["conv_b"],
      p["wa_bd"], p["b_a"], p["wx_bd"], p["b_x"], p["lam"], p["lb_logits"], p["hg_norm_w"],
      p["w_out"])


def _mix_sample_kernel(x_ref, m_ref, pre_ref, post_ref, win_ref, cw_ref, cb_ref,
                       wa_ref, ba_ref, wx_ref, bx_ref, lam_ref, lbl_ref, hgw_ref, wout_ref,
                       h0_ref, conv0_ref, s0_ref,
                       o_ref, h_out_ref, conv_out_ref, s_out_ref,
                       qf_ref, f_ref, k_ref, v_ref, g_ref, qk_ref, olru_ref, ohg_ref, ostage_ref):
    i = pl.program_id(0)
    n_i = pl.num_programs(0)
    nb = SAMPLE_SEQ_BLOCK

    @pl.when(i == 0)
    def _():
        x = x_ref[...]
        h = _rms(x, pre_ref[...]) * (1.0 + m_ref[1]) + m_ref[0]
        proj = _dot(h.astype(BF16), win_ref[...])
        u = proj[:, 0:LRU_WIDTH]
        y_lru = proj[:, LRU_WIDTH:2 * LRU_WIDTH]
        o0 = 2 * LRU_WIDTH
        q = proj[:, o0:o0 + HG_WIDTH]
        f_raw = proj[:, o0 + HG_WIDTH:o0 + 2 * HG_WIDTH]
        v = proj[:, o0 + 2 * HG_WIDTH:o0 + 3 * HG_WIDTH]
        g_ref[...] = proj[:, o0 + 3 * HG_WIDTH:o0 + 4 * HG_WIDTH]

        cw = cw_ref[...]
        u_conv = cb_ref[...]
        for k in range(CONV_WIDTH - 1):
            u_conv = u_conv + conv0_ref[k] * cw[k:k + 1]
            if k > 0:
                conv_out_ref[k - 1] = conv0_ref[k]
        u_conv = u_conv + u * cw[CONV_WIDTH - 1:CONV_WIDTH]
        conv_out_ref[CONV_WIDTH - 2] = u

        a, mult, ig = _lru_gates(u_conv, wa_ref, ba_ref[...], wx_ref, bx_ref[...], lam_ref[...])
        hs = (mult * ig) * u_conv + a * h0_ref[...]
        h_out_ref[...] = hs
        olru_ref[...] = hs * _gelu_tanh(y_lru)

        lb = _lower_bound(lbl_ref[...])
        f = lb + (1.0 - lb) * _sigmoid(f_raw)
        logf = jnp.log(f)
        kk = 1.0 - f
        qs = q * (HG_HEAD_DIM ** -0.5)
        decay = jnp.exp(logf)
        f_ref[...] = decay
        qf_ref[...] = qs * decay
        k_ref[...] = kk
        v_ref[...] = v
        qk = qs * kk
        for hd in range(HG_HEADS):
            l0 = hd * HG_HEAD_DIM
            tot = jnp.sum(qk[:, l0:l0 + HG_HEAD_DIM], axis=-1, keepdims=True)
            qk_ref[:, l0:l0 + HG_HEAD_DIM] = jnp.broadcast_to(tot, (qk.shape[0], HG_HEAD_DIM))

    base = pl.multiple_of(i * nb, nb)
    grp = lambda ref: ref[pl.ds(base, nb), :]
    f_g, k_g, v_g, qf_g, qk_g = grp(f_ref), grp(k_ref), grp(v_ref), grp(qf_ref), grp(qk_ref)
    square = (HG_HEAD_DIM, HG_HEAD_DIM)
    for j in range(nb):
        for hd in range(HG_HEADS):
            l0 = hd * HG_HEAD_DIM
            rowv = lambda z: z[j:j + 1, l0:l0 + HG_HEAD_DIM]
            s_old = s0_ref[j, hd]
            f_col = jnp.broadcast_to(rowv(f_g), square).T
            k_col = jnp.broadcast_to(rowv(k_g), square).T
            v_row = rowv(v_g)
            outer = k_col.astype(BF16).astype(F32) * v_row.astype(BF16).astype(F32)
            s_out_ref[j, hd] = f_col * s_old + outer
            qf8 = jnp.broadcast_to(rowv(qf_g), (V7X_SUBLANES, HG_HEAD_DIM)).astype(BF16)
            o1 = _dot(qf8, s_old.astype(BF16))[0:1]
            ostage_ref[j:j + 1, l0:l0 + HG_HEAD_DIM] = o1 + rowv(qk_g) * v_row
    ohg_ref[pl.ds(base, nb), :] = ostage_ref[...]

    @pl.when(i == n_i - 1)
    def _():
        hgw = hgw_ref[...]
        parts = [olru_ref[...]]
        for hd in range(HG_HEADS):
            l0 = hd * HG_HEAD_DIM
            parts.append(_head_rms_gate(ohg_ref[:, l0:l0 + HG_HEAD_DIM], hgw,
                                        g_ref[:, l0:l0 + HG_HEAD_DIM]))
        mix_in = jnp.concatenate(parts, axis=-1).astype(BF16)
        mix = _dot(mix_in, wout_ref[...])
        o_ref[...] = x_ref[...] + m_ref[2] * _rms(mix, post_ref[...])


def _mix_sample(x, mod3, p, h0, conv0, s0):
    n_seq = x.shape[0]
    nb = SAMPLE_SEQ_BLOCK
    state_spec = pl.BlockSpec((nb, HG_HEADS, HG_HEAD_DIM, HG_HEAD_DIM), lambda i: (i, 0, 0, 0))
    full2 = lambda shape: pl.BlockSpec(shape, lambda i: (0,) * len(shape))
    return pl.pallas_call(
        _mix_sample_kernel,
        grid=(n_seq // nb,),
        in_specs=[
            _const_spec((n_seq, D_MODEL)),
            _const_spec((3, n_seq, D_MODEL)),
            _const_spec((1, D_MODEL)),
            _const_spec((1, D_MODEL)),
            _const_spec((D_MODEL, D_PROJ)),
            _const_spec((CONV_WIDTH, LRU_WIDTH)),
            _const_spec((1, LRU_WIDTH)),
            _const_spec((LRU_WIDTH, LRU_WIDTH)),
            _const_spec((1, LRU_WIDTH)),
            _const_spec((LRU_WIDTH, LRU_WIDTH)),
            _const_spec((1, LRU_WIDTH)),
            _const_spec((1, LRU_WIDTH)),
            _const_spec(p["lb_logits"].shape),
            _const_spec((1, HG_HEAD_DIM)),
            _const_spec((D_MODEL, D_MODEL)),
            _const_spec((n_seq, LRU_WIDTH)),
            _const_spec((CONV_WIDTH - 1, n_seq, LRU_WIDTH)),
            state_spec,
        ],
        out_specs=[
            full2((n_seq, D_MODEL)),
            full2((n_seq, LRU_WIDTH)),
            full2((CONV_WIDTH - 1, n_seq, LRU_WIDTH)),
            state_spec,
        ],
        out_shape=[
            jax.ShapeDtypeStruct((n_seq, D_MODEL), F32),
            jax.ShapeDtypeStruct((n_seq, LRU_WIDTH), F32),
            jax.ShapeDtypeStruct((CONV_WIDTH - 1, n_seq, LRU_WIDTH), F32),
            jax.ShapeDtypeStruct((n_seq, HG_HEADS, HG_HEAD_DIM, HG_HEAD_DIM), F32),
        ],
        scratch_shapes=[pltpu.VMEM((n_seq, HG_WIDTH), F32) for _ in range(8)]
        + [pltpu.VMEM((nb, HG_WIDTH), F32)],
        compiler_params=pltpu.CompilerParams(
            dimension_semantics=("arbitrary",), vmem_limit_bytes=VMEM_LIMIT),
        name="mix_sample",
    )(x, mod3, p["ln_mix_pre"], p["ln_mix_post"], p["w_in"], p["conv_w"], p["conv_b"],
      p["wa_bd"], p["b_a"], p["wx_bd"], p["b_x"], p["lam"], p["lb_logits"], p["hg_norm_w"],
      p["w_out"], h0, conv0, s0)


def _block_diag(w):
    heads, blk, _ = w.shape
    eye = jnp.eye(heads, dtype=w.dtype)
    return (eye[:, None, :, None] * w[:, :, None, :]).reshape(heads * blk, heads * blk)


def kernel(x_prompt, x_sample, c_prompt, c_sample, state_lru_h, state_lru_conv, state_hgrn_S, w_ada, b_ada, ln_ffn1_pre, ln_ffn1_post, ffn1_w_gate, ffn1_w_up, ffn1_w_down, ln_mix_pre, ln_mix_post, w_in, lru_conv_w, lru_conv_b, lru_w_a, lru_b_a, lru_w_x, lru_b_x, lru_lambda, hg_lb_logits, hg_norm_w, w_out, ln_ffn2_pre, ln_ffn2_post, ffn2_w_gate, ffn2_w_up, ffn2_w_down):
    depth = w_ada.shape[0]
    batch, seq_len, _ = x_prompt.shape
    n_seq = x_sample.shape[0]
    assert depth == 1 and x_sample.shape[1] == 1
    assert seq_len % FFN_TILE == 0 and seq_len % MIX_TILE == 0 and n_seq % SAMPLE_SEQ_BLOCK == 0

    xp = x_prompt.reshape(batch * seq_len, D_MODEL)
    xs = x_sample.reshape(n_seq, D_MODEL)
    ph, pc, pS, sh, sc, sS = [], [], [], [], [], []
    for l in range(depth):
        row = lambda w: w[l].reshape(1, -1)
        mod = _ada(jnp.concatenate([c_prompt, c_sample], axis=0), w_ada[l], b_ada[l])
        mod = mod.reshape(batch + n_seq, N_MOD, D_MODEL)
        mod_p = mod[:batch]
        mod_s = jnp.transpose(mod[batch:], (1, 0, 2))
        p = {
            "ln_mix_pre": row(ln_mix_pre), "ln_mix_post": row(ln_mix_post),
            "w_in": w_in[l].astype(BF16), "w_out": w_out[l].astype(BF16),
            "conv_w": lru_conv_w[l], "conv_b": row(lru_conv_b),
            "wa_bd": _block_diag(lru_w_a[l]).astype(BF16), "b_a": row(lru_b_a),
            "wx_bd": _block_diag(lru_w_x[l]).astype(BF16), "b_x": row(lru_b_x),
            "lam": row(lru_lambda), "lb_logits": hg_lb_logits,
            "hg_norm_w": row(hg_norm_w),
        }
        f1 = (row(ln_ffn1_pre), row(ln_ffn1_post), ffn1_w_gate[l].astype(BF16),
              ffn1_w_up[l].astype(BF16), ffn1_w_down[l].astype(BF16))
        f2 = (row(ln_ffn2_pre), row(ln_ffn2_post), ffn2_w_gate[l].astype(BF16),
              ffn2_w_up[l].astype(BF16), ffn2_w_down[l].astype(BF16))

        xp = _ffn_prompt(xp, mod_p[:, 0:3], *f1, seq_len)
        xp, h_p, c_p, S_p = _mix_prompt(xp, mod_p[:, 3:6], p, batch, seq_len)
        xp = _ffn_prompt(xp, mod_p[:, 6:9], *f2, seq_len)

        xs = _ffn_sample(xs, mod_s[0:3], *f1)
        conv0 = jnp.transpose(state_lru_conv[l], (1, 0, 2))
        xs, h_s, c_s, S_s = _mix_sample(xs, mod_s[3:6], p, state_lru_h[l], conv0, state_hgrn_S[l])
        xs = _ffn_sample(xs, mod_s[6:9], *f2)

        ph.append(h_p.reshape(batch, LRU_WIDTH)); pc.append(c_p); pS.append(S_p)
        sh.append(h_s); sc.append(jnp.transpose(c_s, (1, 0, 2))); sS.append(S_s)

    return (xp.reshape(batch, seq_len, D_MODEL), xs.reshape(n_seq, 1, D_MODEL),
            jnp.stack(ph), jnp.stack(pc), jnp.stack(pS), jnp.stack(sh), jnp.stack(sc), jnp.stack(sS))
```

```python
import jax
import jax.numpy as jnp
from jax import lax
from jax.experimental import pallas as pl
from jax.experimental.pallas import tpu as pltpu

F32 = jnp.float32
BF16 = jnp.bfloat16

D_MODEL = 1024
D_FF = 2816
LRU_WIDTH = 512
CONV_WIDTH = 4
LRU_C = 8.0
HG_WIDTH = 512
HG_HEAD_DIM = 128
HG_HEADS = HG_WIDTH // HG_HEAD_DIM
HG_CHUNK = 64
N_MOD = 9
D_PROJ = 2 * LRU_WIDTH + 4 * HG_WIDTH
EPS = 1e-6

V7X_SUBLANES = 8
V7X_VMEM_BYTES = 64 * 1024 * 1024
VMEM_LIMIT = V7X_VMEM_BYTES - 8 * 1024 * 1024

FFN_TILE = 512
MIX_TILE = 256
MIX_SEQS = 2
MIX_STAGE_LAG = 7
ADA_TILE = 1152
SAMPLE_SEQ_BLOCK = V7X_SUBLANES

HG_LEVELS = (1, 2, 4, 8, 16, 32)


def _rms(x, w):
    return (x * lax.rsqrt(jnp.mean(x * x, axis=-1, keepdims=True) + EPS)) * w


def _sigmoid(x):
    return 1.0 / (1.0 + jnp.exp(-x))


def _silu(x):
    return x * _sigmoid(x)


def _gelu_tanh(x):
    c = 0.7978845608028654
    return x * (0.5 * (1.0 + jnp.tanh(c * (x + 0.044715 * (x * x * x)))))


def _softplus(z):
    return jnp.maximum(z, 0.0) + jnp.log1p(jnp.exp(-jnp.abs(z)))


def _dot(a, b):
    return jnp.dot(a, b, preferred_element_type=F32)


def _dot_nt(a, b):
    return lax.dot_general(a, b, (((1,), (1,)), ((), ())), preferred_element_type=F32)


def _dot_tn(a, b):
    return lax.dot_general(a, b, (((0,), (0,)), ((), ())), preferred_element_type=F32)


def _const_spec(shape):
    nd = len(shape)
    return pl.BlockSpec(shape, lambda *_: (0,) * nd, pipeline_mode=pl.Buffered(1))


def _ada_kernel(c_ref, w_ref, b_ref, o_ref):
    s = _silu(c_ref[...])
    o_ref[...] = _dot(s.astype(BF16), w_ref[...].astype(BF16)) + b_ref[...]


def _ada(c_all, w_ada, b_ada):
    rows = c_all.shape[0]
    n = w_ada.shape[1]
    return pl.pallas_call(
        _ada_kernel,
        grid=(n // ADA_TILE,),
        in_specs=[
            pl.BlockSpec((rows, D_MODEL), lambda j: (0, 0)),
            pl.BlockSpec((D_MODEL, ADA_TILE), lambda j: (0, j)),
            pl.BlockSpec((1, ADA_TILE), lambda j: (0, j)),
        ],
        out_specs=pl.BlockSpec((rows, ADA_TILE), lambda j: (0, j)),
        out_shape=jax.ShapeDtypeStruct((rows, n), F32),
        compiler_params=pltpu.CompilerParams(
            dimension_semantics=("arbitrary",), vmem_limit_bytes=VMEM_LIMIT),
        name="ada_mod",
    )(c_all, w_ada, b_ada.reshape(1, n))


def _ffn_body(x, shift, scale, gate, pre_w, post_w, wg_ref, wu_ref, wd_ref):
    h = _rms(x, pre_w) * (1.0 + scale) + shift
    hb = h.astype(BF16)
    a = _dot(hb, wg_ref[...])
    u = _dot(hb, wu_ref[...])
    act = (_silu(a) * u).astype(BF16)
    y = _dot(act, wd_ref[...])
    return x + (0.5 * gate) * _rms(y, post_w)


def _ffn_prompt_kernel(x_ref, m_ref, pre_ref, post_ref, wg_ref, wu_ref, wd_ref, o_ref):
    m = m_ref[0]
    o_ref[...] = _ffn_body(x_ref[...], m[0:1], m[1:2], m[2:3], pre_ref[...], post_ref[...],
                           wg_ref, wu_ref, wd_ref)


def _ffn_sample_kernel(x_ref, m_ref, pre_ref, post_ref, wg_ref, wu_ref, wd_ref, o_ref):
    o_ref[...] = _ffn_body(x_ref[...], m_ref[0], m_ref[1], m_ref[2], pre_ref[...], post_ref[...],
                           wg_ref, wu_ref, wd_ref)


def _ffn_prompt(x, mod3, pre_w, post_w, wg, wu, wd, seq_len):
    n_tok = x.shape[0]
    steps_per_seq = seq_len // FFN_TILE
    return pl.pallas_call(
        _ffn_prompt_kernel,
        grid=(n_tok // FFN_TILE,),
        in_specs=[
            pl.BlockSpec((FFN_TILE, D_MODEL), lambda i: (i, 0)),
            pl.BlockSpec((1, 3, D_MODEL), lambda i: (i // steps_per_seq, 0, 0)),
            _const_spec((1, D_MODEL)),
            _const_spec((1, D_MODEL)),
            _const_spec((D_MODEL, D_FF)),
            _const_spec((D_MODEL, D_FF)),
            _const_spec((D_FF, D_MODEL)),
        ],
        out_specs=pl.BlockSpec((FFN_TILE, D_MODEL), lambda i: (i, 0)),
        out_shape=jax.ShapeDtypeStruct((n_tok, D_MODEL), F32),
        compiler_params=pltpu.CompilerParams(
            dimension_semantics=("arbitrary",), vmem_limit_bytes=VMEM_LIMIT),
        name="ffn_prompt",
    )(x, mod3, pre_w, post_w, wg, wu, wd)


def _ffn_sample(x, mod3, pre_w, post_w, wg, wu, wd):
    n_seq = x.shape[0]
    return pl.pallas_call(
        _ffn_sample_kernel,
        grid=(1,),
        in_specs=[
            _const_spec((n_seq, D_MODEL)),
            _const_spec((3, n_seq, D_MODEL)),
            _const_spec((1, D_MODEL)),
            _const_spec((1, D_MODEL)),
            _const_spec((D_MODEL, D_FF)),
            _const_spec((D_MODEL, D_FF)),
            _const_spec((D_FF, D_MODEL)),
        ],
        out_specs=pl.BlockSpec((n_seq, D_MODEL), lambda i: (0, 0)),
        out_shape=jax.ShapeDtypeStruct((n_seq, D_MODEL), F32),
        compiler_params=pltpu.CompilerParams(
            dimension_semantics=("arbitrary",), vmem_limit_bytes=VMEM_LIMIT),
        name="ffn_sample",
    )(x, mod3, pre_w, post_w, wg, wu, wd)


def _lower_bound(lb_logits):
    z = lb_logits - jnp.max(lb_logits, axis=0, keepdims=True)
    e = jnp.exp(z)
    return e[0:1] / jnp.sum(e, axis=0, keepdims=True)


def _lru_gates(u_conv, wa_ref, ba, wx_ref, bx, lam):
    ub = u_conv.astype(BF16)
    r = _sigmoid(_dot(ub, wa_ref[...]) + ba)
    ig = _sigmoid(_dot(ub, wx_ref[...]) + bx)
    log_a = (-LRU_C * r) * _softplus(-lam)
    a = jnp.exp(log_a)
    th = jnp.tanh(log_a)
    mult = jnp.sqrt((-2.0 * th) / (1.0 - th))
    return a, mult, ig


def _head_rms_gate(o, norm_w, g):
    return _rms(o, norm_w) * _silu(g)


def _mix_tile(x_ref, m_ref, o_ref, t, pre_ref, post_ref, win_ref, cw_ref, cb_ref, wa_ref, ba_ref,
              wx_ref, bx_ref, lam_ref, lbl_ref, hgw_ref, wout_ref, ucar_ref, hcar_ref, st_ref, ohg_ref):
    tb = MIX_TILE
    x = x_ref[...]
    m = m_ref[...]
    hb = (_rms(x, pre_ref[...]) * (1.0 + m[1:2]) + m[0:1]).astype(BF16)
    yield "pre"
    segs = []
    for c0 in range(0, D_PROJ, LRU_WIDTH):
        segs.append(_dot(hb, win_ref[:, c0:c0 + LRU_WIDTH]))
        yield "in"
    u, y_lru, q, f_raw, v, g = segs

    n_grp = tb // V7X_SUBLANES
    grouped = lambda z: z.reshape(n_grp, V7X_SUBLANES, z.shape[-1])
    flat = lambda z: z.reshape(tb, z.shape[-1])
    gshape = (n_grp, V7X_SUBLANES, LRU_WIDTH)
    sub = lax.broadcasted_iota(jnp.int32, gshape, 1)
    grp = lax.broadcasted_iota(jnp.int32, gshape, 0)

    u3 = grouped(u)
    u_all = jnp.concatenate([ucar_ref[...][None], u3], axis=0)
    ucar_ref[...] = u3[n_grp - 1]
    cw = cw_ref[...]
    u_conv = cb_ref[...]
    for k in range(CONV_WIDTH - 1):
        back = CONV_WIDTH - 1 - k
        rot = pltpu.roll(u_all, back, 1)
        u_conv = u_conv + jnp.where(sub >= back, rot[1:], rot[:-1]) * cw[k:k + 1]
    u_conv = flat(u_conv + u3 * cw[CONV_WIDTH - 1:CONV_WIDTH])
    yield "vec"

    a, mult, ig = _lru_gates(u_conv, wa_ref, ba_ref[...], wx_ref, bx_ref[...], lam_ref[...])
    yield "vec"
    a3, mult3 = grouped(a), grouped(mult)
    first_row = jnp.where(t == 0, 0, -1)
    mult3 = jnp.where(grp * V7X_SUBLANES + sub == first_row, 1.0, mult3)
    b3 = (mult3 * grouped(ig)) * grouped(u_conv)
    s = 1
    while s < V7X_SUBLANES:
        keep = sub >= s
        b3 = jnp.where(keep, a3 * pltpu.roll(b3, s, 1) + b3, b3)
        a3 = jnp.where(keep, a3 * pltpu.roll(a3, s, 1), a3)
        s *= 2
    yield "vec"
    carry = hcar_ref[...]
    hs = []
    for gi in range(n_grp):
        h_g = a3[gi] * carry + b3[gi]
        carry = h_g[V7X_SUBLANES - 1:V7X_SUBLANES]
        hs.append(h_g)
    hcar_ref[...] = carry
    o_lru = jnp.concatenate(hs, axis=0) * _gelu_tanh(y_lru)
    yield "vec"

    lb = _lower_bound(lbl_ref[...])
    f3 = grouped(lb + (1.0 - lb) * _sigmoid(f_raw))
    kk3 = 1.0 - f3
    qs3 = grouped(q * (HG_HEAD_DIM ** -0.5))
    to_mxu = lambda z: flat(z).astype(BF16)
    q_lv = [to_mxu(qs3)]
    k_lv = [to_mxu(kk3)]
    tot, pre, post = f3, f3, None
    for lvl in HG_LEVELS:
        yield "vec"
        q_lv.append(to_mxu(qs3 * pre))
        k_lv.append(k_lv[0] if post is None else to_mxu(kk3 * post))
        if lvl < V7X_SUBLANES:
            upper = (sub & lvl) != 0
            below = pltpu.roll(tot, lvl, 1)
            above = pltpu.roll(tot, V7X_SUBLANES - lvl, 1)
        else:
            upper = (grp & (lvl // V7X_SUBLANES)) != 0
            below = jnp.roll(tot, lvl // V7X_SUBLANES, axis=0)
            above = jnp.roll(tot, -(lvl // V7X_SUBLANES), axis=0)
        pre = jnp.where(upper, pre * below, pre)
        post = jnp.where(upper, 1.0, above) if post is None else jnp.where(upper, post, post * above)
        tot = tot * jnp.where(upper, below, above)
    q_in = to_mxu(qs3 * pre)
    k_out = to_mxu(kk3 * post)
    s_decay = flat(tot)
    vb = v.astype(BF16)

    ti = lax.broadcasted_iota(jnp.int32, (HG_CHUNK, HG_CHUNK), 0)
    si = lax.broadcasted_iota(jnp.int32, (HG_CHUNK, HG_CHUNK), 1)
    masks = [ti == si]
    tx = ti ^ si
    for lvl in HG_LEVELS:
        masks.append((tx >= lvl) & (tx < 2 * lvl) & ((ti & lvl) != 0))

    hgw = hgw_ref[...]
    for c in range(tb // HG_CHUNK):
        yield "chunk"
        r0 = c * HG_CHUNK
        for hd in range(HG_HEADS):
            l0 = hd * HG_HEAD_DIM
            blk = lambda z: z[r0:r0 + HG_CHUNK, l0:l0 + HG_HEAD_DIM]
            scores = jnp.zeros((HG_CHUNK, HG_CHUNK), F32)
            for ql, kl, msk in zip(q_lv, k_lv, masks):
                scores = jnp.where(msk, _dot_nt(blk(ql), blk(kl)), scores)
            st = st_ref[hd]
            o = _dot_nt(blk(q_in), st.astype(BF16)) + _dot(scores.astype(BF16), blk(vb))
            dec = s_decay[r0 + HG_CHUNK - 1:r0 + HG_CHUNK, l0:l0 + HG_HEAD_DIM]
            st_ref[hd] = st * dec + _dot_tn(blk(vb), blk(k_out))
            ohg_ref[r0:r0 + HG_CHUNK, l0:l0 + HG_HEAD_DIM] = _head_rms_gate(o, hgw, blk(g))
    yield "chunk"

    mix_in = jnp.concatenate([o_lru, ohg_ref[...]], axis=-1).astype(BF16)
    mix = _dot(mix_in, wout_ref[...])
    o_ref[...] = x + m[2:3] * _rms(mix, post_ref[...])


def _mix_prompt_kernel(x_ref, m_ref, *refs):
    weights = refs[:13]
    o_ref, h_out_ref, conv_out_ref, s_out_ref, ucar_ref, hcar_ref, st_ref, ohg_ref = refs[13:]
    t = pl.program_id(1)
    n_t = pl.num_programs(1)

    @pl.when(t == 0)
    def _():
        ucar_ref[...] = jnp.zeros_like(ucar_ref)
        hcar_ref[...] = jnp.zeros_like(hcar_ref)
        st_ref[...] = jnp.zeros_like(st_ref)

    tiles = [_mix_tile(x_ref.at[sq], m_ref.at[sq], o_ref.at[sq], t, *weights,
                       ucar_ref.at[sq], hcar_ref.at[sq], st_ref.at[sq], ohg_ref.at[sq])
             for sq in range(MIX_SEQS)]
    waiting, running, tick = tiles, [], 0
    while waiting or running:
        if waiting and tick % MIX_STAGE_LAG == 0:
            running.append(waiting.pop(0))
        running = [tile for tile in running if next(tile, tile) is not tile]
        tick += 1

    @pl.when(t == n_t - 1)
    def _():
        for sq in range(MIX_SEQS):
            h_out_ref[sq] = hcar_ref[sq]
            conv_out_ref[sq] = ucar_ref[sq, V7X_SUBLANES - (CONV_WIDTH - 1):V7X_SUBLANES, :]
            for hd in range(HG_HEADS):
                s_out_ref[sq, hd] = st_ref[sq, hd].T


def _mix_prompt(x, mod3, p):
    batch, seq_len, _ = x.shape
    n_t = seq_len // MIX_TILE
    nsq = MIX_SEQS
    tok_spec = pl.BlockSpec((nsq, MIX_TILE, D_MODEL), lambda b, t: (b, t, 0))
    return pl.pallas_call(
        _mix_prompt_kernel,
        grid=(batch // nsq, n_t),
        in_specs=[
            tok_spec,
            pl.BlockSpec((nsq, 3, D_MODEL), lambda b, t: (b, 0, 0)),
            _const_spec((1, D_MODEL)),
            _const_spec((1, D_MODEL)),
            _const_spec((D_MODEL, D_PROJ)),
            _const_spec((CONV_WIDTH, LRU_WIDTH)),
            _const_spec((1, LRU_WIDTH)),
            _const_spec((LRU_WIDTH, LRU_WIDTH)),
            _const_spec((1, LRU_WIDTH)),
            _const_spec((LRU_WIDTH, LRU_WIDTH)),
            _const_spec((1, LRU_WIDTH)),
            _const_spec((1, LRU_WIDTH)),
            _const_spec(p["lb_logits"].shape),
            _const_spec((1, HG_HEAD_DIM)),
            _const_spec((D_MODEL, D_MODEL)),
        ],
        out_specs=[
            tok_spec,
            pl.BlockSpec((nsq, 1, LRU_WIDTH), lambda b, t: (b, 0, 0)),
            pl.BlockSpec((nsq, CONV_WIDTH - 1, LRU_WIDTH), lambda b, t: (b, 0, 0)),
            pl.BlockSpec((nsq, HG_HEADS, HG_HEAD_DIM, HG_HEAD_DIM), lambda b, t: (b, 0, 0, 0)),
        ],
        out_shape=[
            jax.ShapeDtypeStruct((batch, seq_len, D_MODEL), F32),
            jax.ShapeDtypeStruct((batch, 1, LRU_WIDTH), F32),
            jax.ShapeDtypeStruct((batch, CONV_WIDTH - 1, LRU_WIDTH), F32),
            jax.ShapeDtypeStruct((batch, HG_HEADS, HG_HEAD_DIM, HG_HEAD_DIM), F32),
        ],
        scratch_shapes=[
            pltpu.VMEM((nsq, V7X_SUBLANES, LRU_WIDTH), F32),
            pltpu.VMEM((nsq, 1, LRU_WIDTH), F32),
            pltpu.VMEM((nsq, HG_HEADS, HG_HEAD_DIM, HG_HEAD_DIM), F32),
            pltpu.VMEM((nsq, MIX_TILE, HG_WIDTH), F32),
        ],
        compiler_params=pltpu.CompilerParams(
            dimension_semantics=("arbitrary", "arbitrary"), vmem_limit_bytes=VMEM_LIMIT),
        name="mix_prompt",
    )(x, mod3, p["ln_mix_pre"], p["ln_mix_post"], p["w_in"], p["conv_w"], p["conv_b"],
      p["wa_bd"], p["b_a"], p["wx_bd"], p["b_x"], p["lam"], p["lb_logits"], p["hg_norm_w"],
      p["w_out"])


def _mix_sample_kernel(x_ref, m_ref, pre_ref, post_ref, win_ref, cw_ref, cb_ref,
                       wa_ref, ba_ref, wx_ref, bx_ref, lam_ref, lbl_ref, hgw_ref, wout_ref,
                       h0_ref, conv0_ref, s0_ref,
                       o_ref, h_out_ref, conv_out_ref, s_out_ref,
                       qf_ref, f_ref, k_ref, v_ref, g_ref, qk_ref, olru_ref, ohg_ref, ostage_ref):
    i = pl.program_id(0)
    n_i = pl.num_programs(0)
    nb = SAMPLE_SEQ_BLOCK

    @pl.when(i == 0)
    def _():
        x = x_ref[...]
        h = _rms(x, pre_ref[...]) * (1.0 + m_ref[1]) + m_ref[0]
        proj = _dot(h.astype(BF16), win_ref[...])
        u = proj[:, 0:LRU_WIDTH]
        y_lru = proj[:, LRU_WIDTH:2 * LRU_WIDTH]
        o0 = 2 * LRU_WIDTH
        q = proj[:, o0:o0 + HG_WIDTH]
        f_raw = proj[:, o0 + HG_WIDTH:o0 + 2 * HG_WIDTH]
        v = proj[:, o0 + 2 * HG_WIDTH:o0 + 3 * HG_WIDTH]
        g_ref[...] = proj[:, o0 + 3 * HG_WIDTH:o0 + 4 * HG_WIDTH]

        cw = cw_ref[...]
        u_conv = cb_ref[...]
        for k in range(CONV_WIDTH - 1):
            u_conv = u_conv + conv0_ref[k] * cw[k:k + 1]
            if k > 0:
                conv_out_ref[k - 1] = conv0_ref[k]
        u_conv = u_conv + u * cw[CONV_WIDTH - 1:CONV_WIDTH]
        conv_out_ref[CONV_WIDTH - 2] = u

        a, mult, ig = _lru_gates(u_conv, wa_ref, ba_ref[...], wx_ref, bx_ref[...], lam_ref[...])
        hs = (mult * ig) * u_conv + a * h0_ref[...]
        h_out_ref[...] = hs
        olru_ref[...] = hs * _gelu_tanh(y_lru)

        lb = _lower_bound(lbl_ref[...])
        f = lb + (1.0 - lb) * _sigmoid(f_raw)
        kk = 1.0 - f
        qs = q * (HG_HEAD_DIM ** -0.5)
        f_ref[...] = f
        qf_ref[...] = qs * f
        k_ref[...] = kk
        v_ref[...] = v
        qk = qs * kk
        for hd in range(HG_HEADS):
            l0 = hd * HG_HEAD_DIM
            tot = jnp.sum(qk[:, l0:l0 + HG_HEAD_DIM], axis=-1, keepdims=True)
            qk_ref[:, l0:l0 + HG_HEAD_DIM] = jnp.broadcast_to(tot, (qk.shape[0], HG_HEAD_DIM))

    base = pl.multiple_of(i * nb, nb)
    grp = lambda ref: ref[pl.ds(base, nb), :]
    f_g, k_g, v_g, qf_g, qk_g = grp(f_ref), grp(k_ref), grp(v_ref), grp(qf_ref), grp(qk_ref)
    square = (HG_HEAD_DIM, HG_HEAD_DIM)
    for j in range(nb):
        for hd in range(HG_HEADS):
            l0 = hd * HG_HEAD_DIM
            rowv = lambda z: z[j:j + 1, l0:l0 + HG_HEAD_DIM]
            s_old = s0_ref[j, hd]
            f_col = jnp.broadcast_to(rowv(f_g), square).T
            k_col = jnp.broadcast_to(rowv(k_g), square).T
            v_row = rowv(v_g)
            outer = k_col.astype(BF16).astype(F32) * v_row.astype(BF16).astype(F32)
            s_out_ref[j, hd] = f_col * s_old + outer
            qf8 = jnp.broadcast_to(rowv(qf_g), (V7X_SUBLANES, HG_HEAD_DIM)).astype(BF16)
            o1 = _dot(qf8, s_old.astype(BF16))[0:1]
            ostage_ref[j:j + 1, l0:l0 + HG_HEAD_DIM] = o1 + rowv(qk_g) * v_row
    ohg_ref[pl.ds(base, nb), :] = ostage_ref[...]

    @pl.when(i == n_i - 1)
    def _():
        hgw = hgw_ref[...]
        parts = [olru_ref[...]]
        for hd in range(HG_HEADS):
            l0 = hd * HG_HEAD_DIM
            parts.append(_head_rms_gate(ohg_ref[:, l0:l0 + HG_HEAD_DIM], hgw,
                                        g_ref[:, l0:l0 + HG_HEAD_DIM]))
        mix_in = jnp.concatenate(parts, axis=-1).astype(BF16)
        mix = _dot(mix_in, wout_ref[...])
        o_ref[...] = x_ref[...] + m_ref[2] * _rms(mix, post_ref[...])


def _mix_sample(x, mod3, p, h0, conv0, s0):
    n_seq = x.shape[0]
    nb = SAMPLE_SEQ_BLOCK
    state_spec = pl.BlockSpec((nb, HG_HEADS, HG_HEAD_DIM, HG_HEAD_DIM), lambda i: (i, 0, 0, 0))
    full2 = lambda shape: pl.BlockSpec(shape, lambda i: (0,) * len(shape))
    return pl.pallas_call(
        _mix_sample_kernel,
        grid=(n_seq // nb,),
        in_specs=[
            _const_spec((n_seq, D_MODEL)),
            _const_spec((3, n_seq, D_MODEL)),
            _const_spec((1, D_MODEL)),
            _const_spec((1, D_MODEL)),
            _const_spec((D_MODEL, D_PROJ)),
            _const_spec((CONV_WIDTH, LRU_WIDTH)),
            _const_spec((1, LRU_WIDTH)),
            _const_spec((LRU_WIDTH, LRU_WIDTH)),
            _const_spec((1, LRU_WIDTH)),
            _const_spec((LRU_WIDTH, LRU_WIDTH)),
            _const_spec((1, LRU_WIDTH)),
            _const_spec((1, LRU_WIDTH)),
            _const_spec(p["lb_logits"].shape),
            _const_spec((1, HG_HEAD_DIM)),
            _const_spec((D_MODEL, D_MODEL)),
            _const_spec((n_seq, LRU_WIDTH)),
            _const_spec((CONV_WIDTH - 1, n_seq, LRU_WIDTH)),
            state_spec,
        ],
        out_specs=[
            full2((n_seq, D_MODEL)),
            full2((n_seq, LRU_WIDTH)),
            full2((CONV_WIDTH - 1, n_seq, LRU_WIDTH)),
            state_spec,
        ],
        out_shape=[
            jax.ShapeDtypeStruct((n_seq, D_MODEL), F32),
            jax.ShapeDtypeStruct((n_seq, LRU_WIDTH), F32),
            jax.ShapeDtypeStruct((CONV_WIDTH - 1, n_seq, LRU_WIDTH), F32),
            jax.ShapeDtypeStruct((n_seq, HG_HEADS, HG_HEAD_DIM, HG_HEAD_DIM), F32),
        ],
        scratch_shapes=[pltpu.VMEM((n_seq, HG_WIDTH), F32) for _ in range(8)]
        + [pltpu.VMEM((nb, HG_WIDTH), F32)],
        compiler_params=pltpu.CompilerParams(
            dimension_semantics=("arbitrary",), vmem_limit_bytes=VMEM_LIMIT),
        name="mix_sample",
    )(x, mod3, p["ln_mix_pre"], p["ln_mix_post"], p["w_in"], p["conv_w"], p["conv_b"],
      p["wa_bd"], p["b_a"], p["wx_bd"], p["b_x"], p["lam"], p["lb_logits"], p["hg_norm_w"],
      p["w_out"], h0, conv0, s0)


def _block_diag(w):
    heads, blk, _ = w.shape
    eye = jnp.eye(heads, dtype=w.dtype)
    return (eye[:, None, :, None] * w[:, :, None, :]).reshape(heads * blk, heads * blk)


def kernel(x_prompt, x_sample, c_prompt, c_sample, state_lru_h, state_lru_conv, state_hgrn_S, w_ada, b_ada, ln_ffn1_pre, ln_ffn1_post, ffn1_w_gate, ffn1_w_up, ffn1_w_down, ln_mix_pre, ln_mix_post, w_in, lru_conv_w, lru_conv_b, lru_w_a, lru_b_a, lru_w_x, lru_b_x, lru_lambda, hg_lb_logits, hg_norm_w, w_out, ln_ffn2_pre, ln_ffn2_post, ffn2_w_gate, ffn2_w_up, ffn2_w_down):
    depth = w_ada.shape[0]
    batch, seq_len, _ = x_prompt.shape
    n_seq = x_sample.shape[0]
    assert depth == 1 and x_sample.shape[1] == 1
    assert seq_len % FFN_TILE == 0 and seq_len % MIX_TILE == 0 and n_seq % SAMPLE_SEQ_BLOCK == 0
    assert batch % MIX_SEQS == 0

    xp = x_prompt.reshape(batch * seq_len, D_MODEL)
    xs = x_sample.reshape(n_seq, D_MODEL)
    ph, pc, pS, sh, sc, sS = [], [], [], [], [], []
    for l in range(depth):
        row = lambda w: w[l].reshape(1, -1)
        mod = _ada(jnp.concatenate([c_prompt, c_sample], axis=0), w_ada[l], b_ada[l])
        mod = mod.reshape(batch + n_seq, N_MOD, D_MODEL)
        mod_p = mod[:batch]
        mod_s = jnp.transpose(mod[batch:], (1, 0, 2))
        p = {
            "ln_mix_pre": row(ln_mix_pre), "ln_mix_post": row(ln_mix_post),
            "w_in": w_in[l].astype(BF16), "w_out": w_out[l].astype(BF16),
            "conv_w": lru_conv_w[l], "conv_b": row(lru_conv_b),
            "wa_bd": _block_diag(lru_w_a[l]).astype(BF16), "b_a": row(lru_b_a),
            "wx_bd": _block_diag(lru_w_x[l]).astype(BF16), "b_x": row(lru_b_x),
            "lam": row(lru_lambda), "lb_logits": hg_lb_logits,
            "hg_norm_w": row(hg_norm_w),
        }
        f1 = (row(ln_ffn1_pre), row(ln_ffn1_post), ffn1_w_gate[l].astype(BF16),
              ffn1_w_up[l].astype(BF16), ffn1_w_down[l].astype(BF16))
        f2 = (row(ln_ffn2_pre), row(ln_ffn2_post), ffn2_w_gate[l].astype(BF16),
              ffn2_w_up[l].astype(BF16), ffn2_w_down[l].astype(BF16))

        xp = _ffn_prompt(xp, mod_p[:, 0:3], *f1, seq_len)
        xp, h_p, c_p, S_p = _mix_prompt(xp.reshape(batch, seq_len, D_MODEL), mod_p[:, 3:6], p)
        xp = xp.reshape(batch * seq_len, D_MODEL)
        xp = _ffn_prompt(xp, mod_p[:, 6:9], *f2, seq_len)

        xs = _ffn_sample(xs, mod_s[0:3], *f1)
        conv0 = jnp.transpose(state_lru_conv[l], (1, 0, 2))
        xs, h_s, c_s, S_s = _mix_sample(xs, mod_s[3:6], p, state_lru_h[l], conv0, state_hgrn_S[l])
        xs = _ffn_sample(xs, mod_s[6:9], *f2)

        ph.append(h_p.reshape(batch, LRU_WIDTH)); pc.append(c_p); pS.append(S_p)
        sh.append(h_s); sc.append(jnp.transpose(c_s, (1, 0, 2))); sS.append(S_s)

    return (xp.reshape(batch, seq_len, D_MODEL), xs.reshape(n_seq, 1, D_MODEL),
            jnp.stack(ph), jnp.stack(pc), jnp.stack(pS), jnp.stack(sh), jnp.stack(sc), jnp.stack(sS))
```

```python
import jax
import jax.numpy as jnp
from jax import lax
from jax.experimental import pallas as pl
from jax.experimental.pallas import tpu as pltpu

F32 = jnp.float32
BF16 = jnp.bfloat16

D_MODEL = 1024
D_FF = 2816
LRU_WIDTH = 512
CONV_WIDTH = 4
LRU_C = 8.0
HG_WIDTH = 512
HG_HEAD_DIM = 128
HG_HEADS = HG_WIDTH // HG_HEAD_DIM
HG_CHUNK = 64
N_MOD = 9
D_PROJ = 2 * LRU_WIDTH + 4 * HG_WIDTH
EPS = 1e-6

V7X_SUBLANES = 8
V7X_VMEM_BYTES = 64 * 1024 * 1024
VMEM_LIMIT = V7X_VMEM_BYTES - 8 * 1024 * 1024

FFN_TILE = 512
MIX_TILE = 256
MIX_SEQS = 2
MIX_STAGE_LAG = 7
ADA_TILE = 1152
SAMPLE_SEQ_BLOCK = V7X_SUBLANES

HG_LEVELS = (1, 2, 4, 8, 16, 32)
HG_DIRECT_MIN = 2.0 ** -100


def _rms(x, w):
    return (x * lax.rsqrt(jnp.mean(x * x, axis=-1, keepdims=True) + EPS)) * w


def _sigmoid(x):
    return 1.0 / (1.0 + jnp.exp(-x))


def _silu(x):
    return x * _sigmoid(x)


def _gelu_tanh(x):
    c = 0.7978845608028654
    return x * (0.5 * (1.0 + jnp.tanh(c * (x + 0.044715 * (x * x * x)))))


def _softplus(z):
    return jnp.maximum(z, 0.0) + jnp.log1p(jnp.exp(-jnp.abs(z)))


def _dot(a, b):
    return jnp.dot(a, b, preferred_element_type=F32)


def _dot_nt(a, b):
    return lax.dot_general(a, b, (((1,), (1,)), ((), ())), preferred_element_type=F32)


def _dot_tn(a, b):
    return lax.dot_general(a, b, (((0,), (0,)), ((), ())), preferred_element_type=F32)


def _const_spec(shape):
    nd = len(shape)
    return pl.BlockSpec(shape, lambda *_: (0,) * nd, pipeline_mode=pl.Buffered(1))


def _ada_kernel(c_ref, w_ref, b_ref, o_ref):
    s = _silu(c_ref[...])
    o_ref[...] = _dot(s.astype(BF16), w_ref[...].astype(BF16)) + b_ref[...]


def _ada(c_all, w_ada, b_ada):
    rows = c_all.shape[0]
    n = w_ada.shape[1]
    return pl.pallas_call(
        _ada_kernel,
        grid=(n // ADA_TILE,),
        in_specs=[
            pl.BlockSpec((rows, D_MODEL), lambda j: (0, 0)),
            pl.BlockSpec((D_MODEL, ADA_TILE), lambda j: (0, j)),
            pl.BlockSpec((1, ADA_TILE), lambda j: (0, j)),
        ],
        out_specs=pl.BlockSpec((rows, ADA_TILE), lambda j: (0, j)),
        out_shape=jax.ShapeDtypeStruct((rows, n), F32),
        compiler_params=pltpu.CompilerParams(
            dimension_semantics=("arbitrary",), vmem_limit_bytes=VMEM_LIMIT),
        name="ada_mod",
    )(c_all, w_ada, b_ada.reshape(1, n))


def _ffn_body(x, shift, scale, gate, pre_w, post_w, wg_ref, wu_ref, wd_ref):
    h = _rms(x, pre_w) * (1.0 + scale) + shift
    hb = h.astype(BF16)
    a = _dot(hb, wg_ref[...])
    u = _dot(hb, wu_ref[...])
    act = (_silu(a) * u).astype(BF16)
    y = _dot(act, wd_ref[...])
    return x + (0.5 * gate) * _rms(y, post_w)


def _ffn_prompt_kernel(x_ref, m_ref, pre_ref, post_ref, wg_ref, wu_ref, wd_ref, o_ref):
    m = m_ref[0]
    half = FFN_TILE // 2
    for r0 in (0, half):
        o_ref[r0:r0 + half, :] = _ffn_body(x_ref[r0:r0 + half, :], m[0:1], m[1:2], m[2:3],
                                           pre_ref[...], post_ref[...], wg_ref, wu_ref, wd_ref)


def _ffn_sample_kernel(x_ref, m_ref, pre_ref, post_ref, wg_ref, wu_ref, wd_ref, o_ref):
    o_ref[...] = _ffn_body(x_ref[...], m_ref[0], m_ref[1], m_ref[2], pre_ref[...], post_ref[...],
                           wg_ref, wu_ref, wd_ref)


def _ffn_prompt(x, mod3, pre_w, post_w, wg, wu, wd, seq_len):
    n_tok = x.shape[0]
    steps_per_seq = seq_len // FFN_TILE
    return pl.pallas_call(
        _ffn_prompt_kernel,
        grid=(n_tok // FFN_TILE,),
        in_specs=[
            pl.BlockSpec((FFN_TILE, D_MODEL), lambda i: (i, 0)),
            pl.BlockSpec((1, 3, D_MODEL), lambda i: (i // steps_per_seq, 0, 0)),
            _const_spec((1, D_MODEL)),
            _const_spec((1, D_MODEL)),
            _const_spec((D_MODEL, D_FF)),
            _const_spec((D_MODEL, D_FF)),
            _const_spec((D_FF, D_MODEL)),
        ],
        out_specs=pl.BlockSpec((FFN_TILE, D_MODEL), lambda i: (i, 0)),
        out_shape=jax.ShapeDtypeStruct((n_tok, D_MODEL), F32),
        compiler_params=pltpu.CompilerParams(
            dimension_semantics=("arbitrary",), vmem_limit_bytes=VMEM_LIMIT),
        name="ffn_prompt",
    )(x, mod3, pre_w, post_w, wg, wu, wd)


def _ffn_sample(x, mod3, pre_w, post_w, wg, wu, wd):
    n_seq = x.shape[0]
    return pl.pallas_call(
        _ffn_sample_kernel,
        grid=(1,),
        in_specs=[
            _const_spec((n_seq, D_MODEL)),
            _const_spec((3, n_seq, D_MODEL)),
            _const_spec((1, D_MODEL)),
            _const_spec((1, D_MODEL)),
            _const_spec((D_MODEL, D_FF)),
            _const_spec((D_MODEL, D_FF)),
            _const_spec((D_FF, D_MODEL)),
        ],
        out_specs=pl.BlockSpec((n_seq, D_MODEL), lambda i: (0, 0)),
        out_shape=jax.ShapeDtypeStruct((n_seq, D_MODEL), F32),
        compiler_params=pltpu.CompilerParams(
            dimension_semantics=("arbitrary",), vmem_limit_bytes=VMEM_LIMIT),
        name="ffn_sample",
    )(x, mod3, pre_w, post_w, wg, wu, wd)


def _lower_bound(lb_logits):
    z = lb_logits - jnp.max(lb_logits, axis=0, keepdims=True)
    e = jnp.exp(z)
    return e[0:1] / jnp.sum(e, axis=0, keepdims=True)


def _lru_gates(u_conv, wa_ref, ba, wx_ref, bx, lam):
    ub = u_conv.astype(BF16)
    r = _sigmoid(_dot(ub, wa_ref[...]) + ba)
    ig = _sigmoid(_dot(ub, wx_ref[...]) + bx)
    log_a = (-LRU_C * r) * _softplus(-lam)
    a = jnp.exp(log_a)
    th = jnp.tanh(log_a)
    mult = jnp.sqrt((-2.0 * th) / (1.0 - th))
    return a, mult, ig


def _head_rms_gate(o, norm_w, g):
    return _rms(o, norm_w) * _silu(g)


def _hgrn_chunk_head(scores, q_in, k_out, dec, vb_blk, g_blk, hgw, st_ref, hd, ohg_ref, r0, l0):
    st = st_ref[hd]
    o = _dot_nt(q_in, st.astype(BF16)) + _dot(scores.astype(BF16), vb_blk)
    st_ref[hd] = st * dec + _dot_tn(vb_blk, k_out)
    ohg_ref[r0:r0 + HG_CHUNK, l0:l0 + HG_HEAD_DIM] = _head_rms_gate(o, hgw, g_blk)


def _hgrn_chunks_direct(qs, kk, pfx, tails, vb, g, hgw, st_ref, ohg_ref):
    inv = 1.0 / pfx
    q_in = (qs * pfx).astype(BF16)
    k_in = (kk * inv).astype(BF16)
    ti = lax.broadcasted_iota(jnp.int32, (HG_CHUNK, HG_CHUNK), 0)
    si = lax.broadcasted_iota(jnp.int32, (HG_CHUNK, HG_CHUNK), 1)
    causal = ti >= si
    for c, tail in enumerate(tails):
        r0 = c * HG_CHUNK
        k_out = ((kk[r0:r0 + HG_CHUNK] * inv[r0:r0 + HG_CHUNK]) * tail).astype(BF16)
        for hd in range(HG_HEADS):
            l0 = hd * HG_HEAD_DIM
            blk = lambda z: z[r0:r0 + HG_CHUNK, l0:l0 + HG_HEAD_DIM]
            scores = jnp.where(causal, _dot_nt(blk(q_in), blk(k_in)), 0.0)
            _hgrn_chunk_head(scores, blk(q_in), k_out[:, l0:l0 + HG_HEAD_DIM],
                             tail[:, l0:l0 + HG_HEAD_DIM], blk(vb), blk(g), hgw,
                             st_ref, hd, ohg_ref, r0, l0)


def _hgrn_chunks_levels(f3, kk3, qs3, vb, g, hgw, st_ref, ohg_ref):
    n_grp = f3.shape[0]
    tb = n_grp * V7X_SUBLANES
    sub = lax.broadcasted_iota(jnp.int32, f3.shape, 1)
    grp = lax.broadcasted_iota(jnp.int32, f3.shape, 0)
    to_mxu = lambda z: z.reshape(tb, z.shape[-1]).astype(BF16)
    q_lv = [to_mxu(qs3)]
    k_lv = [to_mxu(kk3)]
    tot, pre, post = f3, f3, None
    for lvl in HG_LEVELS:
        q_lv.append(to_mxu(qs3 * pre))
        k_lv.append(k_lv[0] if post is None else to_mxu(kk3 * post))
        if lvl < V7X_SUBLANES:
            upper = (sub & lvl) != 0
            below = pltpu.roll(tot, lvl, 1)
            above = pltpu.roll(tot, V7X_SUBLANES - lvl, 1)
        else:
            upper = (grp & (lvl // V7X_SUBLANES)) != 0
            below = jnp.roll(tot, lvl // V7X_SUBLANES, axis=0)
            above = jnp.roll(tot, -(lvl // V7X_SUBLANES), axis=0)
        pre = jnp.where(upper, pre * below, pre)
        post = jnp.where(upper, 1.0, above) if post is None else jnp.where(upper, post, post * above)
        tot = tot * jnp.where(upper, below, above)
    q_in = to_mxu(qs3 * pre)
    k_out = to_mxu(kk3 * post)
    s_decay = tot.reshape(tb, tot.shape[-1])

    ti = lax.broadcasted_iota(jnp.int32, (HG_CHUNK, HG_CHUNK), 0)
    si = lax.broadcasted_iota(jnp.int32, (HG_CHUNK, HG_CHUNK), 1)
    masks = [ti == si]
    tx = ti ^ si
    for lvl in HG_LEVELS:
        masks.append((tx >= lvl) & (tx < 2 * lvl) & ((ti & lvl) != 0))

    for c in range(tb // HG_CHUNK):
        r0 = c * HG_CHUNK
        for hd in range(HG_HEADS):
            l0 = hd * HG_HEAD_DIM
            blk = lambda z: z[r0:r0 + HG_CHUNK, l0:l0 + HG_HEAD_DIM]
            scores = jnp.zeros((HG_CHUNK, HG_CHUNK), F32)
            for ql, kl, msk in zip(q_lv, k_lv, masks):
                scores = jnp.where(msk, _dot_nt(blk(ql), blk(kl)), scores)
            dec = s_decay[r0 + HG_CHUNK - 1:r0 + HG_CHUNK, l0:l0 + HG_HEAD_DIM]
            _hgrn_chunk_head(scores, blk(q_in), blk(k_out), dec, blk(vb), blk(g), hgw,
                             st_ref, hd, ohg_ref, r0, l0)

def _mix_tile(x_ref, m_ref, o_ref, t, pre_ref, post_ref, win_ref, cw_ref, cb_ref, wa_ref, ba_ref,
              wx_ref, bx_ref, lam_ref, lbl_ref, hgw_ref, wout_ref, ucar_ref, hcar_ref, st_ref, ohg_ref):
    tb = MIX_TILE
    x = x_ref[...]
    m = m_ref[...]
    hb = (_rms(x, pre_ref[...]) * (1.0 + m[1:2]) + m[0:1]).astype(BF16)
    yield "pre"
    segs = []
    for c0 in range(0, D_PROJ, LRU_WIDTH):
        segs.append(_dot(hb, win_ref[:, c0:c0 + LRU_WIDTH]))
        yield "in"
    u, y_lru, q, f_raw, v, g = segs

    n_grp = tb // V7X_SUBLANES
    grouped = lambda z: z.reshape(n_grp, V7X_SUBLANES, z.shape[-1])
    flat = lambda z: z.reshape(tb, z.shape[-1])
    gshape = (n_grp, V7X_SUBLANES, LRU_WIDTH)
    sub = lax.broadcasted_iota(jnp.int32, gshape, 1)
    grp = lax.broadcasted_iota(jnp.int32, gshape, 0)

    u3 = grouped(u)
    u_all = jnp.concatenate([ucar_ref[...][None], u3], axis=0)
    ucar_ref[...] = u3[n_grp - 1]
    cw = cw_ref[...]
    u_conv = cb_ref[...]
    for k in range(CONV_WIDTH - 1):
        back = CONV_WIDTH - 1 - k
        rot = pltpu.roll(u_all, back, 1)
        u_conv = u_conv + jnp.where(sub >= back, rot[1:], rot[:-1]) * cw[k:k + 1]
    u_conv = flat(u_conv + u3 * cw[CONV_WIDTH - 1:CONV_WIDTH])
    yield "vec"

    a, mult, ig = _lru_gates(u_conv, wa_ref, ba_ref[...], wx_ref, bx_ref[...], lam_ref[...])
    yield "vec"
    a3, mult3 = grouped(a), grouped(mult)
    first_row = jnp.where(t == 0, 0, -1)
    mult3 = jnp.where(grp * V7X_SUBLANES + sub == first_row, 1.0, mult3)
    b3 = (mult3 * grouped(ig)) * grouped(u_conv)
    s = 1
    while s < V7X_SUBLANES:
        keep = sub >= s
        b3 = jnp.where(keep, a3 * pltpu.roll(b3, s, 1) + b3, b3)
        a3 = jnp.where(keep, a3 * pltpu.roll(a3, s, 1), a3)
        s *= 2
    yield "vec"
    carry = hcar_ref[...]
    hs = []
    for gi in range(n_grp):
        h_g = a3[gi] * carry + b3[gi]
        carry = h_g[V7X_SUBLANES - 1:V7X_SUBLANES]
        hs.append(h_g)
    hcar_ref[...] = carry
    o_lru = jnp.concatenate(hs, axis=0) * _gelu_tanh(y_lru)
    yield "vec"

    lb = _lower_bound(lbl_ref[...])
    f3 = grouped(lb + (1.0 - lb) * _sigmoid(f_raw))
    kk3 = 1.0 - f3
    qs3 = grouped(q * (HG_HEAD_DIM ** -0.5))
    vb = v.astype(BF16)
    hgw = hgw_ref[...]

    pfx = f3
    s = 1
    while s < V7X_SUBLANES:
        pfx = jnp.where(sub >= s, pfx * pltpu.roll(pfx, s, 1), pfx)
        s *= 2
    grp_per_chunk = HG_CHUNK // V7X_SUBLANES
    rows, tails = [], []
    for gi in range(n_grp):
        p_g = pfx[gi] if gi % grp_per_chunk == 0 else pfx[gi] * carry
        carry = p_g[V7X_SUBLANES - 1:V7X_SUBLANES]
        rows.append(p_g)
        if gi % grp_per_chunk == grp_per_chunk - 1:
            tails.append(carry)
    pfx = jnp.concatenate(rows, axis=0)
    direct = jnp.min(jnp.concatenate(tails, axis=0)) >= HG_DIRECT_MIN
    yield "vec"

    @pl.when(direct)
    def _():
        _hgrn_chunks_direct(flat(qs3), flat(kk3), pfx, tails, vb, g, hgw, st_ref, ohg_ref)

    @pl.when(jnp.logical_not(direct))
    def _():
        _hgrn_chunks_levels(f3, kk3, qs3, vb, g, hgw, st_ref, ohg_ref)
    yield "chunk"

    mix_in = jnp.concatenate([o_lru, ohg_ref[...]], axis=-1).astype(BF16)
    mix = _dot(mix_in, wout_ref[...])
    o_ref[...] = x + m[2:3] * _rms(mix, post_ref[...])


def _mix_prompt_kernel(x_ref, m_ref, *refs):
    weights = refs[:13]
    o_ref, h_out_ref, conv_out_ref, s_out_ref, ucar_ref, hcar_ref, st_ref, ohg_ref = refs[13:]
    t = pl.program_id(1)
    n_t = pl.num_programs(1)

    @pl.when(t == 0)
    def _():
        ucar_ref[...] = jnp.zeros_like(ucar_ref)
        hcar_ref[...] = jnp.zeros_like(hcar_ref)
        st_ref[...] = jnp.zeros_like(st_ref)

    tiles = [_mix_tile(x_ref.at[sq], m_ref.at[sq], o_ref.at[sq], t, *weights,
                       ucar_ref.at[sq], hcar_ref.at[sq], st_ref.at[sq], ohg_ref.at[sq])
             for sq in range(MIX_SEQS)]
    waiting, running, tick = tiles, [], 0
    while waiting or running:
        if waiting and tick % MIX_STAGE_LAG == 0:
            running.append(waiting.pop(0))
        running = [tile for tile in running if next(tile, tile) is not tile]
        tick += 1

    @pl.when(t == n_t - 1)
    def _():
        for sq in range(MIX_SEQS):
            h_out_ref[sq] = hcar_ref[sq]
            conv_out_ref[sq] = ucar_ref[sq, V7X_SUBLANES - (CONV_WIDTH - 1):V7X_SUBLANES, :]
            for hd in range(HG_HEADS):
                s_out_ref[sq, hd] = st_ref[sq, hd].T


def _mix_prompt(x, mod3, p):
    batch, seq_len, _ = x.shape
    n_t = seq_len // MIX_TILE
    nsq = MIX_SEQS
    tok_spec = pl.BlockSpec((nsq, MIX_TILE, D_MODEL), lambda b, t: (b, t, 0))
    return pl.pallas_call(
        _mix_prompt_kernel,
        grid=(batch // nsq, n_t),
        in_specs=[
            tok_spec,
            pl.BlockSpec((nsq, 3, D_MODEL), lambda b, t: (b, 0, 0)),
            _const_spec((1, D_MODEL)),
            _const_spec((1, D_MODEL)),
            _const_spec((D_MODEL, D_PROJ)),
            _const_spec((CONV_WIDTH, LRU_WIDTH)),
            _const_spec((1, LRU_WIDTH)),
            _const_spec((LRU_WIDTH, LRU_WIDTH)),
            _const_spec((1, LRU_WIDTH)),
            _const_spec((LRU_WIDTH, LRU_WIDTH)),
            _const_spec((1, LRU_WIDTH)),
            _const_spec((1, LRU_WIDTH)),
            _const_spec(p["lb_logits"].shape),
            _const_spec((1, HG_HEAD_DIM)),
            _const_spec((D_MODEL, D_MODEL)),
        ],
        out_specs=[
            tok_spec,
            pl.BlockSpec((nsq, 1, LRU_WIDTH), lambda b, t: (b, 0, 0)),
            pl.BlockSpec((nsq, CONV_WIDTH - 1, LRU_WIDTH), lambda b, t: (b, 0, 0)),
            pl.BlockSpec((nsq, HG_HEADS, HG_HEAD_DIM, HG_HEAD_DIM), lambda b, t: (b, 0, 0, 0)),
        ],
        out_shape=[
            jax.ShapeDtypeStruct((batch, seq_len, D_MODEL), F32),
            jax.ShapeDtypeStruct((batch, 1, LRU_WIDTH), F32),
            jax.ShapeDtypeStruct((batch, CONV_WIDTH - 1, LRU_WIDTH), F32),
            jax.ShapeDtypeStruct((batch, HG_HEADS, HG_HEAD_DIM, HG_HEAD_DIM), F32),
        ],
        scratch_shapes=[
            pltpu.VMEM((nsq, V7X_SUBLANES, LRU_WIDTH), F32),
            pltpu.VMEM((nsq, 1, LRU_WIDTH), F32),
            pltpu.VMEM((nsq, HG_HEADS, HG_HEAD_DIM, HG_HEAD_DIM), F32),
            pltpu.VMEM((nsq, MIX_TILE, HG_WIDTH), F32),
        ],
        compiler_params=pltpu.CompilerParams(
            dimension_semantics=("arbitrary", "arbitrary"), vmem_limit_bytes=VMEM_LIMIT),
        name="mix_prompt",
    )(x, mod3, p["ln_mix_pre"], p["ln_mix_post"], p["w_in"], p["conv_w"], p["conv_b"],
      p["wa_bd"], p["b_a"], p["wx_bd"], p["b_x"], p["lam"], p["lb_logits"], p["hg_norm_w"],
      p["w_out"])


def _mix_sample_kernel(x_ref, m_ref, pre_ref, post_ref, win_ref, cw_ref, cb_ref,
                       wa_ref, ba_ref, wx_ref, bx_ref, lam_ref, lbl_ref, hgw_ref, wout_ref,
                       h0_ref, conv0_ref, s0_ref,
                       o_ref, h_out_ref, conv_out_ref, s_out_ref,
                       qf_ref, f_ref, k_ref, v_ref, g_ref, qk_ref, olru_ref, ohg_ref, ostage_ref):
    i = pl.program_id(0)
    n_i = pl.num_programs(0)
    nb = SAMPLE_SEQ_BLOCK

    @pl.when(i == 0)
    def _():
        x = x_ref[...]
        h = _rms(x, pre_ref[...]) * (1.0 + m_ref[1]) + m_ref[0]
        proj = _dot(h.astype(BF16), win_ref[...])
        u = proj[:, 0:LRU_WIDTH]
        y_lru = proj[:, LRU_WIDTH:2 * LRU_WIDTH]
        o0 = 2 * LRU_WIDTH
        q = proj[:, o0:o0 + HG_WIDTH]
        f_raw = proj[:, o0 + HG_WIDTH:o0 + 2 * HG_WIDTH]
        v = proj[:, o0 + 2 * HG_WIDTH:o0 + 3 * HG_WIDTH]
        g_ref[...] = proj[:, o0 + 3 * HG_WIDTH:o0 + 4 * HG_WIDTH]

        cw = cw_ref[...]
        u_conv = cb_ref[...]
        for k in range(CONV_WIDTH - 1):
            u_conv = u_conv + conv0_ref[k] * cw[k:k + 1]
            if k > 0:
                conv_out_ref[k - 1] = conv0_ref[k]
        u_conv = u_conv + u * cw[CONV_WIDTH - 1:CONV_WIDTH]
        conv_out_ref[CONV_WIDTH - 2] = u

        a, mult, ig = _lru_gates(u_conv, wa_ref, ba_ref[...], wx_ref, bx_ref[...], lam_ref[...])
        hs = (mult * ig) * u_conv + a * h0_ref[...]
        h_out_ref[...] = hs
        olru_ref[...] = hs * _gelu_tanh(y_lru)

        lb = _lower_bound(lbl_ref[...])
        f = lb + (1.0 - lb) * _sigmoid(f_raw)
        kk = 1.0 - f
        qs = q * (HG_HEAD_DIM ** -0.5)
        f_ref[...] = f
        qf_ref[...] = qs * f
        k_ref[...] = kk
        v_ref[...] = v
        qk = qs * kk
        for hd in range(HG_HEADS):
            l0 = hd * HG_HEAD_DIM
            tot = jnp.sum(qk[:, l0:l0 + HG_HEAD_DIM], axis=-1, keepdims=True)
            qk_ref[:, l0:l0 + HG_HEAD_DIM] = jnp.broadcast_to(tot, (qk.shape[0], HG_HEAD_DIM))

    base = pl.multiple_of(i * nb, nb)
    grp = lambda ref: ref[pl.ds(base, nb), :]
    f_g, k_g, v_g, qf_g, qk_g = grp(f_ref), grp(k_ref), grp(v_ref), grp(qf_ref), grp(qk_ref)
    square = (HG_HEAD_DIM, HG_HEAD_DIM)
    for j in range(nb):
        for hd in range(HG_HEADS):
            l0 = hd * HG_HEAD_DIM
            rowv = lambda z: z[j:j + 1, l0:l0 + HG_HEAD_DIM]
            s_old = s0_ref[j, hd]
            f_col = jnp.broadcast_to(rowv(f_g), square).T
            k_col = jnp.broadcast_to(rowv(k_g), square).T
            v_row = rowv(v_g)
            outer = k_col.astype(BF16).astype(F32) * v_row.astype(BF16).astype(F32)
            s_out_ref[j, hd] = f_col * s_old + outer
            qf8 = jnp.broadcast_to(rowv(qf_g), (V7X_SUBLANES, HG_HEAD_DIM)).astype(BF16)
            o1 = _dot(qf8, s_old.astype(BF16))[0:1]
            ostage_ref[j:j + 1, l0:l0 + HG_HEAD_DIM] = o1 + rowv(qk_g) * v_row
    ohg_ref[pl.ds(base, nb), :] = ostage_ref[...]

    @pl.when(i == n_i - 1)
    def _():
        hgw = hgw_ref[...]
        parts = [olru_ref[...]]
        for hd in range(HG_HEADS):
            l0 = hd * HG_HEAD_DIM
            parts.append(_head_rms_gate(ohg_ref[:, l0:l0 + HG_HEAD_DIM], hgw,
                                        g_ref[:, l0:l0 + HG_HEAD_DIM]))
        mix_in = jnp.concatenate(parts, axis=-1).astype(BF16)
        mix = _dot(mix_in, wout_ref[...])
        o_ref[...] = x_ref[...] + m_ref[2] * _rms(mix, post_ref[...])


def _mix_sample(x, mod3, p, h0, conv0, s0):
    n_seq = x.shape[0]
    nb = SAMPLE_SEQ_BLOCK
    state_spec = pl.BlockSpec((nb, HG_HEADS, HG_HEAD_DIM, HG_HEAD_DIM), lambda i: (i, 0, 0, 0))
    full2 = lambda shape: pl.BlockSpec(shape, lambda i: (0,) * len(shape))
    return pl.pallas_call(
        _mix_sample_kernel,
        grid=(n_seq // nb,),
        in_specs=[
            _const_spec((n_seq, D_MODEL)),
            _const_spec((3, n_seq, D_MODEL)),
            _const_spec((1, D_MODEL)),
            _const_spec((1, D_MODEL)),
            _const_spec((D_MODEL, D_PROJ)),
            _const_spec((CONV_WIDTH, LRU_WIDTH)),
            _const_spec((1, LRU_WIDTH)),
            _const_spec((LRU_WIDTH, LRU_WIDTH)),
            _const_spec((1, LRU_WIDTH)),
            _const_spec((LRU_WIDTH, LRU_WIDTH)),
            _const_spec((1, LRU_WIDTH)),
            _const_spec((1, LRU_WIDTH)),
            _const_spec(p["lb_logits"].shape),
            _const_spec((1, HG_HEAD_DIM)),
            _const_spec((D_MODEL, D_MODEL)),
            _const_spec((n_seq, LRU_WIDTH)),
            _const_spec((CONV_WIDTH - 1, n_seq, LRU_WIDTH)),
            state_spec,
        ],
        out_specs=[
            full2((n_seq, D_MODEL)),
            full2((n_seq, LRU_WIDTH)),
            full2((CONV_WIDTH - 1, n_seq, LRU_WIDTH)),
            state_spec,
        ],
        out_shape=[
            jax.ShapeDtypeStruct((n_seq, D_MODEL), F32),
            jax.ShapeDtypeStruct((n_seq, LRU_WIDTH), F32),
            jax.ShapeDtypeStruct((CONV_WIDTH - 1, n_seq, LRU_WIDTH), F32),
            jax.ShapeDtypeStruct((n_seq, HG_HEADS, HG_HEAD_DIM, HG_HEAD_DIM), F32),
        ],
        scratch_shapes=[pltpu.VMEM((n_seq, HG_WIDTH), F32) for _ in range(8)]
        + [pltpu.VMEM((nb, HG_WIDTH), F32)],
        compiler_params=pltpu.CompilerParams(
            dimension_semantics=("arbitrary",), vmem_limit_bytes=VMEM_LIMIT),
        name="mix_sample",
    )(x, mod3, p["ln_mix_pre"], p["ln_mix_post"], p["w_in"], p["conv_w"], p["conv_b"],
      p["wa_bd"], p["b_a"], p["wx_bd"], p["b_x"], p["lam"], p["lb_logits"], p["hg_norm_w"],
      p["w_out"], h0, conv0, s0)


def _block_diag(w):
    heads, blk, _ = w.shape
    eye = jnp.eye(heads, dtype=w.dtype)
    return (eye[:, None, :, None] * w[:, :, None, :]).reshape(heads * blk, heads * blk)


def kernel(x_prompt, x_sample, c_prompt, c_sample, state_lru_h, state_lru_conv, state_hgrn_S, w_ada, b_ada, ln_ffn1_pre, ln_ffn1_post, ffn1_w_gate, ffn1_w_up, ffn1_w_down, ln_mix_pre, ln_mix_post, w_in, lru_conv_w, lru_conv_b, lru_w_a, lru_b_a, lru_w_x, lru_b_x, lru_lambda, hg_lb_logits, hg_norm_w, w_out, ln_ffn2_pre, ln_ffn2_post, ffn2_w_gate, ffn2_w_up, ffn2_w_down):
    depth = w_ada.shape[0]
    batch, seq_len, _ = x_prompt.shape
    n_seq = x_sample.shape[0]
    assert depth == 1 and x_sample.shape[1] == 1
    assert seq_len % FFN_TILE == 0 and seq_len % MIX_TILE == 0 and n_seq % SAMPLE_SEQ_BLOCK == 0
    assert batch % MIX_SEQS == 0

    xp = x_prompt.reshape(batch * seq_len, D_MODEL)
    xs = x_sample.reshape(n_seq, D_MODEL)
    ph, pc, pS, sh, sc, sS = [], [], [], [], [], []
    for l in range(depth):
        row = lambda w: w[l].reshape(1, -1)
        mod = _ada(jnp.concatenate([c_prompt, c_sample], axis=0), w_ada[l], b_ada[l])
        mod = mod.reshape(batch + n_seq, N_MOD, D_MODEL)
        mod_p = mod[:batch]
        mod_s = jnp.transpose(mod[batch:], (1, 0, 2))
        p = {
            "ln_mix_pre": row(ln_mix_pre), "ln_mix_post": row(ln_mix_post),
            "w_in": w_in[l].astype(BF16), "w_out": w_out[l].astype(BF16),
            "conv_w": lru_conv_w[l], "conv_b": row(lru_conv_b),
            "wa_bd": _block_diag(lru_w_a[l]).astype(BF16), "b_a": row(lru_b_a),
            "wx_bd": _block_diag(lru_w_x[l]).astype(BF16), "b_x": row(lru_b_x),
            "lam": row(lru_lambda), "lb_logits": hg_lb_logits,
            "hg_norm_w": row(hg_norm_w),
        }
        f1 = (row(ln_ffn1_pre), row(ln_ffn1_post), ffn1_w_gate[l].astype(BF16),
              ffn1_w_up[l].astype(BF16), ffn1_w_down[l].astype(BF16))
        f2 = (row(ln_ffn2_pre), row(ln_ffn2_post), ffn2_w_gate[l].astype(BF16),
              ffn2_w_up[l].astype(BF16), ffn2_w_down[l].astype(BF16))

        xp = _ffn_prompt(xp, mod_p[:, 0:3], *f1, seq_len)
        xp, h_p, c_p, S_p = _mix_prompt(xp.reshape(batch, seq_len, D_MODEL), mod_p[:, 3:6], p)
        xp = xp.reshape(batch * seq_len, D_MODEL)
        xp = _ffn_prompt(xp, mod_p[:, 6:9], *f2, seq_len)

        xs = _ffn_sample(xs, mod_s[0:3], *f1)
        conv0 = jnp.transpose(state_lru_conv[l], (1, 0, 2))
        xs, h_s, c_s, S_s = _mix_sample(xs, mod_s[3:6], p, state_lru_h[l], conv0, state_hgrn_S[l])
        xs = _ffn_sample(xs, mod_s[6:9], *f2)

        ph.append(h_p.reshape(batch, LRU_WIDTH)); pc.append(c_p); pS.append(S_p)
        sh.append(h_s); sc.append(jnp.transpose(c_s, (1, 0, 2))); sS.append(S_s)

    return (xp.reshape(batch, seq_len, D_MODEL), xs.reshape(n_seq, 1, D_MODEL),
            jnp.stack(ph), jnp.stack(pc), jnp.stack(pS), jnp.stack(sh), jnp.stack(sc), jnp.stack(sS))
```

```python
import jax
import jax.numpy as jnp
from jax import lax
from jax.experimental import pallas as pl
from jax.experimental.pallas import tpu as pltpu

F32 = jnp.float32
BF16 = jnp.bfloat16

D_MODEL = 1024
D_FF = 2816
LRU_WIDTH = 512
CONV_WIDTH = 4
LRU_C = 8.0
HG_WIDTH = 512
HG_HEAD_DIM = 128
HG_HEADS = HG_WIDTH // HG_HEAD_DIM
HG_CHUNK = 64
N_MOD = 9
D_PROJ = 2 * LRU_WIDTH + 4 * HG_WIDTH
EPS = 1e-6

V7X_SUBLANES = 8
V7X_VMEM_BYTES = 64 * 1024 * 1024
VMEM_LIMIT = V7X_VMEM_BYTES - 8 * 1024 * 1024

FFN_TILE = 512
MIX_TILE = 256
MIX_SEQS = 2
MIX_STAGE_LAG = 7
FFN_WBLOCK = 256
FFN_WSTEPS = D_FF // FFN_WBLOCK
SAMPLE_SEQ_BLOCK = V7X_SUBLANES

HG_LEVELS = (1, 2, 4, 8, 16, 32)
HG_DIRECT_MIN = 2.0 ** -100


def _rms(x, w):
    return (x * lax.rsqrt(jnp.mean(x * x, axis=-1, keepdims=True) + EPS)) * w


def _sigmoid(x):
    return 1.0 / (1.0 + jnp.exp(-x))


def _silu(x):
    return x * _sigmoid(x)


def _gelu_tanh(x):
    c = 0.7978845608028654
    return x * (0.5 * (1.0 + jnp.tanh(c * (x + 0.044715 * (x * x * x)))))


def _softplus(z):
    return jnp.maximum(z, 0.0) + jnp.log1p(jnp.exp(-jnp.abs(z)))


def _dot(a, b):
    return jnp.dot(a, b, preferred_element_type=F32)


def _dot_nt(a, b):
    return lax.dot_general(a, b, (((1,), (1,)), ((), ())), preferred_element_type=F32)


def _dot_tn(a, b):
    return lax.dot_general(a, b, (((0,), (0,)), ((), ())), preferred_element_type=F32)


def _const_spec(shape):
    nd = len(shape)
    return pl.BlockSpec(shape, lambda *_: (0,) * nd, pipeline_mode=pl.Buffered(1))


def _ada_kernel(c_ref, w_ref, b_ref, o_ref):
    s = _silu(c_ref[...])
    o_ref[0] = _dot(s.astype(BF16), w_ref[...].astype(BF16)) + b_ref[...]


def _ada(c_all, w_ada, b_ada):
    rows = c_all.shape[0]
    return pl.pallas_call(
        _ada_kernel,
        grid=(N_MOD,),
        in_specs=[
            pl.BlockSpec((rows, D_MODEL), lambda j: (0, 0)),
            pl.BlockSpec((D_MODEL, D_MODEL), lambda j: (0, j)),
            pl.BlockSpec((1, D_MODEL), lambda j: (0, j)),
        ],
        out_specs=pl.BlockSpec((1, rows, D_MODEL), lambda j: (j, 0, 0)),
        out_shape=jax.ShapeDtypeStruct((N_MOD, rows, D_MODEL), F32),
        compiler_params=pltpu.CompilerParams(
            dimension_semantics=("arbitrary",), vmem_limit_bytes=VMEM_LIMIT),
        name="ada_mod",
    )(c_all, w_ada, b_ada.reshape(1, N_MOD * D_MODEL))


def _ffn_body(x, shift, scale, gate, pre_w, post_w, wg_ref, wu_ref, wd_ref):
    h = _rms(x, pre_w) * (1.0 + scale) + shift
    hb = h.astype(BF16)
    a = _dot(hb, wg_ref[...])
    u = _dot(hb, wu_ref[...])
    act = (_silu(a) * u).astype(BF16)
    y = _dot(act, wd_ref[...])
    return x + (0.5 * gate) * _rms(y, post_w)


def _ffn_kernel(xp_ref, mp_ref, xs_ref, ms_ref, pre_ref, post_ref, wg32_ref, wu32_ref, wd32_ref,
                op_ref, os_ref, wg_ref, wu_ref, wd_ref):
    s = pl.program_id(0)
    n_steps = pl.num_programs(0)

    for j in range(FFN_WSTEPS):
        @pl.when(s == j)
        def _(j=j):
            c0 = j * FFN_WBLOCK
            wg_ref[:, c0:c0 + FFN_WBLOCK] = wg32_ref[...].astype(BF16)
            wu_ref[:, c0:c0 + FFN_WBLOCK] = wu32_ref[...].astype(BF16)
            wd_ref[c0:c0 + FFN_WBLOCK, :] = wd32_ref[...].astype(BF16)

    @pl.when((s >= FFN_WSTEPS) & (s < n_steps - 1))
    def _():
        m = mp_ref[0]
        half = FFN_TILE // 2
        for r0 in (0, half):
            op_ref[r0:r0 + half, :] = _ffn_body(xp_ref[r0:r0 + half, :], m[0:1], m[1:2], m[2:3],
                                                pre_ref[...], post_ref[...], wg_ref, wu_ref, wd_ref)

    @pl.when(s == n_steps - 1)
    def _():
        os_ref[...] = _ffn_body(xs_ref[...], ms_ref[0], ms_ref[1], ms_ref[2], pre_ref[...],
                                post_ref[...], wg_ref, wu_ref, wd_ref)


def _ffn(xp, xs, mod_p3, mod9, sub_layer, pre_w, post_w, wg, wu, wd, seq_len):
    n_tok = xp.shape[0]
    n_seq = xs.shape[0]
    n_tiles = n_tok // FFN_TILE
    steps_per_seq = seq_len // FFN_TILE
    tile = lambda s: jnp.clip(s - FFN_WSTEPS, 0, n_tiles - 1)
    wblk = lambda s: jnp.minimum(s, FFN_WSTEPS - 1)
    tok_spec = pl.BlockSpec((FFN_TILE, D_MODEL), lambda s: (tile(s), 0))
    return pl.pallas_call(
        _ffn_kernel,
        grid=(FFN_WSTEPS + n_tiles + 1,),
        in_specs=[
            tok_spec,
            pl.BlockSpec((1, 3, D_MODEL), lambda s: (tile(s) // steps_per_seq, 0, 0)),
            _const_spec((n_seq, D_MODEL)),
            pl.BlockSpec((3, n_seq, D_MODEL), lambda s: (sub_layer, 0, 0), pipeline_mode=pl.Buffered(1)),
            _const_spec((1, D_MODEL)),
            _const_spec((1, D_MODEL)),
            pl.BlockSpec((D_MODEL, FFN_WBLOCK), lambda s: (0, wblk(s))),
            pl.BlockSpec((D_MODEL, FFN_WBLOCK), lambda s: (0, wblk(s))),
            pl.BlockSpec((FFN_WBLOCK, D_MODEL), lambda s: (wblk(s), 0)),
        ],
        out_specs=[tok_spec, pl.BlockSpec((n_seq, D_MODEL), lambda s: (0, 0))],
        out_shape=[jax.ShapeDtypeStruct((n_tok, D_MODEL), F32),
                   jax.ShapeDtypeStruct((n_seq, D_MODEL), F32)],
        scratch_shapes=[pltpu.VMEM((D_MODEL, D_FF), BF16), pltpu.VMEM((D_MODEL, D_FF), BF16),
                        pltpu.VMEM((D_FF, D_MODEL), BF16)],
        compiler_params=pltpu.CompilerParams(
            dimension_semantics=("arbitrary",), vmem_limit_bytes=VMEM_LIMIT),
        name="ffn",
    )(xp, mod_p3, xs, mod9, pre_w, post_w, wg, wu, wd)


def _lower_bound(lb_logits):
    z = lb_logits - jnp.max(lb_logits, axis=0, keepdims=True)
    e = jnp.exp(z)
    return e[0:1] / jnp.sum(e, axis=0, keepdims=True)


def _lru_gates(u_conv, wa_ref, ba, wx_ref, bx, lam):
    ub = u_conv.astype(BF16)
    r = _sigmoid(_dot(ub, wa_ref[...]) + ba)
    ig = _sigmoid(_dot(ub, wx_ref[...]) + bx)
    log_a = (-LRU_C * r) * _softplus(-lam)
    a = jnp.exp(log_a)
    th = jnp.tanh(log_a)
    mult = jnp.sqrt((-2.0 * th) / (1.0 - th))
    return a, mult, ig


def _head_rms_gate(o, norm_w, g):
    return _rms(o, norm_w) * _silu(g)


def _hgrn_chunk_head(scores, q_in, k_out, dec, vb_blk, g_blk, hgw, st_ref, hd, ohg_ref, r0, l0):
    st = st_ref[hd]
    o = _dot_nt(q_in, st.astype(BF16)) + _dot(scores.astype(BF16), vb_blk)
    st_ref[hd] = st * dec + _dot_tn(vb_blk, k_out)
    ohg_ref[r0:r0 + HG_CHUNK, l0:l0 + HG_HEAD_DIM] = _head_rms_gate(o, hgw, g_blk)


def _hgrn_chunks_direct(qs, kk, pfx, tails, vb, g, hgw, st_ref, ohg_ref):
    inv = 1.0 / pfx
    q_in = (qs * pfx).astype(BF16)
    k_in = (kk * inv).astype(BF16)
    ti = lax.broadcasted_iota(jnp.int32, (HG_CHUNK, HG_CHUNK), 0)
    si = lax.broadcasted_iota(jnp.int32, (HG_CHUNK, HG_CHUNK), 1)
    causal = ti >= si
    for c, tail in enumerate(tails):
        r0 = c * HG_CHUNK
        k_out = ((kk[r0:r0 + HG_CHUNK] * inv[r0:r0 + HG_CHUNK]) * tail).astype(BF16)
        for hd in range(HG_HEADS):
            l0 = hd * HG_HEAD_DIM
            blk = lambda z: z[r0:r0 + HG_CHUNK, l0:l0 + HG_HEAD_DIM]
            scores = jnp.where(causal, _dot_nt(blk(q_in), blk(k_in)), 0.0)
            _hgrn_chunk_head(scores, blk(q_in), k_out[:, l0:l0 + HG_HEAD_DIM],
                             tail[:, l0:l0 + HG_HEAD_DIM], blk(vb), blk(g), hgw,
                             st_ref, hd, ohg_ref, r0, l0)


def _hgrn_chunks_levels(f3, kk3, qs3, vb, g, hgw, st_ref, ohg_ref):
    n_grp = f3.shape[0]
    tb = n_grp * V7X_SUBLANES
    sub = lax.broadcasted_iota(jnp.int32, f3.shape, 1)
    grp = lax.broadcasted_iota(jnp.int32, f3.shape, 0)
    to_mxu = lambda z: z.reshape(tb, z.shape[-1]).astype(BF16)
    q_lv = [to_mxu(qs3)]
    k_lv = [to_mxu(kk3)]
    tot, pre, post = f3, f3, None
    for lvl in HG_LEVELS:
        q_lv.append(to_mxu(qs3 * pre))
        k_lv.append(k_lv[0] if post is None else to_mxu(kk3 * post))
        if lvl < V7X_SUBLANES:
            upper = (sub & lvl) != 0
            below = pltpu.roll(tot, lvl, 1)
            above = pltpu.roll(tot, V7X_SUBLANES - lvl, 1)
        else:
            upper = (grp & (lvl // V7X_SUBLANES)) != 0
            below = jnp.roll(tot, lvl // V7X_SUBLANES, axis=0)
            above = jnp.roll(tot, -(lvl // V7X_SUBLANES), axis=0)
        pre = jnp.where(upper, pre * below, pre)
        post = jnp.where(upper, 1.0, above) if post is None else jnp.where(upper, post, post * above)
        tot = tot * jnp.where(upper, below, above)
    q_in = to_mxu(qs3 * pre)
    k_out = to_mxu(kk3 * post)
    s_decay = tot.reshape(tb, tot.shape[-1])

    ti = lax.broadcasted_iota(jnp.int32, (HG_CHUNK, HG_CHUNK), 0)
    si = lax.broadcasted_iota(jnp.int32, (HG_CHUNK, HG_CHUNK), 1)
    masks = [ti == si]
    tx = ti ^ si
    for lvl in HG_LEVELS:
        masks.append((tx >= lvl) & (tx < 2 * lvl) & ((ti & lvl) != 0))

    for c in range(tb // HG_CHUNK):
        r0 = c * HG_CHUNK
        for hd in range(HG_HEADS):
            l0 = hd * HG_HEAD_DIM
            blk = lambda z: z[r0:r0 + HG_CHUNK, l0:l0 + HG_HEAD_DIM]
            scores = jnp.zeros((HG_CHUNK, HG_CHUNK), F32)
            for ql, kl, msk in zip(q_lv, k_lv, masks):
                scores = jnp.where(msk, _dot_nt(blk(ql), blk(kl)), scores)
            dec = s_decay[r0 + HG_CHUNK - 1:r0 + HG_CHUNK, l0:l0 + HG_HEAD_DIM]
            _hgrn_chunk_head(scores, blk(q_in), blk(k_out), dec, blk(vb), blk(g), hgw,
                             st_ref, hd, ohg_ref, r0, l0)

def _mix_tile(x_ref, m_ref, o_ref, t, pre_ref, post_ref, win_ref, cw_ref, cb_ref, wa_ref, ba_ref,
              wx_ref, bx_ref, lam_ref, lbl_ref, hgw_ref, wout_ref, ucar_ref, hcar_ref, st_ref, ohg_ref):
    tb = MIX_TILE
    x = x_ref[...]
    m = m_ref[...]
    hb = (_rms(x, pre_ref[...]) * (1.0 + m[1:2]) + m[0:1]).astype(BF16)
    yield "pre"
    segs = []
    for c0 in range(0, D_PROJ, LRU_WIDTH):
        segs.append(_dot(hb, win_ref[:, c0:c0 + LRU_WIDTH]))
        yield "in"
    u, y_lru, q, f_raw, v, g = segs

    n_grp = tb // V7X_SUBLANES
    grouped = lambda z: z.reshape(n_grp, V7X_SUBLANES, z.shape[-1])
    flat = lambda z: z.reshape(tb, z.shape[-1])
    gshape = (n_grp, V7X_SUBLANES, LRU_WIDTH)
    sub = lax.broadcasted_iota(jnp.int32, gshape, 1)
    grp = lax.broadcasted_iota(jnp.int32, gshape, 0)

    u3 = grouped(u)
    u_all = jnp.concatenate([ucar_ref[...][None], u3], axis=0)
    ucar_ref[...] = u3[n_grp - 1]
    cw = cw_ref[...]
    u_conv = cb_ref[...]
    for k in range(CONV_WIDTH - 1):
        back = CONV_WIDTH - 1 - k
        rot = pltpu.roll(u_all, back, 1)
        u_conv = u_conv + jnp.where(sub >= back, rot[1:], rot[:-1]) * cw[k:k + 1]
    u_conv = flat(u_conv + u3 * cw[CONV_WIDTH - 1:CONV_WIDTH])
    yield "vec"

    a, mult, ig = _lru_gates(u_conv, wa_ref, ba_ref[...], wx_ref, bx_ref[...], lam_ref[...])
    yield "vec"
    a3, mult3 = grouped(a), grouped(mult)
    first_row = jnp.where(t == 0, 0, -1)
    mult3 = jnp.where(grp * V7X_SUBLANES + sub == first_row, 1.0, mult3)
    b3 = (mult3 * grouped(ig)) * grouped(u_conv)
    s = 1
    while s < V7X_SUBLANES:
        keep = sub >= s
        b3 = jnp.where(keep, a3 * pltpu.roll(b3, s, 1) + b3, b3)
        a3 = jnp.where(keep, a3 * pltpu.roll(a3, s, 1), a3)
        s *= 2
    yield "vec"
    carry = hcar_ref[...]
    hs = []
    for gi in range(n_grp):
        h_g = a3[gi] * carry + b3[gi]
        carry = h_g[V7X_SUBLANES - 1:V7X_SUBLANES]
        hs.append(h_g)
    hcar_ref[...] = carry
    o_lru = jnp.concatenate(hs, axis=0) * _gelu_tanh(y_lru)
    yield "vec"

    lb = _lower_bound(lbl_ref[...])
    f3 = grouped(lb + (1.0 - lb) * _sigmoid(f_raw))
    kk3 = 1.0 - f3
    qs3 = grouped(q * (HG_HEAD_DIM ** -0.5))
    vb = v.astype(BF16)
    hgw = hgw_ref[...]

    pfx = f3
    s = 1
    while s < V7X_SUBLANES:
        pfx = jnp.where(sub >= s, pfx * pltpu.roll(pfx, s, 1), pfx)
        s *= 2
    grp_per_chunk = HG_CHUNK // V7X_SUBLANES
    rows, tails = [], []
    for gi in range(n_grp):
        p_g = pfx[gi] if gi % grp_per_chunk == 0 else pfx[gi] * carry
        carry = p_g[V7X_SUBLANES - 1:V7X_SUBLANES]
        rows.append(p_g)
        if gi % grp_per_chunk == grp_per_chunk - 1:
            tails.append(carry)
    pfx = jnp.concatenate(rows, axis=0)
    direct = jnp.min(jnp.concatenate(tails, axis=0)) >= HG_DIRECT_MIN
    yield "vec"

    @pl.when(direct)
    def _():
        _hgrn_chunks_direct(flat(qs3), flat(kk3), pfx, tails, vb, g, hgw, st_ref, ohg_ref)

    @pl.when(jnp.logical_not(direct))
    def _():
        _hgrn_chunks_levels(f3, kk3, qs3, vb, g, hgw, st_ref, ohg_ref)
    yield "chunk"

    mix_in = jnp.concatenate([o_lru, ohg_ref[...]], axis=-1).astype(BF16)
    mix = _dot(mix_in, wout_ref[...])
    o_ref[...] = x + m[2:3] * _rms(mix, post_ref[...])


def _mix_prompt_kernel(x_ref, m_ref, *refs):
    weights = refs[:13]
    o_ref, h_out_ref, conv_out_ref, s_out_ref, ucar_ref, hcar_ref, st_ref, ohg_ref = refs[13:]
    t = pl.program_id(1)
    n_t = pl.num_programs(1)

    @pl.when(t == 0)
    def _():
        ucar_ref[...] = jnp.zeros_like(ucar_ref)
        hcar_ref[...] = jnp.zeros_like(hcar_ref)
        st_ref[...] = jnp.zeros_like(st_ref)

    tiles = [_mix_tile(x_ref.at[sq], m_ref.at[sq], o_ref.at[sq], t, *weights,
                       ucar_ref.at[sq], hcar_ref.at[sq], st_ref.at[sq], ohg_ref.at[sq])
             for sq in range(MIX_SEQS)]
    waiting, running, tick = tiles, [], 0
    while waiting or running:
        if waiting and tick % MIX_STAGE_LAG == 0:
            running.append(waiting.pop(0))
        running = [tile for tile in running if next(tile, tile) is not tile]
        tick += 1

    @pl.when(t == n_t - 1)
    def _():
        for sq in range(MIX_SEQS):
            h_out_ref[sq] = hcar_ref[sq]
            conv_out_ref[sq] = ucar_ref[sq, V7X_SUBLANES - (CONV_WIDTH - 1):V7X_SUBLANES, :]
            for hd in range(HG_HEADS):
                s_out_ref[sq, hd] = st_ref[sq, hd].T


def _mix_prompt(x, mod3, p):
    batch, seq_len, _ = x.shape
    n_t = seq_len // MIX_TILE
    nsq = MIX_SEQS
    tok_spec = pl.BlockSpec((nsq, MIX_TILE, D_MODEL), lambda b, t: (b, t, 0))
    return pl.pallas_call(
        _mix_prompt_kernel,
        grid=(batch // nsq, n_t),
        in_specs=[
            tok_spec,
            pl.BlockSpec((nsq, 3, D_MODEL), lambda b, t: (b, 0, 0)),
            _const_spec((1, D_MODEL)),
            _const_spec((1, D_MODEL)),
            _const_spec((D_MODEL, D_PROJ)),
            _const_spec((CONV_WIDTH, LRU_WIDTH)),
            _const_spec((1, LRU_WIDTH)),
            _const_spec((LRU_WIDTH, LRU_WIDTH)),
            _const_spec((1, LRU_WIDTH)),
            _const_spec((LRU_WIDTH, LRU_WIDTH)),
            _const_spec((1, LRU_WIDTH)),
            _const_spec((1, LRU_WIDTH)),
            _const_spec(p["lb_logits"].shape),
            _const_spec((1, HG_HEAD_DIM)),
            _const_spec((D_MODEL, D_MODEL)),
        ],
        out_specs=[
            tok_spec,
            pl.BlockSpec((nsq, 1, LRU_WIDTH), lambda b, t: (b, 0, 0)),
            pl.BlockSpec((nsq, CONV_WIDTH - 1, LRU_WIDTH), lambda b, t: (b, 0, 0)),
            pl.BlockSpec((nsq, HG_HEADS, HG_HEAD_DIM, HG_HEAD_DIM), lambda b, t: (b, 0, 0, 0)),
        ],
        out_shape=[
            jax.ShapeDtypeStruct((batch, seq_len, D_MODEL), F32),
            jax.ShapeDtypeStruct((batch, 1, LRU_WIDTH), F32),
            jax.ShapeDtypeStruct((batch, CONV_WIDTH - 1, LRU_WIDTH), F32),
            jax.ShapeDtypeStruct((batch, HG_HEADS, HG_HEAD_DIM, HG_HEAD_DIM), F32),
        ],
        scratch_shapes=[
            pltpu.VMEM((nsq, V7X_SUBLANES, LRU_WIDTH), F32),
            pltpu.VMEM((nsq, 1, LRU_WIDTH), F32),
            pltpu.VMEM((nsq, HG_HEADS, HG_HEAD_DIM, HG_HEAD_DIM), F32),
            pltpu.VMEM((nsq, MIX_TILE, HG_WIDTH), F32),
        ],
        compiler_params=pltpu.CompilerParams(
            dimension_semantics=("arbitrary", "arbitrary"), vmem_limit_bytes=VMEM_LIMIT),
        name="mix_prompt",
    )(x, mod3, p["ln_mix_pre"], p["ln_mix_post"], p["w_in"], p["conv_w"], p["conv_b"],
      p["wa_bd"], p["b_a"], p["wx_bd"], p["b_x"], p["lam"], p["lb_logits"], p["hg_norm_w"],
      p["w_out"])


def _mix_sample_kernel(x_ref, m_ref, pre_ref, post_ref, win_ref, cw_ref, cb_ref,
                       wa_ref, ba_ref, wx_ref, bx_ref, lam_ref, lbl_ref, hgw_ref, wout_ref,
                       h0_ref, conv0_ref, s0_ref,
                       o_ref, h_out_ref, conv_out_ref, s_out_ref,
                       qf_ref, f_ref, k_ref, v_ref, g_ref, qk_ref, olru_ref, ohg_ref, ostage_ref):
    i = pl.program_id(0)
    n_i = pl.num_programs(0)
    nb = SAMPLE_SEQ_BLOCK

    @pl.when(i == 0)
    def _():
        x = x_ref[...]
        h = _rms(x, pre_ref[...]) * (1.0 + m_ref[1]) + m_ref[0]
        proj = _dot(h.astype(BF16), win_ref[...])
        u = proj[:, 0:LRU_WIDTH]
        y_lru = proj[:, LRU_WIDTH:2 * LRU_WIDTH]
        o0 = 2 * LRU_WIDTH
        q = proj[:, o0:o0 + HG_WIDTH]
        f_raw = proj[:, o0 + HG_WIDTH:o0 + 2 * HG_WIDTH]
        v = proj[:, o0 + 2 * HG_WIDTH:o0 + 3 * HG_WIDTH]
        g_ref[...] = proj[:, o0 + 3 * HG_WIDTH:o0 + 4 * HG_WIDTH]

        cw = cw_ref[...]
        u_conv = cb_ref[...]
        for k in range(CONV_WIDTH - 1):
            u_conv = u_conv + conv0_ref[k] * cw[k:k + 1]
            if k > 0:
                conv_out_ref[k - 1] = conv0_ref[k]
        u_conv = u_conv + u * cw[CONV_WIDTH - 1:CONV_WIDTH]
        conv_out_ref[CONV_WIDTH - 2] = u

        a, mult, ig = _lru_gates(u_conv, wa_ref, ba_ref[...], wx_ref, bx_ref[...], lam_ref[...])
        hs = (mult * ig) * u_conv + a * h0_ref[...]
        h_out_ref[...] = hs
        olru_ref[...] = hs * _gelu_tanh(y_lru)

        lb = _lower_bound(lbl_ref[...])
        f = lb + (1.0 - lb) * _sigmoid(f_raw)
        kk = 1.0 - f
        qs = q * (HG_HEAD_DIM ** -0.5)
        f_ref[...] = f
        qf_ref[...] = qs * f
        k_ref[...] = kk
        v_ref[...] = v
        qk = qs * kk
        for hd in range(HG_HEADS):
            l0 = hd * HG_HEAD_DIM
            tot = jnp.sum(qk[:, l0:l0 + HG_HEAD_DIM], axis=-1, keepdims=True)
            qk_ref[:, l0:l0 + HG_HEAD_DIM] = jnp.broadcast_to(tot, (qk.shape[0], HG_HEAD_DIM))

    base = pl.multiple_of(i * nb, nb)
    grp = lambda ref: ref[pl.ds(base, nb), :]
    f_g, k_g, v_g, qf_g, qk_g = grp(f_ref), grp(k_ref), grp(v_ref), grp(qf_ref), grp(qk_ref)
    square = (HG_HEAD_DIM, HG_HEAD_DIM)
    for j in range(nb):
        for hd in range(HG_HEADS):
            l0 = hd * HG_HEAD_DIM
            rowv = lambda z: z[j:j + 1, l0:l0 + HG_HEAD_DIM]
            s_old = s0_ref[j, hd]
            f_col = jnp.broadcast_to(rowv(f_g), square).T
            k_col = jnp.broadcast_to(rowv(k_g), square).T
            v_row = rowv(v_g)
            outer = k_col.astype(BF16).astype(F32) * v_row.astype(BF16).astype(F32)
            s_out_ref[j, hd] = f_col * s_old + outer
            qf8 = jnp.broadcast_to(rowv(qf_g), (V7X_SUBLANES, HG_HEAD_DIM)).astype(BF16)
            o1 = _dot(qf8, s_old.astype(BF16))[0:1]
            ostage_ref[j:j + 1, l0:l0 + HG_HEAD_DIM] = o1 + rowv(qk_g) * v_row
    ohg_ref[pl.ds(base, nb), :] = ostage_ref[...]

    @pl.when(i == n_i - 1)
    def _():
        hgw = hgw_ref[...]
        parts = [olru_ref[...]]
        for hd in range(HG_HEADS):
            l0 = hd * HG_HEAD_DIM
            parts.append(_head_rms_gate(ohg_ref[:, l0:l0 + HG_HEAD_DIM], hgw,
                                        g_ref[:, l0:l0 + HG_HEAD_DIM]))
        mix_in = jnp.concatenate(parts, axis=-1).astype(BF16)
        mix = _dot(mix_in, wout_ref[...])
        o_ref[...] = x_ref[...] + m_ref[2] * _rms(mix, post_ref[...])


def _mix_sample(x, mod9, p, h0, conv0, s0):
    n_seq = x.shape[0]
    nb = SAMPLE_SEQ_BLOCK
    state_spec = pl.BlockSpec((nb, HG_HEADS, HG_HEAD_DIM, HG_HEAD_DIM), lambda i: (i, 0, 0, 0))
    full2 = lambda shape: pl.BlockSpec(shape, lambda i: (0,) * len(shape))
    return pl.pallas_call(
        _mix_sample_kernel,
        grid=(n_seq // nb,),
        in_specs=[
            _const_spec((n_seq, D_MODEL)),
            pl.BlockSpec((3, n_seq, D_MODEL), lambda i: (1, 0, 0), pipeline_mode=pl.Buffered(1)),
            _const_spec((1, D_MODEL)),
            _const_spec((1, D_MODEL)),
            _const_spec((D_MODEL, D_PROJ)),
            _const_spec((CONV_WIDTH, LRU_WIDTH)),
            _const_spec((1, LRU_WIDTH)),
            _const_spec((LRU_WIDTH, LRU_WIDTH)),
            _const_spec((1, LRU_WIDTH)),
            _const_spec((LRU_WIDTH, LRU_WIDTH)),
            _const_spec((1, LRU_WIDTH)),
            _const_spec((1, LRU_WIDTH)),
            _const_spec(p["lb_logits"].shape),
            _const_spec((1, HG_HEAD_DIM)),
            _const_spec((D_MODEL, D_MODEL)),
            _const_spec((n_seq, LRU_WIDTH)),
            _const_spec((CONV_WIDTH - 1, n_seq, LRU_WIDTH)),
            state_spec,
        ],
        out_specs=[
            full2((n_seq, D_MODEL)),
            full2((n_seq, LRU_WIDTH)),
            full2((CONV_WIDTH - 1, n_seq, LRU_WIDTH)),
            state_spec,
        ],
        out_shape=[
            jax.ShapeDtypeStruct((n_seq, D_MODEL), F32),
            jax.ShapeDtypeStruct((n_seq, LRU_WIDTH), F32),
            jax.ShapeDtypeStruct((CONV_WIDTH - 1, n_seq, LRU_WIDTH), F32),
            jax.ShapeDtypeStruct((n_seq, HG_HEADS, HG_HEAD_DIM, HG_HEAD_DIM), F32),
        ],
        scratch_shapes=[pltpu.VMEM((n_seq, HG_WIDTH), F32) for _ in range(8)]
        + [pltpu.VMEM((nb, HG_WIDTH), F32)],
        compiler_params=pltpu.CompilerParams(
            dimension_semantics=("arbitrary",), vmem_limit_bytes=VMEM_LIMIT),
        name="mix_sample",
    )(x, mod9, p["ln_mix_pre"], p["ln_mix_post"], p["w_in"], p["conv_w"], p["conv_b"],
      p["wa_bd"], p["b_a"], p["wx_bd"], p["b_x"], p["lam"], p["lb_logits"], p["hg_norm_w"],
      p["w_out"], h0, conv0, s0)


def _block_diag(w):
    heads, blk, _ = w.shape
    eye = jnp.eye(heads, dtype=w.dtype)
    return (eye[:, None, :, None] * w[:, :, None, :]).reshape(heads * blk, heads * blk)


def kernel(x_prompt, x_sample, c_prompt, c_sample, state_lru_h, state_lru_conv, state_hgrn_S, w_ada, b_ada, ln_ffn1_pre, ln_ffn1_post, ffn1_w_gate, ffn1_w_up, ffn1_w_down, ln_mix_pre, ln_mix_post, w_in, lru_conv_w, lru_conv_b, lru_w_a, lru_b_a, lru_w_x, lru_b_x, lru_lambda, hg_lb_logits, hg_norm_w, w_out, ln_ffn2_pre, ln_ffn2_post, ffn2_w_gate, ffn2_w_up, ffn2_w_down):
    depth = w_ada.shape[0]
    batch, seq_len, _ = x_prompt.shape
    n_seq = x_sample.shape[0]
    assert depth == 1 and x_sample.shape[1] == 1
    assert seq_len % FFN_TILE == 0 and seq_len % MIX_TILE == 0 and n_seq % SAMPLE_SEQ_BLOCK == 0
    assert batch % MIX_SEQS == 0

    xp = x_prompt.reshape(batch * seq_len, D_MODEL)
    xs = x_sample.reshape(n_seq, D_MODEL)
    ph, pc, pS, sh, sc, sS = [], [], [], [], [], []
    for l in range(depth):
        row = lambda w: w[l].reshape(1, -1)
        mod9 = _ada(jnp.concatenate([c_sample, c_prompt], axis=0), w_ada[l], b_ada[l])
        mod_p = jnp.transpose(mod9[:, n_seq:], (1, 0, 2))
        p = {
            "ln_mix_pre": row(ln_mix_pre), "ln_mix_post": row(ln_mix_post),
            "w_in": w_in[l].astype(BF16), "w_out": w_out[l].astype(BF16),
            "conv_w": lru_conv_w[l], "conv_b": row(lru_conv_b),
            "wa_bd": _block_diag(lru_w_a[l]).astype(BF16), "b_a": row(lru_b_a),
            "wx_bd": _block_diag(lru_w_x[l]).astype(BF16), "b_x": row(lru_b_x),
            "lam": row(lru_lambda), "lb_logits": hg_lb_logits,
            "hg_norm_w": row(hg_norm_w),
        }
        f1 = (row(ln_ffn1_pre), row(ln_ffn1_post), ffn1_w_gate[l], ffn1_w_up[l], ffn1_w_down[l])
        f2 = (row(ln_ffn2_pre), row(ln_ffn2_post), ffn2_w_gate[l], ffn2_w_up[l], ffn2_w_down[l])

        xp, xs = _ffn(xp, xs, mod_p[:, 0:3], mod9, 0, *f1, seq_len)
        xp, h_p, c_p, S_p = _mix_prompt(xp.reshape(batch, seq_len, D_MODEL), mod_p[:, 3:6], p)
        xp = xp.reshape(batch * seq_len, D_MODEL)
        conv0 = jnp.transpose(state_lru_conv[l], (1, 0, 2))
        xs, h_s, c_s, S_s = _mix_sample(xs, mod9, p, state_lru_h[l], conv0, state_hgrn_S[l])
        xp, xs = _ffn(xp, xs, mod_p[:, 6:9], mod9, 2, *f2, seq_len)

        ph.append(h_p.reshape(batch, LRU_WIDTH)); pc.append(c_p); pS.append(S_p)
        sh.append(h_s); sc.append(jnp.transpose(c_s, (1, 0, 2))); sS.append(S_s)

    return (xp.reshape(batch, seq_len, D_MODEL), xs.reshape(n_seq, 1, D_MODEL),
            jnp.stack(ph), jnp.stack(pc), jnp.stack(pS), jnp.stack(sh), jnp.stack(sc), jnp.stack(sS))
```

```python
import jax
import jax.numpy as jnp
from jax import lax
from jax.experimental import pallas as pl
from jax.experimental.pallas import tpu as pltpu

F32 = jnp.float32
BF16 = jnp.bfloat16

D_MODEL = 1024
D_FF = 2816
LRU_WIDTH = 512
CONV_WIDTH = 4
LRU_C = 8.0
HG_WIDTH = 512
HG_HEAD_DIM = 128
HG_HEADS = HG_WIDTH // HG_HEAD_DIM
HG_CHUNK = 64
N_MOD = 9
D_PROJ = 2 * LRU_WIDTH + 4 * HG_WIDTH
EPS = 1e-6

V7X_SUBLANES = 8
V7X_VMEM_BYTES = 64 * 1024 * 1024
VMEM_LIMIT = V7X_VMEM_BYTES - 8 * 1024 * 1024

FFN_TILE = 512
MIX_TILE = 256
MIX_SEQS = 2
MIX_STAGE_LAG = 7
FFN_WBLOCK = 256
FFN_WSTEPS = D_FF // FFN_WBLOCK
SAMPLE_SEQ_BLOCK = V7X_SUBLANES

HG_LEVELS = (1, 2, 4, 8, 16, 32)
HG_DIRECT_MIN = 2.0 ** -100


def _rms(x, w):
    return (x * lax.rsqrt(jnp.mean(x * x, axis=-1, keepdims=True) + EPS)) * w


def _sigmoid(x):
    return 1.0 / (1.0 + jnp.exp(-x))


def _silu(x):
    return x * _sigmoid(x)


def _gelu_tanh(x):
    c = 0.7978845608028654
    return x * (0.5 * (1.0 + jnp.tanh(c * (x + 0.044715 * (x * x * x)))))


def _softplus(z):
    return jnp.maximum(z, 0.0) + jnp.log1p(jnp.exp(-jnp.abs(z)))


def _dot(a, b):
    return jnp.dot(a, b, preferred_element_type=F32)


def _dot_nt(a, b):
    return lax.dot_general(a, b, (((1,), (1,)), ((), ())), preferred_element_type=F32)


def _dot_tn(a, b):
    return lax.dot_general(a, b, (((0,), (0,)), ((), ())), preferred_element_type=F32)


def _const_spec(shape):
    nd = len(shape)
    return pl.BlockSpec(shape, lambda *_: (0,) * nd, pipeline_mode=pl.Buffered(1))


def _ada_kernel(c_ref, w_ref, b_ref, o_ref):
    s = _silu(c_ref[...])
    o_ref[0] = _dot(s.astype(BF16), w_ref[...].astype(BF16)) + b_ref[...]


def _ada(c_all, w_ada, b_ada):
    rows = c_all.shape[0]
    return pl.pallas_call(
        _ada_kernel,
        grid=(N_MOD,),
        in_specs=[
            pl.BlockSpec((rows, D_MODEL), lambda j: (0, 0)),
            pl.BlockSpec((D_MODEL, D_MODEL), lambda j: (0, j)),
            pl.BlockSpec((1, D_MODEL), lambda j: (0, j)),
        ],
        out_specs=pl.BlockSpec((1, rows, D_MODEL), lambda j: (j, 0, 0)),
        out_shape=jax.ShapeDtypeStruct((N_MOD, rows, D_MODEL), F32),
        compiler_params=pltpu.CompilerParams(
            dimension_semantics=("arbitrary",), vmem_limit_bytes=VMEM_LIMIT),
        name="ada_mod",
    )(c_all, w_ada, b_ada.reshape(1, N_MOD * D_MODEL))


def _ffn_body(x, shift, scale, gate, pre_w, post_w, wg_ref, wu_ref, wd_ref):
    h = _rms(x, pre_w) * (1.0 + scale) + shift
    hb = h.astype(BF16)
    a = _dot(hb, wg_ref[...])
    u = _dot(hb, wu_ref[...])
    act = (_silu(a) * u).astype(BF16)
    y = _dot(act, wd_ref[...])
    return x + (0.5 * gate) * _rms(y, post_w)


def _ffn_kernel(xp_ref, mp_ref, xs_ref, ms_ref, pre_ref, post_ref, wg32_ref, wu32_ref, wd32_ref,
                op_ref, os_ref, wg_ref, wu_ref, wd_ref):
    s = pl.program_id(0)
    n_steps = pl.num_programs(0)

    for j in range(FFN_WSTEPS):
        @pl.when(s == j)
        def _(j=j):
            c0 = j * FFN_WBLOCK
            wg_ref[:, c0:c0 + FFN_WBLOCK] = wg32_ref[...].astype(BF16)
            wu_ref[:, c0:c0 + FFN_WBLOCK] = wu32_ref[...].astype(BF16)
            wd_ref[c0:c0 + FFN_WBLOCK, :] = wd32_ref[...].astype(BF16)

    @pl.when((s >= FFN_WSTEPS) & (s < n_steps - 1))
    def _():
        m = mp_ref[0]
        half = FFN_TILE // 2
        for r0 in (0, half):
            op_ref[r0:r0 + half, :] = _ffn_body(xp_ref[r0:r0 + half, :], m[0:1], m[1:2], m[2:3],
                                                pre_ref[...], post_ref[...], wg_ref, wu_ref, wd_ref)

    @pl.when(s == n_steps - 1)
    def _():
        os_ref[...] = _ffn_body(xs_ref[...], ms_ref[0], ms_ref[1], ms_ref[2], pre_ref[...],
                                post_ref[...], wg_ref, wu_ref, wd_ref)


def _ffn(xp, xs, mod_p3, mod9, sub_layer, pre_w, post_w, wg, wu, wd, seq_len):
    n_tok = xp.shape[0]
    n_seq = xs.shape[0]
    n_tiles = n_tok // FFN_TILE
    steps_per_seq = seq_len // FFN_TILE
    tile = lambda s: jnp.clip(s - FFN_WSTEPS, 0, n_tiles - 1)
    wblk = lambda s: jnp.minimum(s, FFN_WSTEPS - 1)
    tok_spec = pl.BlockSpec((FFN_TILE, D_MODEL), lambda s: (tile(s), 0))
    return pl.pallas_call(
        _ffn_kernel,
        grid=(FFN_WSTEPS + n_tiles + 1,),
        in_specs=[
            tok_spec,
            pl.BlockSpec((1, 3, D_MODEL), lambda s: (tile(s) // steps_per_seq, 0, 0)),
            _const_spec((n_seq, D_MODEL)),
            pl.BlockSpec((3, n_seq, D_MODEL), lambda s: (sub_layer, 0, 0), pipeline_mode=pl.Buffered(1)),
            _const_spec((1, D_MODEL)),
            _const_spec((1, D_MODEL)),
            pl.BlockSpec((D_MODEL, FFN_WBLOCK), lambda s: (0, wblk(s))),
            pl.BlockSpec((D_MODEL, FFN_WBLOCK), lambda s: (0, wblk(s))),
            pl.BlockSpec((FFN_WBLOCK, D_MODEL), lambda s: (wblk(s), 0)),
        ],
        out_specs=[tok_spec, pl.BlockSpec((n_seq, D_MODEL), lambda s: (0, 0))],
        out_shape=[jax.ShapeDtypeStruct((n_tok, D_MODEL), F32),
                   jax.ShapeDtypeStruct((n_seq, D_MODEL), F32)],
        scratch_shapes=[pltpu.VMEM((D_MODEL, D_FF), BF16), pltpu.VMEM((D_MODEL, D_FF), BF16),
                        pltpu.VMEM((D_FF, D_MODEL), BF16)],
        compiler_params=pltpu.CompilerParams(
            dimension_semantics=("arbitrary",), vmem_limit_bytes=VMEM_LIMIT),
        name="ffn",
    )(xp, mod_p3, xs, mod9, pre_w, post_w, wg, wu, wd)


def _lower_bound(lb_logits):
    z = lb_logits - jnp.max(lb_logits, axis=0, keepdims=True)
    e = jnp.exp(z)
    return e[0:1] / jnp.sum(e, axis=0, keepdims=True)


def _lru_gates(u_conv, wa_ref, ba, wx_ref, bx, lam):
    ub = u_conv.astype(BF16)
    r = _sigmoid(_dot(ub, wa_ref[...]) + ba)
    ig = _sigmoid(_dot(ub, wx_ref[...]) + bx)
    log_a = (-LRU_C * r) * _softplus(-lam)
    a = jnp.exp(log_a)
    th = jnp.tanh(log_a)
    mult = jnp.sqrt((-2.0 * th) / (1.0 - th))
    return a, mult, ig


def _head_rms_gate(o, norm_w, g):
    return _rms(o, norm_w) * _silu(g)


def _hgrn_chunk_head(scores, q_in, k_out, dec, vb_blk, g_blk, hgw, st_ref, hd, ohg_ref, r0, l0):
    st = st_ref[hd]
    o = _dot_nt(q_in, st.astype(BF16)) + _dot(scores.astype(BF16), vb_blk)
    st_ref[hd] = st * dec + _dot_tn(vb_blk, k_out)
    ohg_ref[r0:r0 + HG_CHUNK, l0:l0 + HG_HEAD_DIM] = _head_rms_gate(o, hgw, g_blk)


def _hgrn_chunks_direct(qs, kk, pfx, tails, vb, g, hgw, st_ref, ohg_ref):
    tb = qs.shape[0]
    n_chunk = len(tails)
    inv = 1.0 / pfx
    q_dec = qs * pfx
    k_dec = kk * inv
    q_in = q_dec.astype(BF16)
    k_in = k_dec.astype(BF16)
    ti = lax.broadcasted_iota(jnp.int32, (tb, tb), 0)
    si = lax.broadcasted_iota(jnp.int32, (tb, tb), 1)
    visible = (ti >= si) & ((ti ^ si) < HG_CHUNK)
    row_chunk = jnp.right_shift(lax.broadcasted_iota(jnp.int32, (tb, HG_HEAD_DIM), 0),
                                HG_CHUNK.bit_length() - 1)
    tail_rows = jnp.concatenate([jnp.broadcast_to(t, (HG_CHUNK, t.shape[-1])) for t in tails], axis=0)
    k_end = k_dec * tail_rows
    for hd in range(HG_HEADS):
        l0 = hd * HG_HEAD_DIM
        head = lambda z: z[:, l0:l0 + HG_HEAD_DIM]
        by_chunk = lambda z: jnp.concatenate(
            [jnp.where(row_chunk == c, head(z), 0.0) for c in range(n_chunk)], axis=-1).astype(BF16)
        scores = jnp.where(visible, _dot_nt(head(q_in), head(k_in)), 0.0)
        grown = _dot_tn(head(vb), by_chunk(k_end))
        st = st_ref[hd]
        states = []
        for c, tail in enumerate(tails):
            states.append(st.astype(BF16))
            st = st * head(tail) + grown[:, c * HG_HEAD_DIM:(c + 1) * HG_HEAD_DIM]
        st_ref[hd] = st
        o = (_dot_nt(by_chunk(q_dec), jnp.concatenate(states, axis=-1))
             + _dot(scores.astype(BF16), head(vb)))
        ohg_ref[:, l0:l0 + HG_HEAD_DIM] = _head_rms_gate(o, hgw, head(g))


def _hgrn_chunks_levels(f3, kk3, qs3, vb, g, hgw, st_ref, ohg_ref):
    n_grp = f3.shape[0]
    tb = n_grp * V7X_SUBLANES
    sub = lax.broadcasted_iota(jnp.int32, f3.shape, 1)
    grp = lax.broadcasted_iota(jnp.int32, f3.shape, 0)
    to_mxu = lambda z: z.reshape(tb, z.shape[-1]).astype(BF16)
    q_lv = [to_mxu(qs3)]
    k_lv = [to_mxu(kk3)]
    tot, pre, post = f3, f3, None
    for lvl in HG_LEVELS:
        q_lv.append(to_mxu(qs3 * pre))
        k_lv.append(k_lv[0] if post is None else to_mxu(kk3 * post))
        if lvl < V7X_SUBLANES:
            upper = (sub & lvl) != 0
            below = pltpu.roll(tot, lvl, 1)
            above = pltpu.roll(tot, V7X_SUBLANES - lvl, 1)
        else:
            upper = (grp & (lvl // V7X_SUBLANES)) != 0
            below = jnp.roll(tot, lvl // V7X_SUBLANES, axis=0)
            above = jnp.roll(tot, -(lvl // V7X_SUBLANES), axis=0)
        pre = jnp.where(upper, pre * below, pre)
        post = jnp.where(upper, 1.0, above) if post is None else jnp.where(upper, post, post * above)
        tot = tot * jnp.where(upper, below, above)
    q_in = to_mxu(qs3 * pre)
    k_out = to_mxu(kk3 * post)
    s_decay = tot.reshape(tb, tot.shape[-1])

    ti = lax.broadcasted_iota(jnp.int32, (HG_CHUNK, HG_CHUNK), 0)
    si = lax.broadcasted_iota(jnp.int32, (HG_CHUNK, HG_CHUNK), 1)
    masks = [ti == si]
    tx = ti ^ si
    for lvl in HG_LEVELS:
        masks.append((tx >= lvl) & (tx < 2 * lvl) & ((ti & lvl) != 0))

    for c in range(tb // HG_CHUNK):
        r0 = c * HG_CHUNK
        for hd in range(HG_HEADS):
            l0 = hd * HG_HEAD_DIM
            blk = lambda z: z[r0:r0 + HG_CHUNK, l0:l0 + HG_HEAD_DIM]
            scores = jnp.zeros((HG_CHUNK, HG_CHUNK), F32)
            for ql, kl, msk in zip(q_lv, k_lv, masks):
                scores = jnp.where(msk, _dot_nt(blk(ql), blk(kl)), scores)
            dec = s_decay[r0 + HG_CHUNK - 1:r0 + HG_CHUNK, l0:l0 + HG_HEAD_DIM]
            _hgrn_chunk_head(scores, blk(q_in), blk(k_out), dec, blk(vb), blk(g), hgw,
                             st_ref, hd, ohg_ref, r0, l0)

def _mix_tile(x_ref, m_ref, o_ref, t, pre_ref, post_ref, win_ref, cw_ref, cb_ref, wa_ref, ba_ref,
              wx_ref, bx_ref, lam_ref, lbl_ref, hgw_ref, wout_ref, ucar_ref, hcar_ref, st_ref, ohg_ref):
    tb = MIX_TILE
    x = x_ref[...]
    m = m_ref[...]
    hb = (_rms(x, pre_ref[...]) * (1.0 + m[1:2]) + m[0:1]).astype(BF16)
    yield "pre"
    segs = []
    for c0 in range(0, D_PROJ, LRU_WIDTH):
        segs.append(_dot(hb, win_ref[:, c0:c0 + LRU_WIDTH]))
        yield "in"
    u, y_lru, q, f_raw, v, g = segs

    n_grp = tb // V7X_SUBLANES
    grouped = lambda z: z.reshape(n_grp, V7X_SUBLANES, z.shape[-1])
    flat = lambda z: z.reshape(tb, z.shape[-1])
    gshape = (n_grp, V7X_SUBLANES, LRU_WIDTH)
    sub = lax.broadcasted_iota(jnp.int32, gshape, 1)
    grp = lax.broadcasted_iota(jnp.int32, gshape, 0)

    u3 = grouped(u)
    u_all = jnp.concatenate([ucar_ref[...][None], u3], axis=0)
    ucar_ref[...] = u3[n_grp - 1]
    cw = cw_ref[...]
    u_conv = cb_ref[...]
    for k in range(CONV_WIDTH - 1):
        back = CONV_WIDTH - 1 - k
        rot = pltpu.roll(u_all, back, 1)
        u_conv = u_conv + jnp.where(sub >= back, rot[1:], rot[:-1]) * cw[k:k + 1]
    u_conv = flat(u_conv + u3 * cw[CONV_WIDTH - 1:CONV_WIDTH])
    yield "vec"

    a, mult, ig = _lru_gates(u_conv, wa_ref, ba_ref[...], wx_ref, bx_ref[...], lam_ref[...])
    yield "vec"
    a3, mult3 = grouped(a), grouped(mult)
    first_row = jnp.where(t == 0, 0, -1)
    mult3 = jnp.where(grp * V7X_SUBLANES + sub == first_row, 1.0, mult3)
    b3 = (mult3 * grouped(ig)) * grouped(u_conv)
    s = 1
    while s < V7X_SUBLANES:
        keep = sub >= s
        b3 = jnp.where(keep, a3 * pltpu.roll(b3, s, 1) + b3, b3)
        a3 = jnp.where(keep, a3 * pltpu.roll(a3, s, 1), a3)
        s *= 2
    yield "vec"
    carry = hcar_ref[...]
    hs = []
    for gi in range(n_grp):
        h_g = a3[gi] * carry + b3[gi]
        carry = h_g[V7X_SUBLANES - 1:V7X_SUBLANES]
        hs.append(h_g)
    hcar_ref[...] = carry
    o_lru = jnp.concatenate(hs, axis=0) * _gelu_tanh(y_lru)
    yield "vec"

    lb = _lower_bound(lbl_ref[...])
    f3 = grouped(lb + (1.0 - lb) * _sigmoid(f_raw))
    kk3 = 1.0 - f3
    qs3 = grouped(q * (HG_HEAD_DIM ** -0.5))
    vb = v.astype(BF16)
    hgw = hgw_ref[...]

    pfx = f3
    s = 1
    while s < V7X_SUBLANES:
        pfx = jnp.where(sub >= s, pfx * pltpu.roll(pfx, s, 1), pfx)
        s *= 2
    grp_per_chunk = HG_CHUNK // V7X_SUBLANES
    rows, tails = [], []
    for gi in range(n_grp):
        p_g = pfx[gi] if gi % grp_per_chunk == 0 else pfx[gi] * carry
        carry = p_g[V7X_SUBLANES - 1:V7X_SUBLANES]
        rows.append(p_g)
        if gi % grp_per_chunk == grp_per_chunk - 1:
            tails.append(carry)
    pfx = jnp.concatenate(rows, axis=0)
    direct = jnp.min(jnp.concatenate(tails, axis=0)) >= HG_DIRECT_MIN
    yield "vec"

    @pl.when(direct)
    def _():
        _hgrn_chunks_direct(flat(qs3), flat(kk3), pfx, tails, vb, g, hgw, st_ref, ohg_ref)

    @pl.when(jnp.logical_not(direct))
    def _():
        _hgrn_chunks_levels(f3, kk3, qs3, vb, g, hgw, st_ref, ohg_ref)
    yield "chunk"

    mix_in = jnp.concatenate([o_lru, ohg_ref[...]], axis=-1).astype(BF16)
    mix = _dot(mix_in, wout_ref[...])
    o_ref[...] = x + m[2:3] * _rms(mix, post_ref[...])


def _mix_prompt_kernel(x_ref, m_ref, *refs):
    weights = refs[:13]
    o_ref, h_out_ref, conv_out_ref, s_out_ref, ucar_ref, hcar_ref, st_ref, ohg_ref = refs[13:]
    t = pl.program_id(1)
    n_t = pl.num_programs(1)

    @pl.when(t == 0)
    def _():
        ucar_ref[...] = jnp.zeros_like(ucar_ref)
        hcar_ref[...] = jnp.zeros_like(hcar_ref)
        st_ref[...] = jnp.zeros_like(st_ref)

    tiles = [_mix_tile(x_ref.at[sq], m_ref.at[sq], o_ref.at[sq], t, *weights,
                       ucar_ref.at[sq], hcar_ref.at[sq], st_ref.at[sq], ohg_ref.at[sq])
             for sq in range(MIX_SEQS)]
    waiting, running, tick = tiles, [], 0
    while waiting or running:
        if waiting and tick % MIX_STAGE_LAG == 0:
            running.append(waiting.pop(0))
        running = [tile for tile in running if next(tile, tile) is not tile]
        tick += 1

    @pl.when(t == n_t - 1)
    def _():
        for sq in range(MIX_SEQS):
            h_out_ref[sq] = hcar_ref[sq]
            conv_out_ref[sq] = ucar_ref[sq, V7X_SUBLANES - (CONV_WIDTH - 1):V7X_SUBLANES, :]
            for hd in range(HG_HEADS):
                s_out_ref[sq, hd] = st_ref[sq, hd].T


def _mix_prompt(x, mod3, p):
    batch, seq_len, _ = x.shape
    n_t = seq_len // MIX_TILE
    nsq = MIX_SEQS
    tok_spec = pl.BlockSpec((nsq, MIX_TILE, D_MODEL), lambda b, t: (b, t, 0))
    return pl.pallas_call(
        _mix_prompt_kernel,
        grid=(batch // nsq, n_t),
        in_specs=[
            tok_spec,
            pl.BlockSpec((nsq, 3, D_MODEL), lambda b, t: (b, 0, 0)),
            _const_spec((1, D_MODEL)),
            _const_spec((1, D_MODEL)),
            _const_spec((D_MODEL, D_PROJ)),
            _const_spec((CONV_WIDTH, LRU_WIDTH)),
            _const_spec((1, LRU_WIDTH)),
            _const_spec((LRU_WIDTH, LRU_WIDTH)),
            _const_spec((1, LRU_WIDTH)),
            _const_spec((LRU_WIDTH, LRU_WIDTH)),
            _const_spec((1, LRU_WIDTH)),
            _const_spec((1, LRU_WIDTH)),
            _const_spec(p["lb_logits"].shape),
            _const_spec((1, HG_HEAD_DIM)),
            _const_spec((D_MODEL, D_MODEL)),
        ],
        out_specs=[
            tok_spec,
            pl.BlockSpec((nsq, 1, LRU_WIDTH), lambda b, t: (b, 0, 0)),
            pl.BlockSpec((nsq, CONV_WIDTH - 1, LRU_WIDTH), lambda b, t: (b, 0, 0)),
            pl.BlockSpec((nsq, HG_HEADS, HG_HEAD_DIM, HG_HEAD_DIM), lambda b, t: (b, 0, 0, 0)),
        ],
        out_shape=[
            jax.ShapeDtypeStruct((batch, seq_len, D_MODEL), F32),
            jax.ShapeDtypeStruct((batch, 1, LRU_WIDTH), F32),
            jax.ShapeDtypeStruct((batch, CONV_WIDTH - 1, LRU_WIDTH), F32),
            jax.ShapeDtypeStruct((batch, HG_HEADS, HG_HEAD_DIM, HG_HEAD_DIM), F32),
        ],
        scratch_shapes=[
            pltpu.VMEM((nsq, V7X_SUBLANES, LRU_WIDTH), F32),
            pltpu.VMEM((nsq, 1, LRU_WIDTH), F32),
            pltpu.VMEM((nsq, HG_HEADS, HG_HEAD_DIM, HG_HEAD_DIM), F32),
            pltpu.VMEM((nsq, MIX_TILE, HG_WIDTH), F32),
        ],
        compiler_params=pltpu.CompilerParams(
            dimension_semantics=("arbitrary", "arbitrary"), vmem_limit_bytes=VMEM_LIMIT),
        name="mix_prompt",
    )(x, mod3, p["ln_mix_pre"], p["ln_mix_post"], p["w_in"], p["conv_w"], p["conv_b"],
      p["wa_bd"], p["b_a"], p["wx_bd"], p["b_x"], p["lam"], p["lb_logits"], p["hg_norm_w"],
      p["w_out"])


def _mix_sample_kernel(x_ref, m_ref, pre_ref, post_ref, win_ref, cw_ref, cb_ref,
                       wa_ref, ba_ref, wx_ref, bx_ref, lam_ref, lbl_ref, hgw_ref, wout_ref,
                       h0_ref, conv0_ref, s0_ref,
                       o_ref, h_out_ref, conv_out_ref, s_out_ref,
                       qf_ref, f_ref, k_ref, v_ref, g_ref, qk_ref, olru_ref, ohg_ref, ostage_ref):
    i = pl.program_id(0)
    n_i = pl.num_programs(0)
    nb = SAMPLE_SEQ_BLOCK

    @pl.when(i == 0)
    def _():
        x = x_ref[...]
        h = _rms(x, pre_ref[...]) * (1.0 + m_ref[1]) + m_ref[0]
        proj = _dot(h.astype(BF16), win_ref[...])
        u = proj[:, 0:LRU_WIDTH]
        y_lru = proj[:, LRU_WIDTH:2 * LRU_WIDTH]
        o0 = 2 * LRU_WIDTH
        q = proj[:, o0:o0 + HG_WIDTH]
        f_raw = proj[:, o0 + HG_WIDTH:o0 + 2 * HG_WIDTH]
        v = proj[:, o0 + 2 * HG_WIDTH:o0 + 3 * HG_WIDTH]
        g_ref[...] = proj[:, o0 + 3 * HG_WIDTH:o0 + 4 * HG_WIDTH]

        cw = cw_ref[...]
        u_conv = cb_ref[...]
        for k in range(CONV_WIDTH - 1):
            u_conv = u_conv + conv0_ref[k] * cw[k:k + 1]
            if k > 0:
                conv_out_ref[k - 1] = conv0_ref[k]
        u_conv = u_conv + u * cw[CONV_WIDTH - 1:CONV_WIDTH]
        conv_out_ref[CONV_WIDTH - 2] = u

        a, mult, ig = _lru_gates(u_conv, wa_ref, ba_ref[...], wx_ref, bx_ref[...], lam_ref[...])
        hs = (mult * ig) * u_conv + a * h0_ref[...]
        h_out_ref[...] = hs
        olru_ref[...] = hs * _gelu_tanh(y_lru)

        lb = _lower_bound(lbl_ref[...])
        f = lb + (1.0 - lb) * _sigmoid(f_raw)
        kk = 1.0 - f
        qs = q * (HG_HEAD_DIM ** -0.5)
        f_ref[...] = f
        qf_ref[...] = qs * f
        k_ref[...] = kk
        v_ref[...] = v
        qk = qs * kk
        for hd in range(HG_HEADS):
            l0 = hd * HG_HEAD_DIM
            tot = jnp.sum(qk[:, l0:l0 + HG_HEAD_DIM], axis=-1, keepdims=True)
            qk_ref[:, l0:l0 + HG_HEAD_DIM] = jnp.broadcast_to(tot, (qk.shape[0], HG_HEAD_DIM))

    base = pl.multiple_of(i * nb, nb)
    grp = lambda ref: ref[pl.ds(base, nb), :]
    f_g, k_g, v_g, qf_g, qk_g = grp(f_ref), grp(k_ref), grp(v_ref), grp(qf_ref), grp(qk_ref)
    square = (HG_HEAD_DIM, HG_HEAD_DIM)
    for j in range(nb):
        for hd in range(HG_HEADS):
            l0 = hd * HG_HEAD_DIM
            rowv = lambda z: z[j:j + 1, l0:l0 + HG_HEAD_DIM]
            s_old = s0_ref[j, hd]
            f_col = jnp.broadcast_to(rowv(f_g), square).T
            k_col = jnp.broadcast_to(rowv(k_g), square).T
            v_row = rowv(v_g)
            outer = k_col.astype(BF16).astype(F32) * v_row.astype(BF16).astype(F32)
            s_out_ref[j, hd] = f_col * s_old + outer
            qf8 = jnp.broadcast_to(rowv(qf_g), (V7X_SUBLANES, HG_HEAD_DIM)).astype(BF16)
            o1 = _dot(qf8, s_old.astype(BF16))[0:1]
            ostage_ref[j:j + 1, l0:l0 + HG_HEAD_DIM] = o1 + rowv(qk_g) * v_row
    ohg_ref[pl.ds(base, nb), :] = ostage_ref[...]

    @pl.when(i == n_i - 1)
    def _():
        hgw = hgw_ref[...]
        parts = [olru_ref[...]]
        for hd in range(HG_HEADS):
            l0 = hd * HG_HEAD_DIM
            parts.append(_head_rms_gate(ohg_ref[:, l0:l0 + HG_HEAD_DIM], hgw,
                                        g_ref[:, l0:l0 + HG_HEAD_DIM]))
        mix_in = jnp.concatenate(parts, axis=-1).astype(BF16)
        mix = _dot(mix_in, wout_ref[...])
        o_ref[...] = x_ref[...] + m_ref[2] * _rms(mix, post_ref[...])


def _mix_sample(x, mod9, p, h0, conv0, s0):
    n_seq = x.shape[0]
    nb = SAMPLE_SEQ_BLOCK
    state_spec = pl.BlockSpec((nb, HG_HEADS, HG_HEAD_DIM, HG_HEAD_DIM), lambda i: (i, 0, 0, 0))
    full2 = lambda shape: pl.BlockSpec(shape, lambda i: (0,) * len(shape))
    return pl.pallas_call(
        _mix_sample_kernel,
        grid=(n_seq // nb,),
        in_specs=[
            _const_spec((n_seq, D_MODEL)),
            pl.BlockSpec((3, n_seq, D_MODEL), lambda i: (1, 0, 0), pipeline_mode=pl.Buffered(1)),
            _const_spec((1, D_MODEL)),
            _const_spec((1, D_MODEL)),
            _const_spec((D_MODEL, D_PROJ)),
            _const_spec((CONV_WIDTH, LRU_WIDTH)),
            _const_spec((1, LRU_WIDTH)),
            _const_spec((LRU_WIDTH, LRU_WIDTH)),
            _const_spec((1, LRU_WIDTH)),
            _const_spec((LRU_WIDTH, LRU_WIDTH)),
            _const_spec((1, LRU_WIDTH)),
            _const_spec((1, LRU_WIDTH)),
            _const_spec(p["lb_logits"].shape),
            _const_spec((1, HG_HEAD_DIM)),
            _const_spec((D_MODEL, D_MODEL)),
            _const_spec((n_seq, LRU_WIDTH)),
            _const_spec((CONV_WIDTH - 1, n_seq, LRU_WIDTH)),
            state_spec,
        ],
        out_specs=[
            full2((n_seq, D_MODEL)),
            full2((n_seq, LRU_WIDTH)),
            full2((CONV_WIDTH - 1, n_seq, LRU_WIDTH)),
            state_spec,
        ],
        out_shape=[
            jax.ShapeDtypeStruct((n_seq, D_MODEL), F32),
            jax.ShapeDtypeStruct((n_seq, LRU_WIDTH), F32),
            jax.ShapeDtypeStruct((CONV_WIDTH - 1, n_seq, LRU_WIDTH), F32),
            jax.ShapeDtypeStruct((n_seq, HG_HEADS, HG_HEAD_DIM, HG_HEAD_DIM), F32),
        ],
        scratch_shapes=[pltpu.VMEM((n_seq, HG_WIDTH), F32) for _ in range(8)]
        + [pltpu.VMEM((nb, HG_WIDTH), F32)],
        compiler_params=pltpu.CompilerParams(
            dimension_semantics=("arbitrary",), vmem_limit_bytes=VMEM_LIMIT),
        name="mix_sample",
    )(x, mod9, p["ln_mix_pre"], p["ln_mix_post"], p["w_in"], p["conv_w"], p["conv_b"],
      p["wa_bd"], p["b_a"], p["wx_bd"], p["b_x"], p["lam"], p["lb_logits"], p["hg_norm_w"],
      p["w_out"], h0, conv0, s0)


def _block_diag(w):
    heads, blk, _ = w.shape
    eye = jnp.eye(heads, dtype=w.dtype)
    return (eye[:, None, :, None] * w[:, :, None, :]).reshape(heads * blk, heads * blk)


def kernel(x_prompt, x_sample, c_prompt, c_sample, state_lru_h, state_lru_conv, state_hgrn_S, w_ada, b_ada, ln_ffn1_pre, ln_ffn1_post, ffn1_w_gate, ffn1_w_up, ffn1_w_down, ln_mix_pre, ln_mix_post, w_in, lru_conv_w, lru_conv_b, lru_w_a, lru_b_a, lru_w_x, lru_b_x, lru_lambda, hg_lb_logits, hg_norm_w, w_out, ln_ffn2_pre, ln_ffn2_post, ffn2_w_gate, ffn2_w_up, ffn2_w_down):
    depth = w_ada.shape[0]
    batch, seq_len, _ = x_prompt.shape
    n_seq = x_sample.shape[0]
    assert depth == 1 and x_sample.shape[1] == 1
    assert seq_len % FFN_TILE == 0 and seq_len % MIX_TILE == 0 and n_seq % SAMPLE_SEQ_BLOCK == 0
    assert batch % MIX_SEQS == 0

    xp = x_prompt.reshape(batch * seq_len, D_MODEL)
    xs = x_sample.reshape(n_seq, D_MODEL)
    ph, pc, pS, sh, sc, sS = [], [], [], [], [], []
    for l in range(depth):
        row = lambda w: w[l].reshape(1, -1)
        mod9 = _ada(jnp.concatenate([c_sample, c_prompt], axis=0), w_ada[l], b_ada[l])
        mod_p = jnp.transpose(mod9[:, n_seq:], (1, 0, 2))
        p = {
            "ln_mix_pre": row(ln_mix_pre), "ln_mix_post": row(ln_mix_post),
            "w_in": w_in[l].astype(BF16), "w_out": w_out[l].astype(BF16),
            "conv_w": lru_conv_w[l], "conv_b": row(lru_conv_b),
            "wa_bd": _block_diag(lru_w_a[l]).astype(BF16), "b_a": row(lru_b_a),
            "wx_bd": _block_diag(lru_w_x[l]).astype(BF16), "b_x": row(lru_b_x),
            "lam": row(lru_lambda), "lb_logits": hg_lb_logits,
            "hg_norm_w": row(hg_norm_w),
        }
        f1 = (row(ln_ffn1_pre), row(ln_ffn1_post), ffn1_w_gate[l], ffn1_w_up[l], ffn1_w_down[l])
        f2 = (row(ln_ffn2_pre), row(ln_ffn2_post), ffn2_w_gate[l], ffn2_w_up[l], ffn2_w_down[l])

        xp, xs = _ffn(xp, xs, mod_p[:, 0:3], mod9, 0, *f1, seq_len)
        xp, h_p, c_p, S_p = _mix_prompt(xp.reshape(batch, seq_len, D_MODEL), mod_p[:, 3:6], p)
        xp = xp.reshape(batch * seq_len, D_MODEL)
        conv0 = jnp.transpose(state_lru_conv[l], (1, 0, 2))
        xs, h_s, c_s, S_s = _mix_sample(xs, mod9, p, state_lru_h[l], conv0, state_hgrn_S[l])
        xp, xs = _ffn(xp, xs, mod_p[:, 6:9], mod9, 2, *f2, seq_len)

        ph.append(h_p.reshape(batch, LRU_WIDTH)); pc.append(c_p); pS.append(S_p)
        sh.append(h_s); sc.append(jnp.transpose(c_s, (1, 0, 2))); sS.append(S_s)

    return (xp.reshape(batch, seq_len, D_MODEL), xs.reshape(n_seq, 1, D_MODEL),
            jnp.stack(ph), jnp.stack(pc), jnp.stack(pS), jnp.stack(sh), jnp.stack(sc), jnp.stack(sS))
```

```python
import jax
import jax.numpy as jnp
from jax import lax
from jax.experimental import pallas as pl
from jax.experimental.pallas import tpu as pltpu

F32 = jnp.float32
BF16 = jnp.bfloat16

D_MODEL = 1024
D_FF = 2816
LRU_WIDTH = 512
CONV_WIDTH = 4
LRU_C = 8.0
HG_WIDTH = 512
HG_HEAD_DIM = 128
HG_HEADS = HG_WIDTH // HG_HEAD_DIM
HG_CHUNK = 64
N_MOD = 9
D_PROJ = 2 * LRU_WIDTH + 4 * HG_WIDTH
EPS = 1e-6

V7X_SUBLANES = 8
V7X_VMEM_BYTES = 64 * 1024 * 1024
VMEM_LIMIT = V7X_VMEM_BYTES - 8 * 1024 * 1024

FFN_TILE = 512
MIX_TILE = 256
MIX_SEQS = 2
MIX_STAGE_LAG = 7
FFN_WBLOCK = 256
FFN_WSTEPS = D_FF // FFN_WBLOCK
SAMPLE_SEQ_BLOCK = V7X_SUBLANES

HG_LEVELS = (1, 2, 4, 8, 16, 32)
HG_DIRECT_MIN = 2.0 ** -100


def _rms(x, w):
    return (x * lax.rsqrt(jnp.mean(x * x, axis=-1, keepdims=True) + EPS)) * w


def _sigmoid(x):
    return 1.0 / (1.0 + jnp.exp(-x))


def _silu(x):
    return x * _sigmoid(x)


def _gelu_tanh(x):
    c = 0.7978845608028654
    return x * (0.5 * (1.0 + jnp.tanh(c * (x + 0.044715 * (x * x * x)))))


def _softplus(z):
    return jnp.maximum(z, 0.0) + jnp.log1p(jnp.exp(-jnp.abs(z)))


def _dot(a, b):
    return jnp.dot(a, b, preferred_element_type=F32)


def _dot_nt(a, b):
    return lax.dot_general(a, b, (((1,), (1,)), ((), ())), preferred_element_type=F32)


def _dot_tn(a, b):
    return lax.dot_general(a, b, (((0,), (0,)), ((), ())), preferred_element_type=F32)


def _const_spec(shape):
    nd = len(shape)
    return pl.BlockSpec(shape, lambda *_: (0,) * nd, pipeline_mode=pl.Buffered(1))


def _ada_kernel(c_ref, w_ref, b_ref, o_ref):
    s = _silu(c_ref[...])
    o_ref[0] = _dot(s.astype(BF16), w_ref[...].astype(BF16)) + b_ref[...]


def _ada(c_all, w_ada, b_ada):
    rows = c_all.shape[0]
    return pl.pallas_call(
        _ada_kernel,
        grid=(N_MOD,),
        in_specs=[
            pl.BlockSpec((rows, D_MODEL), lambda j: (0, 0)),
            pl.BlockSpec((D_MODEL, D_MODEL), lambda j: (0, j)),
            pl.BlockSpec((1, D_MODEL), lambda j: (0, j)),
        ],
        out_specs=pl.BlockSpec((1, rows, D_MODEL), lambda j: (j, 0, 0)),
        out_shape=jax.ShapeDtypeStruct((N_MOD, rows, D_MODEL), F32),
        compiler_params=pltpu.CompilerParams(
            dimension_semantics=("arbitrary",), vmem_limit_bytes=VMEM_LIMIT),
        name="ada_mod",
    )(c_all, w_ada, b_ada.reshape(1, N_MOD * D_MODEL))


def _ffn_body(x, shift, scale, gate, pre_w, post_w, wg_ref, wu_ref, wd_ref):
    h = _rms(x, pre_w) * (1.0 + scale) + shift
    hb = h.astype(BF16)
    a = _dot(hb, wg_ref[...])
    u = _dot(hb, wu_ref[...])
    act = (_silu(a) * u).astype(BF16)
    y = _dot(act, wd_ref[...])
    return x + (0.5 * gate) * _rms(y, post_w)


def _ffn_kernel(xp_ref, mp_ref, xs_ref, ms_ref, pre_ref, post_ref, wg32_ref, wu32_ref, wd32_ref,
                op_ref, os_ref, wg_ref, wu_ref, wd_ref):
    s = pl.program_id(0)
    n_steps = pl.num_programs(0)

    for j in range(FFN_WSTEPS):
        @pl.when(s == j)
        def _(j=j):
            c0 = j * FFN_WBLOCK
            wg_ref[:, c0:c0 + FFN_WBLOCK] = wg32_ref[...].astype(BF16)
            wu_ref[:, c0:c0 + FFN_WBLOCK] = wu32_ref[...].astype(BF16)
            wd_ref[c0:c0 + FFN_WBLOCK, :] = wd32_ref[...].astype(BF16)

    @pl.when((s >= FFN_WSTEPS) & (s < n_steps - 1))
    def _():
        m = mp_ref[0]
        half = FFN_TILE // 4
        for r0 in range(0, FFN_TILE, half):
            op_ref[r0:r0 + half, :] = _ffn_body(xp_ref[r0:r0 + half, :], m[0:1], m[1:2], m[2:3],
                                                pre_ref[...], post_ref[...], wg_ref, wu_ref, wd_ref)

    @pl.when(s == n_steps - 1)
    def _():
        os_ref[...] = _ffn_body(xs_ref[...], ms_ref[0], ms_ref[1], ms_ref[2], pre_ref[...],
                                post_ref[...], wg_ref, wu_ref, wd_ref)


def _ffn(xp, xs, mod_p3, mod9, sub_layer, pre_w, post_w, wg, wu, wd, seq_len):
    n_tok = xp.shape[0]
    n_seq = xs.shape[0]
    n_tiles = n_tok // FFN_TILE
    steps_per_seq = seq_len // FFN_TILE
    tile = lambda s: jnp.clip(s - FFN_WSTEPS, 0, n_tiles - 1)
    wblk = lambda s: jnp.minimum(s, FFN_WSTEPS - 1)
    tok_spec = pl.BlockSpec((FFN_TILE, D_MODEL), lambda s: (tile(s), 0))
    return pl.pallas_call(
        _ffn_kernel,
        grid=(FFN_WSTEPS + n_tiles + 1,),
        in_specs=[
            tok_spec,
            pl.BlockSpec((1, 3, D_MODEL), lambda s: (tile(s) // steps_per_seq, 0, 0)),
            _const_spec((n_seq, D_MODEL)),
            pl.BlockSpec((3, n_seq, D_MODEL), lambda s: (sub_layer, 0, 0), pipeline_mode=pl.Buffered(1)),
            _const_spec((1, D_MODEL)),
            _const_spec((1, D_MODEL)),
            pl.BlockSpec((D_MODEL, FFN_WBLOCK), lambda s: (0, wblk(s))),
            pl.BlockSpec((D_MODEL, FFN_WBLOCK), lambda s: (0, wblk(s))),
            pl.BlockSpec((FFN_WBLOCK, D_MODEL), lambda s: (wblk(s), 0)),
        ],
        out_specs=[tok_spec, pl.BlockSpec((n_seq, D_MODEL), lambda s: (0, 0))],
        out_shape=[jax.ShapeDtypeStruct((n_tok, D_MODEL), F32),
                   jax.ShapeDtypeStruct((n_seq, D_MODEL), F32)],
        scratch_shapes=[pltpu.VMEM((D_MODEL, D_FF), BF16), pltpu.VMEM((D_MODEL, D_FF), BF16),
                        pltpu.VMEM((D_FF, D_MODEL), BF16)],
        compiler_params=pltpu.CompilerParams(
            dimension_semantics=("arbitrary",), vmem_limit_bytes=VMEM_LIMIT),
        name="ffn",
    )(xp, mod_p3, xs, mod9, pre_w, post_w, wg, wu, wd)


def _lower_bound(lb_logits):
    z = lb_logits - jnp.max(lb_logits, axis=0, keepdims=True)
    e = jnp.exp(z)
    return e[0:1] / jnp.sum(e, axis=0, keepdims=True)


def _lru_gates(u_conv, wa_ref, ba, wx_ref, bx, lam):
    ub = u_conv.astype(BF16)
    r = _sigmoid(_dot(ub, wa_ref[...]) + ba)
    ig = _sigmoid(_dot(ub, wx_ref[...]) + bx)
    log_a = (-LRU_C * r) * _softplus(-lam)
    a = jnp.exp(log_a)
    th = jnp.tanh(log_a)
    mult = jnp.sqrt((-2.0 * th) / (1.0 - th))
    return a, mult, ig


def _head_rms_gate(o, norm_w, g):
    return _rms(o, norm_w) * _silu(g)


def _hgrn_chunk_head(scores, q_in, k_out, dec, vb_blk, g_blk, hgw, st_ref, hd, ohg_ref, r0, l0):
    st = st_ref[hd]
    o = _dot_nt(q_in, st.astype(BF16)) + _dot(scores.astype(BF16), vb_blk)
    st_ref[hd] = st * dec + _dot_tn(vb_blk, k_out)
    ohg_ref[r0:r0 + HG_CHUNK, l0:l0 + HG_HEAD_DIM] = _head_rms_gate(o, hgw, g_blk)


def _hgrn_chunks_direct(qs, kk, pfx, tails, vb, g, hgw, st_ref, ohg_ref):
    tb = qs.shape[0]
    n_chunk = len(tails)
    inv = 1.0 / pfx
    q_dec = qs * pfx
    k_dec = kk * inv
    q_in = q_dec.astype(BF16)
    k_in = k_dec.astype(BF16)
    ti = lax.broadcasted_iota(jnp.int32, (tb, tb), 0)
    si = lax.broadcasted_iota(jnp.int32, (tb, tb), 1)
    visible = (ti >= si) & ((ti ^ si) < HG_CHUNK)
    row_chunk = jnp.right_shift(lax.broadcasted_iota(jnp.int32, (tb, HG_HEAD_DIM), 0),
                                HG_CHUNK.bit_length() - 1)
    tail_rows = jnp.concatenate([jnp.broadcast_to(t, (HG_CHUNK, t.shape[-1])) for t in tails], axis=0)
    k_end = k_dec * tail_rows
    for hd in range(HG_HEADS):
        l0 = hd * HG_HEAD_DIM
        head = lambda z: z[:, l0:l0 + HG_HEAD_DIM]
        by_chunk = lambda z: jnp.concatenate(
            [jnp.where(row_chunk == c, head(z), 0.0) for c in range(n_chunk)], axis=-1).astype(BF16)
        scores = jnp.where(visible, _dot_nt(head(q_in), head(k_in)), 0.0)
        grown = _dot_tn(head(vb), by_chunk(k_end))
        st = st_ref[hd]
        states = []
        for c, tail in enumerate(tails):
            states.append(st.astype(BF16))
            st = st * head(tail) + grown[:, c * HG_HEAD_DIM:(c + 1) * HG_HEAD_DIM]
        st_ref[hd] = st
        o = (_dot_nt(by_chunk(q_dec), jnp.concatenate(states, axis=-1))
             + _dot(scores.astype(BF16), head(vb)))
        ohg_ref[:, l0:l0 + HG_HEAD_DIM] = _head_rms_gate(o, hgw, head(g))


def _hgrn_chunks_levels(f3, kk3, qs3, vb, g, hgw, st_ref, ohg_ref):
    n_grp = f3.shape[0]
    tb = n_grp * V7X_SUBLANES
    sub = lax.broadcasted_iota(jnp.int32, f3.shape, 1)
    grp = lax.broadcasted_iota(jnp.int32, f3.shape, 0)
    to_mxu = lambda z: z.reshape(tb, z.shape[-1]).astype(BF16)
    q_lv = [to_mxu(qs3)]
    k_lv = [to_mxu(kk3)]
    tot, pre, post = f3, f3, None
    for lvl in HG_LEVELS:
        q_lv.append(to_mxu(qs3 * pre))
        k_lv.append(k_lv[0] if post is None else to_mxu(kk3 * post))
        if lvl < V7X_SUBLANES:
            upper = (sub & lvl) != 0
            below = pltpu.roll(tot, lvl, 1)
            above = pltpu.roll(tot, V7X_SUBLANES - lvl, 1)
        else:
            upper = (grp & (lvl // V7X_SUBLANES)) != 0
            below = jnp.roll(tot, lvl // V7X_SUBLANES, axis=0)
            above = jnp.roll(tot, -(lvl // V7X_SUBLANES), axis=0)
        pre = jnp.where(upper, pre * below, pre)
        post = jnp.where(upper, 1.0, above) if post is None else jnp.where(upper, post, post * above)
        tot = tot * jnp.where(upper, below, above)
    q_in = to_mxu(qs3 * pre)
    k_out = to_mxu(kk3 * post)
    s_decay = tot.reshape(tb, tot.shape[-1])

    ti = lax.broadcasted_iota(jnp.int32, (HG_CHUNK, HG_CHUNK), 0)
    si = lax.broadcasted_iota(jnp.int32, (HG_CHUNK, HG_CHUNK), 1)
    masks = [ti == si]
    tx = ti ^ si
    for lvl in HG_LEVELS:
        masks.append((tx >= lvl) & (tx < 2 * lvl) & ((ti & lvl) != 0))

    for c in range(tb // HG_CHUNK):
        r0 = c * HG_CHUNK
        for hd in range(HG_HEADS):
            l0 = hd * HG_HEAD_DIM
            blk = lambda z: z[r0:r0 + HG_CHUNK, l0:l0 + HG_HEAD_DIM]
            scores = jnp.zeros((HG_CHUNK, HG_CHUNK), F32)
            for ql, kl, msk in zip(q_lv, k_lv, masks):
                scores = jnp.where(msk, _dot_nt(blk(ql), blk(kl)), scores)
            dec = s_decay[r0 + HG_CHUNK - 1:r0 + HG_CHUNK, l0:l0 + HG_HEAD_DIM]
            _hgrn_chunk_head(scores, blk(q_in), blk(k_out), dec, blk(vb), blk(g), hgw,
                             st_ref, hd, ohg_ref, r0, l0)

def _mix_tile(x_ref, m_ref, o_ref, t, pre_ref, post_ref, win_ref, cw_ref, cb_ref, wa_ref, ba_ref,
              wx_ref, bx_ref, lam_ref, lbl_ref, hgw_ref, wout_ref, ucar_ref, hcar_ref, st_ref, ohg_ref):
    tb = MIX_TILE
    x = x_ref[...]
    m = m_ref[...]
    hb = (_rms(x, pre_ref[...]) * (1.0 + m[1:2]) + m[0:1]).astype(BF16)
    yield "pre"
    segs = []
    for c0 in range(0, D_PROJ, LRU_WIDTH):
        segs.append(_dot(hb, win_ref[:, c0:c0 + LRU_WIDTH]))
        yield "in"
    u, y_lru, q, f_raw, v, g = segs

    n_grp = tb // V7X_SUBLANES
    grouped = lambda z: z.reshape(n_grp, V7X_SUBLANES, z.shape[-1])
    flat = lambda z: z.reshape(tb, z.shape[-1])
    gshape = (n_grp, V7X_SUBLANES, LRU_WIDTH)
    sub = lax.broadcasted_iota(jnp.int32, gshape, 1)
    grp = lax.broadcasted_iota(jnp.int32, gshape, 0)

    u3 = grouped(u)
    u_all = jnp.concatenate([ucar_ref[...][None], u3], axis=0)
    ucar_ref[...] = u3[n_grp - 1]
    cw = cw_ref[...]
    u_conv = cb_ref[...]
    for k in range(CONV_WIDTH - 1):
        back = CONV_WIDTH - 1 - k
        rot = pltpu.roll(u_all, back, 1)
        u_conv = u_conv + jnp.where(sub >= back, rot[1:], rot[:-1]) * cw[k:k + 1]
    u_conv = flat(u_conv + u3 * cw[CONV_WIDTH - 1:CONV_WIDTH])
    yield "vec"

    a, mult, ig = _lru_gates(u_conv, wa_ref, ba_ref[...], wx_ref, bx_ref[...], lam_ref[...])
    yield "vec"
    a3, mult3 = grouped(a), grouped(mult)
    first_row = jnp.where(t == 0, 0, -1)
    mult3 = jnp.where(grp * V7X_SUBLANES + sub == first_row, 1.0, mult3)
    b3 = (mult3 * grouped(ig)) * grouped(u_conv)
    s = 1
    while s < V7X_SUBLANES:
        keep = sub >= s
        b3 = jnp.where(keep, a3 * pltpu.roll(b3, s, 1) + b3, b3)
        a3 = jnp.where(keep, a3 * pltpu.roll(a3, s, 1), a3)
        s *= 2
    yield "vec"
    carry = hcar_ref[...]
    hs = []
    for gi in range(n_grp):
        h_g = a3[gi] * carry + b3[gi]
        carry = h_g[V7X_SUBLANES - 1:V7X_SUBLANES]
        hs.append(h_g)
    hcar_ref[...] = carry
    o_lru = jnp.concatenate(hs, axis=0) * _gelu_tanh(y_lru)
    yield "vec"

    lb = _lower_bound(lbl_ref[...])
    f3 = grouped(lb + (1.0 - lb) * _sigmoid(f_raw))
    kk3 = 1.0 - f3
    qs3 = grouped(q * (HG_HEAD_DIM ** -0.5))
    vb = v.astype(BF16)
    hgw = hgw_ref[...]

    pfx = f3
    s = 1
    while s < V7X_SUBLANES:
        pfx = jnp.where(sub >= s, pfx * pltpu.roll(pfx, s, 1), pfx)
        s *= 2
    grp_per_chunk = HG_CHUNK // V7X_SUBLANES
    rows, tails = [], []
    for gi in range(n_grp):
        p_g = pfx[gi] if gi % grp_per_chunk == 0 else pfx[gi] * carry
        carry = p_g[V7X_SUBLANES - 1:V7X_SUBLANES]
        rows.append(p_g)
        if gi % grp_per_chunk == grp_per_chunk - 1:
            tails.append(carry)
    pfx = jnp.concatenate(rows, axis=0)
    direct = jnp.min(jnp.concatenate(tails, axis=0)) >= HG_DIRECT_MIN
    yield "vec"

    @pl.when(direct)
    def _():
        _hgrn_chunks_direct(flat(qs3), flat(kk3), pfx, tails, vb, g, hgw, st_ref, ohg_ref)

    @pl.when(jnp.logical_not(direct))
    def _():
        _hgrn_chunks_levels(f3, kk3, qs3, vb, g, hgw, st_ref, ohg_ref)
    yield "chunk"

    mix_in = jnp.concatenate([o_lru, ohg_ref[...]], axis=-1).astype(BF16)
    mix = _dot(mix_in, wout_ref[...])
    o_ref[...] = x + m[2:3] * _rms(mix, post_ref[...])


def _mix_prompt_kernel(x_ref, m_ref, *refs):
    weights = refs[:13]
    o_ref, h_out_ref, conv_out_ref, s_out_ref, ucar_ref, hcar_ref, st_ref, ohg_ref = refs[13:]
    t = pl.program_id(1)
    n_t = pl.num_programs(1)

    @pl.when(t == 0)
    def _():
        ucar_ref[...] = jnp.zeros_like(ucar_ref)
        hcar_ref[...] = jnp.zeros_like(hcar_ref)
        st_ref[...] = jnp.zeros_like(st_ref)

    tiles = [_mix_tile(x_ref.at[sq], m_ref.at[sq], o_ref.at[sq], t, *weights,
                       ucar_ref.at[sq], hcar_ref.at[sq], st_ref.at[sq], ohg_ref.at[sq])
             for sq in range(MIX_SEQS)]
    waiting, running, tick = tiles, [], 0
    while waiting or running:
        if waiting and tick % MIX_STAGE_LAG == 0:
            running.append(waiting.pop(0))
        running = [tile for tile in running if next(tile, tile) is not tile]
        tick += 1

    @pl.when(t == n_t - 1)
    def _():
        for sq in range(MIX_SEQS):
            h_out_ref[sq] = hcar_ref[sq]
            conv_out_ref[sq] = ucar_ref[sq, V7X_SUBLANES - (CONV_WIDTH - 1):V7X_SUBLANES, :]
            for hd in range(HG_HEADS):
                s_out_ref[sq, hd] = st_ref[sq, hd].T


def _mix_prompt(x, mod3, p):
    batch, seq_len, _ = x.shape
    n_t = seq_len // MIX_TILE
    nsq = MIX_SEQS
    tok_spec = pl.BlockSpec((nsq, MIX_TILE, D_MODEL), lambda b, t: (b, t, 0))
    return pl.pallas_call(
        _mix_prompt_kernel,
        grid=(batch // nsq, n_t),
        in_specs=[
            tok_spec,
            pl.BlockSpec((nsq, 3, D_MODEL), lambda b, t: (b, 0, 0)),
            _const_spec((1, D_MODEL)),
            _const_spec((1, D_MODEL)),
            _const_spec((D_MODEL, D_PROJ)),
            _const_spec((CONV_WIDTH, LRU_WIDTH)),
            _const_spec((1, LRU_WIDTH)),
            _const_spec((LRU_WIDTH, LRU_WIDTH)),
            _const_spec((1, LRU_WIDTH)),
            _const_spec((LRU_WIDTH, LRU_WIDTH)),
            _const_spec((1, LRU_WIDTH)),
            _const_spec((1, LRU_WIDTH)),
            _const_spec(p["lb_logits"].shape),
            _const_spec((1, HG_HEAD_DIM)),
            _const_spec((D_MODEL, D_MODEL)),
        ],
        out_specs=[
            tok_spec,
            pl.BlockSpec((nsq, 1, LRU_WIDTH), lambda b, t: (b, 0, 0)),
            pl.BlockSpec((nsq, CONV_WIDTH - 1, LRU_WIDTH), lambda b, t: (b, 0, 0)),
            pl.BlockSpec((nsq, HG_HEADS, HG_HEAD_DIM, HG_HEAD_DIM), lambda b, t: (b, 0, 0, 0)),
        ],
        out_shape=[
            jax.ShapeDtypeStruct((batch, seq_len, D_MODEL), F32),
            jax.ShapeDtypeStruct((batch, 1, LRU_WIDTH), F32),
            jax.ShapeDtypeStruct((batch, CONV_WIDTH - 1, LRU_WIDTH), F32),
            jax.ShapeDtypeStruct((batch, HG_HEADS, HG_HEAD_DIM, HG_HEAD_DIM), F32),
        ],
        scratch_shapes=[
            pltpu.VMEM((nsq, V7X_SUBLANES, LRU_WIDTH), F32),
            pltpu.VMEM((nsq, 1, LRU_WIDTH), F32),
            pltpu.VMEM((nsq, HG_HEADS, HG_HEAD_DIM, HG_HEAD_DIM), F32),
            pltpu.VMEM((nsq, MIX_TILE, HG_WIDTH), F32),
        ],
        compiler_params=pltpu.CompilerParams(
            dimension_semantics=("arbitrary", "arbitrary"), vmem_limit_bytes=VMEM_LIMIT),
        name="mix_prompt",
    )(x, mod3, p["ln_mix_pre"], p["ln_mix_post"], p["w_in"], p["conv_w"], p["conv_b"],
      p["wa_bd"], p["b_a"], p["wx_bd"], p["b_x"], p["lam"], p["lb_logits"], p["hg_norm_w"],
      p["w_out"])


def _mix_sample_kernel(x_ref, m_ref, pre_ref, post_ref, win_ref, cw_ref, cb_ref,
                       wa_ref, ba_ref, wx_ref, bx_ref, lam_ref, lbl_ref, hgw_ref, wout_ref,
                       h0_ref, conv0_ref, s0_ref,
                       o_ref, h_out_ref, conv_out_ref, s_out_ref,
                       qf_ref, f_ref, k_ref, v_ref, g_ref, qk_ref, olru_ref, ohg_ref, ostage_ref):
    i = pl.program_id(0)
    n_i = pl.num_programs(0)
    nb = SAMPLE_SEQ_BLOCK

    @pl.when(i == 0)
    def _():
        x = x_ref[...]
        h = _rms(x, pre_ref[...]) * (1.0 + m_ref[1]) + m_ref[0]
        proj = _dot(h.astype(BF16), win_ref[...])
        u = proj[:, 0:LRU_WIDTH]
        y_lru = proj[:, LRU_WIDTH:2 * LRU_WIDTH]
        o0 = 2 * LRU_WIDTH
        q = proj[:, o0:o0 + HG_WIDTH]
        f_raw = proj[:, o0 + HG_WIDTH:o0 + 2 * HG_WIDTH]
        v = proj[:, o0 + 2 * HG_WIDTH:o0 + 3 * HG_WIDTH]
        g_ref[...] = proj[:, o0 + 3 * HG_WIDTH:o0 + 4 * HG_WIDTH]

        cw = cw_ref[...]
        u_conv = cb_ref[...]
        for k in range(CONV_WIDTH - 1):
            u_conv = u_conv + conv0_ref[k] * cw[k:k + 1]
            if k > 0:
                conv_out_ref[k - 1] = conv0_ref[k]
        u_conv = u_conv + u * cw[CONV_WIDTH - 1:CONV_WIDTH]
        conv_out_ref[CONV_WIDTH - 2] = u

        a, mult, ig = _lru_gates(u_conv, wa_ref, ba_ref[...], wx_ref, bx_ref[...], lam_ref[...])
        hs = (mult * ig) * u_conv + a * h0_ref[...]
        h_out_ref[...] = hs
        olru_ref[...] = hs * _gelu_tanh(y_lru)

        lb = _lower_bound(lbl_ref[...])
        f = lb + (1.0 - lb) * _sigmoid(f_raw)
        kk = 1.0 - f
        qs = q * (HG_HEAD_DIM ** -0.5)
        f_ref[...] = f
        qf_ref[...] = qs * f
        k_ref[...] = kk
        v_ref[...] = v
        qk = qs * kk
        for hd in range(HG_HEADS):
            l0 = hd * HG_HEAD_DIM
            tot = jnp.sum(qk[:, l0:l0 + HG_HEAD_DIM], axis=-1, keepdims=True)
            qk_ref[:, l0:l0 + HG_HEAD_DIM] = jnp.broadcast_to(tot, (qk.shape[0], HG_HEAD_DIM))

    base = pl.multiple_of(i * nb, nb)
    grp = lambda ref: ref[pl.ds(base, nb), :]
    f_g, k_g, v_g, qf_g, qk_g = grp(f_ref), grp(k_ref), grp(v_ref), grp(qf_ref), grp(qk_ref)
    square = (HG_HEAD_DIM, HG_HEAD_DIM)
    for j in range(nb):
        for hd in range(HG_HEADS):
            l0 = hd * HG_HEAD_DIM
            rowv = lambda z: z[j:j + 1, l0:l0 + HG_HEAD_DIM]
            s_old = s0_ref[j, hd]
            f_col = jnp.broadcast_to(rowv(f_g), square).T
            k_col = jnp.broadcast_to(rowv(k_g), square).T
            v_row = rowv(v_g)
            outer = k_col.astype(BF16).astype(F32) * v_row.astype(BF16).astype(F32)
            s_out_ref[j, hd] = f_col * s_old + outer
            qf8 = jnp.broadcast_to(rowv(qf_g), (V7X_SUBLANES, HG_HEAD_DIM)).astype(BF16)
            o1 = _dot(qf8, s_old.astype(BF16))[0:1]
            ostage_ref[j:j + 1, l0:l0 + HG_HEAD_DIM] = o1 + rowv(qk_g) * v_row
    ohg_ref[pl.ds(base, nb), :] = ostage_ref[...]

    @pl.when(i == n_i - 1)
    def _():
        hgw = hgw_ref[...]
        parts = [olru_ref[...]]
        for hd in range(HG_HEADS):
            l0 = hd * HG_HEAD_DIM
            parts.append(_head_rms_gate(ohg_ref[:, l0:l0 + HG_HEAD_DIM], hgw,
                                        g_ref[:, l0:l0 + HG_HEAD_DIM]))
        mix_in = jnp.concatenate(parts, axis=-1).astype(BF16)
        mix = _dot(mix_in, wout_ref[...])
        o_ref[...] = x_ref[...] + m_ref[2] * _rms(mix, post_ref[...])


def _mix_sample(x, mod9, p, h0, conv0, s0):
    n_seq = x.shape[0]
    nb = SAMPLE_SEQ_BLOCK
    state_spec = pl.BlockSpec((nb, HG_HEADS, HG_HEAD_DIM, HG_HEAD_DIM), lambda i: (i, 0, 0, 0))
    full2 = lambda shape: pl.BlockSpec(shape, lambda i: (0,) * len(shape))
    return pl.pallas_call(
        _mix_sample_kernel,
        grid=(n_seq // nb,),
        in_specs=[
            _const_spec((n_seq, D_MODEL)),
            pl.BlockSpec((3, n_seq, D_MODEL), lambda i: (1, 0, 0), pipeline_mode=pl.Buffered(1)),
            _const_spec((1, D_MODEL)),
            _const_spec((1, D_MODEL)),
            _const_spec((D_MODEL, D_PROJ)),
            _const_spec((CONV_WIDTH, LRU_WIDTH)),
            _const_spec((1, LRU_WIDTH)),
            _const_spec((LRU_WIDTH, LRU_WIDTH)),
            _const_spec((1, LRU_WIDTH)),
            _const_spec((LRU_WIDTH, LRU_WIDTH)),
            _const_spec((1, LRU_WIDTH)),
            _const_spec((1, LRU_WIDTH)),
            _const_spec(p["lb_logits"].shape),
            _const_spec((1, HG_HEAD_DIM)),
            _const_spec((D_MODEL, D_MODEL)),
            _const_spec((n_seq, LRU_WIDTH)),
            _const_spec((CONV_WIDTH - 1, n_seq, LRU_WIDTH)),
            state_spec,
        ],
        out_specs=[
            full2((n_seq, D_MODEL)),
            full2((n_seq, LRU_WIDTH)),
            full2((CONV_WIDTH - 1, n_seq, LRU_WIDTH)),
            state_spec,
        ],
        out_shape=[
            jax.ShapeDtypeStruct((n_seq, D_MODEL), F32),
            jax.ShapeDtypeStruct((n_seq, LRU_WIDTH), F32),
            jax.ShapeDtypeStruct((CONV_WIDTH - 1, n_seq, LRU_WIDTH), F32),
            jax.ShapeDtypeStruct((n_seq, HG_HEADS, HG_HEAD_DIM, HG_HEAD_DIM), F32),
        ],
        scratch_shapes=[pltpu.VMEM((n_seq, HG_WIDTH), F32) for _ in range(8)]
        + [pltpu.VMEM((nb, HG_WIDTH), F32)],
        compiler_params=pltpu.CompilerParams(
            dimension_semantics=("arbitrary",), vmem_limit_bytes=VMEM_LIMIT),
        name="mix_sample",
    )(x, mod9, p["ln_mix_pre"], p["ln_mix_post"], p["w_in"], p["conv_w"], p["conv_b"],
      p["wa_bd"], p["b_a"], p["wx_bd"], p["b_x"], p["lam"], p["lb_logits"], p["hg_norm_w"],
      p["w_out"], h0, conv0, s0)


def _block_diag(w):
    heads, blk, _ = w.shape
    eye = jnp.eye(heads, dtype=w.dtype)
    return (eye[:, None, :, None] * w[:, :, None, :]).reshape(heads * blk, heads * blk)


def kernel(x_prompt, x_sample, c_prompt, c_sample, state_lru_h, state_lru_conv, state_hgrn_S, w_ada, b_ada, ln_ffn1_pre, ln_ffn1_post, ffn1_w_gate, ffn1_w_up, ffn1_w_down, ln_mix_pre, ln_mix_post, w_in, lru_conv_w, lru_conv_b, lru_w_a, lru_b_a, lru_w_x, lru_b_x, lru_lambda, hg_lb_logits, hg_norm_w, w_out, ln_ffn2_pre, ln_ffn2_post, ffn2_w_gate, ffn2_w_up, ffn2_w_down):
    depth = w_ada.shape[0]
    batch, seq_len, _ = x_prompt.shape
    n_seq = x_sample.shape[0]
    assert depth == 1 and x_sample.shape[1] == 1
    assert seq_len % FFN_TILE == 0 and seq_len % MIX_TILE == 0 and n_seq % SAMPLE_SEQ_BLOCK == 0
    assert batch % MIX_SEQS == 0

    xp = x_prompt.reshape(batch * seq_len, D_MODEL)
    xs = x_sample.reshape(n_seq, D_MODEL)
    ph, pc, pS, sh, sc, sS = [], [], [], [], [], []
    for l in range(depth):
        row = lambda w: w[l].reshape(1, -1)
        mod9 = _ada(jnp.concatenate([c_sample, c_prompt], axis=0), w_ada[l], b_ada[l])
        mod_p = jnp.transpose(mod9[:, n_seq:], (1, 0, 2))
        p = {
            "ln_mix_pre": row(ln_mix_pre), "ln_mix_post": row(ln_mix_post),
            "w_in": w_in[l].astype(BF16), "w_out": w_out[l].astype(BF16),
            "conv_w": lru_conv_w[l], "conv_b": row(lru_conv_b),
            "wa_bd": _block_diag(lru_w_a[l]).astype(BF16), "b_a": row(lru_b_a),
            "wx_bd": _block_diag(lru_w_x[l]).astype(BF16), "b_x": row(lru_b_x),
            "lam": row(lru_lambda), "lb_logits": hg_lb_logits,
            "hg_norm_w": row(hg_norm_w),
        }
        f1 = (row(ln_ffn1_pre), row(ln_ffn1_post), ffn1_w_gate[l], ffn1_w_up[l], ffn1_w_down[l])
        f2 = (row(ln_ffn2_pre), row(ln_ffn2_post), ffn2_w_gate[l], ffn2_w_up[l], ffn2_w_down[l])

        xp, xs = _ffn(xp, xs, mod_p[:, 0:3], mod9, 0, *f1, seq_len)
        xp, h_p, c_p, S_p = _mix_prompt(xp.reshape(batch, seq_len, D_MODEL), mod_p[:, 3:6], p)
        xp = xp.reshape(batch * seq_len, D_MODEL)
        conv0 = jnp.transpose(state_lru_conv[l], (1, 0, 2))
        xs, h_s, c_s, S_s = _mix_sample(xs, mod9, p, state_lru_h[l], conv0, state_hgrn_S[l])
        xp, xs = _ffn(xp, xs, mod_p[:, 6:9], mod9, 2, *f2, seq_len)

        ph.append(h_p.reshape(batch, LRU_WIDTH)); pc.append(c_p); pS.append(S_p)
        sh.append(h_s); sc.append(jnp.transpose(c_s, (1, 0, 2))); sS.append(S_s)

    return (xp.reshape(batch, seq_len, D_MODEL), xs.reshape(n_seq, 1, D_MODEL),
            jnp.stack(ph), jnp.stack(pc), jnp.stack(pS), jnp.stack(sh), jnp.stack(sc), jnp.stack(sS))
```

```python
import jax
import jax.numpy as jnp
from jax import lax
from jax.experimental import pallas as pl
from jax.experimental.pallas import tpu as pltpu

F32 = jnp.float32
BF16 = jnp.bfloat16

D_MODEL = 1024
D_FF = 2816
LRU_WIDTH = 512
CONV_WIDTH = 4
LRU_C = 8.0
HG_WIDTH = 512
HG_HEAD_DIM = 128
HG_HEADS = HG_WIDTH // HG_HEAD_DIM
HG_CHUNK = 64
N_MOD = 9
D_PROJ = 2 * LRU_WIDTH + 4 * HG_WIDTH
EPS = 1e-6

V7X_SUBLANES = 8
V7X_VMEM_BYTES = 64 * 1024 * 1024
VMEM_LIMIT = V7X_VMEM_BYTES - 8 * 1024 * 1024

FFN_TILE = 1024
FFN_ROWS = 256
MIX_TILE = 256
MIX_SEQS = 4
MIX_STAGE_LAG = 7
FFN_WBLOCK = 256
FFN_WSTEPS = D_FF // FFN_WBLOCK
SAMPLE_SEQ_BLOCK = V7X_SUBLANES

HG_LEVELS = (1, 2, 4, 8, 16, 32)
HG_DIRECT_MIN = 2.0 ** -100


def _rms(x, w):
    return (x * lax.rsqrt(jnp.mean(x * x, axis=-1, keepdims=True) + EPS)) * w


def _sigmoid(x):
    return 1.0 / (1.0 + jnp.exp(-x))


def _silu(x):
    return x * _sigmoid(x)


def _gelu_tanh(x):
    c = 0.7978845608028654
    return x * (0.5 * (1.0 + jnp.tanh(c * (x + 0.044715 * (x * x * x)))))


def _softplus(z):
    return jnp.maximum(z, 0.0) + jnp.log1p(jnp.exp(-jnp.abs(z)))


def _dot(a, b):
    return jnp.dot(a, b, preferred_element_type=F32)


def _dot_nt(a, b):
    return lax.dot_general(a, b, (((1,), (1,)), ((), ())), preferred_element_type=F32)


def _dot_tn(a, b):
    return lax.dot_general(a, b, (((0,), (0,)), ((), ())), preferred_element_type=F32)


def _const_spec(shape):
    nd = len(shape)
    return pl.BlockSpec(shape, lambda *_: (0,) * nd, pipeline_mode=pl.Buffered(1))


def _ada_kernel(c_ref, w_ref, b_ref, o_ref):
    s = _silu(c_ref[...])
    o_ref[0] = _dot(s.astype(BF16), w_ref[...].astype(BF16)) + b_ref[...]


def _ada(c_all, w_ada, b_ada):
    rows = c_all.shape[0]
    return pl.pallas_call(
        _ada_kernel,
        grid=(N_MOD,),
        in_specs=[
            pl.BlockSpec((rows, D_MODEL), lambda j: (0, 0)),
            pl.BlockSpec((D_MODEL, D_MODEL), lambda j: (0, j)),
            pl.BlockSpec((1, D_MODEL), lambda j: (0, j)),
        ],
        out_specs=pl.BlockSpec((1, rows, D_MODEL), lambda j: (j, 0, 0)),
        out_shape=jax.ShapeDtypeStruct((N_MOD, rows, D_MODEL), F32),
        compiler_params=pltpu.CompilerParams(
            dimension_semantics=("arbitrary",), vmem_limit_bytes=VMEM_LIMIT),
        name="ada_mod",
    )(c_all, w_ada, b_ada.reshape(1, N_MOD * D_MODEL))


def _ffn_body(x, shift, scale, gate, pre_w, post_w, wg_ref, wu_ref, wd_ref):
    h = _rms(x, pre_w) * (1.0 + scale) + shift
    hb = h.astype(BF16)
    a = _dot(hb, wg_ref[...])
    u = _dot(hb, wu_ref[...])
    act = (_silu(a) * u).astype(BF16)
    y = _dot(act, wd_ref[...])
    return x + (0.5 * gate) * _rms(y, post_w)


def _ffn_kernel(xp_ref, mp_ref, xs_ref, ms_ref, pre_ref, post_ref, wg32_ref, wu32_ref, wd32_ref,
                op_ref, os_ref, wg_ref, wu_ref, wd_ref):
    s = pl.program_id(0)
    n_steps = pl.num_programs(0)

    for j in range(FFN_WSTEPS):
        @pl.when(s == j)
        def _(j=j):
            c0 = j * FFN_WBLOCK
            wg_ref[:, c0:c0 + FFN_WBLOCK] = wg32_ref[...].astype(BF16)
            wu_ref[:, c0:c0 + FFN_WBLOCK] = wu32_ref[...].astype(BF16)
            wd_ref[c0:c0 + FFN_WBLOCK, :] = wd32_ref[...].astype(BF16)

    @pl.when((s >= FFN_WSTEPS) & (s < n_steps - 1))
    def _():
        m = mp_ref[0]
        for r0 in range(0, FFN_TILE, FFN_ROWS):
            op_ref[r0:r0 + FFN_ROWS, :] = _ffn_body(
                xp_ref[r0:r0 + FFN_ROWS, :], m[0:1], m[1:2], m[2:3],
                pre_ref[...], post_ref[...], wg_ref, wu_ref, wd_ref)

    @pl.when(s == n_steps - 1)
    def _():
        os_ref[...] = _ffn_body(xs_ref[...], ms_ref[0], ms_ref[1], ms_ref[2], pre_ref[...],
                                post_ref[...], wg_ref, wu_ref, wd_ref)


def _ffn(xp, xs, mod_p3, mod9, sub_layer, pre_w, post_w, wg, wu, wd, seq_len):
    n_tok = xp.shape[0]
    n_seq = xs.shape[0]
    n_tiles = n_tok // FFN_TILE
    steps_per_seq = seq_len // FFN_TILE
    tile = lambda s: jnp.clip(s - FFN_WSTEPS, 0, n_tiles - 1)
    wblk = lambda s: jnp.minimum(s, FFN_WSTEPS - 1)
    tok_spec = pl.BlockSpec((FFN_TILE, D_MODEL), lambda s: (tile(s), 0))
    return pl.pallas_call(
        _ffn_kernel,
        grid=(FFN_WSTEPS + n_tiles + 1,),
        in_specs=[
            tok_spec,
            pl.BlockSpec((1, 3, D_MODEL), lambda s: (tile(s) // steps_per_seq, 0, 0)),
            _const_spec((n_seq, D_MODEL)),
            pl.BlockSpec((3, n_seq, D_MODEL), lambda s: (sub_layer, 0, 0), pipeline_mode=pl.Buffered(1)),
            _const_spec((1, D_MODEL)),
            _const_spec((1, D_MODEL)),
            pl.BlockSpec((D_MODEL, FFN_WBLOCK), lambda s: (0, wblk(s))),
            pl.BlockSpec((D_MODEL, FFN_WBLOCK), lambda s: (0, wblk(s))),
            pl.BlockSpec((FFN_WBLOCK, D_MODEL), lambda s: (wblk(s), 0)),
        ],
        out_specs=[tok_spec, pl.BlockSpec((n_seq, D_MODEL), lambda s: (0, 0))],
        out_shape=[jax.ShapeDtypeStruct((n_tok, D_MODEL), F32),
                   jax.ShapeDtypeStruct((n_seq, D_MODEL), F32)],
        scratch_shapes=[pltpu.VMEM((D_MODEL, D_FF), BF16), pltpu.VMEM((D_MODEL, D_FF), BF16),
                        pltpu.VMEM((D_FF, D_MODEL), BF16)],
        compiler_params=pltpu.CompilerParams(
            dimension_semantics=("arbitrary",), vmem_limit_bytes=VMEM_LIMIT),
        name="ffn",
    )(xp, mod_p3, xs, mod9, pre_w, post_w, wg, wu, wd)


def _lower_bound(lb_logits):
    z = lb_logits - jnp.max(lb_logits, axis=0, keepdims=True)
    e = jnp.exp(z)
    return e[0:1] / jnp.sum(e, axis=0, keepdims=True)


def _lru_gates(u_conv, wa_ref, ba, wx_ref, bx, lam):
    ub = u_conv.astype(BF16)
    r = _sigmoid(_dot(ub, wa_ref[...]) + ba)
    ig = _sigmoid(_dot(ub, wx_ref[...]) + bx)
    log_a = (-LRU_C * r) * _softplus(-lam)
    a = jnp.exp(log_a)
    th = jnp.tanh(log_a)
    mult = jnp.sqrt((-2.0 * th) / (1.0 - th))
    return a, mult, ig


def _head_rms_gate(o, norm_w, g):
    return _rms(o, norm_w) * _silu(g)


def _hgrn_chunk_head(scores, q_in, k_out, dec, vb_blk, g_blk, hgw, st_ref, hd, ohg_ref, r0, l0):
    st = st_ref[hd]
    o = _dot_nt(q_in, st.astype(BF16)) + _dot(scores.astype(BF16), vb_blk)
    st_ref[hd] = st * dec + _dot_tn(vb_blk, k_out)
    ohg_ref[r0:r0 + HG_CHUNK, l0:l0 + HG_HEAD_DIM] = _head_rms_gate(o, hgw, g_blk)


def _hgrn_chunks_direct(qs, kk, pfx, tails, vb, g, hgw, st_ref, ohg_ref):
    tb = qs.shape[0]
    n_chunk = len(tails)
    inv = 1.0 / pfx
    q_dec = qs * pfx
    k_dec = kk * inv
    q_in = q_dec.astype(BF16)
    k_in = k_dec.astype(BF16)
    ti = lax.broadcasted_iota(jnp.int32, (tb, tb), 0)
    si = lax.broadcasted_iota(jnp.int32, (tb, tb), 1)
    visible = (ti >= si) & ((ti ^ si) < HG_CHUNK)
    row_chunk = jnp.right_shift(lax.broadcasted_iota(jnp.int32, (tb, HG_HEAD_DIM), 0),
                                HG_CHUNK.bit_length() - 1)
    tail_rows = jnp.concatenate([jnp.broadcast_to(t, (HG_CHUNK, t.shape[-1])) for t in tails], axis=0)
    k_end = k_dec * tail_rows
    for hd in range(HG_HEADS):
        l0 = hd * HG_HEAD_DIM
        head = lambda z: z[:, l0:l0 + HG_HEAD_DIM]
        by_chunk = lambda z: jnp.concatenate(
            [jnp.where(row_chunk == c, head(z), 0.0) for c in range(n_chunk)], axis=-1).astype(BF16)
        scores = jnp.where(visible, _dot_nt(head(q_in), head(k_in)), 0.0)
        grown = _dot_tn(head(vb), by_chunk(k_end))
        st = st_ref[hd]
        states = []
        for c, tail in enumerate(tails):
            states.append(st.astype(BF16))
            st = st * head(tail) + grown[:, c * HG_HEAD_DIM:(c + 1) * HG_HEAD_DIM]
        st_ref[hd] = st
        o = (_dot_nt(by_chunk(q_dec), jnp.concatenate(states, axis=-1))
             + _dot(scores.astype(BF16), head(vb)))
        ohg_ref[:, l0:l0 + HG_HEAD_DIM] = _head_rms_gate(o, hgw, head(g))


def _hgrn_chunks_levels(f3, kk3, qs3, vb, g, hgw, st_ref, ohg_ref):
    n_grp = f3.shape[0]
    tb = n_grp * V7X_SUBLANES
    sub = lax.broadcasted_iota(jnp.int32, f3.shape, 1)
    grp = lax.broadcasted_iota(jnp.int32, f3.shape, 0)
    to_mxu = lambda z: z.reshape(tb, z.shape[-1]).astype(BF16)
    q_lv = [to_mxu(qs3)]
    k_lv = [to_mxu(kk3)]
    tot, pre, post = f3, f3, None
    for lvl in HG_LEVELS:
        q_lv.append(to_mxu(qs3 * pre))
        k_lv.append(k_lv[0] if post is None else to_mxu(kk3 * post))
        if lvl < V7X_SUBLANES:
            upper = (sub & lvl) != 0
            below = pltpu.roll(tot, lvl, 1)
            above = pltpu.roll(tot, V7X_SUBLANES - lvl, 1)
        else:
            upper = (grp & (lvl // V7X_SUBLANES)) != 0
            below = jnp.roll(tot, lvl // V7X_SUBLANES, axis=0)
            above = jnp.roll(tot, -(lvl // V7X_SUBLANES), axis=0)
        pre = jnp.where(upper, pre * below, pre)
        post = jnp.where(upper, 1.0, above) if post is None else jnp.where(upper, post, post * above)
        tot = tot * jnp.where(upper, below, above)
    q_in = to_mxu(qs3 * pre)
    k_out = to_mxu(kk3 * post)
    s_decay = tot.reshape(tb, tot.shape[-1])

    ti = lax.broadcasted_iota(jnp.int32, (HG_CHUNK, HG_CHUNK), 0)
    si = lax.broadcasted_iota(jnp.int32, (HG_CHUNK, HG_CHUNK), 1)
    masks = [ti == si]
    tx = ti ^ si
    for lvl in HG_LEVELS:
        masks.append((tx >= lvl) & (tx < 2 * lvl) & ((ti & lvl) != 0))

    for c in range(tb // HG_CHUNK):
        r0 = c * HG_CHUNK
        for hd in range(HG_HEADS):
            l0 = hd * HG_HEAD_DIM
            blk = lambda z: z[r0:r0 + HG_CHUNK, l0:l0 + HG_HEAD_DIM]
            scores = jnp.zeros((HG_CHUNK, HG_CHUNK), F32)
            for ql, kl, msk in zip(q_lv, k_lv, masks):
                scores = jnp.where(msk, _dot_nt(blk(ql), blk(kl)), scores)
            dec = s_decay[r0 + HG_CHUNK - 1:r0 + HG_CHUNK, l0:l0 + HG_HEAD_DIM]
            _hgrn_chunk_head(scores, blk(q_in), blk(k_out), dec, blk(vb), blk(g), hgw,
                             st_ref, hd, ohg_ref, r0, l0)

def _mix_tile(x_ref, m_ref, o_ref, t, pre_ref, post_ref, win_ref, cw_ref, cb_ref, wa_ref, ba_ref,
              wx_ref, bx_ref, lam_ref, lbl_ref, hgw_ref, wout_ref, ucar_ref, hcar_ref, st_ref, ohg_ref):
    tb = MIX_TILE
    x = x_ref[...]
    m = m_ref[...]
    hb = (_rms(x, pre_ref[...]) * (1.0 + m[1:2]) + m[0:1]).astype(BF16)
    yield "pre"
    segs = []
    for c0 in range(0, D_PROJ, LRU_WIDTH):
        segs.append(_dot(hb, win_ref[:, c0:c0 + LRU_WIDTH]))
        yield "in"
    u, y_lru, q, f_raw, v, g = segs

    n_grp = tb // V7X_SUBLANES
    grouped = lambda z: z.reshape(n_grp, V7X_SUBLANES, z.shape[-1])
    flat = lambda z: z.reshape(tb, z.shape[-1])
    gshape = (n_grp, V7X_SUBLANES, LRU_WIDTH)
    sub = lax.broadcasted_iota(jnp.int32, gshape, 1)
    grp = lax.broadcasted_iota(jnp.int32, gshape, 0)

    u3 = grouped(u)
    u_all = jnp.concatenate([ucar_ref[...][None], u3], axis=0)
    ucar_ref[...] = u3[n_grp - 1]
    cw = cw_ref[...]
    u_conv = cb_ref[...]
    for k in range(CONV_WIDTH - 1):
        back = CONV_WIDTH - 1 - k
        rot = pltpu.roll(u_all, back, 1)
        u_conv = u_conv + jnp.where(sub >= back, rot[1:], rot[:-1]) * cw[k:k + 1]
    u_conv = flat(u_conv + u3 * cw[CONV_WIDTH - 1:CONV_WIDTH])
    yield "vec"

    a, mult, ig = _lru_gates(u_conv, wa_ref, ba_ref[...], wx_ref, bx_ref[...], lam_ref[...])
    yield "vec"
    a3, mult3 = grouped(a), grouped(mult)
    first_row = jnp.where(t == 0, 0, -1)
    mult3 = jnp.where(grp * V7X_SUBLANES + sub == first_row, 1.0, mult3)
    b3 = (mult3 * grouped(ig)) * grouped(u_conv)
    s = 1
    while s < V7X_SUBLANES:
        keep = sub >= s
        b3 = jnp.where(keep, a3 * pltpu.roll(b3, s, 1) + b3, b3)
        a3 = jnp.where(keep, a3 * pltpu.roll(a3, s, 1), a3)
        s *= 2
    yield "vec"
    carry = hcar_ref[...]
    hs = []
    for gi in range(n_grp):
        h_g = a3[gi] * carry + b3[gi]
        carry = h_g[V7X_SUBLANES - 1:V7X_SUBLANES]
        hs.append(h_g)
    hcar_ref[...] = carry
    o_lru = jnp.concatenate(hs, axis=0) * _gelu_tanh(y_lru)
    yield "vec"

    lb = _lower_bound(lbl_ref[...])
    f3 = grouped(lb + (1.0 - lb) * _sigmoid(f_raw))
    kk3 = 1.0 - f3
    qs3 = grouped(q * (HG_HEAD_DIM ** -0.5))
    vb = v.astype(BF16)
    hgw = hgw_ref[...]

    pfx = f3
    s = 1
    while s < V7X_SUBLANES:
        pfx = jnp.where(sub >= s, pfx * pltpu.roll(pfx, s, 1), pfx)
        s *= 2
    grp_per_chunk = HG_CHUNK // V7X_SUBLANES
    rows, tails = [], []
    for gi in range(n_grp):
        p_g = pfx[gi] if gi % grp_per_chunk == 0 else pfx[gi] * carry
        carry = p_g[V7X_SUBLANES - 1:V7X_SUBLANES]
        rows.append(p_g)
        if gi % grp_per_chunk == grp_per_chunk - 1:
            tails.append(carry)
    pfx = jnp.concatenate(rows, axis=0)
    direct = jnp.min(jnp.concatenate(tails, axis=0)) >= HG_DIRECT_MIN
    yield "vec"

    @pl.when(direct)
    def _():
        _hgrn_chunks_direct(flat(qs3), flat(kk3), pfx, tails, vb, g, hgw, st_ref, ohg_ref)

    @pl.when(jnp.logical_not(direct))
    def _():
        _hgrn_chunks_levels(f3, kk3, qs3, vb, g, hgw, st_ref, ohg_ref)
    yield "chunk"

    mix_in = jnp.concatenate([o_lru, ohg_ref[...]], axis=-1).astype(BF16)
    mix = _dot(mix_in, wout_ref[...])
    o_ref[...] = x + m[2:3] * _rms(mix, post_ref[...])


def _mix_prompt_kernel(x_ref, m_ref, *refs):
    weights = refs[:13]
    o_ref, h_out_ref, conv_out_ref, s_out_ref, ucar_ref, hcar_ref, st_ref, ohg_ref = refs[13:]
    t = pl.program_id(1)
    n_t = pl.num_programs(1)

    @pl.when(t == 0)
    def _():
        ucar_ref[...] = jnp.zeros_like(ucar_ref)
        hcar_ref[...] = jnp.zeros_like(hcar_ref)
        st_ref[...] = jnp.zeros_like(st_ref)

    tiles = [_mix_tile(x_ref.at[sq], m_ref.at[sq], o_ref.at[sq], t, *weights,
                       ucar_ref.at[sq], hcar_ref.at[sq], st_ref.at[sq], ohg_ref.at[sq])
             for sq in range(MIX_SEQS)]
    waiting, running, tick = tiles, [], 0
    while waiting or running:
        if waiting and tick % MIX_STAGE_LAG == 0:
            running.append(waiting.pop(0))
        running = [tile for tile in running if next(tile, tile) is not tile]
        tick += 1

    @pl.when(t == n_t - 1)
    def _():
        for sq in range(MIX_SEQS):
            h_out_ref[sq] = hcar_ref[sq]
            conv_out_ref[sq] = ucar_ref[sq, V7X_SUBLANES - (CONV_WIDTH - 1):V7X_SUBLANES, :]
            for hd in range(HG_HEADS):
                s_out_ref[sq, hd] = st_ref[sq, hd].T


def _mix_prompt(x, mod3, p):
    batch, seq_len, _ = x.shape
    n_t = seq_len // MIX_TILE
    nsq = MIX_SEQS
    tok_spec = pl.BlockSpec((nsq, MIX_TILE, D_MODEL), lambda b, t: (b, t, 0))
    return pl.pallas_call(
        _mix_prompt_kernel,
        grid=(batch // nsq, n_t),
        in_specs=[
            tok_spec,
            pl.BlockSpec((nsq, 3, D_MODEL), lambda b, t: (b, 0, 0)),
            _const_spec((1, D_MODEL)),
            _const_spec((1, D_MODEL)),
            _const_spec((D_MODEL, D_PROJ)),
            _const_spec((CONV_WIDTH, LRU_WIDTH)),
            _const_spec((1, LRU_WIDTH)),
            _const_spec((LRU_WIDTH, LRU_WIDTH)),
            _const_spec((1, LRU_WIDTH)),
            _const_spec((LRU_WIDTH, LRU_WIDTH)),
            _const_spec((1, LRU_WIDTH)),
            _const_spec((1, LRU_WIDTH)),
            _const_spec(p["lb_logits"].shape),
            _const_spec((1, HG_HEAD_DIM)),
            _const_spec((D_MODEL, D_MODEL)),
        ],
        out_specs=[
            tok_spec,
            pl.BlockSpec((nsq, 1, LRU_WIDTH), lambda b, t: (b, 0, 0)),
            pl.BlockSpec((nsq, CONV_WIDTH - 1, LRU_WIDTH), lambda b, t: (b, 0, 0)),
            pl.BlockSpec((nsq, HG_HEADS, HG_HEAD_DIM, HG_HEAD_DIM), lambda b, t: (b, 0, 0, 0)),
        ],
        out_shape=[
            jax.ShapeDtypeStruct((batch, seq_len, D_MODEL), F32),
            jax.ShapeDtypeStruct((batch, 1, LRU_WIDTH), F32),
            jax.ShapeDtypeStruct((batch, CONV_WIDTH - 1, LRU_WIDTH), F32),
            jax.ShapeDtypeStruct((batch, HG_HEADS, HG_HEAD_DIM, HG_HEAD_DIM), F32),
        ],
        scratch_shapes=[
            pltpu.VMEM((nsq, V7X_SUBLANES, LRU_WIDTH), F32),
            pltpu.VMEM((nsq, 1, LRU_WIDTH), F32),
            pltpu.VMEM((nsq, HG_HEADS, HG_HEAD_DIM, HG_HEAD_DIM), F32),
            pltpu.VMEM((nsq, MIX_TILE, HG_WIDTH), F32),
        ],
        compiler_params=pltpu.CompilerParams(
            dimension_semantics=("arbitrary", "arbitrary"), vmem_limit_bytes=VMEM_LIMIT),
        name="mix_prompt",
    )(x, mod3, p["ln_mix_pre"], p["ln_mix_post"], p["w_in"], p["conv_w"], p["conv_b"],
      p["wa_bd"], p["b_a"], p["wx_bd"], p["b_x"], p["lam"], p["lb_logits"], p["hg_norm_w"],
      p["w_out"])


def _mix_sample_kernel(x_ref, m_ref, pre_ref, post_ref, win_ref, cw_ref, cb_ref,
                       wa_ref, ba_ref, wx_ref, bx_ref, lam_ref, lbl_ref, hgw_ref, wout_ref,
                       h0_ref, conv0_ref, s0_ref,
                       o_ref, h_out_ref, conv_out_ref, s_out_ref,
                       qf_ref, f_ref, k_ref, v_ref, g_ref, qk_ref, olru_ref, ohg_ref, ostage_ref):
    i = pl.program_id(0)
    n_i = pl.num_programs(0)
    nb = SAMPLE_SEQ_BLOCK

    @pl.when(i == 0)
    def _():
        x = x_ref[...]
        h = _rms(x, pre_ref[...]) * (1.0 + m_ref[1]) + m_ref[0]
        proj = _dot(h.astype(BF16), win_ref[...])
        u = proj[:, 0:LRU_WIDTH]
        y_lru = proj[:, LRU_WIDTH:2 * LRU_WIDTH]
        o0 = 2 * LRU_WIDTH
        q = proj[:, o0:o0 + HG_WIDTH]
        f_raw = proj[:, o0 + HG_WIDTH:o0 + 2 * HG_WIDTH]
        v = proj[:, o0 + 2 * HG_WIDTH:o0 + 3 * HG_WIDTH]
        g_ref[...] = proj[:, o0 + 3 * HG_WIDTH:o0 + 4 * HG_WIDTH]

        cw = cw_ref[...]
        u_conv = cb_ref[...]
        for k in range(CONV_WIDTH - 1):
            u_conv = u_conv + conv0_ref[k] * cw[k:k + 1]
            if k > 0:
                conv_out_ref[k - 1] = conv0_ref[k]
        u_conv = u_conv + u * cw[CONV_WIDTH - 1:CONV_WIDTH]
        conv_out_ref[CONV_WIDTH - 2] = u

        a, mult, ig = _lru_gates(u_conv, wa_ref, ba_ref[...], wx_ref, bx_ref[...], lam_ref[...])
        hs = (mult * ig) * u_conv + a * h0_ref[...]
        h_out_ref[...] = hs
        olru_ref[...] = hs * _gelu_tanh(y_lru)

        lb = _lower_bound(lbl_ref[...])
        f = lb + (1.0 - lb) * _sigmoid(f_raw)
        kk = 1.0 - f
        qs = q * (HG_HEAD_DIM ** -0.5)
        f_ref[...] = f
        qf_ref[...] = qs * f
        k_ref[...] = kk
        v_ref[...] = v
        qk = qs * kk
        for hd in range(HG_HEADS):
            l0 = hd * HG_HEAD_DIM
            tot = jnp.sum(qk[:, l0:l0 + HG_HEAD_DIM], axis=-1, keepdims=True)
            qk_ref[:, l0:l0 + HG_HEAD_DIM] = jnp.broadcast_to(tot, (qk.shape[0], HG_HEAD_DIM))

    base = pl.multiple_of(i * nb, nb)
    grp = lambda ref: ref[pl.ds(base, nb), :]
    f_g, k_g, v_g, qf_g, qk_g = grp(f_ref), grp(k_ref), grp(v_ref), grp(qf_ref), grp(qk_ref)
    square = (HG_HEAD_DIM, HG_HEAD_DIM)
    for j in range(nb):
        for hd in range(HG_HEADS):
            l0 = hd * HG_HEAD_DIM
            rowv = lambda z: z[j:j + 1, l0:l0 + HG_HEAD_DIM]
            s_old = s0_ref[j, hd]
            f_col = jnp.broadcast_to(rowv(f_g), square).T
            k_col = jnp.broadcast_to(rowv(k_g), square).T
            v_row = rowv(v_g)
            outer = k_col.astype(BF16).astype(F32) * v_row.astype(BF16).astype(F32)
            s_out_ref[j, hd] = f_col * s_old + outer
            qf8 = jnp.broadcast_to(rowv(qf_g), (V7X_SUBLANES, HG_HEAD_DIM)).astype(BF16)
            o1 = _dot(qf8, s_old.astype(BF16))[0:1]
            ostage_ref[j:j + 1, l0:l0 + HG_HEAD_DIM] = o1 + rowv(qk_g) * v_row
    ohg_ref[pl.ds(base, nb), :] = ostage_ref[...]

    @pl.when(i == n_i - 1)
    def _():
        hgw = hgw_ref[...]
        parts = [olru_ref[...]]
        for hd in range(HG_HEADS):
            l0 = hd * HG_HEAD_DIM
            parts.append(_head_rms_gate(ohg_ref[:, l0:l0 + HG_HEAD_DIM], hgw,
                                        g_ref[:, l0:l0 + HG_HEAD_DIM]))
        mix_in = jnp.concatenate(parts, axis=-1).astype(BF16)
        mix = _dot(mix_in, wout_ref[...])
        o_ref[...] = x_ref[...] + m_ref[2] * _rms(mix, post_ref[...])


def _mix_sample(x, mod9, p, h0, conv0, s0):
    n_seq = x.shape[0]
    nb = SAMPLE_SEQ_BLOCK
    state_spec = pl.BlockSpec((nb, HG_HEADS, HG_HEAD_DIM, HG_HEAD_DIM), lambda i: (i, 0, 0, 0))
    full2 = lambda shape: pl.BlockSpec(shape, lambda i: (0,) * len(shape))
    return pl.pallas_call(
        _mix_sample_kernel,
        grid=(n_seq // nb,),
        in_specs=[
            _const_spec((n_seq, D_MODEL)),
            pl.BlockSpec((3, n_seq, D_MODEL), lambda i: (1, 0, 0), pipeline_mode=pl.Buffered(1)),
            _const_spec((1, D_MODEL)),
            _const_spec((1, D_MODEL)),
            _const_spec((D_MODEL, D_PROJ)),
            _const_spec((CONV_WIDTH, LRU_WIDTH)),
            _const_spec((1, LRU_WIDTH)),
            _const_spec((LRU_WIDTH, LRU_WIDTH)),
            _const_spec((1, LRU_WIDTH)),
            _const_spec((LRU_WIDTH, LRU_WIDTH)),
            _const_spec((1, LRU_WIDTH)),
            _const_spec((1, LRU_WIDTH)),
            _const_spec(p["lb_logits"].shape),
            _const_spec((1, HG_HEAD_DIM)),
            _const_spec((D_MODEL, D_MODEL)),
            _const_spec((n_seq, LRU_WIDTH)),
            _const_spec((CONV_WIDTH - 1, n_seq, LRU_WIDTH)),
            state_spec,
        ],
        out_specs=[
            full2((n_seq, D_MODEL)),
            full2((n_seq, LRU_WIDTH)),
            full2((CONV_WIDTH - 1, n_seq, LRU_WIDTH)),
            state_spec,
        ],
        out_shape=[
            jax.ShapeDtypeStruct((n_seq, D_MODEL), F32),
            jax.ShapeDtypeStruct((n_seq, LRU_WIDTH), F32),
            jax.ShapeDtypeStruct((CONV_WIDTH - 1, n_seq, LRU_WIDTH), F32),
            jax.ShapeDtypeStruct((n_seq, HG_HEADS, HG_HEAD_DIM, HG_HEAD_DIM), F32),
        ],
        scratch_shapes=[pltpu.VMEM((n_seq, HG_WIDTH), F32) for _ in range(8)]
        + [pltpu.VMEM((nb, HG_WIDTH), F32)],
        compiler_params=pltpu.CompilerParams(
            dimension_semantics=("arbitrary",), vmem_limit_bytes=VMEM_LIMIT),
        name="mix_sample",
    )(x, mod9, p["ln_mix_pre"], p["ln_mix_post"], p["w_in"], p["conv_w"], p["conv_b"],
      p["wa_bd"], p["b_a"], p["wx_bd"], p["b_x"], p["lam"], p["lb_logits"], p["hg_norm_w"],
      p["w_out"], h0, conv0, s0)


def _block_diag(w):
    heads, blk, _ = w.shape
    eye = jnp.eye(heads, dtype=w.dtype)
    return (eye[:, None, :, None] * w[:, :, None, :]).reshape(heads * blk, heads * blk)


def kernel(x_prompt, x_sample, c_prompt, c_sample, state_lru_h, state_lru_conv, state_hgrn_S, w_ada, b_ada, ln_ffn1_pre, ln_ffn1_post, ffn1_w_gate, ffn1_w_up, ffn1_w_down, ln_mix_pre, ln_mix_post, w_in, lru_conv_w, lru_conv_b, lru_w_a, lru_b_a, lru_w_x, lru_b_x, lru_lambda, hg_lb_logits, hg_norm_w, w_out, ln_ffn2_pre, ln_ffn2_post, ffn2_w_gate, ffn2_w_up, ffn2_w_down):
    depth = w_ada.shape[0]
    batch, seq_len, _ = x_prompt.shape
    n_seq = x_sample.shape[0]
    assert depth == 1 and x_sample.shape[1] == 1
    assert seq_len % FFN_TILE == 0 and seq_len % MIX_TILE == 0 and n_seq % SAMPLE_SEQ_BLOCK == 0
    assert batch % MIX_SEQS == 0

    xp = x_prompt.reshape(batch * seq_len, D_MODEL)
    xs = x_sample.reshape(n_seq, D_MODEL)
    ph, pc, pS, sh, sc, sS = [], [], [], [], [], []
    for l in range(depth):
        row = lambda w: w[l].reshape(1, -1)
        mod9 = _ada(jnp.concatenate([c_sample, c_prompt], axis=0), w_ada[l], b_ada[l])
        mod_p = jnp.transpose(mod9[:, n_seq:], (1, 0, 2))
        p = {
            "ln_mix_pre": row(ln_mix_pre), "ln_mix_post": row(ln_mix_post),
            "w_in": w_in[l].astype(BF16), "w_out": w_out[l].astype(BF16),
            "conv_w": lru_conv_w[l], "conv_b": row(lru_conv_b),
            "wa_bd": _block_diag(lru_w_a[l]).astype(BF16), "b_a": row(lru_b_a),
            "wx_bd": _block_diag(lru_w_x[l]).astype(BF16), "b_x": row(lru_b_x),
            "lam": row(lru_lambda), "lb_logits": hg_lb_logits,
            "hg_norm_w": row(hg_norm_w),
        }
        f1 = (row(ln_ffn1_pre), row(ln_ffn1_post), ffn1_w_gate[l], ffn1_w_up[l], ffn1_w_down[l])
        f2 = (row(ln_ffn2_pre), row(ln_ffn2_post), ffn2_w_gate[l], ffn2_w_up[l], ffn2_w_down[l])

        xp, xs = _ffn(xp, xs, mod_p[:, 0:3], mod9, 0, *f1, seq_len)
        xp, h_p, c_p, S_p = _mix_prompt(xp.reshape(batch, seq_len, D_MODEL), mod_p[:, 3:6], p)
        xp = xp.reshape(batch * seq_len, D_MODEL)
        conv0 = jnp.transpose(state_lru_conv[l], (1, 0, 2))
        xs, h_s, c_s, S_s = _mix_sample(xs, mod9, p, state_lru_h[l], conv0, state_hgrn_S[l])
        xp, xs = _ffn(xp, xs, mod_p[:, 6:9], mod9, 2, *f2, seq_len)

        ph.append(h_p.reshape(batch, LRU_WIDTH)); pc.append(c_p); pS.append(S_p)
        sh.append(h_s); sc.append(jnp.transpose(c_s, (1, 0, 2))); sS.append(S_s)

    return (xp.reshape(batch, seq_len, D_MODEL), xs.reshape(n_seq, 1, D_MODEL),
            jnp.stack(ph), jnp.stack(pc), jnp.stack(pS), jnp.stack(sh), jnp.stack(sc), jnp.stack(sS))
```

```python
import jax
import jax.numpy as jnp
from jax import lax
from jax.experimental import pallas as pl
from jax.experimental.pallas import tpu as pltpu

F32 = jnp.float32
BF16 = jnp.bfloat16

D_MODEL = 1024
D_FF = 2816
LRU_WIDTH = 512
CONV_WIDTH = 4
LRU_C = 8.0
HG_WIDTH = 512
HG_HEAD_DIM = 128
HG_HEADS = HG_WIDTH // HG_HEAD_DIM
HG_CHUNK = 64
N_MOD = 9
D_PROJ = 2 * LRU_WIDTH + 4 * HG_WIDTH
EPS = 1e-6

V7X_SUBLANES = 8
V7X_VMEM_BYTES = 64 * 1024 * 1024
VMEM_LIMIT = V7X_VMEM_BYTES - 8 * 1024 * 1024

FFN_TILE = 1024
FFN_ROWS = 256
MIX_TILE = 256
MIX_SEQS = 1
MIX_STAGE_LAG = 7
FFN_WBLOCK = 256
FFN_WSTEPS = D_FF // FFN_WBLOCK
SAMPLE_SEQ_BLOCK = V7X_SUBLANES

HG_LEVELS = (1, 2, 4, 8, 16, 32)
HG_DIRECT_MIN = 2.0 ** -100


def _rms(x, w):
    return (x * lax.rsqrt(jnp.mean(x * x, axis=-1, keepdims=True) + EPS)) * w


def _sigmoid(x):
    return 1.0 / (1.0 + jnp.exp(-x))


def _silu(x):
    return x * _sigmoid(x)


def _gelu_tanh(x):
    c = 0.7978845608028654
    return x * (0.5 * (1.0 + jnp.tanh(c * (x + 0.044715 * (x * x * x)))))


def _softplus(z):
    return jnp.maximum(z, 0.0) + jnp.log1p(jnp.exp(-jnp.abs(z)))


def _dot(a, b):
    return jnp.dot(a, b, preferred_element_type=F32)


def _dot_nt(a, b):
    return lax.dot_general(a, b, (((1,), (1,)), ((), ())), preferred_element_type=F32)


def _dot_tn(a, b):
    return lax.dot_general(a, b, (((0,), (0,)), ((), ())), preferred_element_type=F32)


def _const_spec(shape):
    nd = len(shape)
    return pl.BlockSpec(shape, lambda *_: (0,) * nd, pipeline_mode=pl.Buffered(1))


def _ada_kernel(c_ref, w_ref, b_ref, o_ref):
    s = _silu(c_ref[...])
    o_ref[0] = _dot(s.astype(BF16), w_ref[...].astype(BF16)) + b_ref[...]


def _ada(c_all, w_ada, b_ada):
    rows = c_all.shape[0]
    return pl.pallas_call(
        _ada_kernel,
        grid=(N_MOD,),
        in_specs=[
            pl.BlockSpec((rows, D_MODEL), lambda j: (0, 0)),
            pl.BlockSpec((D_MODEL, D_MODEL), lambda j: (0, j)),
            pl.BlockSpec((1, D_MODEL), lambda j: (0, j)),
        ],
        out_specs=pl.BlockSpec((1, rows, D_MODEL), lambda j: (j, 0, 0)),
        out_shape=jax.ShapeDtypeStruct((N_MOD, rows, D_MODEL), F32),
        compiler_params=pltpu.CompilerParams(
            dimension_semantics=("arbitrary",), vmem_limit_bytes=VMEM_LIMIT),
        name="ada_mod",
    )(c_all, w_ada, b_ada.reshape(1, N_MOD * D_MODEL))


def _ffn_body(x, shift, scale, gate, pre_w, post_w, wg_ref, wu_ref, wd_ref):
    h = _rms(x, pre_w) * (1.0 + scale) + shift
    hb = h.astype(BF16)
    a = _dot(hb, wg_ref[...])
    u = _dot(hb, wu_ref[...])
    act = (_silu(a) * u).astype(BF16)
    y = _dot(act, wd_ref[...])
    return x + (0.5 * gate) * _rms(y, post_w)


def _ffn_kernel(xp_ref, mp_ref, xs_ref, ms_ref, pre_ref, post_ref, wg32_ref, wu32_ref, wd32_ref,
                op_ref, os_ref, wg_ref, wu_ref, wd_ref):
    s = pl.program_id(0)
    n_steps = pl.num_programs(0)

    for j in range(FFN_WSTEPS):
        @pl.when(s == j)
        def _(j=j):
            c0 = j * FFN_WBLOCK
            wg_ref[:, c0:c0 + FFN_WBLOCK] = wg32_ref[...].astype(BF16)
            wu_ref[:, c0:c0 + FFN_WBLOCK] = wu32_ref[...].astype(BF16)
            wd_ref[c0:c0 + FFN_WBLOCK, :] = wd32_ref[...].astype(BF16)

    @pl.when((s >= FFN_WSTEPS) & (s < n_steps - 1))
    def _():
        m = mp_ref[0]
        for r0 in range(0, FFN_TILE, FFN_ROWS):
            op_ref[r0:r0 + FFN_ROWS, :] = _ffn_body(
                xp_ref[r0:r0 + FFN_ROWS, :], m[0:1], m[1:2], m[2:3],
                pre_ref[...], post_ref[...], wg_ref, wu_ref, wd_ref)

    @pl.when(s == n_steps - 1)
    def _():
        os_ref[...] = _ffn_body(xs_ref[...], ms_ref[0], ms_ref[1], ms_ref[2], pre_ref[...],
                                post_ref[...], wg_ref, wu_ref, wd_ref)


def _ffn(xp, xs, mod_p3, mod9, sub_layer, pre_w, post_w, wg, wu, wd, seq_len):
    n_tok = xp.shape[0]
    n_seq = xs.shape[0]
    n_tiles = n_tok // FFN_TILE
    steps_per_seq = seq_len // FFN_TILE
    tile = lambda s: jnp.clip(s - FFN_WSTEPS, 0, n_tiles - 1)
    wblk = lambda s: jnp.minimum(s, FFN_WSTEPS - 1)
    tok_spec = pl.BlockSpec((FFN_TILE, D_MODEL), lambda s: (tile(s), 0))
    return pl.pallas_call(
        _ffn_kernel,
        grid=(FFN_WSTEPS + n_tiles + 1,),
        in_specs=[
            tok_spec,
            pl.BlockSpec((1, 3, D_MODEL), lambda s: (tile(s) // steps_per_seq, 0, 0)),
            _const_spec((n_seq, D_MODEL)),
            pl.BlockSpec((3, n_seq, D_MODEL), lambda s: (sub_layer, 0, 0), pipeline_mode=pl.Buffered(1)),
            _const_spec((1, D_MODEL)),
            _const_spec((1, D_MODEL)),
            pl.BlockSpec((D_MODEL, FFN_WBLOCK), lambda s: (0, wblk(s))),
            pl.BlockSpec((D_MODEL, FFN_WBLOCK), lambda s: (0, wblk(s))),
            pl.BlockSpec((FFN_WBLOCK, D_MODEL), lambda s: (wblk(s), 0)),
        ],
        out_specs=[tok_spec, pl.BlockSpec((n_seq, D_MODEL), lambda s: (0, 0))],
        out_shape=[jax.ShapeDtypeStruct((n_tok, D_MODEL), F32),
                   jax.ShapeDtypeStruct((n_seq, D_MODEL), F32)],
        scratch_shapes=[pltpu.VMEM((D_MODEL, D_FF), BF16), pltpu.VMEM((D_MODEL, D_FF), BF16),
                        pltpu.VMEM((D_FF, D_MODEL), BF16)],
        compiler_params=pltpu.CompilerParams(
            dimension_semantics=("arbitrary",), vmem_limit_bytes=VMEM_LIMIT),
        name="ffn",
    )(xp, mod_p3, xs, mod9, pre_w, post_w, wg, wu, wd)


def _lower_bound(lb_logits):
    z = lb_logits - jnp.max(lb_logits, axis=0, keepdims=True)
    e = jnp.exp(z)
    return e[0:1] / jnp.sum(e, axis=0, keepdims=True)


def _lru_gates(u_conv, wa_ref, ba, wx_ref, bx, lam):
    ub = u_conv.astype(BF16)
    r = _sigmoid(_dot(ub, wa_ref[...]) + ba)
    ig = _sigmoid(_dot(ub, wx_ref[...]) + bx)
    log_a = (-LRU_C * r) * _softplus(-lam)
    a = jnp.exp(log_a)
    th = jnp.tanh(log_a)
    mult = jnp.sqrt((-2.0 * th) / (1.0 - th))
    return a, mult, ig


def _head_rms_gate(o, norm_w, g):
    return _rms(o, norm_w) * _silu(g)


def _hgrn_chunk_head(scores, q_in, k_out, dec, vb_blk, g_blk, hgw, st_ref, hd, ohg_ref, r0, l0):
    st = st_ref[hd]
    o = _dot_nt(q_in, st.astype(BF16)) + _dot(scores.astype(BF16), vb_blk)
    st_ref[hd] = st * dec + _dot_tn(vb_blk, k_out)
    ohg_ref[r0:r0 + HG_CHUNK, l0:l0 + HG_HEAD_DIM] = _head_rms_gate(o, hgw, g_blk)


def _hgrn_chunks_direct(qs, kk, pfx, tails, vb, g, hgw, st_ref, ohg_ref):
    tb = qs.shape[0]
    n_chunk = len(tails)
    inv = 1.0 / pfx
    q_dec = qs * pfx
    k_dec = kk * inv
    q_in = q_dec.astype(BF16)
    k_in = k_dec.astype(BF16)
    ti = lax.broadcasted_iota(jnp.int32, (tb, tb), 0)
    si = lax.broadcasted_iota(jnp.int32, (tb, tb), 1)
    visible = (ti >= si) & ((ti ^ si) < HG_CHUNK)
    row_chunk = jnp.right_shift(lax.broadcasted_iota(jnp.int32, (tb, HG_HEAD_DIM), 0),
                                HG_CHUNK.bit_length() - 1)
    tail_rows = jnp.concatenate([jnp.broadcast_to(t, (HG_CHUNK, t.shape[-1])) for t in tails], axis=0)
    k_end = k_dec * tail_rows
    for hd in range(HG_HEADS):
        l0 = hd * HG_HEAD_DIM
        head = lambda z: z[:, l0:l0 + HG_HEAD_DIM]
        by_chunk = lambda z: jnp.concatenate(
            [jnp.where(row_chunk == c, head(z), 0.0) for c in range(n_chunk)], axis=-1).astype(BF16)
        scores = jnp.where(visible, _dot_nt(head(q_in), head(k_in)), 0.0)
        grown = _dot_tn(head(vb), by_chunk(k_end))
        st = st_ref[hd]
        states = []
        for c, tail in enumerate(tails):
            states.append(st.astype(BF16))
            st = st * head(tail) + grown[:, c * HG_HEAD_DIM:(c + 1) * HG_HEAD_DIM]
        st_ref[hd] = st
        o = (_dot_nt(by_chunk(q_dec), jnp.concatenate(states, axis=-1))
             + _dot(scores.astype(BF16), head(vb)))
        ohg_ref[:, l0:l0 + HG_HEAD_DIM] = _head_rms_gate(o, hgw, head(g))


def _hgrn_chunks_levels(f3, kk3, qs3, vb, g, hgw, st_ref, ohg_ref):
    n_grp = f3.shape[0]
    tb = n_grp * V7X_SUBLANES
    sub = lax.broadcasted_iota(jnp.int32, f3.shape, 1)
    grp = lax.broadcasted_iota(jnp.int32, f3.shape, 0)
    to_mxu = lambda z: z.reshape(tb, z.shape[-1]).astype(BF16)
    q_lv = [to_mxu(qs3)]
    k_lv = [to_mxu(kk3)]
    tot, pre, post = f3, f3, None
    for lvl in HG_LEVELS:
        q_lv.append(to_mxu(qs3 * pre))
        k_lv.append(k_lv[0] if post is None else to_mxu(kk3 * post))
        if lvl < V7X_SUBLANES:
            upper = (sub & lvl) != 0
            below = pltpu.roll(tot, lvl, 1)
            above = pltpu.roll(tot, V7X_SUBLANES - lvl, 1)
        else:
            upper = (grp & (lvl // V7X_SUBLANES)) != 0
            below = jnp.roll(tot, lvl // V7X_SUBLANES, axis=0)
            above = jnp.roll(tot, -(lvl // V7X_SUBLANES), axis=0)
        pre = jnp.where(upper, pre * below, pre)
        post = jnp.where(upper, 1.0, above) if post is None else jnp.where(upper, post, post * above)
        tot = tot * jnp.where(upper, below, above)
    q_in = to_mxu(qs3 * pre)
    k_out = to_mxu(kk3 * post)
    s_decay = tot.reshape(tb, tot.shape[-1])

    ti = lax.broadcasted_iota(jnp.int32, (HG_CHUNK, HG_CHUNK), 0)
    si = lax.broadcasted_iota(jnp.int32, (HG_CHUNK, HG_CHUNK), 1)
    masks = [ti == si]
    tx = ti ^ si
    for lvl in HG_LEVELS:
        masks.append((tx >= lvl) & (tx < 2 * lvl) & ((ti & lvl) != 0))

    for c in range(tb // HG_CHUNK):
        r0 = c * HG_CHUNK
        for hd in range(HG_HEADS):
            l0 = hd * HG_HEAD_DIM
            blk = lambda z: z[r0:r0 + HG_CHUNK, l0:l0 + HG_HEAD_DIM]
            scores = jnp.zeros((HG_CHUNK, HG_CHUNK), F32)
            for ql, kl, msk in zip(q_lv, k_lv, masks):
                scores = jnp.where(msk, _dot_nt(blk(ql), blk(kl)), scores)
            dec = s_decay[r0 + HG_CHUNK - 1:r0 + HG_CHUNK, l0:l0 + HG_HEAD_DIM]
            _hgrn_chunk_head(scores, blk(q_in), blk(k_out), dec, blk(vb), blk(g), hgw,
                             st_ref, hd, ohg_ref, r0, l0)

def _mix_tile(x_ref, m_ref, o_ref, t, pre_ref, post_ref, win_ref, cw_ref, cb_ref, wa_ref, ba_ref,
              wx_ref, bx_ref, lam_ref, lbl_ref, hgw_ref, wout_ref, ucar_ref, hcar_ref, st_ref, ohg_ref):
    tb = MIX_TILE
    x = x_ref[...]
    m = m_ref[...]
    hb = (_rms(x, pre_ref[...]) * (1.0 + m[1:2]) + m[0:1]).astype(BF16)
    yield "pre"
    segs = []
    for c0 in range(0, D_PROJ, LRU_WIDTH):
        segs.append(_dot(hb, win_ref[:, c0:c0 + LRU_WIDTH]))
        yield "in"
    u, y_lru, q, f_raw, v, g = segs

    n_grp = tb // V7X_SUBLANES
    grouped = lambda z: z.reshape(n_grp, V7X_SUBLANES, z.shape[-1])
    flat = lambda z: z.reshape(tb, z.shape[-1])
    gshape = (n_grp, V7X_SUBLANES, LRU_WIDTH)
    sub = lax.broadcasted_iota(jnp.int32, gshape, 1)
    grp = lax.broadcasted_iota(jnp.int32, gshape, 0)

    u3 = grouped(u)
    u_all = jnp.concatenate([ucar_ref[...][None], u3], axis=0)
    ucar_ref[...] = u3[n_grp - 1]
    cw = cw_ref[...]
    u_conv = cb_ref[...]
    for k in range(CONV_WIDTH - 1):
        back = CONV_WIDTH - 1 - k
        rot = pltpu.roll(u_all, back, 1)
        u_conv = u_conv + jnp.where(sub >= back, rot[1:], rot[:-1]) * cw[k:k + 1]
    u_conv = flat(u_conv + u3 * cw[CONV_WIDTH - 1:CONV_WIDTH])
    yield "vec"

    a, mult, ig = _lru_gates(u_conv, wa_ref, ba_ref[...], wx_ref, bx_ref[...], lam_ref[...])
    yield "vec"
    a3, mult3 = grouped(a), grouped(mult)
    first_row = jnp.where(t == 0, 0, -1)
    mult3 = jnp.where(grp * V7X_SUBLANES + sub == first_row, 1.0, mult3)
    b3 = (mult3 * grouped(ig)) * grouped(u_conv)
    s = 1
    while s < V7X_SUBLANES:
        keep = sub >= s
        b3 = jnp.where(keep, a3 * pltpu.roll(b3, s, 1) + b3, b3)
        a3 = jnp.where(keep, a3 * pltpu.roll(a3, s, 1), a3)
        s *= 2
    yield "vec"
    carry = hcar_ref[...]
    hs = []
    for gi in range(n_grp):
        h_g = a3[gi] * carry + b3[gi]
        carry = h_g[V7X_SUBLANES - 1:V7X_SUBLANES]
        hs.append(h_g)
    hcar_ref[...] = carry
    o_lru = jnp.concatenate(hs, axis=0) * _gelu_tanh(y_lru)
    yield "vec"

    lb = _lower_bound(lbl_ref[...])
    f3 = grouped(lb + (1.0 - lb) * _sigmoid(f_raw))
    kk3 = 1.0 - f3
    qs3 = grouped(q * (HG_HEAD_DIM ** -0.5))
    vb = v.astype(BF16)
    hgw = hgw_ref[...]

    pfx = f3
    s = 1
    while s < V7X_SUBLANES:
        pfx = jnp.where(sub >= s, pfx * pltpu.roll(pfx, s, 1), pfx)
        s *= 2
    grp_per_chunk = HG_CHUNK // V7X_SUBLANES
    rows, tails = [], []
    for gi in range(n_grp):
        p_g = pfx[gi] if gi % grp_per_chunk == 0 else pfx[gi] * carry
        carry = p_g[V7X_SUBLANES - 1:V7X_SUBLANES]
        rows.append(p_g)
        if gi % grp_per_chunk == grp_per_chunk - 1:
            tails.append(carry)
    pfx = jnp.concatenate(rows, axis=0)
    direct = jnp.min(jnp.concatenate(tails, axis=0)) >= HG_DIRECT_MIN
    yield "vec"

    @pl.when(direct)
    def _():
        _hgrn_chunks_direct(flat(qs3), flat(kk3), pfx, tails, vb, g, hgw, st_ref, ohg_ref)

    @pl.when(jnp.logical_not(direct))
    def _():
        _hgrn_chunks_levels(f3, kk3, qs3, vb, g, hgw, st_ref, ohg_ref)
    yield "chunk"

    mix_in = jnp.concatenate([o_lru, ohg_ref[...]], axis=-1).astype(BF16)
    mix = _dot(mix_in, wout_ref[...])
    o_ref[...] = x + m[2:3] * _rms(mix, post_ref[...])


def _mix_prompt_kernel(x_ref, m_ref, *refs):
    weights = refs[:13]
    o_ref, h_out_ref, conv_out_ref, s_out_ref, ucar_ref, hcar_ref, st_ref, ohg_ref = refs[13:]
    t = pl.program_id(1)
    n_t = pl.num_programs(1)

    @pl.when(t == 0)
    def _():
        ucar_ref[...] = jnp.zeros_like(ucar_ref)
        hcar_ref[...] = jnp.zeros_like(hcar_ref)
        st_ref[...] = jnp.zeros_like(st_ref)

    tiles = [_mix_tile(x_ref.at[sq], m_ref.at[sq], o_ref.at[sq], t, *weights,
                       ucar_ref.at[sq], hcar_ref.at[sq], st_ref.at[sq], ohg_ref.at[sq])
             for sq in range(MIX_SEQS)]
    waiting, running, tick = tiles, [], 0
    while waiting or running:
        if waiting and tick % MIX_STAGE_LAG == 0:
            running.append(waiting.pop(0))
        running = [tile for tile in running if next(tile, tile) is not tile]
        tick += 1

    @pl.when(t == n_t - 1)
    def _():
        for sq in range(MIX_SEQS):
            h_out_ref[sq] = hcar_ref[sq]
            conv_out_ref[sq] = ucar_ref[sq, V7X_SUBLANES - (CONV_WIDTH - 1):V7X_SUBLANES, :]
            for hd in range(HG_HEADS):
                s_out_ref[sq, hd] = st_ref[sq, hd].T


def _mix_prompt(x, mod3, p):
    batch, seq_len, _ = x.shape
    n_t = seq_len // MIX_TILE
    nsq = MIX_SEQS
    tok_spec = pl.BlockSpec((nsq, MIX_TILE, D_MODEL), lambda b, t: (b, t, 0))
    return pl.pallas_call(
        _mix_prompt_kernel,
        grid=(batch // nsq, n_t),
        in_specs=[
            tok_spec,
            pl.BlockSpec((nsq, 3, D_MODEL), lambda b, t: (b, 0, 0)),
            _const_spec((1, D_MODEL)),
            _const_spec((1, D_MODEL)),
            _const_spec((D_MODEL, D_PROJ)),
            _const_spec((CONV_WIDTH, LRU_WIDTH)),
            _const_spec((1, LRU_WIDTH)),
            _const_spec((LRU_WIDTH, LRU_WIDTH)),
            _const_spec((1, LRU_WIDTH)),
            _const_spec((LRU_WIDTH, LRU_WIDTH)),
            _const_spec((1, LRU_WIDTH)),
            _const_spec((1, LRU_WIDTH)),
            _const_spec(p["lb_logits"].shape),
            _const_spec((1, HG_HEAD_DIM)),
            _const_spec((D_MODEL, D_MODEL)),
        ],
        out_specs=[
            tok_spec,
            pl.BlockSpec((nsq, 1, LRU_WIDTH), lambda b, t: (b, 0, 0)),
            pl.BlockSpec((nsq, CONV_WIDTH - 1, LRU_WIDTH), lambda b, t: (b, 0, 0)),
            pl.BlockSpec((nsq, HG_HEADS, HG_HEAD_DIM, HG_HEAD_DIM), lambda b, t: (b, 0, 0, 0)),
        ],
        out_shape=[
            jax.ShapeDtypeStruct((batch, seq_len, D_MODEL), F32),
            jax.ShapeDtypeStruct((batch, 1, LRU_WIDTH), F32),
            jax.ShapeDtypeStruct((batch, CONV_WIDTH - 1, LRU_WIDTH), F32),
            jax.ShapeDtypeStruct((batch, HG_HEADS, HG_HEAD_DIM, HG_HEAD_DIM), F32),
        ],
        scratch_shapes=[
            pltpu.VMEM((nsq, V7X_SUBLANES, LRU_WIDTH), F32),
            pltpu.VMEM((nsq, 1, LRU_WIDTH), F32),
            pltpu.VMEM((nsq, HG_HEADS, HG_HEAD_DIM, HG_HEAD_DIM), F32),
            pltpu.VMEM((nsq, MIX_TILE, HG_WIDTH), F32),
        ],
        compiler_params=pltpu.CompilerParams(
            dimension_semantics=("arbitrary", "arbitrary"), vmem_limit_bytes=VMEM_LIMIT),
        name="mix_prompt",
    )(x, mod3, p["ln_mix_pre"], p["ln_mix_post"], p["w_in"], p["conv_w"], p["conv_b"],
      p["wa_bd"], p["b_a"], p["wx_bd"], p["b_x"], p["lam"], p["lb_logits"], p["hg_norm_w"],
      p["w_out"])


def _mix_sample_kernel(x_ref, m_ref, pre_ref, post_ref, win_ref, cw_ref, cb_ref,
                       wa_ref, ba_ref, wx_ref, bx_ref, lam_ref, lbl_ref, hgw_ref, wout_ref,
                       h0_ref, conv0_ref, s0_ref,
                       o_ref, h_out_ref, conv_out_ref, s_out_ref,
                       qf_ref, f_ref, k_ref, v_ref, g_ref, qk_ref, olru_ref, ohg_ref, ostage_ref):
    i = pl.program_id(0)
    n_i = pl.num_programs(0)
    nb = SAMPLE_SEQ_BLOCK

    @pl.when(i == 0)
    def _():
        x = x_ref[...]
        h = _rms(x, pre_ref[...]) * (1.0 + m_ref[1]) + m_ref[0]
        proj = _dot(h.astype(BF16), win_ref[...])
        u = proj[:, 0:LRU_WIDTH]
        y_lru = proj[:, LRU_WIDTH:2 * LRU_WIDTH]
        o0 = 2 * LRU_WIDTH
        q = proj[:, o0:o0 + HG_WIDTH]
        f_raw = proj[:, o0 + HG_WIDTH:o0 + 2 * HG_WIDTH]
        v = proj[:, o0 + 2 * HG_WIDTH:o0 + 3 * HG_WIDTH]
        g_ref[...] = proj[:, o0 + 3 * HG_WIDTH:o0 + 4 * HG_WIDTH]

        cw = cw_ref[...]
        u_conv = cb_ref[...]
        for k in range(CONV_WIDTH - 1):
            u_conv = u_conv + conv0_ref[k] * cw[k:k + 1]
            if k > 0:
                conv_out_ref[k - 1] = conv0_ref[k]
        u_conv = u_conv + u * cw[CONV_WIDTH - 1:CONV_WIDTH]
        conv_out_ref[CONV_WIDTH - 2] = u

        a, mult, ig = _lru_gates(u_conv, wa_ref, ba_ref[...], wx_ref, bx_ref[...], lam_ref[...])
        hs = (mult * ig) * u_conv + a * h0_ref[...]
        h_out_ref[...] = hs
        olru_ref[...] = hs * _gelu_tanh(y_lru)

        lb = _lower_bound(lbl_ref[...])
        f = lb + (1.0 - lb) * _sigmoid(f_raw)
        kk = 1.0 - f
        qs = q * (HG_HEAD_DIM ** -0.5)
        f_ref[...] = f
        qf_ref[...] = qs * f
        k_ref[...] = kk
        v_ref[...] = v
        qk = qs * kk
        for hd in range(HG_HEADS):
            l0 = hd * HG_HEAD_DIM
            tot = jnp.sum(qk[:, l0:l0 + HG_HEAD_DIM], axis=-1, keepdims=True)
            qk_ref[:, l0:l0 + HG_HEAD_DIM] = jnp.broadcast_to(tot, (qk.shape[0], HG_HEAD_DIM))

    base = pl.multiple_of(i * nb, nb)
    grp = lambda ref: ref[pl.ds(base, nb), :]
    f_g, k_g, v_g, qf_g, qk_g = grp(f_ref), grp(k_ref), grp(v_ref), grp(qf_ref), grp(qk_ref)
    square = (HG_HEAD_DIM, HG_HEAD_DIM)
    for j in range(nb):
        for hd in range(HG_HEADS):
            l0 = hd * HG_HEAD_DIM
            rowv = lambda z: z[j:j + 1, l0:l0 + HG_HEAD_DIM]
            s_old = s0_ref[j, hd]
            f_col = jnp.broadcast_to(rowv(f_g), square).T
            k_col = jnp.broadcast_to(rowv(k_g), square).T
            v_row = rowv(v_g)
            outer = k_col.astype(BF16).astype(F32) * v_row.astype(BF16).astype(F32)
            s_out_ref[j, hd] = f_col * s_old + outer
            qf8 = jnp.broadcast_to(rowv(qf_g), (V7X_SUBLANES, HG_HEAD_DIM)).astype(BF16)
            o1 = _dot(qf8, s_old.astype(BF16))[0:1]
            ostage_ref[j:j + 1, l0:l0 + HG_HEAD_DIM] = o1 + rowv(qk_g) * v_row
    ohg_ref[pl.ds(base, nb), :] = ostage_ref[...]

    @pl.when(i == n_i - 1)
    def _():
        hgw = hgw_ref[...]
        parts = [olru_ref[...]]
        for hd in range(HG_HEADS):
            l0 = hd * HG_HEAD_DIM
            parts.append(_head_rms_gate(ohg_ref[:, l0:l0 + HG_HEAD_DIM], hgw,
                                        g_ref[:, l0:l0 + HG_HEAD_DIM]))
        mix_in = jnp.concatenate(parts, axis=-1).astype(BF16)
        mix = _dot(mix_in, wout_ref[...])
        o_ref[...] = x_ref[...] + m_ref[2] * _rms(mix, post_ref[...])


def _mix_sample(x, mod9, p, h0, conv0, s0):
    n_seq = x.shape[0]
    nb = SAMPLE_SEQ_BLOCK
    state_spec = pl.BlockSpec((nb, HG_HEADS, HG_HEAD_DIM, HG_HEAD_DIM), lambda i: (i, 0, 0, 0))
    full2 = lambda shape: pl.BlockSpec(shape, lambda i: (0,) * len(shape))
    return pl.pallas_call(
        _mix_sample_kernel,
        grid=(n_seq // nb,),
        in_specs=[
            _const_spec((n_seq, D_MODEL)),
            pl.BlockSpec((3, n_seq, D_MODEL), lambda i: (1, 0, 0), pipeline_mode=pl.Buffered(1)),
            _const_spec((1, D_MODEL)),
            _const_spec((1, D_MODEL)),
            _const_spec((D_MODEL, D_PROJ)),
            _const_spec((CONV_WIDTH, LRU_WIDTH)),
            _const_spec((1, LRU_WIDTH)),
            _const_spec((LRU_WIDTH, LRU_WIDTH)),
            _const_spec((1, LRU_WIDTH)),
            _const_spec((LRU_WIDTH, LRU_WIDTH)),
            _const_spec((1, LRU_WIDTH)),
            _const_spec((1, LRU_WIDTH)),
            _const_spec(p["lb_logits"].shape),
            _const_spec((1, HG_HEAD_DIM)),
            _const_spec((D_MODEL, D_MODEL)),
            _const_spec((n_seq, LRU_WIDTH)),
            _const_spec((CONV_WIDTH - 1, n_seq, LRU_WIDTH)),
            state_spec,
        ],
        out_specs=[
            full2((n_seq, D_MODEL)),
            full2((n_seq, LRU_WIDTH)),
            full2((CONV_WIDTH - 1, n_seq, LRU_WIDTH)),
            state_spec,
        ],
        out_shape=[
            jax.ShapeDtypeStruct((n_seq, D_MODEL), F32),
            jax.ShapeDtypeStruct((n_seq, LRU_WIDTH), F32),
            jax.ShapeDtypeStruct((CONV_WIDTH - 1, n_seq, LRU_WIDTH), F32),
            jax.ShapeDtypeStruct((n_seq, HG_HEADS, HG_HEAD_DIM, HG_HEAD_DIM), F32),
        ],
        scratch_shapes=[pltpu.VMEM((n_seq, HG_WIDTH), F32) for _ in range(8)]
        + [pltpu.VMEM((nb, HG_WIDTH), F32)],
        compiler_params=pltpu.CompilerParams(
            dimension_semantics=("arbitrary",), vmem_limit_bytes=VMEM_LIMIT),
        name="mix_sample",
    )(x, mod9, p["ln_mix_pre"], p["ln_mix_post"], p["w_in"], p["conv_w"], p["conv_b"],
      p["wa_bd"], p["b_a"], p["wx_bd"], p["b_x"], p["lam"], p["lb_logits"], p["hg_norm_w"],
      p["w_out"], h0, conv0, s0)


def _block_diag(w):
    heads, blk, _ = w.shape
    eye = jnp.eye(heads, dtype=w.dtype)
    return (eye[:, None, :, None] * w[:, :, None, :]).reshape(heads * blk, heads * blk)


def kernel(x_prompt, x_sample, c_prompt, c_sample, state_lru_h, state_lru_conv, state_hgrn_S, w_ada, b_ada, ln_ffn1_pre, ln_ffn1_post, ffn1_w_gate, ffn1_w_up, ffn1_w_down, ln_mix_pre, ln_mix_post, w_in, lru_conv_w, lru_conv_b, lru_w_a, lru_b_a, lru_w_x, lru_b_x, lru_lambda, hg_lb_logits, hg_norm_w, w_out, ln_ffn2_pre, ln_ffn2_post, ffn2_w_gate, ffn2_w_up, ffn2_w_down):
    depth = w_ada.shape[0]
    batch, seq_len, _ = x_prompt.shape
    n_seq = x_sample.shape[0]
    assert depth == 1 and x_sample.shape[1] == 1
    assert seq_len % FFN_TILE == 0 and seq_len % MIX_TILE == 0 and n_seq % SAMPLE_SEQ_BLOCK == 0
    assert batch % MIX_SEQS == 0

    xp = x_prompt.reshape(batch * seq_len, D_MODEL)
    xs = x_sample.reshape(n_seq, D_MODEL)
    ph, pc, pS, sh, sc, sS = [], [], [], [], [], []
    for l in range(depth):
        row = lambda w: w[l].reshape(1, -1)
        mod9 = _ada(jnp.concatenate([c_sample, c_prompt], axis=0), w_ada[l], b_ada[l])
        mod_p = jnp.transpose(mod9[:, n_seq:], (1, 0, 2))
        p = {
            "ln_mix_pre": row(ln_mix_pre), "ln_mix_post": row(ln_mix_post),
            "w_in": w_in[l].astype(BF16), "w_out": w_out[l].astype(BF16),
            "conv_w": lru_conv_w[l], "conv_b": row(lru_conv_b),
            "wa_bd": _block_diag(lru_w_a[l]).astype(BF16), "b_a": row(lru_b_a),
            "wx_bd": _block_diag(lru_w_x[l]).astype(BF16), "b_x": row(lru_b_x),
            "lam": row(lru_lambda), "lb_logits": hg_lb_logits,
            "hg_norm_w": row(hg_norm_w),
        }
        f1 = (row(ln_ffn1_pre), row(ln_ffn1_post), ffn1_w_gate[l], ffn1_w_up[l], ffn1_w_down[l])
        f2 = (row(ln_ffn2_pre), row(ln_ffn2_post), ffn2_w_gate[l], ffn2_w_up[l], ffn2_w_down[l])

        xp, xs = _ffn(xp, xs, mod_p[:, 0:3], mod9, 0, *f1, seq_len)
        xp, h_p, c_p, S_p = _mix_prompt(xp.reshape(batch, seq_len, D_MODEL), mod_p[:, 3:6], p)
        xp = xp.reshape(batch * seq_len, D_MODEL)
        conv0 = jnp.transpose(state_lru_conv[l], (1, 0, 2))
        xs, h_s, c_s, S_s = _mix_sample(xs, mod9, p, state_lru_h[l], conv0, state_hgrn_S[l])
        xp, xs = _ffn(xp, xs, mod_p[:, 6:9], mod9, 2, *f2, seq_len)

        ph.append(h_p.reshape(batch, LRU_WIDTH)); pc.append(c_p); pS.append(S_p)
        sh.append(h_s); sc.append(jnp.transpose(c_s, (1, 0, 2))); sS.append(S_s)

    return (xp.reshape(batch, seq_len, D_MODEL), xs.reshape(n_seq, 1, D_MODEL),
            jnp.stack(ph), jnp.stack(pc), jnp.stack(pS), jnp.stack(sh), jnp.stack(sc), jnp.stack(sS))
```

```python
import jax
import jax.numpy as jnp
from jax import lax
from jax.experimental import pallas as pl
from jax.experimental.pallas import tpu as pltpu

F32 = jnp.float32
BF16 = jnp.bfloat16

D_MODEL = 1024
D_FF = 2816
LRU_WIDTH = 512
CONV_WIDTH = 4
LRU_C = 8.0
HG_WIDTH = 512
HG_HEAD_DIM = 128
HG_HEADS = HG_WIDTH // HG_HEAD_DIM
HG_CHUNK = 64
N_MOD = 9
D_PROJ = 2 * LRU_WIDTH + 4 * HG_WIDTH
EPS = 1e-6

V7X_SUBLANES = 8
V7X_VMEM_BYTES = 64 * 1024 * 1024
VMEM_LIMIT = V7X_VMEM_BYTES - 8 * 1024 * 1024

FFN_TILE = 1024
FFN_ROWS = 256
MIX_TILE = 256
FFN_WBLOCK = 256
FFN_WSTEPS = D_FF // FFN_WBLOCK
SAMPLE_SEQ_BLOCK = V7X_SUBLANES

HG_LEVELS = (1, 2, 4, 8, 16, 32)
HG_DIRECT_MIN = 2.0 ** -100


def _rms(x, w):
    return (x * lax.rsqrt(jnp.mean(x * x, axis=-1, keepdims=True) + EPS)) * w


def _sigmoid(x):
    return 1.0 / (1.0 + jnp.exp(-x))


def _silu(x):
    return x * _sigmoid(x)


def _gelu_tanh(x):
    c = 0.7978845608028654
    return x * (0.5 * (1.0 + jnp.tanh(c * (x + 0.044715 * (x * x * x)))))


def _softplus(z):
    return jnp.maximum(z, 0.0) + jnp.log1p(jnp.exp(-jnp.abs(z)))


def _dot(a, b):
    return jnp.dot(a, b, preferred_element_type=F32)


def _dot_nt(a, b):
    return lax.dot_general(a, b, (((1,), (1,)), ((), ())), preferred_element_type=F32)


def _dot_tn(a, b):
    return lax.dot_general(a, b, (((0,), (0,)), ((), ())), preferred_element_type=F32)


def _const_spec(shape):
    nd = len(shape)
    return pl.BlockSpec(shape, lambda *_: (0,) * nd, pipeline_mode=pl.Buffered(1))


def _ada_kernel(c_ref, w_ref, b_ref, o_ref):
    s = _silu(c_ref[...])
    o_ref[0] = _dot(s.astype(BF16), w_ref[...].astype(BF16)) + b_ref[...]


def _ada(c_all, w_ada, b_ada):
    rows = c_all.shape[0]
    return pl.pallas_call(
        _ada_kernel,
        grid=(N_MOD,),
        in_specs=[
            pl.BlockSpec((rows, D_MODEL), lambda j: (0, 0)),
            pl.BlockSpec((D_MODEL, D_MODEL), lambda j: (0, j)),
            pl.BlockSpec((1, D_MODEL), lambda j: (0, j)),
        ],
        out_specs=pl.BlockSpec((1, rows, D_MODEL), lambda j: (j, 0, 0)),
        out_shape=jax.ShapeDtypeStruct((N_MOD, rows, D_MODEL), F32),
        compiler_params=pltpu.CompilerParams(
            dimension_semantics=("arbitrary",), vmem_limit_bytes=VMEM_LIMIT),
        name="ada_mod",
    )(c_all, w_ada, b_ada.reshape(1, N_MOD * D_MODEL))


def _ffn_body(x, shift, scale, gate, pre_w, post_w, wg_ref, wu_ref, wd_ref):
    h = _rms(x, pre_w) * (1.0 + scale) + shift
    hb = h.astype(BF16)
    a = _dot(hb, wg_ref[...])
    u = _dot(hb, wu_ref[...])
    act = (_silu(a) * u).astype(BF16)
    y = _dot(act, wd_ref[...])
    return x + (0.5 * gate) * _rms(y, post_w)


def _ffn_kernel(xp_ref, mp_ref, xs_ref, ms_ref, pre_ref, post_ref, wg32_ref, wu32_ref, wd32_ref,
                op_ref, os_ref, wg_ref, wu_ref, wd_ref):
    s = pl.program_id(0)
    n_steps = pl.num_programs(0)

    for j in range(FFN_WSTEPS):
        @pl.when(s == j)
        def _(j=j):
            c0 = j * FFN_WBLOCK
            wg_ref[:, c0:c0 + FFN_WBLOCK] = wg32_ref[...].astype(BF16)
            wu_ref[:, c0:c0 + FFN_WBLOCK] = wu32_ref[...].astype(BF16)
            wd_ref[c0:c0 + FFN_WBLOCK, :] = wd32_ref[...].astype(BF16)

    @pl.when((s >= FFN_WSTEPS) & (s < n_steps - 1))
    def _():
        m = mp_ref[0]
        for r0 in range(0, FFN_TILE, FFN_ROWS):
            op_ref[r0:r0 + FFN_ROWS, :] = _ffn_body(
                xp_ref[r0:r0 + FFN_ROWS, :], m[0:1], m[1:2], m[2:3],
                pre_ref[...], post_ref[...], wg_ref, wu_ref, wd_ref)

    @pl.when(s == n_steps - 1)
    def _():
        os_ref[...] = _ffn_body(xs_ref[...], ms_ref[0], ms_ref[1], ms_ref[2], pre_ref[...],
                                post_ref[...], wg_ref, wu_ref, wd_ref)


def _ffn(xp, xs, mod_p3, mod9, sub_layer, pre_w, post_w, wg, wu, wd, seq_len):
    n_tok = xp.shape[0]
    n_seq = xs.shape[0]
    n_tiles = n_tok // FFN_TILE
    steps_per_seq = seq_len // FFN_TILE
    tile = lambda s: jnp.clip(s - FFN_WSTEPS, 0, n_tiles - 1)
    wblk = lambda s: jnp.minimum(s, FFN_WSTEPS - 1)
    tok_spec = pl.BlockSpec((FFN_TILE, D_MODEL), lambda s: (tile(s), 0))
    return pl.pallas_call(
        _ffn_kernel,
        grid=(FFN_WSTEPS + n_tiles + 1,),
        in_specs=[
            tok_spec,
            pl.BlockSpec((1, 3, D_MODEL), lambda s: (tile(s) // steps_per_seq, 0, 0)),
            _const_spec((n_seq, D_MODEL)),
            pl.BlockSpec((3, n_seq, D_MODEL), lambda s: (sub_layer, 0, 0), pipeline_mode=pl.Buffered(1)),
            _const_spec((1, D_MODEL)),
            _const_spec((1, D_MODEL)),
            pl.BlockSpec((D_MODEL, FFN_WBLOCK), lambda s: (0, wblk(s))),
            pl.BlockSpec((D_MODEL, FFN_WBLOCK), lambda s: (0, wblk(s))),
            pl.BlockSpec((FFN_WBLOCK, D_MODEL), lambda s: (wblk(s), 0)),
        ],
        out_specs=[tok_spec, pl.BlockSpec((n_seq, D_MODEL), lambda s: (0, 0))],
        out_shape=[jax.ShapeDtypeStruct((n_tok, D_MODEL), F32),
                   jax.ShapeDtypeStruct((n_seq, D_MODEL), F32)],
        scratch_shapes=[pltpu.VMEM((D_MODEL, D_FF), BF16), pltpu.VMEM((D_MODEL, D_FF), BF16),
                        pltpu.VMEM((D_FF, D_MODEL), BF16)],
        compiler_params=pltpu.CompilerParams(
            dimension_semantics=("arbitrary",), vmem_limit_bytes=VMEM_LIMIT),
        name="ffn",
    )(xp, mod_p3, xs, mod9, pre_w, post_w, wg, wu, wd)


def _lower_bound(lb_logits):
    z = lb_logits - jnp.max(lb_logits, axis=0, keepdims=True)
    e = jnp.exp(z)
    return e[0:1] / jnp.sum(e, axis=0, keepdims=True)


def _lru_gates(u_conv, wa_ref, ba, wx_ref, bx, lam):
    ub = u_conv.astype(BF16)
    r = _sigmoid(_dot(ub, wa_ref[...]) + ba)
    ig = _sigmoid(_dot(ub, wx_ref[...]) + bx)
    log_a = (-LRU_C * r) * _softplus(-lam)
    a = jnp.exp(log_a)
    th = jnp.tanh(log_a)
    mult = jnp.sqrt((-2.0 * th) / (1.0 - th))
    return a, mult, ig


def _head_rms_gate(o, norm_w, g):
    return _rms(o, norm_w) * _silu(g)


def _hgrn_chunk_head(scores, q_in, k_out, dec, vb_blk, g_blk, hgw, st_ref, hd, ohg_ref, r0, l0):
    st = st_ref[hd]
    o = _dot_nt(q_in, st.astype(BF16)) + _dot(scores.astype(BF16), vb_blk)
    st_ref[hd] = st * dec + _dot_tn(vb_blk, k_out)
    ohg_ref[r0:r0 + HG_CHUNK, l0:l0 + HG_HEAD_DIM] = _head_rms_gate(o, hgw, g_blk)


def _hgrn_chunks_direct(qs, kk, pfx, tails, vb, g, hgw, st_ref, ohg_ref):
    tb = qs.shape[0]
    n_chunk = len(tails)
    inv = 1.0 / pfx
    q_dec = qs * pfx
    k_dec = kk * inv
    q_in = q_dec.astype(BF16)
    k_in = k_dec.astype(BF16)
    ti = lax.broadcasted_iota(jnp.int32, (tb, tb), 0)
    si = lax.broadcasted_iota(jnp.int32, (tb, tb), 1)
    visible = (ti >= si) & ((ti ^ si) < HG_CHUNK)
    row_chunk = jnp.right_shift(lax.broadcasted_iota(jnp.int32, (tb, HG_HEAD_DIM), 0),
                                HG_CHUNK.bit_length() - 1)
    tail_rows = jnp.concatenate([jnp.broadcast_to(t, (HG_CHUNK, t.shape[-1])) for t in tails], axis=0)
    k_end = k_dec * tail_rows
    for hd in range(HG_HEADS):
        l0 = hd * HG_HEAD_DIM
        head = lambda z: z[:, l0:l0 + HG_HEAD_DIM]
        by_chunk = lambda z: jnp.concatenate(
            [jnp.where(row_chunk == c, head(z), 0.0) for c in range(n_chunk)], axis=-1).astype(BF16)
        scores = jnp.where(visible, _dot_nt(head(q_in), head(k_in)), 0.0)
        grown = _dot_tn(head(vb), by_chunk(k_end))
        st = st_ref[hd]
        states = []
        for c, tail in enumerate(tails):
            states.append(st.astype(BF16))
            st = st * head(tail) + grown[:, c * HG_HEAD_DIM:(c + 1) * HG_HEAD_DIM]
        st_ref[hd] = st
        o = (_dot_nt(by_chunk(q_dec), jnp.concatenate(states, axis=-1))
             + _dot(scores.astype(BF16), head(vb)))
        ohg_ref[:, l0:l0 + HG_HEAD_DIM] = _head_rms_gate(o, hgw, head(g))


def _hgrn_chunks_levels(f3, kk3, qs3, vb, g, hgw, st_ref, ohg_ref):
    n_grp = f3.shape[0]
    tb = n_grp * V7X_SUBLANES
    sub = lax.broadcasted_iota(jnp.int32, f3.shape, 1)
    grp = lax.broadcasted_iota(jnp.int32, f3.shape, 0)
    to_mxu = lambda z: z.reshape(tb, z.shape[-1]).astype(BF16)
    q_lv = [to_mxu(qs3)]
    k_lv = [to_mxu(kk3)]
    tot, pre, post = f3, f3, None
    for lvl in HG_LEVELS:
        q_lv.append(to_mxu(qs3 * pre))
        k_lv.append(k_lv[0] if post is None else to_mxu(kk3 * post))
        if lvl < V7X_SUBLANES:
            upper = (sub & lvl) != 0
            below = pltpu.roll(tot, lvl, 1)
            above = pltpu.roll(tot, V7X_SUBLANES - lvl, 1)
        else:
            upper = (grp & (lvl // V7X_SUBLANES)) != 0
            below = jnp.roll(tot, lvl // V7X_SUBLANES, axis=0)
            above = jnp.roll(tot, -(lvl // V7X_SUBLANES), axis=0)
        pre = jnp.where(upper, pre * below, pre)
        post = jnp.where(upper, 1.0, above) if post is None else jnp.where(upper, post, post * above)
        tot = tot * jnp.where(upper, below, above)
    q_in = to_mxu(qs3 * pre)
    k_out = to_mxu(kk3 * post)
    s_decay = tot.reshape(tb, tot.shape[-1])

    ti = lax.broadcasted_iota(jnp.int32, (HG_CHUNK, HG_CHUNK), 0)
    si = lax.broadcasted_iota(jnp.int32, (HG_CHUNK, HG_CHUNK), 1)
    masks = [ti == si]
    tx = ti ^ si
    for lvl in HG_LEVELS:
        masks.append((tx >= lvl) & (tx < 2 * lvl) & ((ti & lvl) != 0))

    for c in range(tb // HG_CHUNK):
        r0 = c * HG_CHUNK
        for hd in range(HG_HEADS):
            l0 = hd * HG_HEAD_DIM
            blk = lambda z: z[r0:r0 + HG_CHUNK, l0:l0 + HG_HEAD_DIM]
            scores = jnp.zeros((HG_CHUNK, HG_CHUNK), F32)
            for ql, kl, msk in zip(q_lv, k_lv, masks):
                scores = jnp.where(msk, _dot_nt(blk(ql), blk(kl)), scores)
            dec = s_decay[r0 + HG_CHUNK - 1:r0 + HG_CHUNK, l0:l0 + HG_HEAD_DIM]
            _hgrn_chunk_head(scores, blk(q_in), blk(k_out), dec, blk(vb), blk(g), hgw,
                             st_ref, hd, ohg_ref, r0, l0)

def _mix_tile(x_ref, m_ref, o_ref, t, pre_ref, post_ref, win_ref, cw_ref, cb_ref, wa_ref, ba_ref,
              wx_ref, bx_ref, lam_ref, lbl_ref, hgw_ref, wout_ref, ucar_ref, hcar_ref, st_ref, ohg_ref):
    tb = MIX_TILE
    x = x_ref[...]
    m = m_ref[...]
    hb = (_rms(x, pre_ref[...]) * (1.0 + m[1:2]) + m[0:1]).astype(BF16)
    u, y_lru, q, f_raw, v, g = [_dot(hb, win_ref[:, c0:c0 + LRU_WIDTH])
                                for c0 in range(0, D_PROJ, LRU_WIDTH)]

    n_grp = tb // V7X_SUBLANES
    grouped = lambda z: z.reshape(n_grp, V7X_SUBLANES, z.shape[-1])
    flat = lambda z: z.reshape(tb, z.shape[-1])
    gshape = (n_grp, V7X_SUBLANES, LRU_WIDTH)
    sub = lax.broadcasted_iota(jnp.int32, gshape, 1)
    grp = lax.broadcasted_iota(jnp.int32, gshape, 0)

    u3 = grouped(u)
    u_all = jnp.concatenate([ucar_ref[...][None], u3], axis=0)
    ucar_ref[...] = u3[n_grp - 1]
    cw = cw_ref[...]
    u_conv = cb_ref[...]
    for k in range(CONV_WIDTH - 1):
        back = CONV_WIDTH - 1 - k
        rot = pltpu.roll(u_all, back, 1)
        u_conv = u_conv + jnp.where(sub >= back, rot[1:], rot[:-1]) * cw[k:k + 1]
    u_conv = flat(u_conv + u3 * cw[CONV_WIDTH - 1:CONV_WIDTH])

    a, mult, ig = _lru_gates(u_conv, wa_ref, ba_ref[...], wx_ref, bx_ref[...], lam_ref[...])
    a3, mult3 = grouped(a), grouped(mult)
    first_row = jnp.where(t == 0, 0, -1)
    mult3 = jnp.where(grp * V7X_SUBLANES + sub == first_row, 1.0, mult3)
    b3 = (mult3 * grouped(ig)) * grouped(u_conv)
    s = 1
    while s < V7X_SUBLANES:
        keep = sub >= s
        b3 = jnp.where(keep, a3 * pltpu.roll(b3, s, 1) + b3, b3)
        a3 = jnp.where(keep, a3 * pltpu.roll(a3, s, 1), a3)
        s *= 2
    carry = hcar_ref[...]
    hs = []
    for gi in range(n_grp):
        h_g = a3[gi] * carry + b3[gi]
        carry = h_g[V7X_SUBLANES - 1:V7X_SUBLANES]
        hs.append(h_g)
    hcar_ref[...] = carry
    o_lru = jnp.concatenate(hs, axis=0) * _gelu_tanh(y_lru)

    lb = _lower_bound(lbl_ref[...])
    f3 = grouped(lb + (1.0 - lb) * _sigmoid(f_raw))
    kk3 = 1.0 - f3
    qs3 = grouped(q * (HG_HEAD_DIM ** -0.5))
    vb = v.astype(BF16)
    hgw = hgw_ref[...]

    pfx = f3
    s = 1
    while s < V7X_SUBLANES:
        pfx = jnp.where(sub >= s, pfx * pltpu.roll(pfx, s, 1), pfx)
        s *= 2
    grp_per_chunk = HG_CHUNK // V7X_SUBLANES
    rows, tails = [], []
    for gi in range(n_grp):
        p_g = pfx[gi] if gi % grp_per_chunk == 0 else pfx[gi] * carry
        carry = p_g[V7X_SUBLANES - 1:V7X_SUBLANES]
        rows.append(p_g)
        if gi % grp_per_chunk == grp_per_chunk - 1:
            tails.append(carry)
    pfx = jnp.concatenate(rows, axis=0)
    direct = jnp.min(jnp.concatenate(tails, axis=0)) >= HG_DIRECT_MIN

    @pl.when(direct)
    def _():
        _hgrn_chunks_direct(flat(qs3), flat(kk3), pfx, tails, vb, g, hgw, st_ref, ohg_ref)

    @pl.when(jnp.logical_not(direct))
    def _():
        _hgrn_chunks_levels(f3, kk3, qs3, vb, g, hgw, st_ref, ohg_ref)

    mix_in = jnp.concatenate([o_lru, ohg_ref[...]], axis=-1).astype(BF16)
    mix = _dot(mix_in, wout_ref[...])
    o_ref[...] = x + m[2:3] * _rms(mix, post_ref[...])


def _mix_prompt_kernel(x_ref, m_ref, *refs):
    weights = refs[:13]
    o_ref, h_out_ref, conv_out_ref, s_out_ref, ucar_ref, hcar_ref, st_ref, ohg_ref = refs[13:]
    t = pl.program_id(1)
    n_t = pl.num_programs(1)

    @pl.when(t == 0)
    def _():
        ucar_ref[...] = jnp.zeros_like(ucar_ref)
        hcar_ref[...] = jnp.zeros_like(hcar_ref)
        st_ref[...] = jnp.zeros_like(st_ref)

    _mix_tile(x_ref.at[0], m_ref.at[0], o_ref.at[0], t, *weights,
              ucar_ref, hcar_ref, st_ref, ohg_ref)

    @pl.when(t == n_t - 1)
    def _():
        h_out_ref[0] = hcar_ref[...]
        conv_out_ref[0] = ucar_ref[V7X_SUBLANES - (CONV_WIDTH - 1):V7X_SUBLANES, :]
        for hd in range(HG_HEADS):
            s_out_ref[0, hd] = st_ref[hd].T


def _mix_prompt(x, mod3, p):
    batch, seq_len, _ = x.shape
    n_t = seq_len // MIX_TILE
    tok_spec = pl.BlockSpec((1, MIX_TILE, D_MODEL), lambda b, t: (b, t, 0))
    return pl.pallas_call(
        _mix_prompt_kernel,
        grid=(batch, n_t),
        in_specs=[
            tok_spec,
            pl.BlockSpec((1, 3, D_MODEL), lambda b, t: (b, 0, 0)),
            _const_spec((1, D_MODEL)),
            _const_spec((1, D_MODEL)),
            _const_spec((D_MODEL, D_PROJ)),
            _const_spec((CONV_WIDTH, LRU_WIDTH)),
            _const_spec((1, LRU_WIDTH)),
            _const_spec((LRU_WIDTH, LRU_WIDTH)),
            _const_spec((1, LRU_WIDTH)),
            _const_spec((LRU_WIDTH, LRU_WIDTH)),
            _const_spec((1, LRU_WIDTH)),
            _const_spec((1, LRU_WIDTH)),
            _const_spec(p["lb_logits"].shape),
            _const_spec((1, HG_HEAD_DIM)),
            _const_spec((D_MODEL, D_MODEL)),
        ],
        out_specs=[
            tok_spec,
            pl.BlockSpec((1, 1, LRU_WIDTH), lambda b, t: (b, 0, 0)),
            pl.BlockSpec((1, CONV_WIDTH - 1, LRU_WIDTH), lambda b, t: (b, 0, 0)),
            pl.BlockSpec((1, HG_HEADS, HG_HEAD_DIM, HG_HEAD_DIM), lambda b, t: (b, 0, 0, 0)),
        ],
        out_shape=[
            jax.ShapeDtypeStruct((batch, seq_len, D_MODEL), F32),
            jax.ShapeDtypeStruct((batch, 1, LRU_WIDTH), F32),
            jax.ShapeDtypeStruct((batch, CONV_WIDTH - 1, LRU_WIDTH), F32),
            jax.ShapeDtypeStruct((batch, HG_HEADS, HG_HEAD_DIM, HG_HEAD_DIM), F32),
        ],
        scratch_shapes=[
            pltpu.VMEM((V7X_SUBLANES, LRU_WIDTH), F32),
            pltpu.VMEM((1, LRU_WIDTH), F32),
            pltpu.VMEM((HG_HEADS, HG_HEAD_DIM, HG_HEAD_DIM), F32),
            pltpu.VMEM((MIX_TILE, HG_WIDTH), F32),
        ],
        compiler_params=pltpu.CompilerParams(
            dimension_semantics=("arbitrary", "arbitrary"), vmem_limit_bytes=VMEM_LIMIT),
        name="mix_prompt",
    )(x, mod3, p["ln_mix_pre"], p["ln_mix_post"], p["w_in"], p["conv_w"], p["conv_b"],
      p["wa_bd"], p["b_a"], p["wx_bd"], p["b_x"], p["lam"], p["lb_logits"], p["hg_norm_w"],
      p["w_out"])


def _mix_sample_kernel(x_ref, m_ref, pre_ref, post_ref, win_ref, cw_ref, cb_ref,
                       wa_ref, ba_ref, wx_ref, bx_ref, lam_ref, lbl_ref, hgw_ref, wout_ref,
                       h0_ref, conv0_ref, s0_ref,
                       o_ref, h_out_ref, conv_out_ref, s_out_ref,
                       qf_ref, f_ref, k_ref, v_ref, g_ref, qk_ref, olru_ref, ohg_ref, ostage_ref):
    i = pl.program_id(0)
    n_i = pl.num_programs(0)
    nb = SAMPLE_SEQ_BLOCK

    @pl.when(i == 0)
    def _():
        x = x_ref[...]
        h = _rms(x, pre_ref[...]) * (1.0 + m_ref[1]) + m_ref[0]
        proj = _dot(h.astype(BF16), win_ref[...])
        u = proj[:, 0:LRU_WIDTH]
        y_lru = proj[:, LRU_WIDTH:2 * LRU_WIDTH]
        o0 = 2 * LRU_WIDTH
        q = proj[:, o0:o0 + HG_WIDTH]
        f_raw = proj[:, o0 + HG_WIDTH:o0 + 2 * HG_WIDTH]
        v = proj[:, o0 + 2 * HG_WIDTH:o0 + 3 * HG_WIDTH]
        g_ref[...] = proj[:, o0 + 3 * HG_WIDTH:o0 + 4 * HG_WIDTH]

        cw = cw_ref[...]
        u_conv = cb_ref[...]
        for k in range(CONV_WIDTH - 1):
            u_conv = u_conv + conv0_ref[k] * cw[k:k + 1]
            if k > 0:
                conv_out_ref[k - 1] = conv0_ref[k]
        u_conv = u_conv + u * cw[CONV_WIDTH - 1:CONV_WIDTH]
        conv_out_ref[CONV_WIDTH - 2] = u

        a, mult, ig = _lru_gates(u_conv, wa_ref, ba_ref[...], wx_ref, bx_ref[...], lam_ref[...])
        hs = (mult * ig) * u_conv + a * h0_ref[...]
        h_out_ref[...] = hs
        olru_ref[...] = hs * _gelu_tanh(y_lru)

        lb = _lower_bound(lbl_ref[...])
        f = lb + (1.0 - lb) * _sigmoid(f_raw)
        kk = 1.0 - f
        qs = q * (HG_HEAD_DIM ** -0.5)
        f_ref[...] = f
        qf_ref[...] = qs * f
        k_ref[...] = kk
        v_ref[...] = v
        qk = qs * kk
        for hd in range(HG_HEADS):
            l0 = hd * HG_HEAD_DIM
            tot = jnp.sum(qk[:, l0:l0 + HG_HEAD_DIM], axis=-1, keepdims=True)
            qk_ref[:, l0:l0 + HG_HEAD_DIM] = jnp.broadcast_to(tot, (qk.shape[0], HG_HEAD_DIM))

    base = pl.multiple_of(i * nb, nb)
    grp = lambda ref: ref[pl.ds(base, nb), :]
    f_g, k_g, v_g, qf_g, qk_g = grp(f_ref), grp(k_ref), grp(v_ref), grp(qf_ref), grp(qk_ref)
    square = (HG_HEAD_DIM, HG_HEAD_DIM)
    for j in range(nb):
        for hd in range(HG_HEADS):
            l0 = hd * HG_HEAD_DIM
            rowv = lambda z: z[j:j + 1, l0:l0 + HG_HEAD_DIM]
            s_old = s0_ref[j, hd]
            f_col = jnp.broadcast_to(rowv(f_g), square).T
            k_col = jnp.broadcast_to(rowv(k_g), square).T
            v_row = rowv(v_g)
            outer = k_col.astype(BF16).astype(F32) * v_row.astype(BF16).astype(F32)
            s_out_ref[j, hd] = f_col * s_old + outer
            qf8 = jnp.broadcast_to(rowv(qf_g), (V7X_SUBLANES, HG_HEAD_DIM)).astype(BF16)
            o1 = _dot(qf8, s_old.astype(BF16))[0:1]
            ostage_ref[j:j + 1, l0:l0 + HG_HEAD_DIM] = o1 + rowv(qk_g) * v_row
    ohg_ref[pl.ds(base, nb), :] = ostage_ref[...]

    @pl.when(i == n_i - 1)
    def _():
        hgw = hgw_ref[...]
        parts = [olru_ref[...]]
        for hd in range(HG_HEADS):
            l0 = hd * HG_HEAD_DIM
            parts.append(_head_rms_gate(ohg_ref[:, l0:l0 + HG_HEAD_DIM], hgw,
                                        g_ref[:, l0:l0 + HG_HEAD_DIM]))
        mix_in = jnp.concatenate(parts, axis=-1).astype(BF16)
        mix = _dot(mix_in, wout_ref[...])
        o_ref[...] = x_ref[...] + m_ref[2] * _rms(mix, post_ref[...])


def _mix_sample(x, mod9, p, h0, conv0, s0):
    n_seq = x.shape[0]
    nb = SAMPLE_SEQ_BLOCK
    state_spec = pl.BlockSpec((nb, HG_HEADS, HG_HEAD_DIM, HG_HEAD_DIM), lambda i: (i, 0, 0, 0))
    full2 = lambda shape: pl.BlockSpec(shape, lambda i: (0,) * len(shape))
    return pl.pallas_call(
        _mix_sample_kernel,
        grid=(n_seq // nb,),
        in_specs=[
            _const_spec((n_seq, D_MODEL)),
            pl.BlockSpec((3, n_seq, D_MODEL), lambda i: (1, 0, 0), pipeline_mode=pl.Buffered(1)),
            _const_spec((1, D_MODEL)),
            _const_spec((1, D_MODEL)),
            _const_spec((D_MODEL, D_PROJ)),
            _const_spec((CONV_WIDTH, LRU_WIDTH)),
            _const_spec((1, LRU_WIDTH)),
            _const_spec((LRU_WIDTH, LRU_WIDTH)),
            _const_spec((1, LRU_WIDTH)),
            _const_spec((LRU_WIDTH, LRU_WIDTH)),
            _const_spec((1, LRU_WIDTH)),
            _const_spec((1, LRU_WIDTH)),
            _const_spec(p["lb_logits"].shape),
            _const_spec((1, HG_HEAD_DIM)),
            _const_spec((D_MODEL, D_MODEL)),
            _const_spec((n_seq, LRU_WIDTH)),
            _const_spec((CONV_WIDTH - 1, n_seq, LRU_WIDTH)),
            state_spec,
        ],
        out_specs=[
            full2((n_seq, D_MODEL)),
            full2((n_seq, LRU_WIDTH)),
            full2((CONV_WIDTH - 1, n_seq, LRU_WIDTH)),
            state_spec,
        ],
        out_shape=[
            jax.ShapeDtypeStruct((n_seq, D_MODEL), F32),
            jax.ShapeDtypeStruct((n_seq, LRU_WIDTH), F32),
            jax.ShapeDtypeStruct((CONV_WIDTH - 1, n_seq, LRU_WIDTH), F32),
            jax.ShapeDtypeStruct((n_seq, HG_HEADS, HG_HEAD_DIM, HG_HEAD_DIM), F32),
        ],
        scratch_shapes=[pltpu.VMEM((n_seq, HG_WIDTH), F32) for _ in range(8)]
        + [pltpu.VMEM((nb, HG_WIDTH), F32)],
        compiler_params=pltpu.CompilerParams(
            dimension_semantics=("arbitrary",), vmem_limit_bytes=VMEM_LIMIT),
        name="mix_sample",
    )(x, mod9, p["ln_mix_pre"], p["ln_mix_post"], p["w_in"], p["conv_w"], p["conv_b"],
      p["wa_bd"], p["b_a"], p["wx_bd"], p["b_x"], p["lam"], p["lb_logits"], p["hg_norm_w"],
      p["w_out"], h0, conv0, s0)


def _block_diag(w):
    heads, blk, _ = w.shape
    eye = jnp.eye(heads, dtype=w.dtype)
    return (eye[:, None, :, None] * w[:, :, None, :]).reshape(heads * blk, heads * blk)


def kernel(x_prompt, x_sample, c_prompt, c_sample, state_lru_h, state_lru_conv, state_hgrn_S, w_ada, b_ada, ln_ffn1_pre, ln_ffn1_post, ffn1_w_gate, ffn1_w_up, ffn1_w_down, ln_mix_pre, ln_mix_post, w_in, lru_conv_w, lru_conv_b, lru_w_a, lru_b_a, lru_w_x, lru_b_x, lru_lambda, hg_lb_logits, hg_norm_w, w_out, ln_ffn2_pre, ln_ffn2_post, ffn2_w_gate, ffn2_w_up, ffn2_w_down):
    depth = w_ada.shape[0]
    batch, seq_len, _ = x_prompt.shape
    n_seq = x_sample.shape[0]
    assert depth == 1 and x_sample.shape[1] == 1
    assert seq_len % FFN_TILE == 0 and seq_len % MIX_TILE == 0 and n_seq % SAMPLE_SEQ_BLOCK == 0
    assert D_PROJ == 6 * LRU_WIDTH and LRU_WIDTH == HG_WIDTH

    xp = x_prompt.reshape(batch * seq_len, D_MODEL)
    xs = x_sample.reshape(n_seq, D_MODEL)
    ph, pc, pS, sh, sc, sS = [], [], [], [], [], []
    for l in range(depth):
        row = lambda w: w[l].reshape(1, -1)
        mod9 = _ada(jnp.concatenate([c_sample, c_prompt], axis=0), w_ada[l], b_ada[l])
        mod_p = jnp.transpose(mod9[:, n_seq:], (1, 0, 2))
        p = {
            "ln_mix_pre": row(ln_mix_pre), "ln_mix_post": row(ln_mix_post),
            "w_in": w_in[l].astype(BF16), "w_out": w_out[l].astype(BF16),
            "conv_w": lru_conv_w[l], "conv_b": row(lru_conv_b),
            "wa_bd": _block_diag(lru_w_a[l]).astype(BF16), "b_a": row(lru_b_a),
            "wx_bd": _block_diag(lru_w_x[l]).astype(BF16), "b_x": row(lru_b_x),
            "lam": row(lru_lambda), "lb_logits": hg_lb_logits,
            "hg_norm_w": row(hg_norm_w),
        }
        f1 = (row(ln_ffn1_pre), row(ln_ffn1_post), ffn1_w_gate[l], ffn1_w_up[l], ffn1_w_down[l])
        f2 = (row(ln_ffn2_pre), row(ln_ffn2_post), ffn2_w_gate[l], ffn2_w_up[l], ffn2_w_down[l])

        xp, xs = _ffn(xp, xs, mod_p[:, 0:3], mod9, 0, *f1, seq_len)
        xp, h_p, c_p, S_p = _mix_prompt(xp.reshape(batch, seq_len, D_MODEL), mod_p[:, 3:6], p)
        xp = xp.reshape(batch * seq_len, D_MODEL)
        conv0 = jnp.transpose(state_lru_conv[l], (1, 0, 2))
        xs, h_s, c_s, S_s = _mix_sample(xs, mod9, p, state_lru_h[l], conv0, state_hgrn_S[l])
        xp, xs = _ffn(xp, xs, mod_p[:, 6:9], mod9, 2, *f2, seq_len)

        ph.append(h_p.reshape(batch, LRU_WIDTH)); pc.append(c_p); pS.append(S_p)
        sh.append(h_s); sc.append(jnp.transpose(c_s, (1, 0, 2))); sS.append(S_s)

    return (xp.reshape(batch, seq_len, D_MODEL), xs.reshape(n_seq, 1, D_MODEL),
            jnp.stack(ph), jnp.stack(pc), jnp.stack(pS), jnp.stack(sh), jnp.stack(sc), jnp.stack(sS))
```

```python
import jax
import jax.numpy as jnp
from jax import lax
from jax.experimental import pallas as pl
from jax.experimental.pallas import tpu as pltpu

F32 = jnp.float32
BF16 = jnp.bfloat16

D_MODEL = 1024
D_FF = 2816
LRU_WIDTH = 512
CONV_WIDTH = 4
LRU_C = 8.0
HG_WIDTH = 512
HG_HEAD_DIM = 128
HG_HEADS = HG_WIDTH // HG_HEAD_DIM
HG_CHUNK = 64
N_MOD = 9
D_PROJ = 2 * LRU_WIDTH + 4 * HG_WIDTH
EPS = 1e-6

V7X_SUBLANES = 8
V7X_VMEM_BYTES = 64 * 1024 * 1024
VMEM_LIMIT = V7X_VMEM_BYTES - 8 * 1024 * 1024

FFN_TILE = 1024
FFN_ROWS = 256
MIX_TILE = 256
FFN_WBLOCK = 256
FFN_WSTEPS = D_FF // FFN_WBLOCK
SAMPLE_SEQ_BLOCK = V7X_SUBLANES

HG_LEVELS = (1, 2, 4, 8, 16, 32)
HG_DIRECT_MIN = 2.0 ** -100


def _rms(x, w):
    return (x * lax.rsqrt(jnp.mean(x * x, axis=-1, keepdims=True) + EPS)) * w


def _sigmoid(x):
    return 1.0 / (1.0 + jnp.exp(-x))


def _silu(x):
    return x * _sigmoid(x)


def _gelu_tanh(x):
    c = 0.7978845608028654
    return x * (0.5 * (1.0 + jnp.tanh(c * (x + 0.044715 * (x * x * x)))))


def _softplus(z):
    return jnp.maximum(z, 0.0) + jnp.log1p(jnp.exp(-jnp.abs(z)))


def _dot(a, b):
    return jnp.dot(a, b, preferred_element_type=F32)


def _dot_nt(a, b):
    return lax.dot_general(a, b, (((1,), (1,)), ((), ())), preferred_element_type=F32)


def _dot_tn(a, b):
    return lax.dot_general(a, b, (((0,), (0,)), ((), ())), preferred_element_type=F32)


def _const_spec(shape):
    nd = len(shape)
    return pl.BlockSpec(shape, lambda *_: (0,) * nd, pipeline_mode=pl.Buffered(1))


def _ada_kernel(cs_ref, cp_ref, w_ref, b_ref, o_ref):
    s = _silu(jnp.concatenate([cs_ref[...], cp_ref[...]], axis=0))
    o_ref[0] = _dot(s.astype(BF16), w_ref[...].astype(BF16)) + b_ref[...]


def _ada(c_sample, c_prompt, w_ada, b_ada):
    rows = c_sample.shape[0] + c_prompt.shape[0]
    return pl.pallas_call(
        _ada_kernel,
        grid=(N_MOD,),
        in_specs=[
            _const_spec(c_sample.shape),
            _const_spec(c_prompt.shape),
            pl.BlockSpec((D_MODEL, D_MODEL), lambda j: (0, j)),
            pl.BlockSpec((1, D_MODEL), lambda j: (0, j)),
        ],
        out_specs=pl.BlockSpec((1, rows, D_MODEL), lambda j: (j, 0, 0)),
        out_shape=jax.ShapeDtypeStruct((N_MOD, rows, D_MODEL), F32),
        compiler_params=pltpu.CompilerParams(
            dimension_semantics=("arbitrary",), vmem_limit_bytes=VMEM_LIMIT),
        name="ada_mod",
    )(c_sample, c_prompt, w_ada, b_ada.reshape(1, N_MOD * D_MODEL))


def _ffn_body(x, shift, scale, gate, pre_w, post_w, wg_ref, wu_ref, wd_ref):
    h = _rms(x, pre_w) * (1.0 + scale) + shift
    hb = h.astype(BF16)
    a = _dot(hb, wg_ref[...])
    u = _dot(hb, wu_ref[...])
    act = (_silu(a) * u).astype(BF16)
    y = _dot(act, wd_ref[...])
    return x + (0.5 * gate) * _rms(y, post_w)


def _ffn_kernel(xp_ref, mp_ref, xs_ref, ms_ref, pre_ref, post_ref, wg32_ref, wu32_ref, wd32_ref,
                op_ref, os_ref, wg_ref, wu_ref, wd_ref):
    s = pl.program_id(0)
    n_steps = pl.num_programs(0)

    for j in range(FFN_WSTEPS):
        @pl.when(s == j)
        def _(j=j):
            c0 = j * FFN_WBLOCK
            wg_ref[:, c0:c0 + FFN_WBLOCK] = wg32_ref[...].astype(BF16)
            wu_ref[:, c0:c0 + FFN_WBLOCK] = wu32_ref[...].astype(BF16)
            wd_ref[c0:c0 + FFN_WBLOCK, :] = wd32_ref[...].astype(BF16)

    @pl.when((s >= FFN_WSTEPS) & (s < n_steps - 1))
    def _():
        m = mp_ref[0]
        for r0 in range(0, FFN_TILE, FFN_ROWS):
            op_ref[r0:r0 + FFN_ROWS, :] = _ffn_body(
                xp_ref[r0:r0 + FFN_ROWS, :], m[0:1], m[1:2], m[2:3],
                pre_ref[...], post_ref[...], wg_ref, wu_ref, wd_ref)

    @pl.when(s == n_steps - 1)
    def _():
        os_ref[...] = _ffn_body(xs_ref[...], ms_ref[0], ms_ref[1], ms_ref[2], pre_ref[...],
                                post_ref[...], wg_ref, wu_ref, wd_ref)


def _ffn(xp, xs, mod_p3, mod9, sub_layer, pre_w, post_w, wg, wu, wd, seq_len):
    n_tok = xp.shape[0]
    n_seq = xs.shape[0]
    n_tiles = n_tok // FFN_TILE
    steps_per_seq = seq_len // FFN_TILE
    tile = lambda s: jnp.clip(s - FFN_WSTEPS, 0, n_tiles - 1)
    wblk = lambda s: jnp.minimum(s, FFN_WSTEPS - 1)
    tok_spec = pl.BlockSpec((FFN_TILE, D_MODEL), lambda s: (tile(s), 0))
    return pl.pallas_call(
        _ffn_kernel,
        grid=(FFN_WSTEPS + n_tiles + 1,),
        in_specs=[
            tok_spec,
            pl.BlockSpec((1, 3, D_MODEL), lambda s: (tile(s) // steps_per_seq, 0, 0)),
            _const_spec((n_seq, D_MODEL)),
            pl.BlockSpec((3, n_seq, D_MODEL), lambda s: (sub_layer, 0, 0), pipeline_mode=pl.Buffered(1)),
            _const_spec((1, D_MODEL)),
            _const_spec((1, D_MODEL)),
            pl.BlockSpec((D_MODEL, FFN_WBLOCK), lambda s: (0, wblk(s))),
            pl.BlockSpec((D_MODEL, FFN_WBLOCK), lambda s: (0, wblk(s))),
            pl.BlockSpec((FFN_WBLOCK, D_MODEL), lambda s: (wblk(s), 0)),
        ],
        out_specs=[tok_spec, pl.BlockSpec((n_seq, D_MODEL), lambda s: (0, 0))],
        out_shape=[jax.ShapeDtypeStruct((n_tok, D_MODEL), F32),
                   jax.ShapeDtypeStruct((n_seq, D_MODEL), F32)],
        scratch_shapes=[pltpu.VMEM((D_MODEL, D_FF), BF16), pltpu.VMEM((D_MODEL, D_FF), BF16),
                        pltpu.VMEM((D_FF, D_MODEL), BF16)],
        compiler_params=pltpu.CompilerParams(
            dimension_semantics=("arbitrary",), vmem_limit_bytes=VMEM_LIMIT),
        name="ffn",
    )(xp, mod_p3, xs, mod9, pre_w, post_w, wg, wu, wd)


def _lower_bound(lb_logits):
    z = lb_logits - jnp.max(lb_logits, axis=0, keepdims=True)
    e = jnp.exp(z)
    return e[0:1] / jnp.sum(e, axis=0, keepdims=True)


def _lru_gates(u_conv, wa_ref, ba, wx_ref, bx, lam):
    ub = u_conv.astype(BF16)
    r = _sigmoid(_dot(ub, wa_ref[...]) + ba)
    ig = _sigmoid(_dot(ub, wx_ref[...]) + bx)
    log_a = (-LRU_C * r) * _softplus(-lam)
    a = jnp.exp(log_a)
    th = jnp.tanh(log_a)
    mult = jnp.sqrt((-2.0 * th) / (1.0 - th))
    return a, mult, ig


def _head_rms_gate(o, norm_w, g):
    return _rms(o, norm_w) * _silu(g)


def _hgrn_chunk_head(scores, q_in, k_out, dec, vb_blk, g_blk, hgw, st_ref, hd, ohg_ref, r0, l0):
    st = st_ref[hd]
    o = _dot_nt(q_in, st.astype(BF16)) + _dot(scores.astype(BF16), vb_blk)
    st_ref[hd] = st * dec + _dot_tn(vb_blk, k_out)
    ohg_ref[r0:r0 + HG_CHUNK, l0:l0 + HG_HEAD_DIM] = _head_rms_gate(o, hgw, g_blk)


def _hgrn_chunks_direct(qs, kk, pfx, tails, vb, g, hgw, st_ref, ohg_ref):
    tb = qs.shape[0]
    n_chunk = len(tails)
    inv = 1.0 / pfx
    q_dec = qs * pfx
    k_dec = kk * inv
    q_in = q_dec.astype(BF16)
    k_in = k_dec.astype(BF16)
    ti = lax.broadcasted_iota(jnp.int32, (tb, tb), 0)
    si = lax.broadcasted_iota(jnp.int32, (tb, tb), 1)
    visible = (ti >= si) & ((ti ^ si) < HG_CHUNK)
    row_chunk = jnp.right_shift(lax.broadcasted_iota(jnp.int32, (tb, HG_HEAD_DIM), 0),
                                HG_CHUNK.bit_length() - 1)
    tail_rows = jnp.concatenate([jnp.broadcast_to(t, (HG_CHUNK, t.shape[-1])) for t in tails], axis=0)
    k_end = k_dec * tail_rows
    for hd in range(HG_HEADS):
        l0 = hd * HG_HEAD_DIM
        head = lambda z: z[:, l0:l0 + HG_HEAD_DIM]
        by_chunk = lambda z: jnp.concatenate(
            [jnp.where(row_chunk == c, head(z), 0.0) for c in range(n_chunk)], axis=-1).astype(BF16)
        scores = jnp.where(visible, _dot_nt(head(q_in), head(k_in)), 0.0)
        grown = _dot_tn(head(vb), by_chunk(k_end))
        st = st_ref[hd]
        states = []
        for c, tail in enumerate(tails):
            states.append(st.astype(BF16))
            st = st * head(tail) + grown[:, c * HG_HEAD_DIM:(c + 1) * HG_HEAD_DIM]
        st_ref[hd] = st
        o = (_dot_nt(by_chunk(q_dec), jnp.concatenate(states, axis=-1))
             + _dot(scores.astype(BF16), head(vb)))
        ohg_ref[:, l0:l0 + HG_HEAD_DIM] = _head_rms_gate(o, hgw, head(g))


def _hgrn_chunks_levels(f3, kk3, qs3, vb, g, hgw, st_ref, ohg_ref):
    n_grp = f3.shape[0]
    tb = n_grp * V7X_SUBLANES
    sub = lax.broadcasted_iota(jnp.int32, f3.shape, 1)
    grp = lax.broadcasted_iota(jnp.int32, f3.shape, 0)
    to_mxu = lambda z: z.reshape(tb, z.shape[-1]).astype(BF16)
    q_lv = [to_mxu(qs3)]
    k_lv = [to_mxu(kk3)]
    tot, pre, post = f3, f3, None
    for lvl in HG_LEVELS:
        q_lv.append(to_mxu(qs3 * pre))
        k_lv.append(k_lv[0] if post is None else to_mxu(kk3 * post))
        if lvl < V7X_SUBLANES:
            upper = (sub & lvl) != 0
            below = pltpu.roll(tot, lvl, 1)
            above = pltpu.roll(tot, V7X_SUBLANES - lvl, 1)
        else:
            upper = (grp & (lvl // V7X_SUBLANES)) != 0
            below = jnp.roll(tot, lvl // V7X_SUBLANES, axis=0)
            above = jnp.roll(tot, -(lvl // V7X_SUBLANES), axis=0)
        pre = jnp.where(upper, pre * below, pre)
        post = jnp.where(upper, 1.0, above) if post is None else jnp.where(upper, post, post * above)
        tot = tot * jnp.where(upper, below, above)
    q_in = to_mxu(qs3 * pre)
    k_out = to_mxu(kk3 * post)
    s_decay = tot.reshape(tb, tot.shape[-1])

    ti = lax.broadcasted_iota(jnp.int32, (HG_CHUNK, HG_CHUNK), 0)
    si = lax.broadcasted_iota(jnp.int32, (HG_CHUNK, HG_CHUNK), 1)
    masks = [ti == si]
    tx = ti ^ si
    for lvl in HG_LEVELS:
        masks.append((tx >= lvl) & (tx < 2 * lvl) & ((ti & lvl) != 0))

    for c in range(tb // HG_CHUNK):
        r0 = c * HG_CHUNK
        for hd in range(HG_HEADS):
            l0 = hd * HG_HEAD_DIM
            blk = lambda z: z[r0:r0 + HG_CHUNK, l0:l0 + HG_HEAD_DIM]
            scores = jnp.zeros((HG_CHUNK, HG_CHUNK), F32)
            for ql, kl, msk in zip(q_lv, k_lv, masks):
                scores = jnp.where(msk, _dot_nt(blk(ql), blk(kl)), scores)
            dec = s_decay[r0 + HG_CHUNK - 1:r0 + HG_CHUNK, l0:l0 + HG_HEAD_DIM]
            _hgrn_chunk_head(scores, blk(q_in), blk(k_out), dec, blk(vb), blk(g), hgw,
                             st_ref, hd, ohg_ref, r0, l0)

def _mix_tile(x_ref, m_ref, o_ref, t, pre_ref, post_ref, win_ref, cw_ref, cb_ref, wa_ref, ba_ref,
              wx_ref, bx_ref, lam_ref, lbl_ref, hgw_ref, wout_ref, ucar_ref, hcar_ref, st_ref, ohg_ref):
    tb = MIX_TILE
    x = x_ref[...]
    m = m_ref[...]
    hb = (_rms(x, pre_ref[...]) * (1.0 + m[1:2]) + m[0:1]).astype(BF16)
    u, y_lru, q, f_raw, v, g = [_dot(hb, win_ref[:, c0:c0 + LRU_WIDTH])
                                for c0 in range(0, D_PROJ, LRU_WIDTH)]

    n_grp = tb // V7X_SUBLANES
    grouped = lambda z: z.reshape(n_grp, V7X_SUBLANES, z.shape[-1])
    flat = lambda z: z.reshape(tb, z.shape[-1])
    gshape = (n_grp, V7X_SUBLANES, LRU_WIDTH)
    sub = lax.broadcasted_iota(jnp.int32, gshape, 1)
    grp = lax.broadcasted_iota(jnp.int32, gshape, 0)

    u3 = grouped(u)
    u_all = jnp.concatenate([ucar_ref[...][None], u3], axis=0)
    ucar_ref[...] = u3[n_grp - 1]
    cw = cw_ref[...]
    u_conv = cb_ref[...]
    for k in range(CONV_WIDTH - 1):
        back = CONV_WIDTH - 1 - k
        rot = pltpu.roll(u_all, back, 1)
        u_conv = u_conv + jnp.where(sub >= back, rot[1:], rot[:-1]) * cw[k:k + 1]
    u_conv = flat(u_conv + u3 * cw[CONV_WIDTH - 1:CONV_WIDTH])

    a, mult, ig = _lru_gates(u_conv, wa_ref, ba_ref[...], wx_ref, bx_ref[...], lam_ref[...])
    a3, mult3 = grouped(a), grouped(mult)
    first_row = jnp.where(t == 0, 0, -1)
    mult3 = jnp.where(grp * V7X_SUBLANES + sub == first_row, 1.0, mult3)
    b3 = (mult3 * grouped(ig)) * grouped(u_conv)
    s = 1
    while s < V7X_SUBLANES:
        keep = sub >= s
        b3 = jnp.where(keep, a3 * pltpu.roll(b3, s, 1) + b3, b3)
        a3 = jnp.where(keep, a3 * pltpu.roll(a3, s, 1), a3)
        s *= 2
    carry = hcar_ref[...]
    hs = []
    for gi in range(n_grp):
        h_g = a3[gi] * carry + b3[gi]
        carry = h_g[V7X_SUBLANES - 1:V7X_SUBLANES]
        hs.append(h_g)
    hcar_ref[...] = carry
    o_lru = jnp.concatenate(hs, axis=0) * _gelu_tanh(y_lru)

    lb = _lower_bound(lbl_ref[...])
    f3 = grouped(lb + (1.0 - lb) * _sigmoid(f_raw))
    kk3 = 1.0 - f3
    qs3 = grouped(q * (HG_HEAD_DIM ** -0.5))
    vb = v.astype(BF16)
    hgw = hgw_ref[...]

    pfx = f3
    s = 1
    while s < V7X_SUBLANES:
        pfx = jnp.where(sub >= s, pfx * pltpu.roll(pfx, s, 1), pfx)
        s *= 2
    grp_per_chunk = HG_CHUNK // V7X_SUBLANES
    rows, tails = [], []
    for gi in range(n_grp):
        p_g = pfx[gi] if gi % grp_per_chunk == 0 else pfx[gi] * carry
        carry = p_g[V7X_SUBLANES - 1:V7X_SUBLANES]
        rows.append(p_g)
        if gi % grp_per_chunk == grp_per_chunk - 1:
            tails.append(carry)
    pfx = jnp.concatenate(rows, axis=0)
    direct = jnp.min(jnp.concatenate(tails, axis=0)) >= HG_DIRECT_MIN

    @pl.when(direct)
    def _():
        _hgrn_chunks_direct(flat(qs3), flat(kk3), pfx, tails, vb, g, hgw, st_ref, ohg_ref)

    @pl.when(jnp.logical_not(direct))
    def _():
        _hgrn_chunks_levels(f3, kk3, qs3, vb, g, hgw, st_ref, ohg_ref)

    mix_in = jnp.concatenate([o_lru, ohg_ref[...]], axis=-1).astype(BF16)
    mix = _dot(mix_in, wout_ref[...])
    o_ref[...] = x + m[2:3] * _rms(mix, post_ref[...])


def _mix_prompt_kernel(x_ref, m_ref, *refs):
    (pre_ref, post_ref, win32_ref, cw_ref, cb_ref, wa_ref, ba_ref, wx_ref, bx_ref, lam_ref, lbl_ref,
     hgw_ref, wout32_ref) = refs[:13]
    (o_ref, h_out_ref, conv_out_ref, s_out_ref, win_ref, wout_ref,
     ucar_ref, hcar_ref, st_ref, ohg_ref) = refs[13:]
    t = pl.program_id(1)
    n_t = pl.num_programs(1)

    @pl.when((pl.program_id(0) == 0) & (t == 0))
    def _():
        win_ref[...] = win32_ref[...].astype(BF16)
        wout_ref[...] = wout32_ref[...].astype(BF16)

    @pl.when(t == 0)
    def _():
        ucar_ref[...] = jnp.zeros_like(ucar_ref)
        hcar_ref[...] = jnp.zeros_like(hcar_ref)
        st_ref[...] = jnp.zeros_like(st_ref)

    _mix_tile(x_ref.at[0], m_ref.at[0], o_ref.at[0], t, pre_ref, post_ref, win_ref, cw_ref, cb_ref,
              wa_ref, ba_ref, wx_ref, bx_ref, lam_ref, lbl_ref, hgw_ref, wout_ref,
              ucar_ref, hcar_ref, st_ref, ohg_ref)

    @pl.when(t == n_t - 1)
    def _():
        h_out_ref[0] = hcar_ref[...]
        conv_out_ref[0] = ucar_ref[V7X_SUBLANES - (CONV_WIDTH - 1):V7X_SUBLANES, :]
        for hd in range(HG_HEADS):
            s_out_ref[0, hd] = st_ref[hd].T


def _mix_prompt(x, mod3, p):
    batch, seq_len, _ = x.shape
    n_t = seq_len // MIX_TILE
    tok_spec = pl.BlockSpec((1, MIX_TILE, D_MODEL), lambda b, t: (b, t, 0))
    return pl.pallas_call(
        _mix_prompt_kernel,
        grid=(batch, n_t),
        in_specs=[
            tok_spec,
            pl.BlockSpec((1, 3, D_MODEL), lambda b, t: (b, 0, 0)),
            _const_spec((1, D_MODEL)),
            _const_spec((1, D_MODEL)),
            _const_spec((D_MODEL, D_PROJ)),
            _const_spec((CONV_WIDTH, LRU_WIDTH)),
            _const_spec((1, LRU_WIDTH)),
            _const_spec((LRU_WIDTH, LRU_WIDTH)),
            _const_spec((1, LRU_WIDTH)),
            _const_spec((LRU_WIDTH, LRU_WIDTH)),
            _const_spec((1, LRU_WIDTH)),
            _const_spec((1, LRU_WIDTH)),
            _const_spec(p["lb_logits"].shape),
            _const_spec((1, HG_HEAD_DIM)),
            _const_spec((D_MODEL, D_MODEL)),
        ],
        out_specs=[
            tok_spec,
            pl.BlockSpec((1, 1, LRU_WIDTH), lambda b, t: (b, 0, 0)),
            pl.BlockSpec((1, CONV_WIDTH - 1, LRU_WIDTH), lambda b, t: (b, 0, 0)),
            pl.BlockSpec((1, HG_HEADS, HG_HEAD_DIM, HG_HEAD_DIM), lambda b, t: (b, 0, 0, 0)),
            _const_spec((D_MODEL, D_PROJ)),
            _const_spec((D_MODEL, D_MODEL)),
        ],
        out_shape=[
            jax.ShapeDtypeStruct((batch, seq_len, D_MODEL), F32),
            jax.ShapeDtypeStruct((batch, 1, LRU_WIDTH), F32),
            jax.ShapeDtypeStruct((batch, CONV_WIDTH - 1, LRU_WIDTH), F32),
            jax.ShapeDtypeStruct((batch, HG_HEADS, HG_HEAD_DIM, HG_HEAD_DIM), F32),
            jax.ShapeDtypeStruct((D_MODEL, D_PROJ), BF16),
            jax.ShapeDtypeStruct((D_MODEL, D_MODEL), BF16),
        ],
        scratch_shapes=[
            pltpu.VMEM((V7X_SUBLANES, LRU_WIDTH), F32),
            pltpu.VMEM((1, LRU_WIDTH), F32),
            pltpu.VMEM((HG_HEADS, HG_HEAD_DIM, HG_HEAD_DIM), F32),
            pltpu.VMEM((MIX_TILE, HG_WIDTH), F32),
        ],
        compiler_params=pltpu.CompilerParams(
            dimension_semantics=("arbitrary", "arbitrary"), vmem_limit_bytes=VMEM_LIMIT),
        name="mix_prompt",
    )(x, mod3, p["ln_mix_pre"], p["ln_mix_post"], p["w_in"], p["conv_w"], p["conv_b"],
      p["wa_bd"], p["b_a"], p["wx_bd"], p["b_x"], p["lam"], p["lb_logits"], p["hg_norm_w"],
      p["w_out"])


def _mix_sample_kernel(x_ref, m_ref, pre_ref, post_ref, win_ref, cw_ref, cb_ref,
                       wa_ref, ba_ref, wx_ref, bx_ref, lam_ref, lbl_ref, hgw_ref, wout_ref,
                       h0_ref, conv0_ref, s0_ref,
                       o_ref, h_out_ref, conv_out_ref, s_out_ref,
                       qf_ref, f_ref, k_ref, v_ref, g_ref, qk_ref, olru_ref, ohg_ref, ostage_ref):
    i = pl.program_id(0)
    n_i = pl.num_programs(0)
    nb = SAMPLE_SEQ_BLOCK

    @pl.when(i == 0)
    def _():
        x = x_ref[...]
        h = _rms(x, pre_ref[...]) * (1.0 + m_ref[1]) + m_ref[0]
        proj = _dot(h.astype(BF16), win_ref[...])
        u = proj[:, 0:LRU_WIDTH]
        y_lru = proj[:, LRU_WIDTH:2 * LRU_WIDTH]
        o0 = 2 * LRU_WIDTH
        q = proj[:, o0:o0 + HG_WIDTH]
        f_raw = proj[:, o0 + HG_WIDTH:o0 + 2 * HG_WIDTH]
        v = proj[:, o0 + 2 * HG_WIDTH:o0 + 3 * HG_WIDTH]
        g_ref[...] = proj[:, o0 + 3 * HG_WIDTH:o0 + 4 * HG_WIDTH]

        cw = cw_ref[...]
        u_conv = cb_ref[...]
        for k in range(CONV_WIDTH - 1):
            u_conv = u_conv + conv0_ref[k] * cw[k:k + 1]
            if k > 0:
                conv_out_ref[k - 1] = conv0_ref[k]
        u_conv = u_conv + u * cw[CONV_WIDTH - 1:CONV_WIDTH]
        conv_out_ref[CONV_WIDTH - 2] = u

        a, mult, ig = _lru_gates(u_conv, wa_ref, ba_ref[...], wx_ref, bx_ref[...], lam_ref[...])
        hs = (mult * ig) * u_conv + a * h0_ref[...]
        h_out_ref[...] = hs
        olru_ref[...] = hs * _gelu_tanh(y_lru)

        lb = _lower_bound(lbl_ref[...])
        f = lb + (1.0 - lb) * _sigmoid(f_raw)
        kk = 1.0 - f
        qs = q * (HG_HEAD_DIM ** -0.5)
        f_ref[...] = f
        qf_ref[...] = qs * f
        k_ref[...] = kk
        v_ref[...] = v
        qk = qs * kk
        for hd in range(HG_HEADS):
            l0 = hd * HG_HEAD_DIM
            tot = jnp.sum(qk[:, l0:l0 + HG_HEAD_DIM], axis=-1, keepdims=True)
            qk_ref[:, l0:l0 + HG_HEAD_DIM] = jnp.broadcast_to(tot, (qk.shape[0], HG_HEAD_DIM))

    base = pl.multiple_of(i * nb, nb)
    grp = lambda ref: ref[pl.ds(base, nb), :]
    f_g, k_g, v_g, qf_g, qk_g = grp(f_ref), grp(k_ref), grp(v_ref), grp(qf_ref), grp(qk_ref)
    square = (HG_HEAD_DIM, HG_HEAD_DIM)
    for j in range(nb):
        for hd in range(HG_HEADS):
            l0 = hd * HG_HEAD_DIM
            rowv = lambda z: z[j:j + 1, l0:l0 + HG_HEAD_DIM]
            s_old = s0_ref[j, hd]
            f_col = jnp.broadcast_to(rowv(f_g), square).T
            k_col = jnp.broadcast_to(rowv(k_g), square).T
            v_row = rowv(v_g)
            outer = k_col.astype(BF16).astype(F32) * v_row.astype(BF16).astype(F32)
            s_out_ref[j, hd] = f_col * s_old + outer
            qf8 = jnp.broadcast_to(rowv(qf_g), (V7X_SUBLANES, HG_HEAD_DIM)).astype(BF16)
            o1 = _dot(qf8, s_old.astype(BF16))[0:1]
            ostage_ref[j:j + 1, l0:l0 + HG_HEAD_DIM] = o1 + rowv(qk_g) * v_row
    ohg_ref[pl.ds(base, nb), :] = ostage_ref[...]

    @pl.when(i == n_i - 1)
    def _():
        hgw = hgw_ref[...]
        parts = [olru_ref[...]]
        for hd in range(HG_HEADS):
            l0 = hd * HG_HEAD_DIM
            parts.append(_head_rms_gate(ohg_ref[:, l0:l0 + HG_HEAD_DIM], hgw,
                                        g_ref[:, l0:l0 + HG_HEAD_DIM]))
        mix_in = jnp.concatenate(parts, axis=-1).astype(BF16)
        mix = _dot(mix_in, wout_ref[...])
        o_ref[...] = x_ref[...] + m_ref[2] * _rms(mix, post_ref[...])


def _mix_sample(x, mod9, p, h0, conv0, s0):
    n_seq = x.shape[0]
    nb = SAMPLE_SEQ_BLOCK
    state_spec = pl.BlockSpec((nb, HG_HEADS, HG_HEAD_DIM, HG_HEAD_DIM), lambda i: (i, 0, 0, 0))
    full2 = lambda shape: pl.BlockSpec(shape, lambda i: (0,) * len(shape))
    return pl.pallas_call(
        _mix_sample_kernel,
        grid=(n_seq // nb,),
        in_specs=[
            _const_spec((n_seq, D_MODEL)),
            pl.BlockSpec((3, n_seq, D_MODEL), lambda i: (1, 0, 0), pipeline_mode=pl.Buffered(1)),
            _const_spec((1, D_MODEL)),
            _const_spec((1, D_MODEL)),
            _const_spec((D_MODEL, D_PROJ)),
            _const_spec((CONV_WIDTH, LRU_WIDTH)),
            _const_spec((1, LRU_WIDTH)),
            _const_spec((LRU_WIDTH, LRU_WIDTH)),
            _const_spec((1, LRU_WIDTH)),
            _const_spec((LRU_WIDTH, LRU_WIDTH)),
            _const_spec((1, LRU_WIDTH)),
            _const_spec((1, LRU_WIDTH)),
            _const_spec(p["lb_logits"].shape),
            _const_spec((1, HG_HEAD_DIM)),
            _const_spec((D_MODEL, D_MODEL)),
            _const_spec((n_seq, LRU_WIDTH)),
            _const_spec((CONV_WIDTH - 1, n_seq, LRU_WIDTH)),
            state_spec,
        ],
        out_specs=[
            full2((n_seq, D_MODEL)),
            full2((n_seq, LRU_WIDTH)),
            full2((CONV_WIDTH - 1, n_seq, LRU_WIDTH)),
            state_spec,
        ],
        out_shape=[
            jax.ShapeDtypeStruct((n_seq, D_MODEL), F32),
            jax.ShapeDtypeStruct((n_seq, LRU_WIDTH), F32),
            jax.ShapeDtypeStruct((CONV_WIDTH - 1, n_seq, LRU_WIDTH), F32),
            jax.ShapeDtypeStruct((n_seq, HG_HEADS, HG_HEAD_DIM, HG_HEAD_DIM), F32),
        ],
        scratch_shapes=[pltpu.VMEM((n_seq, HG_WIDTH), F32) for _ in range(8)]
        + [pltpu.VMEM((nb, HG_WIDTH), F32)],
        compiler_params=pltpu.CompilerParams(
            dimension_semantics=("arbitrary",), vmem_limit_bytes=VMEM_LIMIT),
        name="mix_sample",
    )(x, mod9, p["ln_mix_pre"], p["ln_mix_post"], p["w_in"], p["conv_w"], p["conv_b"],
      p["wa_bd"], p["b_a"], p["wx_bd"], p["b_x"], p["lam"], p["lb_logits"], p["hg_norm_w"],
      p["w_out"], h0, conv0, s0)


def _block_diag(w):
    heads, blk, _ = w.shape
    eye = jnp.eye(heads, dtype=w.dtype)
    return (eye[:, None, :, None] * w[:, :, None, :]).reshape(heads * blk, heads * blk)


def kernel(x_prompt, x_sample, c_prompt, c_sample, state_lru_h, state_lru_conv, state_hgrn_S, w_ada, b_ada, ln_ffn1_pre, ln_ffn1_post, ffn1_w_gate, ffn1_w_up, ffn1_w_down, ln_mix_pre, ln_mix_post, w_in, lru_conv_w, lru_conv_b, lru_w_a, lru_b_a, lru_w_x, lru_b_x, lru_lambda, hg_lb_logits, hg_norm_w, w_out, ln_ffn2_pre, ln_ffn2_post, ffn2_w_gate, ffn2_w_up, ffn2_w_down):
    depth = w_ada.shape[0]
    batch, seq_len, _ = x_prompt.shape
    n_seq = x_sample.shape[0]
    assert depth == 1 and x_sample.shape[1] == 1
    assert seq_len % FFN_TILE == 0 and seq_len % MIX_TILE == 0 and n_seq % SAMPLE_SEQ_BLOCK == 0
    assert D_PROJ == 6 * LRU_WIDTH and LRU_WIDTH == HG_WIDTH

    xp = x_prompt.reshape(batch * seq_len, D_MODEL)
    xs = x_sample.reshape(n_seq, D_MODEL)
    ph, pc, pS, sh, sc, sS = [], [], [], [], [], []
    for l in range(depth):
        row = lambda w: w[l].reshape(1, -1)
        mod9 = _ada(c_sample, c_prompt, w_ada[l], b_ada[l])
        mod_p = jnp.transpose(mod9[:, n_seq:], (1, 0, 2))
        p = {
            "ln_mix_pre": row(ln_mix_pre), "ln_mix_post": row(ln_mix_post),
            "w_in": w_in[l], "w_out": w_out[l],
            "conv_w": lru_conv_w[l], "conv_b": row(lru_conv_b),
            "wa_bd": _block_diag(lru_w_a[l]).astype(BF16), "b_a": row(lru_b_a),
            "wx_bd": _block_diag(lru_w_x[l]).astype(BF16), "b_x": row(lru_b_x),
            "lam": row(lru_lambda), "lb_logits": hg_lb_logits,
            "hg_norm_w": row(hg_norm_w),
        }
        f1 = (row(ln_ffn1_pre), row(ln_ffn1_post), ffn1_w_gate[l], ffn1_w_up[l], ffn1_w_down[l])
        f2 = (row(ln_ffn2_pre), row(ln_ffn2_post), ffn2_w_gate[l], ffn2_w_up[l], ffn2_w_down[l])

        xp, xs = _ffn(xp, xs, mod_p[:, 0:3], mod9, 0, *f1, seq_len)
        xp, h_p, c_p, S_p, w_in_bf, w_out_bf = _mix_prompt(
            xp.reshape(batch, seq_len, D_MODEL), mod_p[:, 3:6], p)
        xp = xp.reshape(batch * seq_len, D_MODEL)
        conv0 = jnp.transpose(state_lru_conv[l], (1, 0, 2))
        xs, h_s, c_s, S_s = _mix_sample(xs, mod9, dict(p, w_in=w_in_bf, w_out=w_out_bf),
                                        state_lru_h[l], conv0, state_hgrn_S[l])
        xp, xs = _ffn(xp, xs, mod_p[:, 6:9], mod9, 2, *f2, seq_len)

        ph.append(h_p.reshape(batch, LRU_WIDTH)); pc.append(c_p); pS.append(S_p)
        sh.append(h_s); sc.append(jnp.transpose(c_s, (1, 0, 2))); sS.append(S_s)

    return (xp.reshape(batch, seq_len, D_MODEL), xs.reshape(n_seq, 1, D_MODEL),
            jnp.stack(ph), jnp.stack(pc), jnp.stack(pS), jnp.stack(sh), jnp.stack(sc), jnp.stack(sS))
```

```python
import jax
import jax.numpy as jnp
from jax import lax
from jax.experimental import pallas as pl
from jax.experimental.pallas import tpu as pltpu

F32 = jnp.float32
BF16 = jnp.bfloat16

D_MODEL = 1024
D_FF = 2816
LRU_WIDTH = 512
CONV_WIDTH = 4
LRU_C = 8.0
HG_WIDTH = 512
HG_HEAD_DIM = 128
HG_HEADS = HG_WIDTH // HG_HEAD_DIM
HG_CHUNK = 64
N_MOD = 9
D_PROJ = 2 * LRU_WIDTH + 4 * HG_WIDTH
EPS = 1e-6

V7X_SUBLANES = 8
V7X_VMEM_BYTES = 64 * 1024 * 1024
VMEM_LIMIT = V7X_VMEM_BYTES - 8 * 1024 * 1024

FFN_TILE = 1024
FFN_ROWS = 256
MIX_TILE = 256
FFN_WBLOCK = 256
FFN_WSTEPS = D_FF // FFN_WBLOCK
SAMPLE_SEQ_BLOCK = V7X_SUBLANES

HG_LEVELS = (1, 2, 4, 8, 16, 32)
HG_DIRECT_MIN = 2.0 ** -100


def _rms(x, w):
    return (x * lax.rsqrt(jnp.mean(x * x, axis=-1, keepdims=True) + EPS)) * w


def _sigmoid(x):
    return 1.0 / (1.0 + jnp.exp(-x))


def _silu(x):
    return x * _sigmoid(x)


def _gelu_tanh(x):
    c = 0.7978845608028654
    return x * (0.5 * (1.0 + jnp.tanh(c * (x + 0.044715 * (x * x * x)))))


def _softplus(z):
    return jnp.maximum(z, 0.0) + jnp.log1p(jnp.exp(-jnp.abs(z)))


def _dot(a, b):
    return jnp.dot(a, b, preferred_element_type=F32)


def _dot_nt(a, b):
    return lax.dot_general(a, b, (((1,), (1,)), ((), ())), preferred_element_type=F32)


def _dot_tn(a, b):
    return lax.dot_general(a, b, (((0,), (0,)), ((), ())), preferred_element_type=F32)


def _const_spec(shape):
    nd = len(shape)
    return pl.BlockSpec(shape, lambda *_: (0,) * nd, pipeline_mode=pl.Buffered(1))


def _ada_kernel(cs_ref, cp_ref, w_ref, b_ref, o_ref):
    s = _silu(jnp.concatenate([cs_ref[...], cp_ref[...]], axis=0))
    o_ref[0] = _dot(s.astype(BF16), w_ref[...].astype(BF16)) + b_ref[...]


def _ada(c_sample, c_prompt, w_ada, b_ada):
    rows = c_sample.shape[0] + c_prompt.shape[0]
    return pl.pallas_call(
        _ada_kernel,
        grid=(N_MOD,),
        in_specs=[
            _const_spec(c_sample.shape),
            _const_spec(c_prompt.shape),
            pl.BlockSpec((D_MODEL, D_MODEL), lambda j: (0, j)),
            pl.BlockSpec((1, D_MODEL), lambda j: (0, j)),
        ],
        out_specs=pl.BlockSpec((1, rows, D_MODEL), lambda j: (j, 0, 0)),
        out_shape=jax.ShapeDtypeStruct((N_MOD, rows, D_MODEL), F32),
        compiler_params=pltpu.CompilerParams(
            dimension_semantics=("arbitrary",), vmem_limit_bytes=VMEM_LIMIT),
        name="ada_mod",
    )(c_sample, c_prompt, w_ada, b_ada.reshape(1, N_MOD * D_MODEL))


def _ffn_body(x, shift, scale, gate, pre_w, post_w, wg_ref, wu_ref, wd_ref):
    h = _rms(x, pre_w) * (1.0 + scale) + shift
    hb = h.astype(BF16)
    a = _dot(hb, wg_ref[...])
    u = _dot(hb, wu_ref[...])
    act = (_silu(a) * u).astype(BF16)
    y = _dot(act, wd_ref[...])
    return x + (0.5 * gate) * _rms(y, post_w)


def _ffn_kernel(xp_ref, mp_ref, xs_ref, ms_ref, pre_ref, post_ref, wg32_ref, wu32_ref, wd32_ref,
                op_ref, os_ref, wg_ref, wu_ref, wd_ref, hb0_ref, y0_ref):
    s = pl.program_id(0)
    n_steps = pl.num_programs(0)
    row_blocks = range(0, FFN_TILE, FFN_ROWS)

    @pl.when(s == 0)
    def _():
        m = mp_ref[0]
        hb0_ref[...] = (_rms(xp_ref[...], pre_ref[...]) * (1.0 + m[1:2]) + m[0:1]).astype(BF16)
        y0_ref[...] = jnp.zeros_like(y0_ref)

    for j in range(FFN_WSTEPS):
        @pl.when(s == j)
        def _(j=j):
            c0 = j * FFN_WBLOCK
            wg_ref[:, c0:c0 + FFN_WBLOCK] = wg32_ref[...].astype(BF16)
            wu_ref[:, c0:c0 + FFN_WBLOCK] = wu32_ref[...].astype(BF16)
            wd_ref[c0:c0 + FFN_WBLOCK, :] = wd32_ref[...].astype(BF16)

    @pl.when(s < FFN_WSTEPS)
    def _():
        wg_j, wu_j, wd_j = (w[...].astype(BF16) for w in (wg32_ref, wu32_ref, wd32_ref))
        for r0 in row_blocks:
            hb = hb0_ref[r0:r0 + FFN_ROWS, :]
            act = (_silu(_dot(hb, wg_j)) * _dot(hb, wu_j)).astype(BF16)
            y0_ref[r0:r0 + FFN_ROWS, :] += _dot(act, wd_j)

    @pl.when(s == FFN_WSTEPS - 1)
    def _():
        m = mp_ref[0]
        op_ref[...] = xp_ref[...] + (0.5 * m[2:3]) * _rms(y0_ref[...], post_ref[...])

    @pl.when((s >= FFN_WSTEPS) & (s < n_steps - 1))
    def _():
        m = mp_ref[0]
        for r0 in row_blocks:
            op_ref[r0:r0 + FFN_ROWS, :] = _ffn_body(
                xp_ref[r0:r0 + FFN_ROWS, :], m[0:1], m[1:2], m[2:3],
                pre_ref[...], post_ref[...], wg_ref, wu_ref, wd_ref)

    @pl.when(s == n_steps - 1)
    def _():
        os_ref[...] = _ffn_body(xs_ref[...], ms_ref[0], ms_ref[1], ms_ref[2], pre_ref[...],
                                post_ref[...], wg_ref, wu_ref, wd_ref)


def _ffn(xp, xs, mod_p3, mod9, sub_layer, pre_w, post_w, wg, wu, wd, seq_len):
    n_tok = xp.shape[0]
    n_seq = xs.shape[0]
    n_tiles = n_tok // FFN_TILE
    steps_per_seq = seq_len // FFN_TILE
    tile = lambda s: jnp.clip(s - (FFN_WSTEPS - 1), 0, n_tiles - 1)
    wblk = lambda s: jnp.minimum(s, FFN_WSTEPS - 1)
    tok_spec = pl.BlockSpec((FFN_TILE, D_MODEL), lambda s: (tile(s), 0))
    return pl.pallas_call(
        _ffn_kernel,
        grid=(FFN_WSTEPS + n_tiles,),
        in_specs=[
            tok_spec,
            pl.BlockSpec((1, 3, D_MODEL), lambda s: (tile(s) // steps_per_seq, 0, 0)),
            _const_spec((n_seq, D_MODEL)),
            pl.BlockSpec((3, n_seq, D_MODEL), lambda s: (sub_layer, 0, 0), pipeline_mode=pl.Buffered(1)),
            _const_spec((1, D_MODEL)),
            _const_spec((1, D_MODEL)),
            pl.BlockSpec((D_MODEL, FFN_WBLOCK), lambda s: (0, wblk(s))),
            pl.BlockSpec((D_MODEL, FFN_WBLOCK), lambda s: (0, wblk(s))),
            pl.BlockSpec((FFN_WBLOCK, D_MODEL), lambda s: (wblk(s), 0)),
        ],
        out_specs=[tok_spec, pl.BlockSpec((n_seq, D_MODEL), lambda s: (0, 0))],
        out_shape=[jax.ShapeDtypeStruct((n_tok, D_MODEL), F32),
                   jax.ShapeDtypeStruct((n_seq, D_MODEL), F32)],
        scratch_shapes=[pltpu.VMEM((D_MODEL, D_FF), BF16), pltpu.VMEM((D_MODEL, D_FF), BF16),
                        pltpu.VMEM((D_FF, D_MODEL), BF16),
                        pltpu.VMEM((FFN_TILE, D_MODEL), BF16),
                        pltpu.VMEM((FFN_TILE, D_MODEL), F32)],
        compiler_params=pltpu.CompilerParams(
            dimension_semantics=("arbitrary",), vmem_limit_bytes=VMEM_LIMIT),
        name="ffn",
    )(xp, mod_p3, xs, mod9, pre_w, post_w, wg, wu, wd)


def _lower_bound(lb_logits):
    z = lb_logits - jnp.max(lb_logits, axis=0, keepdims=True)
    e = jnp.exp(z)
    return e[0:1] / jnp.sum(e, axis=0, keepdims=True)


def _lru_gates(u_conv, wa_ref, ba, wx_ref, bx, lam):
    ub = u_conv.astype(BF16)
    r = _sigmoid(_dot(ub, wa_ref[...]) + ba)
    ig = _sigmoid(_dot(ub, wx_ref[...]) + bx)
    log_a = (-LRU_C * r) * _softplus(-lam)
    a = jnp.exp(log_a)
    th = jnp.tanh(log_a)
    mult = jnp.sqrt((-2.0 * th) / (1.0 - th))
    return a, mult, ig


def _head_rms_gate(o, norm_w, g):
    return _rms(o, norm_w) * _silu(g)


def _hgrn_chunk_head(scores, q_in, k_out, dec, vb_blk, g_blk, hgw, st_ref, hd, ohg_ref, r0, l0):
    st = st_ref[hd]
    o = _dot_nt(q_in, st.astype(BF16)) + _dot(scores.astype(BF16), vb_blk)
    st_ref[hd] = st * dec + _dot_tn(vb_blk, k_out)
    ohg_ref[r0:r0 + HG_CHUNK, l0:l0 + HG_HEAD_DIM] = _head_rms_gate(o, hgw, g_blk)


def _hgrn_chunks_direct(qs, kk, pfx, tails, vb, g, hgw, st_ref, ohg_ref):
    tb = qs.shape[0]
    n_chunk = len(tails)
    inv = 1.0 / pfx
    q_dec = qs * pfx
    k_dec = kk * inv
    q_in = q_dec.astype(BF16)
    k_in = k_dec.astype(BF16)
    ti = lax.broadcasted_iota(jnp.int32, (tb, tb), 0)
    si = lax.broadcasted_iota(jnp.int32, (tb, tb), 1)
    visible = (ti >= si) & ((ti ^ si) < HG_CHUNK)
    row_chunk = jnp.right_shift(lax.broadcasted_iota(jnp.int32, (tb, HG_HEAD_DIM), 0),
                                HG_CHUNK.bit_length() - 1)
    tail_rows = jnp.concatenate([jnp.broadcast_to(t, (HG_CHUNK, t.shape[-1])) for t in tails], axis=0)
    k_end = k_dec * tail_rows
    for hd in range(HG_HEADS):
        l0 = hd * HG_HEAD_DIM
        head = lambda z: z[:, l0:l0 + HG_HEAD_DIM]
        by_chunk = lambda z: jnp.concatenate(
            [jnp.where(row_chunk == c, head(z), 0.0) for c in range(n_chunk)], axis=-1).astype(BF16)
        scores = jnp.where(visible, _dot_nt(head(q_in), head(k_in)), 0.0)
        grown = _dot_tn(head(vb), by_chunk(k_end))
        st = st_ref[hd]
        states = []
        for c, tail in enumerate(tails):
            states.append(st.astype(BF16))
            st = st * head(tail) + grown[:, c * HG_HEAD_DIM:(c + 1) * HG_HEAD_DIM]
        st_ref[hd] = st
        o = (_dot_nt(by_chunk(q_dec), jnp.concatenate(states, axis=-1))
             + _dot(scores.astype(BF16), head(vb)))
        ohg_ref[:, l0:l0 + HG_HEAD_DIM] = _head_rms_gate(o, hgw, head(g))


def _hgrn_chunks_levels(f3, kk3, qs3, vb, g, hgw, st_ref, ohg_ref):
    n_grp = f3.shape[0]
    tb = n_grp * V7X_SUBLANES
    sub = lax.broadcasted_iota(jnp.int32, f3.shape, 1)
    grp = lax.broadcasted_iota(jnp.int32, f3.shape, 0)
    to_mxu = lambda z: z.reshape(tb, z.shape[-1]).astype(BF16)
    q_lv = [to_mxu(qs3)]
    k_lv = [to_mxu(kk3)]
    tot, pre, post = f3, f3, None
    for lvl in HG_LEVELS:
        q_lv.append(to_mxu(qs3 * pre))
        k_lv.append(k_lv[0] if post is None else to_mxu(kk3 * post))
        if lvl < V7X_SUBLANES:
            upper = (sub & lvl) != 0
            below = pltpu.roll(tot, lvl, 1)
            above = pltpu.roll(tot, V7X_SUBLANES - lvl, 1)
        else:
            upper = (grp & (lvl // V7X_SUBLANES)) != 0
            below = jnp.roll(tot, lvl // V7X_SUBLANES, axis=0)
            above = jnp.roll(tot, -(lvl // V7X_SUBLANES), axis=0)
        pre = jnp.where(upper, pre * below, pre)
        post = jnp.where(upper, 1.0, above) if post is None else jnp.where(upper, post, post * above)
        tot = tot * jnp.where(upper, below, above)
    q_in = to_mxu(qs3 * pre)
    k_out = to_mxu(kk3 * post)
    s_decay = tot.reshape(tb, tot.shape[-1])

    ti = lax.broadcasted_iota(jnp.int32, (HG_CHUNK, HG_CHUNK), 0)
    si = lax.broadcasted_iota(jnp.int32, (HG_CHUNK, HG_CHUNK), 1)
    masks = [ti == si]
    tx = ti ^ si
    for lvl in HG_LEVELS:
        masks.append((tx >= lvl) & (tx < 2 * lvl) & ((ti & lvl) != 0))

    for c in range(tb // HG_CHUNK):
        r0 = c * HG_CHUNK
        for hd in range(HG_HEADS):
            l0 = hd * HG_HEAD_DIM
            blk = lambda z: z[r0:r0 + HG_CHUNK, l0:l0 + HG_HEAD_DIM]
            scores = jnp.zeros((HG_CHUNK, HG_CHUNK), F32)
            for ql, kl, msk in zip(q_lv, k_lv, masks):
                scores = jnp.where(msk, _dot_nt(blk(ql), blk(kl)), scores)
            dec = s_decay[r0 + HG_CHUNK - 1:r0 + HG_CHUNK, l0:l0 + HG_HEAD_DIM]
            _hgrn_chunk_head(scores, blk(q_in), blk(k_out), dec, blk(vb), blk(g), hgw,
                             st_ref, hd, ohg_ref, r0, l0)

def _mix_tile(x_ref, m_ref, o_ref, t, pre_ref, post_ref, win_ref, cw_ref, cb_ref, wa_ref, ba_ref,
              wx_ref, bx_ref, lam_ref, lbl_ref, hgw_ref, wout_ref, ucar_ref, hcar_ref, st_ref, ohg_ref):
    tb = MIX_TILE
    x = x_ref[...]
    m = m_ref[...]
    hb = (_rms(x, pre_ref[...]) * (1.0 + m[1:2]) + m[0:1]).astype(BF16)
    u, y_lru, q, f_raw, v, g = [_dot(hb, win_ref[:, c0:c0 + LRU_WIDTH])
                                for c0 in range(0, D_PROJ, LRU_WIDTH)]

    n_grp = tb // V7X_SUBLANES
    grouped = lambda z: z.reshape(n_grp, V7X_SUBLANES, z.shape[-1])
    flat = lambda z: z.reshape(tb, z.shape[-1])
    gshape = (n_grp, V7X_SUBLANES, LRU_WIDTH)
    sub = lax.broadcasted_iota(jnp.int32, gshape, 1)
    grp = lax.broadcasted_iota(jnp.int32, gshape, 0)

    u3 = grouped(u)
    u_all = jnp.concatenate([ucar_ref[...][None], u3], axis=0)
    ucar_ref[...] = u3[n_grp - 1]
    cw = cw_ref[...]
    u_conv = cb_ref[...]
    for k in range(CONV_WIDTH - 1):
        back = CONV_WIDTH - 1 - k
        rot = pltpu.roll(u_all, back, 1)
        u_conv = u_conv + jnp.where(sub >= back, rot[1:], rot[:-1]) * cw[k:k + 1]
    u_conv = flat(u_conv + u3 * cw[CONV_WIDTH - 1:CONV_WIDTH])

    a, mult, ig = _lru_gates(u_conv, wa_ref, ba_ref[...], wx_ref, bx_ref[...], lam_ref[...])
    a3, mult3 = grouped(a), grouped(mult)
    first_row = jnp.where(t == 0, 0, -1)
    mult3 = jnp.where(grp * V7X_SUBLANES + sub == first_row, 1.0, mult3)
    b3 = (mult3 * grouped(ig)) * grouped(u_conv)
    s = 1
    while s < V7X_SUBLANES:
        keep = sub >= s
        b3 = jnp.where(keep, a3 * pltpu.roll(b3, s, 1) + b3, b3)
        a3 = jnp.where(keep, a3 * pltpu.roll(a3, s, 1), a3)
        s *= 2
    carry = hcar_ref[...]
    hs = []
    for gi in range(n_grp):
        h_g = a3[gi] * carry + b3[gi]
        carry = h_g[V7X_SUBLANES - 1:V7X_SUBLANES]
        hs.append(h_g)
    hcar_ref[...] = carry
    o_lru = jnp.concatenate(hs, axis=0) * _gelu_tanh(y_lru)

    lb = _lower_bound(lbl_ref[...])
    f3 = grouped(lb + (1.0 - lb) * _sigmoid(f_raw))
    kk3 = 1.0 - f3
    qs3 = grouped(q * (HG_HEAD_DIM ** -0.5))
    vb = v.astype(BF16)
    hgw = hgw_ref[...]

    pfx = f3
    s = 1
    while s < V7X_SUBLANES:
        pfx = jnp.where(sub >= s, pfx * pltpu.roll(pfx, s, 1), pfx)
        s *= 2
    grp_per_chunk = HG_CHUNK // V7X_SUBLANES
    rows, tails = [], []
    for gi in range(n_grp):
        p_g = pfx[gi] if gi % grp_per_chunk == 0 else pfx[gi] * carry
        carry = p_g[V7X_SUBLANES - 1:V7X_SUBLANES]
        rows.append(p_g)
        if gi % grp_per_chunk == grp_per_chunk - 1:
            tails.append(carry)
    pfx = jnp.concatenate(rows, axis=0)
    direct = jnp.min(jnp.concatenate(tails, axis=0)) >= HG_DIRECT_MIN

    @pl.when(direct)
    def _():
        _hgrn_chunks_direct(flat(qs3), flat(kk3), pfx, tails, vb, g, hgw, st_ref, ohg_ref)

    @pl.when(jnp.logical_not(direct))
    def _():
        _hgrn_chunks_levels(f3, kk3, qs3, vb, g, hgw, st_ref, ohg_ref)

    mix_in = jnp.concatenate([o_lru, ohg_ref[...]], axis=-1).astype(BF16)
    mix = _dot(mix_in, wout_ref[...])
    o_ref[...] = x + m[2:3] * _rms(mix, post_ref[...])


def _mix_prompt_kernel(x_ref, m_ref, *refs):
    (pre_ref, post_ref, win32_ref, cw_ref, cb_ref, wa_ref, ba_ref, wx_ref, bx_ref, lam_ref, lbl_ref,
     hgw_ref, wout32_ref) = refs[:13]
    (o_ref, h_out_ref, conv_out_ref, s_out_ref, win_ref, wout_ref,
     ucar_ref, hcar_ref, st_ref, ohg_ref) = refs[13:]
    t = pl.program_id(1)
    n_t = pl.num_programs(1)

    @pl.when((pl.program_id(0) == 0) & (t == 0))
    def _():
        win_ref[...] = win32_ref[...].astype(BF16)
        wout_ref[...] = wout32_ref[...].astype(BF16)

    @pl.when(t == 0)
    def _():
        ucar_ref[...] = jnp.zeros_like(ucar_ref)
        hcar_ref[...] = jnp.zeros_like(hcar_ref)
        st_ref[...] = jnp.zeros_like(st_ref)

    _mix_tile(x_ref.at[0], m_ref.at[0], o_ref.at[0], t, pre_ref, post_ref, win_ref, cw_ref, cb_ref,
              wa_ref, ba_ref, wx_ref, bx_ref, lam_ref, lbl_ref, hgw_ref, wout_ref,
              ucar_ref, hcar_ref, st_ref, ohg_ref)

    @pl.when(t == n_t - 1)
    def _():
        h_out_ref[0] = hcar_ref[...]
        conv_out_ref[0] = ucar_ref[V7X_SUBLANES - (CONV_WIDTH - 1):V7X_SUBLANES, :]
        for hd in range(HG_HEADS):
            s_out_ref[0, hd] = st_ref[hd].T


def _mix_prompt(x, mod3, p):
    batch, seq_len, _ = x.shape
    n_t = seq_len // MIX_TILE
    tok_spec = pl.BlockSpec((1, MIX_TILE, D_MODEL), lambda b, t: (b, t, 0))
    return pl.pallas_call(
        _mix_prompt_kernel,
        grid=(batch, n_t),
        in_specs=[
            tok_spec,
            pl.BlockSpec((1, 3, D_MODEL), lambda b, t: (b, 0, 0)),
            _const_spec((1, D_MODEL)),
            _const_spec((1, D_MODEL)),
            _const_spec((D_MODEL, D_PROJ)),
            _const_spec((CONV_WIDTH, LRU_WIDTH)),
            _const_spec((1, LRU_WIDTH)),
            _const_spec((LRU_WIDTH, LRU_WIDTH)),
            _const_spec((1, LRU_WIDTH)),
            _const_spec((LRU_WIDTH, LRU_WIDTH)),
            _const_spec((1, LRU_WIDTH)),
            _const_spec((1, LRU_WIDTH)),
            _const_spec(p["lb_logits"].shape),
            _const_spec((1, HG_HEAD_DIM)),
            _const_spec((D_MODEL, D_MODEL)),
        ],
        out_specs=[
            tok_spec,
            pl.BlockSpec((1, 1, LRU_WIDTH), lambda b, t: (b, 0, 0)),
            pl.BlockSpec((1, CONV_WIDTH - 1, LRU_WIDTH), lambda b, t: (b, 0, 0)),
            pl.BlockSpec((1, HG_HEADS, HG_HEAD_DIM, HG_HEAD_DIM), lambda b, t: (b, 0, 0, 0)),
            _const_spec((D_MODEL, D_PROJ)),
            _const_spec((D_MODEL, D_MODEL)),
        ],
        out_shape=[
            jax.ShapeDtypeStruct((batch, seq_len, D_MODEL), F32),
            jax.ShapeDtypeStruct((batch, 1, LRU_WIDTH), F32),
            jax.ShapeDtypeStruct((batch, CONV_WIDTH - 1, LRU_WIDTH), F32),
            jax.ShapeDtypeStruct((batch, HG_HEADS, HG_HEAD_DIM, HG_HEAD_DIM), F32),
            jax.ShapeDtypeStruct((D_MODEL, D_PROJ), BF16),
            jax.ShapeDtypeStruct((D_MODEL, D_MODEL), BF16),
        ],
        scratch_shapes=[
            pltpu.VMEM((V7X_SUBLANES, LRU_WIDTH), F32),
            pltpu.VMEM((1, LRU_WIDTH), F32),
            pltpu.VMEM((HG_HEADS, HG_HEAD_DIM, HG_HEAD_DIM), F32),
            pltpu.VMEM((MIX_TILE, HG_WIDTH), F32),
        ],
        compiler_params=pltpu.CompilerParams(
            dimension_semantics=("arbitrary", "arbitrary"), vmem_limit_bytes=VMEM_LIMIT),
        name="mix_prompt",
    )(x, mod3, p["ln_mix_pre"], p["ln_mix_post"], p["w_in"], p["conv_w"], p["conv_b"],
      p["wa_bd"], p["b_a"], p["wx_bd"], p["b_x"], p["lam"], p["lb_logits"], p["hg_norm_w"],
      p["w_out"])


def _mix_sample_kernel(x_ref, m_ref, pre_ref, post_ref, win_ref, cw_ref, cb_ref,
                       wa_ref, ba_ref, wx_ref, bx_ref, lam_ref, lbl_ref, hgw_ref, wout_ref,
                       h0_ref, conv0_ref, s0_ref,
                       o_ref, h_out_ref, conv_out_ref, s_out_ref,
                       qf_ref, f_ref, k_ref, v_ref, g_ref, qk_ref, olru_ref, ohg_ref, ostage_ref):
    i = pl.program_id(0)
    n_i = pl.num_programs(0)
    nb = SAMPLE_SEQ_BLOCK

    @pl.when(i == 0)
    def _():
        x = x_ref[...]
        h = _rms(x, pre_ref[...]) * (1.0 + m_ref[1]) + m_ref[0]
        proj = _dot(h.astype(BF16), win_ref[...])
        u = proj[:, 0:LRU_WIDTH]
        y_lru = proj[:, LRU_WIDTH:2 * LRU_WIDTH]
        o0 = 2 * LRU_WIDTH
        q = proj[:, o0:o0 + HG_WIDTH]
        f_raw = proj[:, o0 + HG_WIDTH:o0 + 2 * HG_WIDTH]
        v = proj[:, o0 + 2 * HG_WIDTH:o0 + 3 * HG_WIDTH]
        g_ref[...] = proj[:, o0 + 3 * HG_WIDTH:o0 + 4 * HG_WIDTH]

        cw = cw_ref[...]
        u_conv = cb_ref[...]
        for k in range(CONV_WIDTH - 1):
            u_conv = u_conv + conv0_ref[k] * cw[k:k + 1]
            if k > 0:
                conv_out_ref[k - 1] = conv0_ref[k]
        u_conv = u_conv + u * cw[CONV_WIDTH - 1:CONV_WIDTH]
        conv_out_ref[CONV_WIDTH - 2] = u

        a, mult, ig = _lru_gates(u_conv, wa_ref, ba_ref[...], wx_ref, bx_ref[...], lam_ref[...])
        hs = (mult * ig) * u_conv + a * h0_ref[...]
        h_out_ref[...] = hs
        olru_ref[...] = hs * _gelu_tanh(y_lru)

        lb = _lower_bound(lbl_ref[...])
        f = lb + (1.0 - lb) * _sigmoid(f_raw)
        kk = 1.0 - f
        qs = q * (HG_HEAD_DIM ** -0.5)
        f_ref[...] = f
        qf_ref[...] = qs * f
        k_ref[...] = kk
        v_ref[...] = v
        qk = qs * kk
        for hd in range(HG_HEADS):
            l0 = hd * HG_HEAD_DIM
            tot = jnp.sum(qk[:, l0:l0 + HG_HEAD_DIM], axis=-1, keepdims=True)
            qk_ref[:, l0:l0 + HG_HEAD_DIM] = jnp.broadcast_to(tot, (qk.shape[0], HG_HEAD_DIM))

    base = pl.multiple_of(i * nb, nb)
    grp = lambda ref: ref[pl.ds(base, nb), :]
    f_g, k_g, v_g, qf_g, qk_g = grp(f_ref), grp(k_ref), grp(v_ref), grp(qf_ref), grp(qk_ref)
    square = (HG_HEAD_DIM, HG_HEAD_DIM)
    for j in range(nb):
        for hd in range(HG_HEADS):
            l0 = hd * HG_HEAD_DIM
            rowv = lambda z: z[j:j + 1, l0:l0 + HG_HEAD_DIM]
            s_old = s0_ref[j, hd]
            f_col = jnp.broadcast_to(rowv(f_g), square).T
            k_col = jnp.broadcast_to(rowv(k_g), square).T
            v_row = rowv(v_g)
            outer = k_col.astype(BF16).astype(F32) * v_row.astype(BF16).astype(F32)
            s_out_ref[j, hd] = f_col * s_old + outer
            qf8 = jnp.broadcast_to(rowv(qf_g), (V7X_SUBLANES, HG_HEAD_DIM)).astype(BF16)
            o1 = _dot(qf8, s_old.astype(BF16))[0:1]
            ostage_ref[j:j + 1, l0:l0 + HG_HEAD_DIM] = o1 + rowv(qk_g) * v_row
    ohg_ref[pl.ds(base, nb), :] = ostage_ref[...]

    @pl.when(i == n_i - 1)
    def _():
        hgw = hgw_ref[...]
        parts = [olru_ref[...]]
        for hd in range(HG_HEADS):
            l0 = hd * HG_HEAD_DIM
            parts.append(_head_rms_gate(ohg_ref[:, l0:l0 + HG_HEAD_DIM], hgw,
                                        g_ref[:, l0:l0 + HG_HEAD_DIM]))
        mix_in = jnp.concatenate(parts, axis=-1).astype(BF16)
        mix = _dot(mix_in, wout_ref[...])
        o_ref[...] = x_ref[...] + m_ref[2] * _rms(mix, post_ref[...])


def _mix_sample(x, mod9, p, h0, conv0, s0):
    n_seq = x.shape[0]
    nb = SAMPLE_SEQ_BLOCK
    state_spec = pl.BlockSpec((nb, HG_HEADS, HG_HEAD_DIM, HG_HEAD_DIM), lambda i: (i, 0, 0, 0))
    full2 = lambda shape: pl.BlockSpec(shape, lambda i: (0,) * len(shape))
    return pl.pallas_call(
        _mix_sample_kernel,
        grid=(n_seq // nb,),
        in_specs=[
            _const_spec((n_seq, D_MODEL)),
            pl.BlockSpec((3, n_seq, D_MODEL), lambda i: (1, 0, 0), pipeline_mode=pl.Buffered(1)),
            _const_spec((1, D_MODEL)),
            _const_spec((1, D_MODEL)),
            _const_spec((D_MODEL, D_PROJ)),
            _const_spec((CONV_WIDTH, LRU_WIDTH)),
            _const_spec((1, LRU_WIDTH)),
            _const_spec((LRU_WIDTH, LRU_WIDTH)),
            _const_spec((1, LRU_WIDTH)),
            _const_spec((LRU_WIDTH, LRU_WIDTH)),
            _const_spec((1, LRU_WIDTH)),
            _const_spec((1, LRU_WIDTH)),
            _const_spec(p["lb_logits"].shape),
            _const_spec((1, HG_HEAD_DIM)),
            _const_spec((D_MODEL, D_MODEL)),
            _const_spec((n_seq, LRU_WIDTH)),
            _const_spec((CONV_WIDTH - 1, n_seq, LRU_WIDTH)),
            state_spec,
        ],
        out_specs=[
            full2((n_seq, D_MODEL)),
            full2((n_seq, LRU_WIDTH)),
            full2((CONV_WIDTH - 1, n_seq, LRU_WIDTH)),
            state_spec,
        ],
        out_shape=[
            jax.ShapeDtypeStruct((n_seq, D_MODEL), F32),
            jax.ShapeDtypeStruct((n_seq, LRU_WIDTH), F32),
            jax.ShapeDtypeStruct((CONV_WIDTH - 1, n_seq, LRU_WIDTH), F32),
            jax.ShapeDtypeStruct((n_seq, HG_HEADS, HG_HEAD_DIM, HG_HEAD_DIM), F32),
        ],
        scratch_shapes=[pltpu.VMEM((n_seq, HG_WIDTH), F32) for _ in range(8)]
        + [pltpu.VMEM((nb, HG_WIDTH), F32)],
        compiler_params=pltpu.CompilerParams(
            dimension_semantics=("arbitrary",), vmem_limit_bytes=VMEM_LIMIT),
        name="mix_sample",
    )(x, mod9, p["ln_mix_pre"], p["ln_mix_post"], p["w_in"], p["conv_w"], p["conv_b"],
      p["wa_bd"], p["b_a"], p["wx_bd"], p["b_x"], p["lam"], p["lb_logits"], p["hg_norm_w"],
      p["w_out"], h0, conv0, s0)


def _block_diag(w):
    heads, blk, _ = w.shape
    eye = jnp.eye(heads, dtype=w.dtype)
    return (eye[:, None, :, None] * w[:, :, None, :]).reshape(heads * blk, heads * blk)


def kernel(x_prompt, x_sample, c_prompt, c_sample, state_lru_h, state_lru_conv, state_hgrn_S, w_ada, b_ada, ln_ffn1_pre, ln_ffn1_post, ffn1_w_gate, ffn1_w_up, ffn1_w_down, ln_mix_pre, ln_mix_post, w_in, lru_conv_w, lru_conv_b, lru_w_a, lru_b_a, lru_w_x, lru_b_x, lru_lambda, hg_lb_logits, hg_norm_w, w_out, ln_ffn2_pre, ln_ffn2_post, ffn2_w_gate, ffn2_w_up, ffn2_w_down):
    depth = w_ada.shape[0]
    batch, seq_len, _ = x_prompt.shape
    n_seq = x_sample.shape[0]
    assert depth == 1 and x_sample.shape[1] == 1
    assert seq_len % FFN_TILE == 0 and seq_len % MIX_TILE == 0 and n_seq % SAMPLE_SEQ_BLOCK == 0
    assert D_PROJ == 6 * LRU_WIDTH and LRU_WIDTH == HG_WIDTH

    xp = x_prompt.reshape(batch * seq_len, D_MODEL)
    xs = x_sample.reshape(n_seq, D_MODEL)
    ph, pc, pS, sh, sc, sS = [], [], [], [], [], []
    for l in range(depth):
        row = lambda w: w[l].reshape(1, -1)
        mod9 = _ada(c_sample, c_prompt, w_ada[l], b_ada[l])
        mod_p = jnp.transpose(mod9[:, n_seq:], (1, 0, 2))
        p = {
            "ln_mix_pre": row(ln_mix_pre), "ln_mix_post": row(ln_mix_post),
            "w_in": w_in[l], "w_out": w_out[l],
            "conv_w": lru_conv_w[l], "conv_b": row(lru_conv_b),
            "wa_bd": _block_diag(lru_w_a[l]).astype(BF16), "b_a": row(lru_b_a),
            "wx_bd": _block_diag(lru_w_x[l]).astype(BF16), "b_x": row(lru_b_x),
            "lam": row(lru_lambda), "lb_logits": hg_lb_logits,
            "hg_norm_w": row(hg_norm_w),
        }
        f1 = (row(ln_ffn1_pre), row(ln_ffn1_post), ffn1_w_gate[l], ffn1_w_up[l], ffn1_w_down[l])
        f2 = (row(ln_ffn2_pre), row(ln_ffn2_post), ffn2_w_gate[l], ffn2_w_up[l], ffn2_w_down[l])

        xp, xs = _ffn(xp, xs, mod_p[:, 0:3], mod9, 0, *f1, seq_len)
        xp, h_p, c_p, S_p, w_in_bf, w_out_bf = _mix_prompt(
            xp.reshape(batch, seq_len, D_MODEL), mod_p[:, 3:6], p)
        xp = xp.reshape(batch * seq_len, D_MODEL)
        conv0 = jnp.transpose(state_lru_conv[l], (1, 0, 2))
        xs, h_s, c_s, S_s = _mix_sample(xs, mod9, dict(p, w_in=w_in_bf, w_out=w_out_bf),
                                        state_lru_h[l], conv0, state_hgrn_S[l])
        xp, xs = _ffn(xp, xs, mod_p[:, 6:9], mod9, 2, *f2, seq_len)

        ph.append(h_p.reshape(batch, LRU_WIDTH)); pc.append(c_p); pS.append(S_p)
        sh.append(h_s); sc.append(jnp.transpose(c_s, (1, 0, 2))); sS.append(S_s)

    return (xp.reshape(batch, seq_len, D_MODEL), xs.reshape(n_seq, 1, D_MODEL),
            jnp.stack(ph), jnp.stack(pc), jnp.stack(pS), jnp.stack(sh), jnp.stack(sc), jnp.stack(sS))
```

```python
import jax
import jax.numpy as jnp
from jax import lax
from jax.experimental import pallas as pl
from jax.experimental.pallas import tpu as pltpu

F32 = jnp.float32
BF16 = jnp.bfloat16

D_MODEL = 1024
D_FF = 2816
LRU_WIDTH = 512
CONV_WIDTH = 4
LRU_C = 8.0
HG_WIDTH = 512
HG_HEAD_DIM = 128
HG_HEADS = HG_WIDTH // HG_HEAD_DIM
HG_CHUNK = 64
N_MOD = 9
D_PROJ = 2 * LRU_WIDTH + 4 * HG_WIDTH
EPS = 1e-6

V7X_SUBLANES = 8
V7X_VMEM_BYTES = 64 * 1024 * 1024
VMEM_LIMIT = V7X_VMEM_BYTES - 8 * 1024 * 1024

FFN_TILE = 1024
FFN_ROWS = 256
MIX_TILE = 512
HG_DIRECT_ROWS = 256
FFN_WBLOCK = 256
FFN_WSTEPS = D_FF // FFN_WBLOCK
SAMPLE_SEQ_BLOCK = V7X_SUBLANES

HG_LEVELS = (1, 2, 4, 8, 16, 32)
HG_DIRECT_MIN = 2.0 ** -100


def _rms(x, w):
    return (x * lax.rsqrt(jnp.mean(x * x, axis=-1, keepdims=True) + EPS)) * w


def _sigmoid(x):
    return 1.0 / (1.0 + jnp.exp(-x))


def _silu(x):
    return x * _sigmoid(x)


def _gelu_tanh(x):
    c = 0.7978845608028654
    return x * (0.5 * (1.0 + jnp.tanh(c * (x + 0.044715 * (x * x * x)))))


def _softplus(z):
    return jnp.maximum(z, 0.0) + jnp.log1p(jnp.exp(-jnp.abs(z)))


def _dot(a, b):
    return jnp.dot(a, b, preferred_element_type=F32)


def _dot_nt(a, b):
    return lax.dot_general(a, b, (((1,), (1,)), ((), ())), preferred_element_type=F32)


def _dot_tn(a, b):
    return lax.dot_general(a, b, (((0,), (0,)), ((), ())), preferred_element_type=F32)


def _const_spec(shape):
    nd = len(shape)
    return pl.BlockSpec(shape, lambda *_: (0,) * nd, pipeline_mode=pl.Buffered(1))


def _ada_kernel(cs_ref, cp_ref, w_ref, b_ref, o_ref):
    s = _silu(jnp.concatenate([cs_ref[...], cp_ref[...]], axis=0))
    o_ref[0] = _dot(s.astype(BF16), w_ref[...].astype(BF16)) + b_ref[...]


def _ada(c_sample, c_prompt, w_ada, b_ada):
    rows = c_sample.shape[0] + c_prompt.shape[0]
    return pl.pallas_call(
        _ada_kernel,
        grid=(N_MOD,),
        in_specs=[
            _const_spec(c_sample.shape),
            _const_spec(c_prompt.shape),
            pl.BlockSpec((D_MODEL, D_MODEL), lambda j: (0, j)),
            pl.BlockSpec((1, D_MODEL), lambda j: (0, j)),
        ],
        out_specs=pl.BlockSpec((1, rows, D_MODEL), lambda j: (j, 0, 0)),
        out_shape=jax.ShapeDtypeStruct((N_MOD, rows, D_MODEL), F32),
        compiler_params=pltpu.CompilerParams(
            dimension_semantics=("arbitrary",), vmem_limit_bytes=VMEM_LIMIT),
        name="ada_mod",
    )(c_sample, c_prompt, w_ada, b_ada.reshape(1, N_MOD * D_MODEL))


def _ffn_body(x, shift, scale, gate, pre_w, post_w, wg_ref, wu_ref, wd_ref):
    h = _rms(x, pre_w) * (1.0 + scale) + shift
    hb = h.astype(BF16)
    a = _dot(hb, wg_ref[...])
    u = _dot(hb, wu_ref[...])
    act = (_silu(a) * u).astype(BF16)
    y = _dot(act, wd_ref[...])
    return x + (0.5 * gate) * _rms(y, post_w)


def _ffn_kernel(xp_ref, mp_ref, xs_ref, ms_ref, pre_ref, post_ref, wg32_ref, wu32_ref, wd32_ref,
                op_ref, os_ref, wg_ref, wu_ref, wd_ref):
    s = pl.program_id(0)
    n_steps = pl.num_programs(0)

    for j in range(FFN_WSTEPS):
        @pl.when(s == j)
        def _(j=j):
            c0 = j * FFN_WBLOCK
            wg_ref[:, c0:c0 + FFN_WBLOCK] = wg32_ref[...].astype(BF16)
            wu_ref[:, c0:c0 + FFN_WBLOCK] = wu32_ref[...].astype(BF16)
            wd_ref[c0:c0 + FFN_WBLOCK, :] = wd32_ref[...].astype(BF16)

    @pl.when((s >= FFN_WSTEPS) & (s < n_steps - 1))
    def _():
        m = mp_ref[0]
        for r0 in range(0, FFN_TILE, FFN_ROWS):
            op_ref[r0:r0 + FFN_ROWS, :] = _ffn_body(
                xp_ref[r0:r0 + FFN_ROWS, :], m[0:1], m[1:2], m[2:3],
                pre_ref[...], post_ref[...], wg_ref, wu_ref, wd_ref)

    @pl.when(s == n_steps - 1)
    def _():
        os_ref[...] = _ffn_body(xs_ref[...], ms_ref[0], ms_ref[1], ms_ref[2], pre_ref[...],
                                post_ref[...], wg_ref, wu_ref, wd_ref)


def _ffn(xp, xs, mod_p3, mod9, sub_layer, pre_w, post_w, wg, wu, wd, seq_len):
    n_tok = xp.shape[0]
    n_seq = xs.shape[0]
    n_tiles = n_tok // FFN_TILE
    steps_per_seq = seq_len // FFN_TILE
    tile = lambda s: jnp.clip(s - FFN_WSTEPS, 0, n_tiles - 1)
    wblk = lambda s: jnp.minimum(s, FFN_WSTEPS - 1)
    tok_spec = pl.BlockSpec((FFN_TILE, D_MODEL), lambda s: (tile(s), 0))
    return pl.pallas_call(
        _ffn_kernel,
        grid=(FFN_WSTEPS + n_tiles + 1,),
        in_specs=[
            tok_spec,
            pl.BlockSpec((1, 3, D_MODEL), lambda s: (tile(s) // steps_per_seq, 0, 0)),
            _const_spec((n_seq, D_MODEL)),
            pl.BlockSpec((3, n_seq, D_MODEL), lambda s: (sub_layer, 0, 0), pipeline_mode=pl.Buffered(1)),
            _const_spec((1, D_MODEL)),
            _const_spec((1, D_MODEL)),
            pl.BlockSpec((D_MODEL, FFN_WBLOCK), lambda s: (0, wblk(s))),
            pl.BlockSpec((D_MODEL, FFN_WBLOCK), lambda s: (0, wblk(s))),
            pl.BlockSpec((FFN_WBLOCK, D_MODEL), lambda s: (wblk(s), 0)),
        ],
        out_specs=[tok_spec, pl.BlockSpec((n_seq, D_MODEL), lambda s: (0, 0))],
        out_shape=[jax.ShapeDtypeStruct((n_tok, D_MODEL), F32),
                   jax.ShapeDtypeStruct((n_seq, D_MODEL), F32)],
        scratch_shapes=[pltpu.VMEM((D_MODEL, D_FF), BF16), pltpu.VMEM((D_MODEL, D_FF), BF16),
                        pltpu.VMEM((D_FF, D_MODEL), BF16)],
        compiler_params=pltpu.CompilerParams(
            dimension_semantics=("arbitrary",), vmem_limit_bytes=VMEM_LIMIT),
        name="ffn",
    )(xp, mod_p3, xs, mod9, pre_w, post_w, wg, wu, wd)


def _lower_bound(lb_logits):
    z = lb_logits - jnp.max(lb_logits, axis=0, keepdims=True)
    e = jnp.exp(z)
    return e[0:1] / jnp.sum(e, axis=0, keepdims=True)


def _lru_gates(u_conv, wa_ref, ba, wx_ref, bx, lam):
    ub = u_conv.astype(BF16)
    r = _sigmoid(_dot(ub, wa_ref[...]) + ba)
    ig = _sigmoid(_dot(ub, wx_ref[...]) + bx)
    log_a = (-LRU_C * r) * _softplus(-lam)
    a = jnp.exp(log_a)
    th = jnp.tanh(log_a)
    mult = jnp.sqrt((-2.0 * th) / (1.0 - th))
    return a, mult, ig


def _head_rms_gate(o, norm_w, g):
    return _rms(o, norm_w) * _silu(g)


def _hgrn_chunk_head(scores, q_in, k_out, dec, vb_blk, g_blk, hgw, st_ref, hd, ohg_ref, r0, l0):
    st = st_ref[hd]
    o = _dot_nt(q_in, st.astype(BF16)) + _dot(scores.astype(BF16), vb_blk)
    st_ref[hd] = st * dec + _dot_tn(vb_blk, k_out)
    ohg_ref[r0:r0 + HG_CHUNK, l0:l0 + HG_HEAD_DIM] = _head_rms_gate(o, hgw, g_blk)


def _hgrn_chunks_direct(qs, kk, pfx, tails, vb, g, hgw, st_ref, ohg_ref, r0):
    tb = qs.shape[0]
    n_chunk = len(tails)
    inv = 1.0 / pfx
    q_dec = qs * pfx
    k_dec = kk * inv
    q_in = q_dec.astype(BF16)
    k_in = k_dec.astype(BF16)
    ti = lax.broadcasted_iota(jnp.int32, (tb, tb), 0)
    si = lax.broadcasted_iota(jnp.int32, (tb, tb), 1)
    visible = (ti >= si) & ((ti ^ si) < HG_CHUNK)
    row_chunk = jnp.right_shift(lax.broadcasted_iota(jnp.int32, (tb, HG_HEAD_DIM), 0),
                                HG_CHUNK.bit_length() - 1)
    tail_rows = jnp.concatenate([jnp.broadcast_to(t, (HG_CHUNK, t.shape[-1])) for t in tails], axis=0)
    k_end = k_dec * tail_rows
    for hd in range(HG_HEADS):
        l0 = hd * HG_HEAD_DIM
        head = lambda z: z[:, l0:l0 + HG_HEAD_DIM]
        by_chunk = lambda z: jnp.concatenate(
            [jnp.where(row_chunk == c, head(z), 0.0) for c in range(n_chunk)], axis=-1).astype(BF16)
        scores = jnp.where(visible, _dot_nt(head(q_in), head(k_in)), 0.0)
        grown = _dot_tn(head(vb), by_chunk(k_end))
        st = st_ref[hd]
        states = []
        for c, tail in enumerate(tails):
            states.append(st.astype(BF16))
            st = st * head(tail) + grown[:, c * HG_HEAD_DIM:(c + 1) * HG_HEAD_DIM]
        st_ref[hd] = st
        o = (_dot_nt(by_chunk(q_dec), jnp.concatenate(states, axis=-1))
             + _dot(scores.astype(BF16), head(vb)))
        ohg_ref[r0:r0 + tb, l0:l0 + HG_HEAD_DIM] = _head_rms_gate(o, hgw, head(g))


def _hgrn_chunks_levels(f3, kk3, qs3, vb, g, hgw, st_ref, ohg_ref):
    n_grp = f3.shape[0]
    tb = n_grp * V7X_SUBLANES
    sub = lax.broadcasted_iota(jnp.int32, f3.shape, 1)
    grp = lax.broadcasted_iota(jnp.int32, f3.shape, 0)
    to_mxu = lambda z: z.reshape(tb, z.shape[-1]).astype(BF16)
    q_lv = [to_mxu(qs3)]
    k_lv = [to_mxu(kk3)]
    tot, pre, post = f3, f3, None
    for lvl in HG_LEVELS:
        q_lv.append(to_mxu(qs3 * pre))
        k_lv.append(k_lv[0] if post is None else to_mxu(kk3 * post))
        if lvl < V7X_SUBLANES:
            upper = (sub & lvl) != 0
            below = pltpu.roll(tot, lvl, 1)
            above = pltpu.roll(tot, V7X_SUBLANES - lvl, 1)
        else:
            upper = (grp & (lvl // V7X_SUBLANES)) != 0
            below = jnp.roll(tot, lvl // V7X_SUBLANES, axis=0)
            above = jnp.roll(tot, -(lvl // V7X_SUBLANES), axis=0)
        pre = jnp.where(upper, pre * below, pre)
        post = jnp.where(upper, 1.0, above) if post is None else jnp.where(upper, post, post * above)
        tot = tot * jnp.where(upper, below, above)
    q_in = to_mxu(qs3 * pre)
    k_out = to_mxu(kk3 * post)
    s_decay = tot.reshape(tb, tot.shape[-1])

    ti = lax.broadcasted_iota(jnp.int32, (HG_CHUNK, HG_CHUNK), 0)
    si = lax.broadcasted_iota(jnp.int32, (HG_CHUNK, HG_CHUNK), 1)
    masks = [ti == si]
    tx = ti ^ si
    for lvl in HG_LEVELS:
        masks.append((tx >= lvl) & (tx < 2 * lvl) & ((ti & lvl) != 0))

    for c in range(tb // HG_CHUNK):
        r0 = c * HG_CHUNK
        for hd in range(HG_HEADS):
            l0 = hd * HG_HEAD_DIM
            blk = lambda z: z[r0:r0 + HG_CHUNK, l0:l0 + HG_HEAD_DIM]
            scores = jnp.zeros((HG_CHUNK, HG_CHUNK), F32)
            for ql, kl, msk in zip(q_lv, k_lv, masks):
                scores = jnp.where(msk, _dot_nt(blk(ql), blk(kl)), scores)
            dec = s_decay[r0 + HG_CHUNK - 1:r0 + HG_CHUNK, l0:l0 + HG_HEAD_DIM]
            _hgrn_chunk_head(scores, blk(q_in), blk(k_out), dec, blk(vb), blk(g), hgw,
                             st_ref, hd, ohg_ref, r0, l0)

def _mix_tile(x_ref, m_ref, o_ref, t, pre_ref, post_ref, win_ref, cw_ref, cb_ref, wa_ref, ba_ref,
              wx_ref, bx_ref, lam_ref, lbl_ref, hgw_ref, wout_ref, ucar_ref, hcar_ref, st_ref, ohg_ref):
    tb = MIX_TILE
    x = x_ref[...]
    m = m_ref[...]
    hb = (_rms(x, pre_ref[...]) * (1.0 + m[1:2]) + m[0:1]).astype(BF16)
    u, y_lru, q, f_raw, v, g = [_dot(hb, win_ref[:, c0:c0 + LRU_WIDTH])
                                for c0 in range(0, D_PROJ, LRU_WIDTH)]

    n_grp = tb // V7X_SUBLANES
    grouped = lambda z: z.reshape(n_grp, V7X_SUBLANES, z.shape[-1])
    flat = lambda z: z.reshape(tb, z.shape[-1])
    gshape = (n_grp, V7X_SUBLANES, LRU_WIDTH)
    sub = lax.broadcasted_iota(jnp.int32, gshape, 1)
    grp = lax.broadcasted_iota(jnp.int32, gshape, 0)

    u3 = grouped(u)
    u_all = jnp.concatenate([ucar_ref[...][None], u3], axis=0)
    ucar_ref[...] = u3[n_grp - 1]
    cw = cw_ref[...]
    u_conv = cb_ref[...]
    for k in range(CONV_WIDTH - 1):
        back = CONV_WIDTH - 1 - k
        rot = pltpu.roll(u_all, back, 1)
        u_conv = u_conv + jnp.where(sub >= back, rot[1:], rot[:-1]) * cw[k:k + 1]
    u_conv = flat(u_conv + u3 * cw[CONV_WIDTH - 1:CONV_WIDTH])

    a, mult, ig = _lru_gates(u_conv, wa_ref, ba_ref[...], wx_ref, bx_ref[...], lam_ref[...])
    a3, mult3 = grouped(a), grouped(mult)
    first_row = jnp.where(t == 0, 0, -1)
    mult3 = jnp.where(grp * V7X_SUBLANES + sub == first_row, 1.0, mult3)
    b3 = (mult3 * grouped(ig)) * grouped(u_conv)
    s = 1
    while s < V7X_SUBLANES:
        keep = sub >= s
        b3 = jnp.where(keep, a3 * pltpu.roll(b3, s, 1) + b3, b3)
        a3 = jnp.where(keep, a3 * pltpu.roll(a3, s, 1), a3)
        s *= 2
    carry = hcar_ref[...]
    hs = []
    for gi in range(n_grp):
        h_g = a3[gi] * carry + b3[gi]
        carry = h_g[V7X_SUBLANES - 1:V7X_SUBLANES]
        hs.append(h_g)
    hcar_ref[...] = carry
    o_lru = jnp.concatenate(hs, axis=0) * _gelu_tanh(y_lru)

    lb = _lower_bound(lbl_ref[...])
    f3 = grouped(lb + (1.0 - lb) * _sigmoid(f_raw))
    kk3 = 1.0 - f3
    qs3 = grouped(q * (HG_HEAD_DIM ** -0.5))
    vb = v.astype(BF16)
    hgw = hgw_ref[...]

    pfx = f3
    s = 1
    while s < V7X_SUBLANES:
        pfx = jnp.where(sub >= s, pfx * pltpu.roll(pfx, s, 1), pfx)
        s *= 2
    grp_per_chunk = HG_CHUNK // V7X_SUBLANES
    rows, tails = [], []
    for gi in range(n_grp):
        p_g = pfx[gi] if gi % grp_per_chunk == 0 else pfx[gi] * carry
        carry = p_g[V7X_SUBLANES - 1:V7X_SUBLANES]
        rows.append(p_g)
        if gi % grp_per_chunk == grp_per_chunk - 1:
            tails.append(carry)
    pfx = jnp.concatenate(rows, axis=0)
    direct = jnp.min(jnp.concatenate(tails, axis=0)) >= HG_DIRECT_MIN

    @pl.when(direct)
    def _():
        qs, kk = flat(qs3), flat(kk3)
        per_block = HG_DIRECT_ROWS // HG_CHUNK
        for bi, r0 in enumerate(range(0, tb, HG_DIRECT_ROWS)):
            rows = lambda z: z[r0:r0 + HG_DIRECT_ROWS]
            _hgrn_chunks_direct(rows(qs), rows(kk), rows(pfx), tails[bi * per_block:(bi + 1) * per_block],
                                rows(vb), rows(g), hgw, st_ref, ohg_ref, r0)

    @pl.when(jnp.logical_not(direct))
    def _():
        _hgrn_chunks_levels(f3, kk3, qs3, vb, g, hgw, st_ref, ohg_ref)

    mix_in = jnp.concatenate([o_lru, ohg_ref[...]], axis=-1).astype(BF16)
    mix = _dot(mix_in, wout_ref[...])
    o_ref[...] = x + m[2:3] * _rms(mix, post_ref[...])


def _mix_prompt_kernel(x_ref, m_ref, *refs):
    (pre_ref, post_ref, win32_ref, cw_ref, cb_ref, wa_ref, ba_ref, wx_ref, bx_ref, lam_ref, lbl_ref,
     hgw_ref, wout32_ref) = refs[:13]
    (o_ref, h_out_ref, conv_out_ref, s_out_ref, win_ref, wout_ref,
     ucar_ref, hcar_ref, st_ref, ohg_ref) = refs[13:]
    t = pl.program_id(1)
    n_t = pl.num_programs(1)

    @pl.when((pl.program_id(0) == 0) & (t == 0))
    def _():
        win_ref[...] = win32_ref[...].astype(BF16)
        wout_ref[...] = wout32_ref[...].astype(BF16)

    @pl.when(t == 0)
    def _():
        ucar_ref[...] = jnp.zeros_like(ucar_ref)
        hcar_ref[...] = jnp.zeros_like(hcar_ref)
        st_ref[...] = jnp.zeros_like(st_ref)

    _mix_tile(x_ref.at[0], m_ref.at[0], o_ref.at[0], t, pre_ref, post_ref, win_ref, cw_ref, cb_ref,
              wa_ref, ba_ref, wx_ref, bx_ref, lam_ref, lbl_ref, hgw_ref, wout_ref,
              ucar_ref, hcar_ref, st_ref, ohg_ref)

    @pl.when(t == n_t - 1)
    def _():
        h_out_ref[0] = hcar_ref[...]
        conv_out_ref[0] = ucar_ref[V7X_SUBLANES - (CONV_WIDTH - 1):V7X_SUBLANES, :]
        for hd in range(HG_HEADS):
            s_out_ref[0, hd] = st_ref[hd].T


def _mix_prompt(x, mod3, p):
    batch, seq_len, _ = x.shape
    n_t = seq_len // MIX_TILE
    tok_spec = pl.BlockSpec((1, MIX_TILE, D_MODEL), lambda b, t: (b, t, 0))
    return pl.pallas_call(
        _mix_prompt_kernel,
        grid=(batch, n_t),
        in_specs=[
            tok_spec,
            pl.BlockSpec((1, 3, D_MODEL), lambda b, t: (b, 0, 0)),
            _const_spec((1, D_MODEL)),
            _const_spec((1, D_MODEL)),
            _const_spec((D_MODEL, D_PROJ)),
            _const_spec((CONV_WIDTH, LRU_WIDTH)),
            _const_spec((1, LRU_WIDTH)),
            _const_spec((LRU_WIDTH, LRU_WIDTH)),
            _const_spec((1, LRU_WIDTH)),
            _const_spec((LRU_WIDTH, LRU_WIDTH)),
            _const_spec((1, LRU_WIDTH)),
            _const_spec((1, LRU_WIDTH)),
            _const_spec(p["lb_logits"].shape),
            _const_spec((1, HG_HEAD_DIM)),
            _const_spec((D_MODEL, D_MODEL)),
        ],
        out_specs=[
            tok_spec,
            pl.BlockSpec((1, 1, LRU_WIDTH), lambda b, t: (b, 0, 0)),
            pl.BlockSpec((1, CONV_WIDTH - 1, LRU_WIDTH), lambda b, t: (b, 0, 0)),
            pl.BlockSpec((1, HG_HEADS, HG_HEAD_DIM, HG_HEAD_DIM), lambda b, t: (b, 0, 0, 0)),
            _const_spec((D_MODEL, D_PROJ)),
            _const_spec((D_MODEL, D_MODEL)),
        ],
        out_shape=[
            jax.ShapeDtypeStruct((batch, seq_len, D_MODEL), F32),
            jax.ShapeDtypeStruct((batch, 1, LRU_WIDTH), F32),
            jax.ShapeDtypeStruct((batch, CONV_WIDTH - 1, LRU_WIDTH), F32),
            jax.ShapeDtypeStruct((batch, HG_HEADS, HG_HEAD_DIM, HG_HEAD_DIM), F32),
            jax.ShapeDtypeStruct((D_MODEL, D_PROJ), BF16),
            jax.ShapeDtypeStruct((D_MODEL, D_MODEL), BF16),
        ],
        scratch_shapes=[
            pltpu.VMEM((V7X_SUBLANES, LRU_WIDTH), F32),
            pltpu.VMEM((1, LRU_WIDTH), F32),
            pltpu.VMEM((HG_HEADS, HG_HEAD_DIM, HG_HEAD_DIM), F32),
            pltpu.VMEM((MIX_TILE, HG_WIDTH), F32),
        ],
        compiler_params=pltpu.CompilerParams(
            dimension_semantics=("arbitrary", "arbitrary"), vmem_limit_bytes=VMEM_LIMIT),
        name="mix_prompt",
    )(x, mod3, p["ln_mix_pre"], p["ln_mix_post"], p["w_in"], p["conv_w"], p["conv_b"],
      p["wa_bd"], p["b_a"], p["wx_bd"], p["b_x"], p["lam"], p["lb_logits"], p["hg_norm_w"],
      p["w_out"])


def _mix_sample_kernel(x_ref, m_ref, pre_ref, post_ref, win_ref, cw_ref, cb_ref,
                       wa_ref, ba_ref, wx_ref, bx_ref, lam_ref, lbl_ref, hgw_ref, wout_ref,
                       h0_ref, conv0_ref, s0_ref,
                       o_ref, h_out_ref, conv_out_ref, s_out_ref,
                       qf_ref, f_ref, k_ref, v_ref, g_ref, qk_ref, olru_ref, ohg_ref, ostage_ref):
    i = pl.program_id(0)
    n_i = pl.num_programs(0)
    nb = SAMPLE_SEQ_BLOCK

    @pl.when(i == 0)
    def _():
        x = x_ref[...]
        h = _rms(x, pre_ref[...]) * (1.0 + m_ref[1]) + m_ref[0]
        proj = _dot(h.astype(BF16), win_ref[...])
        u = proj[:, 0:LRU_WIDTH]
        y_lru = proj[:, LRU_WIDTH:2 * LRU_WIDTH]
        o0 = 2 * LRU_WIDTH
        q = proj[:, o0:o0 + HG_WIDTH]
        f_raw = proj[:, o0 + HG_WIDTH:o0 + 2 * HG_WIDTH]
        v = proj[:, o0 + 2 * HG_WIDTH:o0 + 3 * HG_WIDTH]
        g_ref[...] = proj[:, o0 + 3 * HG_WIDTH:o0 + 4 * HG_WIDTH]

        cw = cw_ref[...]
        u_conv = cb_ref[...]
        for k in range(CONV_WIDTH - 1):
            u_conv = u_conv + conv0_ref[k] * cw[k:k + 1]
            if k > 0:
                conv_out_ref[k - 1] = conv0_ref[k]
        u_conv = u_conv + u * cw[CONV_WIDTH - 1:CONV_WIDTH]
        conv_out_ref[CONV_WIDTH - 2] = u

        a, mult, ig = _lru_gates(u_conv, wa_ref, ba_ref[...], wx_ref, bx_ref[...], lam_ref[...])
        hs = (mult * ig) * u_conv + a * h0_ref[...]
        h_out_ref[...] = hs
        olru_ref[...] = hs * _gelu_tanh(y_lru)

        lb = _lower_bound(lbl_ref[...])
        f = lb + (1.0 - lb) * _sigmoid(f_raw)
        kk = 1.0 - f
        qs = q * (HG_HEAD_DIM ** -0.5)
        f_ref[...] = f
        qf_ref[...] = qs * f
        k_ref[...] = kk
        v_ref[...] = v
        qk = qs * kk
        for hd in range(HG_HEADS):
            l0 = hd * HG_HEAD_DIM
            tot = jnp.sum(qk[:, l0:l0 + HG_HEAD_DIM], axis=-1, keepdims=True)
            qk_ref[:, l0:l0 + HG_HEAD_DIM] = jnp.broadcast_to(tot, (qk.shape[0], HG_HEAD_DIM))

    base = pl.multiple_of(i * nb, nb)
    grp = lambda ref: ref[pl.ds(base, nb), :]
    f_g, k_g, v_g, qf_g, qk_g = grp(f_ref), grp(k_ref), grp(v_ref), grp(qf_ref), grp(qk_ref)
    square = (HG_HEAD_DIM, HG_HEAD_DIM)
    for j in range(nb):
        for hd in range(HG_HEADS):
            l0 = hd * HG_HEAD_DIM
            rowv = lambda z: z[j:j + 1, l0:l0 + HG_HEAD_DIM]
            s_old = s0_ref[j, hd]
            f_col = jnp.broadcast_to(rowv(f_g), square).T
            k_col = jnp.broadcast_to(rowv(k_g), square).T
            v_row = rowv(v_g)
            outer = k_col.astype(BF16).astype(F32) * v_row.astype(BF16).astype(F32)
            s_out_ref[j, hd] = f_col * s_old + outer
            qf8 = jnp.broadcast_to(rowv(qf_g), (V7X_SUBLANES, HG_HEAD_DIM)).astype(BF16)
            o1 = _dot(qf8, s_old.astype(BF16))[0:1]
            ostage_ref[j:j + 1, l0:l0 + HG_HEAD_DIM] = o1 + rowv(qk_g) * v_row
    ohg_ref[pl.ds(base, nb), :] = ostage_ref[...]

    @pl.when(i == n_i - 1)
    def _():
        hgw = hgw_ref[...]
        parts = [olru_ref[...]]
        for hd in range(HG_HEADS):
            l0 = hd * HG_HEAD_DIM
            parts.append(_head_rms_gate(ohg_ref[:, l0:l0 + HG_HEAD_DIM], hgw,
                                        g_ref[:, l0:l0 + HG_HEAD_DIM]))
        mix_in = jnp.concatenate(parts, axis=-1).astype(BF16)
        mix = _dot(mix_in, wout_ref[...])
        o_ref[...] = x_ref[...] + m_ref[2] * _rms(mix, post_ref[...])


def _mix_sample(x, mod9, p, h0, conv0, s0):
    n_seq = x.shape[0]
    nb = SAMPLE_SEQ_BLOCK
    state_spec = pl.BlockSpec((nb, HG_HEADS, HG_HEAD_DIM, HG_HEAD_DIM), lambda i: (i, 0, 0, 0))
    full2 = lambda shape: pl.BlockSpec(shape, lambda i: (0,) * len(shape))
    return pl.pallas_call(
        _mix_sample_kernel,
        grid=(n_seq // nb,),
        in_specs=[
            _const_spec((n_seq, D_MODEL)),
            pl.BlockSpec((3, n_seq, D_MODEL), lambda i: (1, 0, 0), pipeline_mode=pl.Buffered(1)),
            _const_spec((1, D_MODEL)),
            _const_spec((1, D_MODEL)),
            _const_spec((D_MODEL, D_PROJ)),
            _const_spec((CONV_WIDTH, LRU_WIDTH)),
            _const_spec((1, LRU_WIDTH)),
            _const_spec((LRU_WIDTH, LRU_WIDTH)),
            _const_spec((1, LRU_WIDTH)),
            _const_spec((LRU_WIDTH, LRU_WIDTH)),
            _const_spec((1, LRU_WIDTH)),
            _const_spec((1, LRU_WIDTH)),
            _const_spec(p["lb_logits"].shape),
            _const_spec((1, HG_HEAD_DIM)),
            _const_spec((D_MODEL, D_MODEL)),
            _const_spec((n_seq, LRU_WIDTH)),
            _const_spec((CONV_WIDTH - 1, n_seq, LRU_WIDTH)),
            state_spec,
        ],
        out_specs=[
            full2((n_seq, D_MODEL)),
            full2((n_seq, LRU_WIDTH)),
            full2((CONV_WIDTH - 1, n_seq, LRU_WIDTH)),
            state_spec,
        ],
        out_shape=[
            jax.ShapeDtypeStruct((n_seq, D_MODEL), F32),
            jax.ShapeDtypeStruct((n_seq, LRU_WIDTH), F32),
            jax.ShapeDtypeStruct((CONV_WIDTH - 1, n_seq, LRU_WIDTH), F32),
            jax.ShapeDtypeStruct((n_seq, HG_HEADS, HG_HEAD_DIM, HG_HEAD_DIM), F32),
        ],
        scratch_shapes=[pltpu.VMEM((n_seq, HG_WIDTH), F32) for _ in range(8)]
        + [pltpu.VMEM((nb, HG_WIDTH), F32)],
        compiler_params=pltpu.CompilerParams(
            dimension_semantics=("arbitrary",), vmem_limit_bytes=VMEM_LIMIT),
        name="mix_sample",
    )(x, mod9, p["ln_mix_pre"], p["ln_mix_post"], p["w_in"], p["conv_w"], p["conv_b"],
      p["wa_bd"], p["b_a"], p["wx_bd"], p["b_x"], p["lam"], p["lb_logits"], p["hg_norm_w"],
      p["w_out"], h0, conv0, s0)


def _block_diag(w):
    heads, blk, _ = w.shape
    eye = jnp.eye(heads, dtype=w.dtype)
    return (eye[:, None, :, None] * w[:, :, None, :]).reshape(heads * blk, heads * blk)


def kernel(x_prompt, x_sample, c_prompt, c_sample, state_lru_h, state_lru_conv, state_hgrn_S, w_ada, b_ada, ln_ffn1_pre, ln_ffn1_post, ffn1_w_gate, ffn1_w_up, ffn1_w_down, ln_mix_pre, ln_mix_post, w_in, lru_conv_w, lru_conv_b, lru_w_a, lru_b_a, lru_w_x, lru_b_x, lru_lambda, hg_lb_logits, hg_norm_w, w_out, ln_ffn2_pre, ln_ffn2_post, ffn2_w_gate, ffn2_w_up, ffn2_w_down):
    depth = w_ada.shape[0]
    batch, seq_len, _ = x_prompt.shape
    n_seq = x_sample.shape[0]
    assert depth == 1 and x_sample.shape[1] == 1
    assert seq_len % FFN_TILE == 0 and seq_len % MIX_TILE == 0 and n_seq % SAMPLE_SEQ_BLOCK == 0
    assert D_PROJ == 6 * LRU_WIDTH and LRU_WIDTH == HG_WIDTH

    xp = x_prompt.reshape(batch * seq_len, D_MODEL)
    xs = x_sample.reshape(n_seq, D_MODEL)
    ph, pc, pS, sh, sc, sS = [], [], [], [], [], []
    for l in range(depth):
        row = lambda w: w[l].reshape(1, -1)
        mod9 = _ada(c_sample, c_prompt, w_ada[l], b_ada[l])
        mod_p = jnp.transpose(mod9[:, n_seq:], (1, 0, 2))
        p = {
            "ln_mix_pre": row(ln_mix_pre), "ln_mix_post": row(ln_mix_post),
            "w_in": w_in[l], "w_out": w_out[l],
            "conv_w": lru_conv_w[l], "conv_b": row(lru_conv_b),
            "wa_bd": _block_diag(lru_w_a[l]).astype(BF16), "b_a": row(lru_b_a),
            "wx_bd": _block_diag(lru_w_x[l]).astype(BF16), "b_x": row(lru_b_x),
            "lam": row(lru_lambda), "lb_logits": hg_lb_logits,
            "hg_norm_w": row(hg_norm_w),
        }
        f1 = (row(ln_ffn1_pre), row(ln_ffn1_post), ffn1_w_gate[l], ffn1_w_up[l], ffn1_w_down[l])
        f2 = (row(ln_ffn2_pre), row(ln_ffn2_post), ffn2_w_gate[l], ffn2_w_up[l], ffn2_w_down[l])

        xp, xs = _ffn(xp, xs, mod_p[:, 0:3], mod9, 0, *f1, seq_len)
        xp, h_p, c_p, S_p, w_in_bf, w_out_bf = _mix_prompt(
            xp.reshape(batch, seq_len, D_MODEL), mod_p[:, 3:6], p)
        xp = xp.reshape(batch * seq_len, D_MODEL)
        conv0 = jnp.transpose(state_lru_conv[l], (1, 0, 2))
        xs, h_s, c_s, S_s = _mix_sample(xs, mod9, dict(p, w_in=w_in_bf, w_out=w_out_bf),
                                        state_lru_h[l], conv0, state_hgrn_S[l])
        xp, xs = _ffn(xp, xs, mod_p[:, 6:9], mod9, 2, *f2, seq_len)

        ph.append(h_p.reshape(batch, LRU_WIDTH)); pc.append(c_p); pS.append(S_p)
        sh.append(h_s); sc.append(jnp.transpose(c_s, (1, 0, 2))); sS.append(S_s)

    return (xp.reshape(batch, seq_len, D_MODEL), xs.reshape(n_seq, 1, D_MODEL),
            jnp.stack(ph), jnp.stack(pc), jnp.stack(pS), jnp.stack(sh), jnp.stack(sc), jnp.stack(sS))
```

```python
import jax
import jax.numpy as jnp
from jax import lax
from jax.experimental import pallas as pl
from jax.experimental.pallas import tpu as pltpu

F32 = jnp.float32
BF16 = jnp.bfloat16

D_MODEL = 1024
D_FF = 2816
LRU_WIDTH = 512
CONV_WIDTH = 4
LRU_C = 8.0
HG_WIDTH = 512
HG_HEAD_DIM = 128
HG_HEADS = HG_WIDTH // HG_HEAD_DIM
HG_CHUNK = 64
N_MOD = 9
D_PROJ = 2 * LRU_WIDTH + 4 * HG_WIDTH
EPS = 1e-6

V7X_SUBLANES = 8
V7X_VMEM_BYTES = 64 * 1024 * 1024
VMEM_LIMIT = V7X_VMEM_BYTES - 8 * 1024 * 1024

FFN_TILE = 1024
FFN_ROWS = 256
MIX_TILE = 512
HG_DIRECT_ROWS = 256
FFN_WBLOCK = 256
FFN_WSTEPS = D_FF // FFN_WBLOCK
SAMPLE_SEQ_BLOCK = 2 * V7X_SUBLANES

HG_LEVELS = (1, 2, 4, 8, 16, 32)
HG_DIRECT_MIN = 2.0 ** -100


def _rms(x, w):
    return (x * lax.rsqrt(jnp.mean(x * x, axis=-1, keepdims=True) + EPS)) * w


def _sigmoid(x):
    return 1.0 / (1.0 + jnp.exp(-x))


def _silu(x):
    return x * _sigmoid(x)


def _gelu_tanh(x):
    c = 0.7978845608028654
    return x * (0.5 * (1.0 + jnp.tanh(c * (x + 0.044715 * (x * x * x)))))


def _softplus(z):
    return jnp.maximum(z, 0.0) + jnp.log1p(jnp.exp(-jnp.abs(z)))


def _dot(a, b):
    return jnp.dot(a, b, preferred_element_type=F32)


def _dot_nt(a, b):
    return lax.dot_general(a, b, (((1,), (1,)), ((), ())), preferred_element_type=F32)


def _dot_tn(a, b):
    return lax.dot_general(a, b, (((0,), (0,)), ((), ())), preferred_element_type=F32)


def _const_spec(shape):
    nd = len(shape)
    return pl.BlockSpec(shape, lambda *_: (0,) * nd, pipeline_mode=pl.Buffered(1))


def _ada_kernel(cs_ref, cp_ref, w_ref, b_ref, o_ref):
    s = _silu(jnp.concatenate([cs_ref[...], cp_ref[...]], axis=0))
    o_ref[0] = _dot(s.astype(BF16), w_ref[...].astype(BF16)) + b_ref[...]


def _ada(c_sample, c_prompt, w_ada, b_ada):
    rows = c_sample.shape[0] + c_prompt.shape[0]
    return pl.pallas_call(
        _ada_kernel,
        grid=(N_MOD,),
        in_specs=[
            _const_spec(c_sample.shape),
            _const_spec(c_prompt.shape),
            pl.BlockSpec((D_MODEL, D_MODEL), lambda j: (0, j)),
            pl.BlockSpec((1, D_MODEL), lambda j: (0, j)),
        ],
        out_specs=pl.BlockSpec((1, rows, D_MODEL), lambda j: (j, 0, 0)),
        out_shape=jax.ShapeDtypeStruct((N_MOD, rows, D_MODEL), F32),
        compiler_params=pltpu.CompilerParams(
            dimension_semantics=("arbitrary",), vmem_limit_bytes=VMEM_LIMIT),
        name="ada_mod",
    )(c_sample, c_prompt, w_ada, b_ada.reshape(1, N_MOD * D_MODEL))


def _ffn_body(x, shift, scale, gate, pre_w, post_w, wg_ref, wu_ref, wd_ref):
    h = _rms(x, pre_w) * (1.0 + scale) + shift
    hb = h.astype(BF16)
    a = _dot(hb, wg_ref[...])
    u = _dot(hb, wu_ref[...])
    act = (_silu(a) * u).astype(BF16)
    y = _dot(act, wd_ref[...])
    return x + (0.5 * gate) * _rms(y, post_w)


def _ffn_kernel(xp_ref, mp_ref, xs_ref, ms_ref, pre_ref, post_ref, wg32_ref, wu32_ref, wd32_ref,
                op_ref, os_ref, wg_ref, wu_ref, wd_ref):
    s = pl.program_id(0)
    n_steps = pl.num_programs(0)

    for j in range(FFN_WSTEPS):
        @pl.when(s == j)
        def _(j=j):
            c0 = j * FFN_WBLOCK
            wg_ref[:, c0:c0 + FFN_WBLOCK] = wg32_ref[...].astype(BF16)
            wu_ref[:, c0:c0 + FFN_WBLOCK] = wu32_ref[...].astype(BF16)
            wd_ref[c0:c0 + FFN_WBLOCK, :] = wd32_ref[...].astype(BF16)

    @pl.when((s >= FFN_WSTEPS) & (s < n_steps - 1))
    def _():
        m = mp_ref[0]
        for r0 in range(0, FFN_TILE, FFN_ROWS):
            op_ref[r0:r0 + FFN_ROWS, :] = _ffn_body(
                xp_ref[r0:r0 + FFN_ROWS, :], m[0:1], m[1:2], m[2:3],
                pre_ref[...], post_ref[...], wg_ref, wu_ref, wd_ref)

    @pl.when(s == n_steps - 1)
    def _():
        os_ref[...] = _ffn_body(xs_ref[...], ms_ref[0], ms_ref[1], ms_ref[2], pre_ref[...],
                                post_ref[...], wg_ref, wu_ref, wd_ref)


def _ffn(xp, xs, mod_p3, mod9, sub_layer, pre_w, post_w, wg, wu, wd, seq_len):
    n_tok = xp.shape[0]
    n_seq = xs.shape[0]
    n_tiles = n_tok // FFN_TILE
    steps_per_seq = seq_len // FFN_TILE
    tile = lambda s: jnp.clip(s - FFN_WSTEPS, 0, n_tiles - 1)
    wblk = lambda s: jnp.minimum(s, FFN_WSTEPS - 1)
    tok_spec = pl.BlockSpec((FFN_TILE, D_MODEL), lambda s: (tile(s), 0))
    return pl.pallas_call(
        _ffn_kernel,
        grid=(FFN_WSTEPS + n_tiles + 1,),
        in_specs=[
            tok_spec,
            pl.BlockSpec((1, 3, D_MODEL), lambda s: (tile(s) // steps_per_seq, 0, 0)),
            _const_spec((n_seq, D_MODEL)),
            pl.BlockSpec((3, n_seq, D_MODEL), lambda s: (sub_layer, 0, 0), pipeline_mode=pl.Buffered(1)),
            _const_spec((1, D_MODEL)),
            _const_spec((1, D_MODEL)),
            pl.BlockSpec((D_MODEL, FFN_WBLOCK), lambda s: (0, wblk(s))),
            pl.BlockSpec((D_MODEL, FFN_WBLOCK), lambda s: (0, wblk(s))),
            pl.BlockSpec((FFN_WBLOCK, D_MODEL), lambda s: (wblk(s), 0)),
        ],
        out_specs=[tok_spec, pl.BlockSpec((n_seq, D_MODEL), lambda s: (0, 0))],
        out_shape=[jax.ShapeDtypeStruct((n_tok, D_MODEL), F32),
                   jax.ShapeDtypeStruct((n_seq, D_MODEL), F32)],
        scratch_shapes=[pltpu.VMEM((D_MODEL, D_FF), BF16), pltpu.VMEM((D_MODEL, D_FF), BF16),
                        pltpu.VMEM((D_FF, D_MODEL), BF16)],
        compiler_params=pltpu.CompilerParams(
            dimension_semantics=("arbitrary",), vmem_limit_bytes=VMEM_LIMIT),
        name="ffn",
    )(xp, mod_p3, xs, mod9, pre_w, post_w, wg, wu, wd)


def _lower_bound(lb_logits):
    z = lb_logits - jnp.max(lb_logits, axis=0, keepdims=True)
    e = jnp.exp(z)
    return e[0:1] / jnp.sum(e, axis=0, keepdims=True)


def _lru_gates(u_conv, wa_ref, ba, wx_ref, bx, lam):
    ub = u_conv.astype(BF16)
    r = _sigmoid(_dot(ub, wa_ref[...]) + ba)
    ig = _sigmoid(_dot(ub, wx_ref[...]) + bx)
    log_a = (-LRU_C * r) * _softplus(-lam)
    a = jnp.exp(log_a)
    th = jnp.tanh(log_a)
    mult = jnp.sqrt((-2.0 * th) / (1.0 - th))
    return a, mult, ig


def _head_rms_gate(o, norm_w, g):
    return _rms(o, norm_w) * _silu(g)


def _hgrn_chunk_head(scores, q_in, k_out, dec, vb_blk, g_blk, hgw, st_ref, hd, ohg_ref, r0, l0):
    st = st_ref[hd]
    o = _dot_nt(q_in, st.astype(BF16)) + _dot(scores.astype(BF16), vb_blk)
    st_ref[hd] = st * dec + _dot_tn(vb_blk, k_out)
    ohg_ref[r0:r0 + HG_CHUNK, l0:l0 + HG_HEAD_DIM] = _head_rms_gate(o, hgw, g_blk)


def _hgrn_chunks_direct(qs, kk, pfx, tails, vb, g, hgw, st_ref, ohg_ref, r0):
    tb = qs.shape[0]
    n_chunk = len(tails)
    inv = 1.0 / pfx
    q_dec = qs * pfx
    k_dec = kk * inv
    q_in = q_dec.astype(BF16)
    k_in = k_dec.astype(BF16)
    ti = lax.broadcasted_iota(jnp.int32, (tb, tb), 0)
    si = lax.broadcasted_iota(jnp.int32, (tb, tb), 1)
    visible = (ti >= si) & ((ti ^ si) < HG_CHUNK)
    row_chunk = jnp.right_shift(lax.broadcasted_iota(jnp.int32, (tb, HG_HEAD_DIM), 0),
                                HG_CHUNK.bit_length() - 1)
    tail_rows = jnp.concatenate([jnp.broadcast_to(t, (HG_CHUNK, t.shape[-1])) for t in tails], axis=0)
    k_end = k_dec * tail_rows
    for hd in range(HG_HEADS):
        l0 = hd * HG_HEAD_DIM
        head = lambda z: z[:, l0:l0 + HG_HEAD_DIM]
        by_chunk = lambda z: jnp.concatenate(
            [jnp.where(row_chunk == c, head(z), 0.0) for c in range(n_chunk)], axis=-1).astype(BF16)
        scores = jnp.where(visible, _dot_nt(head(q_in), head(k_in)), 0.0)
        grown = _dot_tn(head(vb), by_chunk(k_end))
        st = st_ref[hd]
        states = []
        for c, tail in enumerate(tails):
            states.append(st.astype(BF16))
            st = st * head(tail) + grown[:, c * HG_HEAD_DIM:(c + 1) * HG_HEAD_DIM]
        st_ref[hd] = st
        o = (_dot_nt(by_chunk(q_dec), jnp.concatenate(states, axis=-1))
             + _dot(scores.astype(BF16), head(vb)))
        ohg_ref[r0:r0 + tb, l0:l0 + HG_HEAD_DIM] = _head_rms_gate(o, hgw, head(g))


def _hgrn_chunks_levels(f3, kk3, qs3, vb, g, hgw, st_ref, ohg_ref):
    n_grp = f3.shape[0]
    tb = n_grp * V7X_SUBLANES
    sub = lax.broadcasted_iota(jnp.int32, f3.shape, 1)
    grp = lax.broadcasted_iota(jnp.int32, f3.shape, 0)
    to_mxu = lambda z: z.reshape(tb, z.shape[-1]).astype(BF16)
    q_lv = [to_mxu(qs3)]
    k_lv = [to_mxu(kk3)]
    tot, pre, post = f3, f3, None
    for lvl in HG_LEVELS:
        q_lv.append(to_mxu(qs3 * pre))
        k_lv.append(k_lv[0] if post is None else to_mxu(kk3 * post))
        if lvl < V7X_SUBLANES:
            upper = (sub & lvl) != 0
            below = pltpu.roll(tot, lvl, 1)
            above = pltpu.roll(tot, V7X_SUBLANES - lvl, 1)
        else:
            upper = (grp & (lvl // V7X_SUBLANES)) != 0
            below = jnp.roll(tot, lvl // V7X_SUBLANES, axis=0)
            above = jnp.roll(tot, -(lvl // V7X_SUBLANES), axis=0)
        pre = jnp.where(upper, pre * below, pre)
        post = jnp.where(upper, 1.0, above) if post is None else jnp.where(upper, post, post * above)
        tot = tot * jnp.where(upper, below, above)
    q_in = to_mxu(qs3 * pre)
    k_out = to_mxu(kk3 * post)
    s_decay = tot.reshape(tb, tot.shape[-1])

    ti = lax.broadcasted_iota(jnp.int32, (HG_CHUNK, HG_CHUNK), 0)
    si = lax.broadcasted_iota(jnp.int32, (HG_CHUNK, HG_CHUNK), 1)
    masks = [ti == si]
    tx = ti ^ si
    for lvl in HG_LEVELS:
        masks.append((tx >= lvl) & (tx < 2 * lvl) & ((ti & lvl) != 0))

    for c in range(tb // HG_CHUNK):
        r0 = c * HG_CHUNK
        for hd in range(HG_HEADS):
            l0 = hd * HG_HEAD_DIM
            blk = lambda z: z[r0:r0 + HG_CHUNK, l0:l0 + HG_HEAD_DIM]
            scores = jnp.zeros((HG_CHUNK, HG_CHUNK), F32)
            for ql, kl, msk in zip(q_lv, k_lv, masks):
                scores = jnp.where(msk, _dot_nt(blk(ql), blk(kl)), scores)
            dec = s_decay[r0 + HG_CHUNK - 1:r0 + HG_CHUNK, l0:l0 + HG_HEAD_DIM]
            _hgrn_chunk_head(scores, blk(q_in), blk(k_out), dec, blk(vb), blk(g), hgw,
                             st_ref, hd, ohg_ref, r0, l0)

def _mix_tile(x_ref, m_ref, o_ref, t, pre_ref, post_ref, win_ref, cw_ref, cb_ref, wa_ref, ba_ref,
              wx_ref, bx_ref, lam_ref, lbl_ref, hgw_ref, wout_ref, ucar_ref, hcar_ref, st_ref, ohg_ref):
    tb = MIX_TILE
    x = x_ref[...]
    m = m_ref[...]
    hb = (_rms(x, pre_ref[...]) * (1.0 + m[1:2]) + m[0:1]).astype(BF16)
    u, y_lru, q, f_raw, v, g = [_dot(hb, win_ref[:, c0:c0 + LRU_WIDTH])
                                for c0 in range(0, D_PROJ, LRU_WIDTH)]

    n_grp = tb // V7X_SUBLANES
    grouped = lambda z: z.reshape(n_grp, V7X_SUBLANES, z.shape[-1])
    flat = lambda z: z.reshape(tb, z.shape[-1])
    gshape = (n_grp, V7X_SUBLANES, LRU_WIDTH)
    sub = lax.broadcasted_iota(jnp.int32, gshape, 1)
    grp = lax.broadcasted_iota(jnp.int32, gshape, 0)

    u3 = grouped(u)
    u_all = jnp.concatenate([ucar_ref[...][None], u3], axis=0)
    ucar_ref[...] = u3[n_grp - 1]
    cw = cw_ref[...]
    u_conv = cb_ref[...]
    for k in range(CONV_WIDTH - 1):
        back = CONV_WIDTH - 1 - k
        rot = pltpu.roll(u_all, back, 1)
        u_conv = u_conv + jnp.where(sub >= back, rot[1:], rot[:-1]) * cw[k:k + 1]
    u_conv = flat(u_conv + u3 * cw[CONV_WIDTH - 1:CONV_WIDTH])

    a, mult, ig = _lru_gates(u_conv, wa_ref, ba_ref[...], wx_ref, bx_ref[...], lam_ref[...])
    a3, mult3 = grouped(a), grouped(mult)
    first_row = jnp.where(t == 0, 0, -1)
    mult3 = jnp.where(grp * V7X_SUBLANES + sub == first_row, 1.0, mult3)
    b3 = (mult3 * grouped(ig)) * grouped(u_conv)
    s = 1
    while s < V7X_SUBLANES:
        keep = sub >= s
        b3 = jnp.where(keep, a3 * pltpu.roll(b3, s, 1) + b3, b3)
        a3 = jnp.where(keep, a3 * pltpu.roll(a3, s, 1), a3)
        s *= 2
    carry = hcar_ref[...]
    hs = []
    for gi in range(n_grp):
        h_g = a3[gi] * carry + b3[gi]
        carry = h_g[V7X_SUBLANES - 1:V7X_SUBLANES]
        hs.append(h_g)
    hcar_ref[...] = carry
    o_lru = jnp.concatenate(hs, axis=0) * _gelu_tanh(y_lru)

    lb = _lower_bound(lbl_ref[...])
    f3 = grouped(lb + (1.0 - lb) * _sigmoid(f_raw))
    kk3 = 1.0 - f3
    qs3 = grouped(q * (HG_HEAD_DIM ** -0.5))
    vb = v.astype(BF16)
    hgw = hgw_ref[...]

    pfx = f3
    s = 1
    while s < V7X_SUBLANES:
        pfx = jnp.where(sub >= s, pfx * pltpu.roll(pfx, s, 1), pfx)
        s *= 2
    grp_per_chunk = HG_CHUNK // V7X_SUBLANES
    rows, tails = [], []
    for gi in range(n_grp):
        p_g = pfx[gi] if gi % grp_per_chunk == 0 else pfx[gi] * carry
        carry = p_g[V7X_SUBLANES - 1:V7X_SUBLANES]
        rows.append(p_g)
        if gi % grp_per_chunk == grp_per_chunk - 1:
            tails.append(carry)
    pfx = jnp.concatenate(rows, axis=0)
    direct = jnp.min(jnp.concatenate(tails, axis=0)) >= HG_DIRECT_MIN

    @pl.when(direct)
    def _():
        qs, kk = flat(qs3), flat(kk3)
        per_block = HG_DIRECT_ROWS // HG_CHUNK
        for bi, r0 in enumerate(range(0, tb, HG_DIRECT_ROWS)):
            rows = lambda z: z[r0:r0 + HG_DIRECT_ROWS]
            _hgrn_chunks_direct(rows(qs), rows(kk), rows(pfx), tails[bi * per_block:(bi + 1) * per_block],
                                rows(vb), rows(g), hgw, st_ref, ohg_ref, r0)

    @pl.when(jnp.logical_not(direct))
    def _():
        _hgrn_chunks_levels(f3, kk3, qs3, vb, g, hgw, st_ref, ohg_ref)

    mix_in = jnp.concatenate([o_lru, ohg_ref[...]], axis=-1).astype(BF16)
    mix = _dot(mix_in, wout_ref[...])
    o_ref[...] = x + m[2:3] * _rms(mix, post_ref[...])


def _mix_prompt_kernel(x_ref, m_ref, *refs):
    (pre_ref, post_ref, win32_ref, cw_ref, cb_ref, wa_ref, ba_ref, wx_ref, bx_ref, lam_ref, lbl_ref,
     hgw_ref, wout32_ref) = refs[:13]
    (o_ref, h_out_ref, conv_out_ref, s_out_ref, win_ref, wout_ref,
     ucar_ref, hcar_ref, st_ref, ohg_ref) = refs[13:]
    t = pl.program_id(1)
    n_t = pl.num_programs(1)

    @pl.when((pl.program_id(0) == 0) & (t == 0))
    def _():
        win_ref[...] = win32_ref[...].astype(BF16)
        wout_ref[...] = wout32_ref[...].astype(BF16)

    @pl.when(t == 0)
    def _():
        ucar_ref[...] = jnp.zeros_like(ucar_ref)
        hcar_ref[...] = jnp.zeros_like(hcar_ref)
        st_ref[...] = jnp.zeros_like(st_ref)

    _mix_tile(x_ref.at[0], m_ref.at[0], o_ref.at[0], t, pre_ref, post_ref, win_ref, cw_ref, cb_ref,
              wa_ref, ba_ref, wx_ref, bx_ref, lam_ref, lbl_ref, hgw_ref, wout_ref,
              ucar_ref, hcar_ref, st_ref, ohg_ref)

    @pl.when(t == n_t - 1)
    def _():
        h_out_ref[0] = hcar_ref[...]
        conv_out_ref[0] = ucar_ref[V7X_SUBLANES - (CONV_WIDTH - 1):V7X_SUBLANES, :]
        for hd in range(HG_HEADS):
            s_out_ref[0, hd] = st_ref[hd].T


def _mix_prompt(x, mod3, p):
    batch, seq_len, _ = x.shape
    n_t = seq_len // MIX_TILE
    tok_spec = pl.BlockSpec((1, MIX_TILE, D_MODEL), lambda b, t: (b, t, 0))
    return pl.pallas_call(
        _mix_prompt_kernel,
        grid=(batch, n_t),
        in_specs=[
            tok_spec,
            pl.BlockSpec((1, 3, D_MODEL), lambda b, t: (b, 0, 0)),
            _const_spec((1, D_MODEL)),
            _const_spec((1, D_MODEL)),
            _const_spec((D_MODEL, D_PROJ)),
            _const_spec((CONV_WIDTH, LRU_WIDTH)),
            _const_spec((1, LRU_WIDTH)),
            _const_spec((LRU_WIDTH, LRU_WIDTH)),
            _const_spec((1, LRU_WIDTH)),
            _const_spec((LRU_WIDTH, LRU_WIDTH)),
            _const_spec((1, LRU_WIDTH)),
            _const_spec((1, LRU_WIDTH)),
            _const_spec(p["lb_logits"].shape),
            _const_spec((1, HG_HEAD_DIM)),
            _const_spec((D_MODEL, D_MODEL)),
        ],
        out_specs=[
            tok_spec,
            pl.BlockSpec((1, 1, LRU_WIDTH), lambda b, t: (b, 0, 0)),
            pl.BlockSpec((1, CONV_WIDTH - 1, LRU_WIDTH), lambda b, t: (b, 0, 0)),
            pl.BlockSpec((1, HG_HEADS, HG_HEAD_DIM, HG_HEAD_DIM), lambda b, t: (b, 0, 0, 0)),
            _const_spec((D_MODEL, D_PROJ)),
            _const_spec((D_MODEL, D_MODEL)),
        ],
        out_shape=[
            jax.ShapeDtypeStruct((batch, seq_len, D_MODEL), F32),
            jax.ShapeDtypeStruct((batch, 1, LRU_WIDTH), F32),
            jax.ShapeDtypeStruct((batch, CONV_WIDTH - 1, LRU_WIDTH), F32),
            jax.ShapeDtypeStruct((batch, HG_HEADS, HG_HEAD_DIM, HG_HEAD_DIM), F32),
            jax.ShapeDtypeStruct((D_MODEL, D_PROJ), BF16),
            jax.ShapeDtypeStruct((D_MODEL, D_MODEL), BF16),
        ],
        scratch_shapes=[
            pltpu.VMEM((V7X_SUBLANES, LRU_WIDTH), F32),
            pltpu.VMEM((1, LRU_WIDTH), F32),
            pltpu.VMEM((HG_HEADS, HG_HEAD_DIM, HG_HEAD_DIM), F32),
            pltpu.VMEM((MIX_TILE, HG_WIDTH), F32),
        ],
        compiler_params=pltpu.CompilerParams(
            dimension_semantics=("arbitrary", "arbitrary"), vmem_limit_bytes=VMEM_LIMIT),
        name="mix_prompt",
    )(x, mod3, p["ln_mix_pre"], p["ln_mix_post"], p["w_in"], p["conv_w"], p["conv_b"],
      p["wa_bd"], p["b_a"], p["wx_bd"], p["b_x"], p["lam"], p["lb_logits"], p["hg_norm_w"],
      p["w_out"])


def _mix_sample_kernel(x_ref, m_ref, pre_ref, post_ref, win_ref, cw_ref, cb_ref,
                       wa_ref, ba_ref, wx_ref, bx_ref, lam_ref, lbl_ref, hgw_ref, wout_ref,
                       h0_ref, conv0_ref, s0_ref,
                       o_ref, h_out_ref, conv_out_ref, s_out_ref,
                       qf_ref, f_ref, k_ref, v_ref, g_ref, qk_ref, olru_ref, ohg_ref, ostage_ref):
    i = pl.program_id(0)
    n_i = pl.num_programs(0)
    nb = SAMPLE_SEQ_BLOCK

    @pl.when(i == 0)
    def _():
        x = x_ref[...]
        h = _rms(x, pre_ref[...]) * (1.0 + m_ref[1]) + m_ref[0]
        proj = _dot(h.astype(BF16), win_ref[...])
        u = proj[:, 0:LRU_WIDTH]
        y_lru = proj[:, LRU_WIDTH:2 * LRU_WIDTH]
        o0 = 2 * LRU_WIDTH
        q = proj[:, o0:o0 + HG_WIDTH]
        f_raw = proj[:, o0 + HG_WIDTH:o0 + 2 * HG_WIDTH]
        v = proj[:, o0 + 2 * HG_WIDTH:o0 + 3 * HG_WIDTH]
        g_ref[...] = proj[:, o0 + 3 * HG_WIDTH:o0 + 4 * HG_WIDTH]

        cw = cw_ref[...]
        u_conv = cb_ref[...]
        for k in range(CONV_WIDTH - 1):
            u_conv = u_conv + conv0_ref[k] * cw[k:k + 1]
            if k > 0:
                conv_out_ref[k - 1] = conv0_ref[k]
        u_conv = u_conv + u * cw[CONV_WIDTH - 1:CONV_WIDTH]
        conv_out_ref[CONV_WIDTH - 2] = u

        a, mult, ig = _lru_gates(u_conv, wa_ref, ba_ref[...], wx_ref, bx_ref[...], lam_ref[...])
        hs = (mult * ig) * u_conv + a * h0_ref[...]
        h_out_ref[...] = hs
        olru_ref[...] = hs * _gelu_tanh(y_lru)

        lb = _lower_bound(lbl_ref[...])
        f = lb + (1.0 - lb) * _sigmoid(f_raw)
        kk = 1.0 - f
        qs = q * (HG_HEAD_DIM ** -0.5)
        f_ref[...] = f
        qf_ref[...] = qs * f
        k_ref[...] = kk
        v_ref[...] = v
        qk = qs * kk
        for hd in range(HG_HEADS):
            l0 = hd * HG_HEAD_DIM
            tot = jnp.sum(qk[:, l0:l0 + HG_HEAD_DIM], axis=-1, keepdims=True)
            qk_ref[:, l0:l0 + HG_HEAD_DIM] = jnp.broadcast_to(tot, (qk.shape[0], HG_HEAD_DIM))

    base = pl.multiple_of(i * nb, nb)
    grp = lambda ref: ref[pl.ds(base, nb), :]
    f_g, k_g, v_g, qf_g, qk_g = grp(f_ref), grp(k_ref), grp(v_ref), grp(qf_ref), grp(qk_ref)
    square = (HG_HEAD_DIM, HG_HEAD_DIM)
    for j in range(nb):
        for hd in range(HG_HEADS):
            l0 = hd * HG_HEAD_DIM
            rowv = lambda z: z[j:j + 1, l0:l0 + HG_HEAD_DIM]
            s_old = s0_ref[j, hd]
            f_col = jnp.broadcast_to(rowv(f_g), square).T
            k_col = jnp.broadcast_to(rowv(k_g), square).T
            v_row = rowv(v_g)
            outer = k_col.astype(BF16).astype(F32) * v_row.astype(BF16).astype(F32)
            s_out_ref[j, hd] = f_col * s_old + outer
            qf8 = jnp.broadcast_to(rowv(qf_g), (V7X_SUBLANES, HG_HEAD_DIM)).astype(BF16)
            o1 = _dot(qf8, s_old.astype(BF16))[0:1]
            ostage_ref[j:j + 1, l0:l0 + HG_HEAD_DIM] = o1 + rowv(qk_g) * v_row
    ohg_ref[pl.ds(base, nb), :] = ostage_ref[...]

    @pl.when(i == n_i - 1)
    def _():
        hgw = hgw_ref[...]
        parts = [olru_ref[...]]
        for hd in range(HG_HEADS):
            l0 = hd * HG_HEAD_DIM
            parts.append(_head_rms_gate(ohg_ref[:, l0:l0 + HG_HEAD_DIM], hgw,
                                        g_ref[:, l0:l0 + HG_HEAD_DIM]))
        mix_in = jnp.concatenate(parts, axis=-1).astype(BF16)
        mix = _dot(mix_in, wout_ref[...])
        o_ref[...] = x_ref[...] + m_ref[2] * _rms(mix, post_ref[...])


def _mix_sample(x, mod9, p, h0, conv0, s0):
    n_seq = x.shape[0]
    nb = SAMPLE_SEQ_BLOCK
    state_spec = pl.BlockSpec((nb, HG_HEADS, HG_HEAD_DIM, HG_HEAD_DIM), lambda i: (i, 0, 0, 0))
    full2 = lambda shape: pl.BlockSpec(shape, lambda i: (0,) * len(shape))
    return pl.pallas_call(
        _mix_sample_kernel,
        grid=(n_seq // nb,),
        in_specs=[
            _const_spec((n_seq, D_MODEL)),
            pl.BlockSpec((3, n_seq, D_MODEL), lambda i: (1, 0, 0), pipeline_mode=pl.Buffered(1)),
            _const_spec((1, D_MODEL)),
            _const_spec((1, D_MODEL)),
            _const_spec((D_MODEL, D_PROJ)),
            _const_spec((CONV_WIDTH, LRU_WIDTH)),
            _const_spec((1, LRU_WIDTH)),
            _const_spec((LRU_WIDTH, LRU_WIDTH)),
            _const_spec((1, LRU_WIDTH)),
            _const_spec((LRU_WIDTH, LRU_WIDTH)),
            _const_spec((1, LRU_WIDTH)),
            _const_spec((1, LRU_WIDTH)),
            _const_spec(p["lb_logits"].shape),
            _const_spec((1, HG_HEAD_DIM)),
            _const_spec((D_MODEL, D_MODEL)),
            _const_spec((n_seq, LRU_WIDTH)),
            _const_spec((CONV_WIDTH - 1, n_seq, LRU_WIDTH)),
            state_spec,
        ],
        out_specs=[
            full2((n_seq, D_MODEL)),
            full2((n_seq, LRU_WIDTH)),
            full2((CONV_WIDTH - 1, n_seq, LRU_WIDTH)),
            state_spec,
        ],
        out_shape=[
            jax.ShapeDtypeStruct((n_seq, D_MODEL), F32),
            jax.ShapeDtypeStruct((n_seq, LRU_WIDTH), F32),
            jax.ShapeDtypeStruct((CONV_WIDTH - 1, n_seq, LRU_WIDTH), F32),
            jax.ShapeDtypeStruct((n_seq, HG_HEADS, HG_HEAD_DIM, HG_HEAD_DIM), F32),
        ],
        scratch_shapes=[pltpu.VMEM((n_seq, HG_WIDTH), F32) for _ in range(8)]
        + [pltpu.VMEM((nb, HG_WIDTH), F32)],
        compiler_params=pltpu.CompilerParams(
            dimension_semantics=("arbitrary",), vmem_limit_bytes=VMEM_LIMIT),
        name="mix_sample",
    )(x, mod9, p["ln_mix_pre"], p["ln_mix_post"], p["w_in"], p["conv_w"], p["conv_b"],
      p["wa_bd"], p["b_a"], p["wx_bd"], p["b_x"], p["lam"], p["lb_logits"], p["hg_norm_w"],
      p["w_out"], h0, conv0, s0)


def _block_diag(w):
    heads, blk, _ = w.shape
    eye = jnp.eye(heads, dtype=w.dtype)
    return (eye[:, None, :, None] * w[:, :, None, :]).reshape(heads * blk, heads * blk)


def kernel(x_prompt, x_sample, c_prompt, c_sample, state_lru_h, state_lru_conv, state_hgrn_S, w_ada, b_ada, ln_ffn1_pre, ln_ffn1_post, ffn1_w_gate, ffn1_w_up, ffn1_w_down, ln_mix_pre, ln_mix_post, w_in, lru_conv_w, lru_conv_b, lru_w_a, lru_b_a, lru_w_x, lru_b_x, lru_lambda, hg_lb_logits, hg_norm_w, w_out, ln_ffn2_pre, ln_ffn2_post, ffn2_w_gate, ffn2_w_up, ffn2_w_down):
    depth = w_ada.shape[0]
    batch, seq_len, _ = x_prompt.shape
    n_seq = x_sample.shape[0]
    assert depth == 1 and x_sample.shape[1] == 1
    assert seq_len % FFN_TILE == 0 and seq_len % MIX_TILE == 0 and n_seq % SAMPLE_SEQ_BLOCK == 0
    assert D_PROJ == 6 * LRU_WIDTH and LRU_WIDTH == HG_WIDTH

    xp = x_prompt.reshape(batch * seq_len, D_MODEL)
    xs = x_sample.reshape(n_seq, D_MODEL)
    ph, pc, pS, sh, sc, sS = [], [], [], [], [], []
    for l in range(depth):
        row = lambda w: w[l].reshape(1, -1)
        mod9 = _ada(c_sample, c_prompt, w_ada[l], b_ada[l])
        mod_p = jnp.transpose(mod9[:, n_seq:], (1, 0, 2))
        p = {
            "ln_mix_pre": row(ln_mix_pre), "ln_mix_post": row(ln_mix_post),
            "w_in": w_in[l], "w_out": w_out[l],
            "conv_w": lru_conv_w[l], "conv_b": row(lru_conv_b),
            "wa_bd": _block_diag(lru_w_a[l]).astype(BF16), "b_a": row(lru_b_a),
            "wx_bd": _block_diag(lru_w_x[l]).astype(BF16), "b_x": row(lru_b_x),
            "lam": row(lru_lambda), "lb_logits": hg_lb_logits,
            "hg_norm_w": row(hg_norm_w),
        }
        f1 = (row(ln_ffn1_pre), row(ln_ffn1_post), ffn1_w_gate[l], ffn1_w_up[l], ffn1_w_down[l])
        f2 = (row(ln_ffn2_pre), row(ln_ffn2_post), ffn2_w_gate[l], ffn2_w_up[l], ffn2_w_down[l])

        xp, xs = _ffn(xp, xs, mod_p[:, 0:3], mod9, 0, *f1, seq_len)
        xp, h_p, c_p, S_p, w_in_bf, w_out_bf = _mix_prompt(
            xp.reshape(batch, seq_len, D_MODEL), mod_p[:, 3:6], p)
        xp = xp.reshape(batch * seq_len, D_MODEL)
        conv0 = jnp.transpose(state_lru_conv[l], (1, 0, 2))
        xs, h_s, c_s, S_s = _mix_sample(xs, mod9, dict(p, w_in=w_in_bf, w_out=w_out_bf),
                                        state_lru_h[l], conv0, state_hgrn_S[l])
        xp, xs = _ffn(xp, xs, mod_p[:, 6:9], mod9, 2, *f2, seq_len)

        ph.append(h_p.reshape(batch, LRU_WIDTH)); pc.append(c_p); pS.append(S_p)
        sh.append(h_s); sc.append(jnp.transpose(c_s, (1, 0, 2))); sS.append(S_s)

    return (xp.reshape(batch, seq_len, D_MODEL), xs.reshape(n_seq, 1, D_MODEL),
            jnp.stack(ph), jnp.stack(pc), jnp.stack(pS), jnp.stack(sh), jnp.stack(sc), jnp.stack(sS))
```

```python
import jax
import jax.numpy as jnp
from jax import lax
from jax.experimental import pallas as pl
from jax.experimental.pallas import tpu as pltpu

F32 = jnp.float32
BF16 = jnp.bfloat16

D_MODEL = 1024
D_FF = 2816
LRU_WIDTH = 512
CONV_WIDTH = 4
LRU_C = 8.0
HG_WIDTH = 512
HG_HEAD_DIM = 128
HG_HEADS = HG_WIDTH // HG_HEAD_DIM
HG_CHUNK = 64
N_MOD = 9
D_PROJ = 2 * LRU_WIDTH + 4 * HG_WIDTH
EPS = 1e-6

V7X_SUBLANES = 8
V7X_VMEM_BYTES = 64 * 1024 * 1024
VMEM_LIMIT = V7X_VMEM_BYTES - 8 * 1024 * 1024

FFN_TILE = 1024
FFN_ROWS = 512
MIX_TILE = 512
HG_DIRECT_ROWS = 256
FFN_WBLOCK = 256
FFN_WSTEPS = D_FF // FFN_WBLOCK
SAMPLE_SEQ_BLOCK = 2 * V7X_SUBLANES

HG_LEVELS = (1, 2, 4, 8, 16, 32)
HG_DIRECT_MIN = 2.0 ** -100


def _rms(x, w):
    return (x * lax.rsqrt(jnp.mean(x * x, axis=-1, keepdims=True) + EPS)) * w


def _sigmoid(x):
    return 1.0 / (1.0 + jnp.exp(-x))


def _silu(x):
    return x * _sigmoid(x)


def _gelu_tanh(x):
    c = 0.7978845608028654
    return x * (0.5 * (1.0 + jnp.tanh(c * (x + 0.044715 * (x * x * x)))))


def _softplus(z):
    return jnp.maximum(z, 0.0) + jnp.log1p(jnp.exp(-jnp.abs(z)))


def _dot(a, b):
    return jnp.dot(a, b, preferred_element_type=F32)


def _dot_nt(a, b):
    return lax.dot_general(a, b, (((1,), (1,)), ((), ())), preferred_element_type=F32)


def _dot_tn(a, b):
    return lax.dot_general(a, b, (((0,), (0,)), ((), ())), preferred_element_type=F32)


def _const_spec(shape):
    nd = len(shape)
    return pl.BlockSpec(shape, lambda *_: (0,) * nd, pipeline_mode=pl.Buffered(1))


def _ada_kernel(cs_ref, cp_ref, w_ref, b_ref, o_ref):
    s = _silu(jnp.concatenate([cs_ref[...], cp_ref[...]], axis=0))
    o_ref[0] = _dot(s.astype(BF16), w_ref[...].astype(BF16)) + b_ref[...]


def _ada(c_sample, c_prompt, w_ada, b_ada):
    rows = c_sample.shape[0] + c_prompt.shape[0]
    return pl.pallas_call(
        _ada_kernel,
        grid=(N_MOD,),
        in_specs=[
            _const_spec(c_sample.shape),
            _const_spec(c_prompt.shape),
            pl.BlockSpec((D_MODEL, D_MODEL), lambda j: (0, j)),
            pl.BlockSpec((1, D_MODEL), lambda j: (0, j)),
        ],
        out_specs=pl.BlockSpec((1, rows, D_MODEL), lambda j: (j, 0, 0)),
        out_shape=jax.ShapeDtypeStruct((N_MOD, rows, D_MODEL), F32),
        compiler_params=pltpu.CompilerParams(
            dimension_semantics=("arbitrary",), vmem_limit_bytes=VMEM_LIMIT),
        name="ada_mod",
    )(c_sample, c_prompt, w_ada, b_ada.reshape(1, N_MOD * D_MODEL))


def _ffn_body(x, shift, scale, gate, pre_w, post_w, wg_ref, wu_ref, wd_ref):
    h = _rms(x, pre_w) * (1.0 + scale) + shift
    hb = h.astype(BF16)
    a = _dot(hb, wg_ref[...])
    u = _dot(hb, wu_ref[...])
    act = (_silu(a) * u).astype(BF16)
    y = _dot(act, wd_ref[...])
    return x + (0.5 * gate) * _rms(y, post_w)


def _ffn_kernel(xp_ref, mp_ref, xs_ref, ms_ref, pre_ref, post_ref, wg32_ref, wu32_ref, wd32_ref,
                op_ref, os_ref, wg_ref, wu_ref, wd_ref):
    s = pl.program_id(0)
    n_steps = pl.num_programs(0)

    for j in range(FFN_WSTEPS):
        @pl.when(s == j)
        def _(j=j):
            c0 = j * FFN_WBLOCK
            wg_ref[:, c0:c0 + FFN_WBLOCK] = wg32_ref[...].astype(BF16)
            wu_ref[:, c0:c0 + FFN_WBLOCK] = wu32_ref[...].astype(BF16)
            wd_ref[c0:c0 + FFN_WBLOCK, :] = wd32_ref[...].astype(BF16)

    @pl.when((s >= FFN_WSTEPS) & (s < n_steps - 1))
    def _():
        m = mp_ref[0]
        for r0 in range(0, FFN_TILE, FFN_ROWS):
            op_ref[r0:r0 + FFN_ROWS, :] = _ffn_body(
                xp_ref[r0:r0 + FFN_ROWS, :], m[0:1], m[1:2], m[2:3],
                pre_ref[...], post_ref[...], wg_ref, wu_ref, wd_ref)

    @pl.when(s == n_steps - 1)
    def _():
        os_ref[...] = _ffn_body(xs_ref[...], ms_ref[0], ms_ref[1], ms_ref[2], pre_ref[...],
                                post_ref[...], wg_ref, wu_ref, wd_ref)


def _ffn(xp, xs, mod_p3, mod9, sub_layer, pre_w, post_w, wg, wu, wd, seq_len):
    n_tok = xp.shape[0]
    n_seq = xs.shape[0]
    n_tiles = n_tok // FFN_TILE
    steps_per_seq = seq_len // FFN_TILE
    tile = lambda s: jnp.clip(s - FFN_WSTEPS, 0, n_tiles - 1)
    wblk = lambda s: jnp.minimum(s, FFN_WSTEPS - 1)
    tok_spec = pl.BlockSpec((FFN_TILE, D_MODEL), lambda s: (tile(s), 0))
    return pl.pallas_call(
        _ffn_kernel,
        grid=(FFN_WSTEPS + n_tiles + 1,),
        in_specs=[
            tok_spec,
            pl.BlockSpec((1, 3, D_MODEL), lambda s: (tile(s) // steps_per_seq, 0, 0)),
            _const_spec((n_seq, D_MODEL)),
            pl.BlockSpec((3, n_seq, D_MODEL), lambda s: (sub_layer, 0, 0), pipeline_mode=pl.Buffered(1)),
            _const_spec((1, D_MODEL)),
            _const_spec((1, D_MODEL)),
            pl.BlockSpec((D_MODEL, FFN_WBLOCK), lambda s: (0, wblk(s))),
            pl.BlockSpec((D_MODEL, FFN_WBLOCK), lambda s: (0, wblk(s))),
            pl.BlockSpec((FFN_WBLOCK, D_MODEL), lambda s: (wblk(s), 0)),
        ],
        out_specs=[tok_spec, pl.BlockSpec((n_seq, D_MODEL), lambda s: (0, 0))],
        out_shape=[jax.ShapeDtypeStruct((n_tok, D_MODEL), F32),
                   jax.ShapeDtypeStruct((n_seq, D_MODEL), F32)],
        scratch_shapes=[pltpu.VMEM((D_MODEL, D_FF), BF16), pltpu.VMEM((D_MODEL, D_FF), BF16),
                        pltpu.VMEM((D_FF, D_MODEL), BF16)],
        compiler_params=pltpu.CompilerParams(
            dimension_semantics=("arbitrary",), vmem_limit_bytes=VMEM_LIMIT),
        name="ffn",
    )(xp, mod_p3, xs, mod9, pre_w, post_w, wg, wu, wd)


def _lower_bound(lb_logits):
    z = lb_logits - jnp.max(lb_logits, axis=0, keepdims=True)
    e = jnp.exp(z)
    return e[0:1] / jnp.sum(e, axis=0, keepdims=True)


def _lru_gates(u_conv, wa_ref, ba, wx_ref, bx, lam):
    ub = u_conv.astype(BF16)
    r = _sigmoid(_dot(ub, wa_ref[...]) + ba)
    ig = _sigmoid(_dot(ub, wx_ref[...]) + bx)
    log_a = (-LRU_C * r) * _softplus(-lam)
    a = jnp.exp(log_a)
    th = jnp.tanh(log_a)
    mult = jnp.sqrt((-2.0 * th) / (1.0 - th))
    return a, mult, ig


def _head_rms_gate(o, norm_w, g):
    return _rms(o, norm_w) * _silu(g)


def _hgrn_chunk_head(scores, q_in, k_out, dec, vb_blk, g_blk, hgw, st_ref, hd, ohg_ref, r0, l0):
    st = st_ref[hd]
    o = _dot_nt(q_in, st.astype(BF16)) + _dot(scores.astype(BF16), vb_blk)
    st_ref[hd] = st * dec + _dot_tn(vb_blk, k_out)
    ohg_ref[r0:r0 + HG_CHUNK, l0:l0 + HG_HEAD_DIM] = _head_rms_gate(o, hgw, g_blk)


def _hgrn_chunks_direct(qs, kk, pfx, tails, vb, g, hgw, st_ref, ohg_ref, r0):
    tb = qs.shape[0]
    n_chunk = len(tails)
    inv = 1.0 / pfx
    q_dec = qs * pfx
    k_dec = kk * inv
    q_in = q_dec.astype(BF16)
    k_in = k_dec.astype(BF16)
    ti = lax.broadcasted_iota(jnp.int32, (tb, tb), 0)
    si = lax.broadcasted_iota(jnp.int32, (tb, tb), 1)
    visible = (ti >= si) & ((ti ^ si) < HG_CHUNK)
    row_chunk = jnp.right_shift(lax.broadcasted_iota(jnp.int32, (tb, HG_HEAD_DIM), 0),
                                HG_CHUNK.bit_length() - 1)
    tail_rows = jnp.concatenate([jnp.broadcast_to(t, (HG_CHUNK, t.shape[-1])) for t in tails], axis=0)
    k_end = k_dec * tail_rows
    for hd in range(HG_HEADS):
        l0 = hd * HG_HEAD_DIM
        head = lambda z: z[:, l0:l0 + HG_HEAD_DIM]
        by_chunk = lambda z: jnp.concatenate(
            [jnp.where(row_chunk == c, head(z), 0.0) for c in range(n_chunk)], axis=-1).astype(BF16)
        scores = jnp.where(visible, _dot_nt(head(q_in), head(k_in)), 0.0)
        grown = _dot_tn(head(vb), by_chunk(k_end))
        st = st_ref[hd]
        states = []
        for c, tail in enumerate(tails):
            states.append(st.astype(BF16))
            st = st * head(tail) + grown[:, c * HG_HEAD_DIM:(c + 1) * HG_HEAD_DIM]
        st_ref[hd] = st
        o = (_dot_nt(by_chunk(q_dec), jnp.concatenate(states, axis=-1))
             + _dot(scores.astype(BF16), head(vb)))
        ohg_ref[r0:r0 + tb, l0:l0 + HG_HEAD_DIM] = _head_rms_gate(o, hgw, head(g))


def _hgrn_chunks_levels(f3, kk3, qs3, vb, g, hgw, st_ref, ohg_ref):
    n_grp = f3.shape[0]
    tb = n_grp * V7X_SUBLANES
    sub = lax.broadcasted_iota(jnp.int32, f3.shape, 1)
    grp = lax.broadcasted_iota(jnp.int32, f3.shape, 0)
    to_mxu = lambda z: z.reshape(tb, z.shape[-1]).astype(BF16)
    q_lv = [to_mxu(qs3)]
    k_lv = [to_mxu(kk3)]
    tot, pre, post = f3, f3, None
    for lvl in HG_LEVELS:
        q_lv.append(to_mxu(qs3 * pre))
        k_lv.append(k_lv[0] if post is None else to_mxu(kk3 * post))
        if lvl < V7X_SUBLANES:
            upper = (sub & lvl) != 0
            below = pltpu.roll(tot, lvl, 1)
            above = pltpu.roll(tot, V7X_SUBLANES - lvl, 1)
        else:
            upper = (grp & (lvl // V7X_SUBLANES)) != 0
            below = jnp.roll(tot, lvl // V7X_SUBLANES, axis=0)
            above = jnp.roll(tot, -(lvl // V7X_SUBLANES), axis=0)
        pre = jnp.where(upper, pre * below, pre)
        post = jnp.where(upper, 1.0, above) if post is None else jnp.where(upper, post, post * above)
        tot = tot * jnp.where(upper, below, above)
    q_in = to_mxu(qs3 * pre)
    k_out = to_mxu(kk3 * post)
    s_decay = tot.reshape(tb, tot.shape[-1])

    ti = lax.broadcasted_iota(jnp.int32, (HG_CHUNK, HG_CHUNK), 0)
    si = lax.broadcasted_iota(jnp.int32, (HG_CHUNK, HG_CHUNK), 1)
    masks = [ti == si]
    tx = ti ^ si
    for lvl in HG_LEVELS:
        masks.append((tx >= lvl) & (tx < 2 * lvl) & ((ti & lvl) != 0))

    for c in range(tb // HG_CHUNK):
        r0 = c * HG_CHUNK
        for hd in range(HG_HEADS):
            l0 = hd * HG_HEAD_DIM
            blk = lambda z: z[r0:r0 + HG_CHUNK, l0:l0 + HG_HEAD_DIM]
            scores = jnp.zeros((HG_CHUNK, HG_CHUNK), F32)
            for ql, kl, msk in zip(q_lv, k_lv, masks):
                scores = jnp.where(msk, _dot_nt(blk(ql), blk(kl)), scores)
            dec = s_decay[r0 + HG_CHUNK - 1:r0 + HG_CHUNK, l0:l0 + HG_HEAD_DIM]
            _hgrn_chunk_head(scores, blk(q_in), blk(k_out), dec, blk(vb), blk(g), hgw,
                             st_ref, hd, ohg_ref, r0, l0)

def _mix_tile(x_ref, m_ref, o_ref, t, pre_ref, post_ref, win_ref, cw_ref, cb_ref, wa_ref, ba_ref,
              wx_ref, bx_ref, lam_ref, lbl_ref, hgw_ref, wout_ref, ucar_ref, hcar_ref, st_ref, ohg_ref):
    tb = MIX_TILE
    x = x_ref[...]
    m = m_ref[...]
    hb = (_rms(x, pre_ref[...]) * (1.0 + m[1:2]) + m[0:1]).astype(BF16)
    u, y_lru, q, f_raw, v, g = [_dot(hb, win_ref[:, c0:c0 + LRU_WIDTH])
                                for c0 in range(0, D_PROJ, LRU_WIDTH)]

    n_grp = tb // V7X_SUBLANES
    grouped = lambda z: z.reshape(n_grp, V7X_SUBLANES, z.shape[-1])
    flat = lambda z: z.reshape(tb, z.shape[-1])
    gshape = (n_grp, V7X_SUBLANES, LRU_WIDTH)
    sub = lax.broadcasted_iota(jnp.int32, gshape, 1)
    grp = lax.broadcasted_iota(jnp.int32, gshape, 0)

    u3 = grouped(u)
    u_all = jnp.concatenate([ucar_ref[...][None], u3], axis=0)
    ucar_ref[...] = u3[n_grp - 1]
    cw = cw_ref[...]
    u_conv = cb_ref[...]
    for k in range(CONV_WIDTH - 1):
        back = CONV_WIDTH - 1 - k
        rot = pltpu.roll(u_all, back, 1)
        u_conv = u_conv + jnp.where(sub >= back, rot[1:], rot[:-1]) * cw[k:k + 1]
    u_conv = flat(u_conv + u3 * cw[CONV_WIDTH - 1:CONV_WIDTH])

    a, mult, ig = _lru_gates(u_conv, wa_ref, ba_ref[...], wx_ref, bx_ref[...], lam_ref[...])
    a3, mult3 = grouped(a), grouped(mult)
    first_row = jnp.where(t == 0, 0, -1)
    mult3 = jnp.where(grp * V7X_SUBLANES + sub == first_row, 1.0, mult3)
    b3 = (mult3 * grouped(ig)) * grouped(u_conv)
    s = 1
    while s < V7X_SUBLANES:
        keep = sub >= s
        b3 = jnp.where(keep, a3 * pltpu.roll(b3, s, 1) + b3, b3)
        a3 = jnp.where(keep, a3 * pltpu.roll(a3, s, 1), a3)
        s *= 2
    carry = hcar_ref[...]
    hs = []
    for gi in range(n_grp):
        h_g = a3[gi] * carry + b3[gi]
        carry = h_g[V7X_SUBLANES - 1:V7X_SUBLANES]
        hs.append(h_g)
    hcar_ref[...] = carry
    o_lru = jnp.concatenate(hs, axis=0) * _gelu_tanh(y_lru)

    lb = _lower_bound(lbl_ref[...])
    f3 = grouped(lb + (1.0 - lb) * _sigmoid(f_raw))
    kk3 = 1.0 - f3
    qs3 = grouped(q * (HG_HEAD_DIM ** -0.5))
    vb = v.astype(BF16)
    hgw = hgw_ref[...]

    pfx = f3
    s = 1
    while s < V7X_SUBLANES:
        pfx = jnp.where(sub >= s, pfx * pltpu.roll(pfx, s, 1), pfx)
        s *= 2
    grp_per_chunk = HG_CHUNK // V7X_SUBLANES
    rows, tails = [], []
    for gi in range(n_grp):
        p_g = pfx[gi] if gi % grp_per_chunk == 0 else pfx[gi] * carry
        carry = p_g[V7X_SUBLANES - 1:V7X_SUBLANES]
        rows.append(p_g)
        if gi % grp_per_chunk == grp_per_chunk - 1:
            tails.append(carry)
    pfx = jnp.concatenate(rows, axis=0)
    direct = jnp.min(jnp.concatenate(tails, axis=0)) >= HG_DIRECT_MIN

    @pl.when(direct)
    def _():
        qs, kk = flat(qs3), flat(kk3)
        per_block = HG_DIRECT_ROWS // HG_CHUNK
        for bi, r0 in enumerate(range(0, tb, HG_DIRECT_ROWS)):
            rows = lambda z: z[r0:r0 + HG_DIRECT_ROWS]
            _hgrn_chunks_direct(rows(qs), rows(kk), rows(pfx), tails[bi * per_block:(bi + 1) * per_block],
                                rows(vb), rows(g), hgw, st_ref, ohg_ref, r0)

    @pl.when(jnp.logical_not(direct))
    def _():
        _hgrn_chunks_levels(f3, kk3, qs3, vb, g, hgw, st_ref, ohg_ref)

    mix_in = jnp.concatenate([o_lru, ohg_ref[...]], axis=-1).astype(BF16)
    mix = _dot(mix_in, wout_ref[...])
    o_ref[...] = x + m[2:3] * _rms(mix, post_ref[...])


def _mix_prompt_kernel(x_ref, m_ref, *refs):
    (pre_ref, post_ref, win32_ref, cw_ref, cb_ref, wa_ref, ba_ref, wx_ref, bx_ref, lam_ref, lbl_ref,
     hgw_ref, wout32_ref) = refs[:13]
    (o_ref, h_out_ref, conv_out_ref, s_out_ref, win_ref, wout_ref,
     ucar_ref, hcar_ref, st_ref, ohg_ref) = refs[13:]
    t = pl.program_id(1)
    n_t = pl.num_programs(1)

    @pl.when((pl.program_id(0) == 0) & (t == 0))
    def _():
        win_ref[...] = win32_ref[...].astype(BF16)
        wout_ref[...] = wout32_ref[...].astype(BF16)

    @pl.when(t == 0)
    def _():
        ucar_ref[...] = jnp.zeros_like(ucar_ref)
        hcar_ref[...] = jnp.zeros_like(hcar_ref)
        st_ref[...] = jnp.zeros_like(st_ref)

    _mix_tile(x_ref.at[0], m_ref.at[0], o_ref.at[0], t, pre_ref, post_ref, win_ref, cw_ref, cb_ref,
              wa_ref, ba_ref, wx_ref, bx_ref, lam_ref, lbl_ref, hgw_ref, wout_ref,
              ucar_ref, hcar_ref, st_ref, ohg_ref)

    @pl.when(t == n_t - 1)
    def _():
        h_out_ref[0] = hcar_ref[...]
        conv_out_ref[0] = ucar_ref[V7X_SUBLANES - (CONV_WIDTH - 1):V7X_SUBLANES, :]
        for hd in range(HG_HEADS):
            s_out_ref[0, hd] = st_ref[hd].T


def _mix_prompt(x, mod3, p):
    batch, seq_len, _ = x.shape
    n_t = seq_len // MIX_TILE
    tok_spec = pl.BlockSpec((1, MIX_TILE, D_MODEL), lambda b, t: (b, t, 0))
    return pl.pallas_call(
        _mix_prompt_kernel,
        grid=(batch, n_t),
        in_specs=[
            tok_spec,
            pl.BlockSpec((1, 3, D_MODEL), lambda b, t: (b, 0, 0)),
            _const_spec((1, D_MODEL)),
            _const_spec((1, D_MODEL)),
            _const_spec((D_MODEL, D_PROJ)),
            _const_spec((CONV_WIDTH, LRU_WIDTH)),
            _const_spec((1, LRU_WIDTH)),
            _const_spec((LRU_WIDTH, LRU_WIDTH)),
            _const_spec((1, LRU_WIDTH)),
            _const_spec((LRU_WIDTH, LRU_WIDTH)),
            _const_spec((1, LRU_WIDTH)),
            _const_spec((1, LRU_WIDTH)),
            _const_spec(p["lb_logits"].shape),
            _const_spec((1, HG_HEAD_DIM)),
            _const_spec((D_MODEL, D_MODEL)),
        ],
        out_specs=[
            tok_spec,
            pl.BlockSpec((1, 1, LRU_WIDTH), lambda b, t: (b, 0, 0)),
            pl.BlockSpec((1, CONV_WIDTH - 1, LRU_WIDTH), lambda b, t: (b, 0, 0)),
            pl.BlockSpec((1, HG_HEADS, HG_HEAD_DIM, HG_HEAD_DIM), lambda b, t: (b, 0, 0, 0)),
            _const_spec((D_MODEL, D_PROJ)),
            _const_spec((D_MODEL, D_MODEL)),
        ],
        out_shape=[
            jax.ShapeDtypeStruct((batch, seq_len, D_MODEL), F32),
            jax.ShapeDtypeStruct((batch, 1, LRU_WIDTH), F32),
            jax.ShapeDtypeStruct((batch, CONV_WIDTH - 1, LRU_WIDTH), F32),
            jax.ShapeDtypeStruct((batch, HG_HEADS, HG_HEAD_DIM, HG_HEAD_DIM), F32),
            jax.ShapeDtypeStruct((D_MODEL, D_PROJ), BF16),
            jax.ShapeDtypeStruct((D_MODEL, D_MODEL), BF16),
        ],
        scratch_shapes=[
            pltpu.VMEM((V7X_SUBLANES, LRU_WIDTH), F32),
            pltpu.VMEM((1, LRU_WIDTH), F32),
            pltpu.VMEM((HG_HEADS, HG_HEAD_DIM, HG_HEAD_DIM), F32),
            pltpu.VMEM((MIX_TILE, HG_WIDTH), F32),
        ],
        compiler_params=pltpu.CompilerParams(
            dimension_semantics=("arbitrary", "arbitrary"), vmem_limit_bytes=VMEM_LIMIT),
        name="mix_prompt",
    )(x, mod3, p["ln_mix_pre"], p["ln_mix_post"], p["w_in"], p["conv_w"], p["conv_b"],
      p["wa_bd"], p["b_a"], p["wx_bd"], p["b_x"], p["lam"], p["lb_logits"], p["hg_norm_w"],
      p["w_out"])


def _mix_sample_kernel(x_ref, m_ref, pre_ref, post_ref, win_ref, cw_ref, cb_ref,
                       wa_ref, ba_ref, wx_ref, bx_ref, lam_ref, lbl_ref, hgw_ref, wout_ref,
                       h0_ref, conv0_ref, s0_ref,
                       o_ref, h_out_ref, conv_out_ref, s_out_ref,
                       qf_ref, f_ref, k_ref, v_ref, g_ref, qk_ref, olru_ref, ohg_ref, ostage_ref):
    i = pl.program_id(0)
    n_i = pl.num_programs(0)
    nb = SAMPLE_SEQ_BLOCK

    @pl.when(i == 0)
    def _():
        x = x_ref[...]
        h = _rms(x, pre_ref[...]) * (1.0 + m_ref[1]) + m_ref[0]
        proj = _dot(h.astype(BF16), win_ref[...])
        u = proj[:, 0:LRU_WIDTH]
        y_lru = proj[:, LRU_WIDTH:2 * LRU_WIDTH]
        o0 = 2 * LRU_WIDTH
        q = proj[:, o0:o0 + HG_WIDTH]
        f_raw = proj[:, o0 + HG_WIDTH:o0 + 2 * HG_WIDTH]
        v = proj[:, o0 + 2 * HG_WIDTH:o0 + 3 * HG_WIDTH]
        g_ref[...] = proj[:, o0 + 3 * HG_WIDTH:o0 + 4 * HG_WIDTH]

        cw = cw_ref[...]
        u_conv = cb_ref[...]
        for k in range(CONV_WIDTH - 1):
            u_conv = u_conv + conv0_ref[k] * cw[k:k + 1]
            if k > 0:
                conv_out_ref[k - 1] = conv0_ref[k]
        u_conv = u_conv + u * cw[CONV_WIDTH - 1:CONV_WIDTH]
        conv_out_ref[CONV_WIDTH - 2] = u

        a, mult, ig = _lru_gates(u_conv, wa_ref, ba_ref[...], wx_ref, bx_ref[...], lam_ref[...])
        hs = (mult * ig) * u_conv + a * h0_ref[...]
        h_out_ref[...] = hs
        olru_ref[...] = hs * _gelu_tanh(y_lru)

        lb = _lower_bound(lbl_ref[...])
        f = lb + (1.0 - lb) * _sigmoid(f_raw)
        kk = 1.0 - f
        qs = q * (HG_HEAD_DIM ** -0.5)
        f_ref[...] = f
        qf_ref[...] = qs * f
        k_ref[...] = kk
        v_ref[...] = v
        qk = qs * kk
        for hd in range(HG_HEADS):
            l0 = hd * HG_HEAD_DIM
            tot = jnp.sum(qk[:, l0:l0 + HG_HEAD_DIM], axis=-1, keepdims=True)
            qk_ref[:, l0:l0 + HG_HEAD_DIM] = jnp.broadcast_to(tot, (qk.shape[0], HG_HEAD_DIM))

    base = pl.multiple_of(i * nb, nb)
    grp = lambda ref: ref[pl.ds(base, nb), :]
    f_g, k_g, v_g, qf_g, qk_g = grp(f_ref), grp(k_ref), grp(v_ref), grp(qf_ref), grp(qk_ref)
    square = (HG_HEAD_DIM, HG_HEAD_DIM)
    for j in range(nb):
        for hd in range(HG_HEADS):
            l0 = hd * HG_HEAD_DIM
            rowv = lambda z: z[j:j + 1, l0:l0 + HG_HEAD_DIM]
            s_old = s0_ref[j, hd]
            f_col = jnp.broadcast_to(rowv(f_g), square).T
            k_col = jnp.broadcast_to(rowv(k_g), square).T
            v_row = rowv(v_g)
            outer = k_col.astype(BF16).astype(F32) * v_row.astype(BF16).astype(F32)
            s_out_ref[j, hd] = f_col * s_old + outer
            qf8 = jnp.broadcast_to(rowv(qf_g), (V7X_SUBLANES, HG_HEAD_DIM)).astype(BF16)
            o1 = _dot(qf8, s_old.astype(BF16))[0:1]
            ostage_ref[j:j + 1, l0:l0 + HG_HEAD_DIM] = o1 + rowv(qk_g) * v_row
    ohg_ref[pl.ds(base, nb), :] = ostage_ref[...]

    @pl.when(i == n_i - 1)
    def _():
        hgw = hgw_ref[...]
        parts = [olru_ref[...]]
        for hd in range(HG_HEADS):
            l0 = hd * HG_HEAD_DIM
            parts.append(_head_rms_gate(ohg_ref[:, l0:l0 + HG_HEAD_DIM], hgw,
                                        g_ref[:, l0:l0 + HG_HEAD_DIM]))
        mix_in = jnp.concatenate(parts, axis=-1).astype(BF16)
        mix = _dot(mix_in, wout_ref[...])
        o_ref[...] = x_ref[...] + m_ref[2] * _rms(mix, post_ref[...])


def _mix_sample(x, mod9, p, h0, conv0, s0):
    n_seq = x.shape[0]
    nb = SAMPLE_SEQ_BLOCK
    state_spec = pl.BlockSpec((nb, HG_HEADS, HG_HEAD_DIM, HG_HEAD_DIM), lambda i: (i, 0, 0, 0))
    full2 = lambda shape: pl.BlockSpec(shape, lambda i: (0,) * len(shape))
    return pl.pallas_call(
        _mix_sample_kernel,
        grid=(n_seq // nb,),
        in_specs=[
            _const_spec((n_seq, D_MODEL)),
            pl.BlockSpec((3, n_seq, D_MODEL), lambda i: (1, 0, 0), pipeline_mode=pl.Buffered(1)),
            _const_spec((1, D_MODEL)),
            _const_spec((1, D_MODEL)),
            _const_spec((D_MODEL, D_PROJ)),
            _const_spec((CONV_WIDTH, LRU_WIDTH)),
            _const_spec((1, LRU_WIDTH)),
            _const_spec((LRU_WIDTH, LRU_WIDTH)),
            _const_spec((1, LRU_WIDTH)),
            _const_spec((LRU_WIDTH, LRU_WIDTH)),
            _const_spec((1, LRU_WIDTH)),
            _const_spec((1, LRU_WIDTH)),
            _const_spec(p["lb_logits"].shape),
            _const_spec((1, HG_HEAD_DIM)),
            _const_spec((D_MODEL, D_MODEL)),
            _const_spec((n_seq, LRU_WIDTH)),
            _const_spec((CONV_WIDTH - 1, n_seq, LRU_WIDTH)),
            state_spec,
        ],
        out_specs=[
            full2((n_seq, D_MODEL)),
            full2((n_seq, LRU_WIDTH)),
            full2((CONV_WIDTH - 1, n_seq, LRU_WIDTH)),
            state_spec,
        ],
        out_shape=[
            jax.ShapeDtypeStruct((n_seq, D_MODEL), F32),
            jax.ShapeDtypeStruct((n_seq, LRU_WIDTH), F32),
            jax.ShapeDtypeStruct((CONV_WIDTH - 1, n_seq, LRU_WIDTH), F32),
            jax.ShapeDtypeStruct((n_seq, HG_HEADS, HG_HEAD_DIM, HG_HEAD_DIM), F32),
        ],
        scratch_shapes=[pltpu.VMEM((n_seq, HG_WIDTH), F32) for _ in range(8)]
        + [pltpu.VMEM((nb, HG_WIDTH), F32)],
        compiler_params=pltpu.CompilerParams(
            dimension_semantics=("arbitrary",), vmem_limit_bytes=VMEM_LIMIT),
        name="mix_sample",
    )(x, mod9, p["ln_mix_pre"], p["ln_mix_post"], p["w_in"], p["conv_w"], p["conv_b"],
      p["wa_bd"], p["b_a"], p["wx_bd"], p["b_x"], p["lam"], p["lb_logits"], p["hg_norm_w"],
      p["w_out"], h0, conv0, s0)


def _block_diag(w):
    heads, blk, _ = w.shape
    eye = jnp.eye(heads, dtype=w.dtype)
    return (eye[:, None, :, None] * w[:, :, None, :]).reshape(heads * blk, heads * blk)


def kernel(x_prompt, x_sample, c_prompt, c_sample, state_lru_h, state_lru_conv, state_hgrn_S, w_ada, b_ada, ln_ffn1_pre, ln_ffn1_post, ffn1_w_gate, ffn1_w_up, ffn1_w_down, ln_mix_pre, ln_mix_post, w_in, lru_conv_w, lru_conv_b, lru_w_a, lru_b_a, lru_w_x, lru_b_x, lru_lambda, hg_lb_logits, hg_norm_w, w_out, ln_ffn2_pre, ln_ffn2_post, ffn2_w_gate, ffn2_w_up, ffn2_w_down):
    depth = w_ada.shape[0]
    batch, seq_len, _ = x_prompt.shape
    n_seq = x_sample.shape[0]
    assert depth == 1 and x_sample.shape[1] == 1
    assert seq_len % FFN_TILE == 0 and seq_len % MIX_TILE == 0 and n_seq % SAMPLE_SEQ_BLOCK == 0
    assert D_PROJ == 6 * LRU_WIDTH and LRU_WIDTH == HG_WIDTH

    xp = x_prompt.reshape(batch * seq_len, D_MODEL)
    xs = x_sample.reshape(n_seq, D_MODEL)
    ph, pc, pS, sh, sc, sS = [], [], [], [], [], []
    for l in range(depth):
        row = lambda w: w[l].reshape(1, -1)
        mod9 = _ada(c_sample, c_prompt, w_ada[l], b_ada[l])
        mod_p = jnp.transpose(mod9[:, n_seq:], (1, 0, 2))
        p = {
            "ln_mix_pre": row(ln_mix_pre), "ln_mix_post": row(ln_mix_post),
            "w_in": w_in[l], "w_out": w_out[l],
            "conv_w": lru_conv_w[l], "conv_b": row(lru_conv_b),
            "wa_bd": _block_diag(lru_w_a[l]).astype(BF16), "b_a": row(lru_b_a),
            "wx_bd": _block_diag(lru_w_x[l]).astype(BF16), "b_x": row(lru_b_x),
            "lam": row(lru_lambda), "lb_logits": hg_lb_logits,
            "hg_norm_w": row(hg_norm_w),
        }
        f1 = (row(ln_ffn1_pre), row(ln_ffn1_post), ffn1_w_gate[l], ffn1_w_up[l], ffn1_w_down[l])
        f2 = (row(ln_ffn2_pre), row(ln_ffn2_post), ffn2_w_gate[l], ffn2_w_up[l], ffn2_w_down[l])

        xp, xs = _ffn(xp, xs, mod_p[:, 0:3], mod9, 0, *f1, seq_len)
        xp, h_p, c_p, S_p, w_in_bf, w_out_bf = _mix_prompt(
            xp.reshape(batch, seq_len, D_MODEL), mod_p[:, 3:6], p)
        xp = xp.reshape(batch * seq_len, D_MODEL)
        conv0 = jnp.transpose(state_lru_conv[l], (1, 0, 2))
        xs, h_s, c_s, S_s = _mix_sample(xs, mod9, dict(p, w_in=w_in_bf, w_out=w_out_bf),
                                        state_lru_h[l], conv0, state_hgrn_S[l])
        xp, xs = _ffn(xp, xs, mod_p[:, 6:9], mod9, 2, *f2, seq_len)

        ph.append(h_p.reshape(batch, LRU_WIDTH)); pc.append(c_p); pS.append(S_p)
        sh.append(h_s); sc.append(jnp.transpose(c_s, (1, 0, 2))); sS.append(S_s)

    return (xp.reshape(batch, seq_len, D_MODEL), xs.reshape(n_seq, 1, D_MODEL),
            jnp.stack(ph), jnp.stack(pc), jnp.stack(pS), jnp.stack(sh), jnp.stack(sc), jnp.stack(sS))
```

```python
import jax
import jax.numpy as jnp
from jax import lax
from jax.experimental import pallas as pl
from jax.experimental.pallas import tpu as pltpu

F32 = jnp.float32
BF16 = jnp.bfloat16

D_MODEL = 1024
D_FF = 2816
LRU_WIDTH = 512
CONV_WIDTH = 4
LRU_C = 8.0
HG_WIDTH = 512
HG_HEAD_DIM = 128
HG_HEADS = HG_WIDTH // HG_HEAD_DIM
HG_CHUNK = 64
N_MOD = 9
D_PROJ = 2 * LRU_WIDTH + 4 * HG_WIDTH
EPS = 1e-6

V7X_SUBLANES = 8
V7X_VMEM_BYTES = 64 * 1024 * 1024
VMEM_LIMIT = V7X_VMEM_BYTES - 8 * 1024 * 1024

FFN_TILE = 1024
FFN_ROWS = 1024
MIX_TILE = 512
HG_DIRECT_ROWS = 256
FFN_WBLOCK = 256
FFN_WSTEPS = D_FF // FFN_WBLOCK
SAMPLE_SEQ_BLOCK = 2 * V7X_SUBLANES

HG_LEVELS = (1, 2, 4, 8, 16, 32)
HG_DIRECT_MIN = 2.0 ** -100


def _rms(x, w):
    return (x * lax.rsqrt(jnp.mean(x * x, axis=-1, keepdims=True) + EPS)) * w


def _sigmoid(x):
    return 1.0 / (1.0 + jnp.exp(-x))


def _silu(x):
    return x * _sigmoid(x)


def _gelu_tanh(x):
    c = 0.7978845608028654
    return x * (0.5 * (1.0 + jnp.tanh(c * (x + 0.044715 * (x * x * x)))))


def _softplus(z):
    return jnp.maximum(z, 0.0) + jnp.log1p(jnp.exp(-jnp.abs(z)))


def _dot(a, b):
    return jnp.dot(a, b, preferred_element_type=F32)


def _dot_nt(a, b):
    return lax.dot_general(a, b, (((1,), (1,)), ((), ())), preferred_element_type=F32)


def _dot_tn(a, b):
    return lax.dot_general(a, b, (((0,), (0,)), ((), ())), preferred_element_type=F32)


def _const_spec(shape):
    nd = len(shape)
    return pl.BlockSpec(shape, lambda *_: (0,) * nd, pipeline_mode=pl.Buffered(1))


def _ada_kernel(cs_ref, cp_ref, w_ref, b_ref, o_ref):
    s = _silu(jnp.concatenate([cs_ref[...], cp_ref[...]], axis=0))
    o_ref[0] = _dot(s.astype(BF16), w_ref[...].astype(BF16)) + b_ref[...]


def _ada(c_sample, c_prompt, w_ada, b_ada):
    rows = c_sample.shape[0] + c_prompt.shape[0]
    return pl.pallas_call(
        _ada_kernel,
        grid=(N_MOD,),
        in_specs=[
            _const_spec(c_sample.shape),
            _const_spec(c_prompt.shape),
            pl.BlockSpec((D_MODEL, D_MODEL), lambda j: (0, j)),
            pl.BlockSpec((1, D_MODEL), lambda j: (0, j)),
        ],
        out_specs=pl.BlockSpec((1, rows, D_MODEL), lambda j: (j, 0, 0)),
        out_shape=jax.ShapeDtypeStruct((N_MOD, rows, D_MODEL), F32),
        compiler_params=pltpu.CompilerParams(
            dimension_semantics=("arbitrary",), vmem_limit_bytes=VMEM_LIMIT),
        name="ada_mod",
    )(c_sample, c_prompt, w_ada, b_ada.reshape(1, N_MOD * D_MODEL))


def _ffn_body(x, shift, scale, gate, pre_w, post_w, wg_ref, wu_ref, wd_ref):
    h = _rms(x, pre_w) * (1.0 + scale) + shift
    hb = h.astype(BF16)
    a = _dot(hb, wg_ref[...])
    u = _dot(hb, wu_ref[...])
    act = (_silu(a) * u).astype(BF16)
    y = _dot(act, wd_ref[...])
    return x + (0.5 * gate) * _rms(y, post_w)


def _ffn_kernel(xp_ref, mp_ref, xs_ref, ms_ref, pre_ref, post_ref, wg32_ref, wu32_ref, wd32_ref,
                op_ref, os_ref, wg_ref, wu_ref, wd_ref):
    s = pl.program_id(0)
    n_steps = pl.num_programs(0)

    for j in range(FFN_WSTEPS):
        @pl.when(s == j)
        def _(j=j):
            c0 = j * FFN_WBLOCK
            wg_ref[:, c0:c0 + FFN_WBLOCK] = wg32_ref[...].astype(BF16)
            wu_ref[:, c0:c0 + FFN_WBLOCK] = wu32_ref[...].astype(BF16)
            wd_ref[c0:c0 + FFN_WBLOCK, :] = wd32_ref[...].astype(BF16)

    @pl.when((s >= FFN_WSTEPS) & (s < n_steps - 1))
    def _():
        m = mp_ref[0]
        for r0 in range(0, FFN_TILE, FFN_ROWS):
            op_ref[r0:r0 + FFN_ROWS, :] = _ffn_body(
                xp_ref[r0:r0 + FFN_ROWS, :], m[0:1], m[1:2], m[2:3],
                pre_ref[...], post_ref[...], wg_ref, wu_ref, wd_ref)

    @pl.when(s == n_steps - 1)
    def _():
        os_ref[...] = _ffn_body(xs_ref[...], ms_ref[0], ms_ref[1], ms_ref[2], pre_ref[...],
                                post_ref[...], wg_ref, wu_ref, wd_ref)


def _ffn(xp, xs, mod_p3, mod9, sub_layer, pre_w, post_w, wg, wu, wd, seq_len):
    n_tok = xp.shape[0]
    n_seq = xs.shape[0]
    n_tiles = n_tok // FFN_TILE
    steps_per_seq = seq_len // FFN_TILE
    tile = lambda s: jnp.clip(s - FFN_WSTEPS, 0, n_tiles - 1)
    wblk = lambda s: jnp.minimum(s, FFN_WSTEPS - 1)
    tok_spec = pl.BlockSpec((FFN_TILE, D_MODEL), lambda s: (tile(s), 0))
    return pl.pallas_call(
        _ffn_kernel,
        grid=(FFN_WSTEPS + n_tiles + 1,),
        in_specs=[
            tok_spec,
            pl.BlockSpec((1, 3, D_MODEL), lambda s: (tile(s) // steps_per_seq, 0, 0)),
            _const_spec((n_seq, D_MODEL)),
            pl.BlockSpec((3, n_seq, D_MODEL), lambda s: (sub_layer, 0, 0), pipeline_mode=pl.Buffered(1)),
            _const_spec((1, D_MODEL)),
            _const_spec((1, D_MODEL)),
            pl.BlockSpec((D_MODEL, FFN_WBLOCK), lambda s: (0, wblk(s))),
            pl.BlockSpec((D_MODEL, FFN_WBLOCK), lambda s: (0, wblk(s))),
            pl.BlockSpec((FFN_WBLOCK, D_MODEL), lambda s: (wblk(s), 0)),
        ],
        out_specs=[tok_spec, pl.BlockSpec((n_seq, D_MODEL), lambda s: (0, 0))],
        out_shape=[jax.ShapeDtypeStruct((n_tok, D_MODEL), F32),
                   jax.ShapeDtypeStruct((n_seq, D_MODEL), F32)],
        scratch_shapes=[pltpu.VMEM((D_MODEL, D_FF), BF16), pltpu.VMEM((D_MODEL, D_FF), BF16),
                        pltpu.VMEM((D_FF, D_MODEL), BF16)],
        compiler_params=pltpu.CompilerParams(
            dimension_semantics=("arbitrary",), vmem_limit_bytes=VMEM_LIMIT),
        name="ffn",
    )(xp, mod_p3, xs, mod9, pre_w, post_w, wg, wu, wd)


def _lower_bound(lb_logits):
    z = lb_logits - jnp.max(lb_logits, axis=0, keepdims=True)
    e = jnp.exp(z)
    return e[0:1] / jnp.sum(e, axis=0, keepdims=True)


def _lru_gates(u_conv, wa_ref, ba, wx_ref, bx, lam):
    ub = u_conv.astype(BF16)
    r = _sigmoid(_dot(ub, wa_ref[...]) + ba)
    ig = _sigmoid(_dot(ub, wx_ref[...]) + bx)
    log_a = (-LRU_C * r) * _softplus(-lam)
    a = jnp.exp(log_a)
    th = jnp.tanh(log_a)
    mult = jnp.sqrt((-2.0 * th) / (1.0 - th))
    return a, mult, ig


def _head_rms_gate(o, norm_w, g):
    return _rms(o, norm_w) * _silu(g)


def _hgrn_chunk_head(scores, q_in, k_out, dec, vb_blk, g_blk, hgw, st_ref, hd, ohg_ref, r0, l0):
    st = st_ref[hd]
    o = _dot_nt(q_in, st.astype(BF16)) + _dot(scores.astype(BF16), vb_blk)
    st_ref[hd] = st * dec + _dot_tn(vb_blk, k_out)
    ohg_ref[r0:r0 + HG_CHUNK, l0:l0 + HG_HEAD_DIM] = _head_rms_gate(o, hgw, g_blk)


def _hgrn_chunks_direct(qs, kk, pfx, tails, vb, g, hgw, st_ref, ohg_ref, r0):
    tb = qs.shape[0]
    n_chunk = len(tails)
    inv = 1.0 / pfx
    q_dec = qs * pfx
    k_dec = kk * inv
    q_in = q_dec.astype(BF16)
    k_in = k_dec.astype(BF16)
    ti = lax.broadcasted_iota(jnp.int32, (tb, tb), 0)
    si = lax.broadcasted_iota(jnp.int32, (tb, tb), 1)
    visible = (ti >= si) & ((ti ^ si) < HG_CHUNK)
    row_chunk = jnp.right_shift(lax.broadcasted_iota(jnp.int32, (tb, HG_HEAD_DIM), 0),
                                HG_CHUNK.bit_length() - 1)
    tail_rows = jnp.concatenate([jnp.broadcast_to(t, (HG_CHUNK, t.shape[-1])) for t in tails], axis=0)
    k_end = k_dec * tail_rows
    for hd in range(HG_HEADS):
        l0 = hd * HG_HEAD_DIM
        head = lambda z: z[:, l0:l0 + HG_HEAD_DIM]
        by_chunk = lambda z: jnp.concatenate(
            [jnp.where(row_chunk == c, head(z), 0.0) for c in range(n_chunk)], axis=-1).astype(BF16)
        scores = jnp.where(visible, _dot_nt(head(q_in), head(k_in)), 0.0)
        grown = _dot_tn(head(vb), by_chunk(k_end))
        st = st_ref[hd]
        states = []
        for c, tail in enumerate(tails):
            states.append(st.astype(BF16))
            st = st * head(tail) + grown[:, c * HG_HEAD_DIM:(c + 1) * HG_HEAD_DIM]
        st_ref[hd] = st
        o = (_dot_nt(by_chunk(q_dec), jnp.concatenate(states, axis=-1))
             + _dot(scores.astype(BF16), head(vb)))
        ohg_ref[r0:r0 + tb, l0:l0 + HG_HEAD_DIM] = _head_rms_gate(o, hgw, head(g))


def _hgrn_chunks_levels(f3, kk3, qs3, vb, g, hgw, st_ref, ohg_ref):
    n_grp = f3.shape[0]
    tb = n_grp * V7X_SUBLANES
    sub = lax.broadcasted_iota(jnp.int32, f3.shape, 1)
    grp = lax.broadcasted_iota(jnp.int32, f3.shape, 0)
    to_mxu = lambda z: z.reshape(tb, z.shape[-1]).astype(BF16)
    q_lv = [to_mxu(qs3)]
    k_lv = [to_mxu(kk3)]
    tot, pre, post = f3, f3, None
    for lvl in HG_LEVELS:
        q_lv.append(to_mxu(qs3 * pre))
        k_lv.append(k_lv[0] if post is None else to_mxu(kk3 * post))
        if lvl < V7X_SUBLANES:
            upper = (sub & lvl) != 0
            below = pltpu.roll(tot, lvl, 1)
            above = pltpu.roll(tot, V7X_SUBLANES - lvl, 1)
        else:
            upper = (grp & (lvl // V7X_SUBLANES)) != 0
            below = jnp.roll(tot, lvl // V7X_SUBLANES, axis=0)
            above = jnp.roll(tot, -(lvl // V7X_SUBLANES), axis=0)
        pre = jnp.where(upper, pre * below, pre)
        post = jnp.where(upper, 1.0, above) if post is None else jnp.where(upper, post, post * above)
        tot = tot * jnp.where(upper, below, above)
    q_in = to_mxu(qs3 * pre)
    k_out = to_mxu(kk3 * post)
    s_decay = tot.reshape(tb, tot.shape[-1])

    ti = lax.broadcasted_iota(jnp.int32, (HG_CHUNK, HG_CHUNK), 0)
    si = lax.broadcasted_iota(jnp.int32, (HG_CHUNK, HG_CHUNK), 1)
    masks = [ti == si]
    tx = ti ^ si
    for lvl in HG_LEVELS:
        masks.append((tx >= lvl) & (tx < 2 * lvl) & ((ti & lvl) != 0))

    for c in range(tb // HG_CHUNK):
        r0 = c * HG_CHUNK
        for hd in range(HG_HEADS):
            l0 = hd * HG_HEAD_DIM
            blk = lambda z: z[r0:r0 + HG_CHUNK, l0:l0 + HG_HEAD_DIM]
            scores = jnp.zeros((HG_CHUNK, HG_CHUNK), F32)
            for ql, kl, msk in zip(q_lv, k_lv, masks):
                scores = jnp.where(msk, _dot_nt(blk(ql), blk(kl)), scores)
            dec = s_decay[r0 + HG_CHUNK - 1:r0 + HG_CHUNK, l0:l0 + HG_HEAD_DIM]
            _hgrn_chunk_head(scores, blk(q_in), blk(k_out), dec, blk(vb), blk(g), hgw,
                             st_ref, hd, ohg_ref, r0, l0)

def _mix_tile(x_ref, m_ref, o_ref, t, pre_ref, post_ref, win_ref, cw_ref, cb_ref, wa_ref, ba_ref,
              wx_ref, bx_ref, lam_ref, lbl_ref, hgw_ref, wout_ref, ucar_ref, hcar_ref, st_ref, ohg_ref):
    tb = MIX_TILE
    x = x_ref[...]
    m = m_ref[...]
    hb = (_rms(x, pre_ref[...]) * (1.0 + m[1:2]) + m[0:1]).astype(BF16)
    u, y_lru, q, f_raw, v, g = [_dot(hb, win_ref[:, c0:c0 + LRU_WIDTH])
                                for c0 in range(0, D_PROJ, LRU_WIDTH)]

    n_grp = tb // V7X_SUBLANES
    grouped = lambda z: z.reshape(n_grp, V7X_SUBLANES, z.shape[-1])
    flat = lambda z: z.reshape(tb, z.shape[-1])
    gshape = (n_grp, V7X_SUBLANES, LRU_WIDTH)
    sub = lax.broadcasted_iota(jnp.int32, gshape, 1)
    grp = lax.broadcasted_iota(jnp.int32, gshape, 0)

    u3 = grouped(u)
    u_all = jnp.concatenate([ucar_ref[...][None], u3], axis=0)
    ucar_ref[...] = u3[n_grp - 1]
    cw = cw_ref[...]
    u_conv = cb_ref[...]
    for k in range(CONV_WIDTH - 1):
        back = CONV_WIDTH - 1 - k
        rot = pltpu.roll(u_all, back, 1)
        u_conv = u_conv + jnp.where(sub >= back, rot[1:], rot[:-1]) * cw[k:k + 1]
    u_conv = flat(u_conv + u3 * cw[CONV_WIDTH - 1:CONV_WIDTH])

    a, mult, ig = _lru_gates(u_conv, wa_ref, ba_ref[...], wx_ref, bx_ref[...], lam_ref[...])
    a3, mult3 = grouped(a), grouped(mult)
    first_row = jnp.where(t == 0, 0, -1)
    mult3 = jnp.where(grp * V7X_SUBLANES + sub == first_row, 1.0, mult3)
    b3 = (mult3 * grouped(ig)) * grouped(u_conv)
    s = 1
    while s < V7X_SUBLANES:
        keep = sub >= s
        b3 = jnp.where(keep, a3 * pltpu.roll(b3, s, 1) + b3, b3)
        a3 = jnp.where(keep, a3 * pltpu.roll(a3, s, 1), a3)
        s *= 2
    carry = hcar_ref[...]
    hs = []
    for gi in range(n_grp):
        h_g = a3[gi] * carry + b3[gi]
        carry = h_g[V7X_SUBLANES - 1:V7X_SUBLANES]
        hs.append(h_g)
    hcar_ref[...] = carry
    o_lru = jnp.concatenate(hs, axis=0) * _gelu_tanh(y_lru)

    lb = _lower_bound(lbl_ref[...])
    f3 = grouped(lb + (1.0 - lb) * _sigmoid(f_raw))
    kk3 = 1.0 - f3
    qs3 = grouped(q * (HG_HEAD_DIM ** -0.5))
    vb = v.astype(BF16)
    hgw = hgw_ref[...]

    pfx = f3
    s = 1
    while s < V7X_SUBLANES:
        pfx = jnp.where(sub >= s, pfx * pltpu.roll(pfx, s, 1), pfx)
        s *= 2
    grp_per_chunk = HG_CHUNK // V7X_SUBLANES
    rows, tails = [], []
    for gi in range(n_grp):
        p_g = pfx[gi] if gi % grp_per_chunk == 0 else pfx[gi] * carry
        carry = p_g[V7X_SUBLANES - 1:V7X_SUBLANES]
        rows.append(p_g)
        if gi % grp_per_chunk == grp_per_chunk - 1:
            tails.append(carry)
    pfx = jnp.concatenate(rows, axis=0)
    direct = jnp.min(jnp.concatenate(tails, axis=0)) >= HG_DIRECT_MIN

    @pl.when(direct)
    def _():
        qs, kk = flat(qs3), flat(kk3)
        per_block = HG_DIRECT_ROWS // HG_CHUNK
        for bi, r0 in enumerate(range(0, tb, HG_DIRECT_ROWS)):
            rows = lambda z: z[r0:r0 + HG_DIRECT_ROWS]
            _hgrn_chunks_direct(rows(qs), rows(kk), rows(pfx), tails[bi * per_block:(bi + 1) * per_block],
                                rows(vb), rows(g), hgw, st_ref, ohg_ref, r0)

    @pl.when(jnp.logical_not(direct))
    def _():
        _hgrn_chunks_levels(f3, kk3, qs3, vb, g, hgw, st_ref, ohg_ref)

    mix_in = jnp.concatenate([o_lru, ohg_ref[...]], axis=-1).astype(BF16)
    mix = _dot(mix_in, wout_ref[...])
    o_ref[...] = x + m[2:3] * _rms(mix, post_ref[...])


def _mix_prompt_kernel(x_ref, m_ref, *refs):
    (pre_ref, post_ref, win32_ref, cw_ref, cb_ref, wa_ref, ba_ref, wx_ref, bx_ref, lam_ref, lbl_ref,
     hgw_ref, wout32_ref) = refs[:13]
    (o_ref, h_out_ref, conv_out_ref, s_out_ref, win_ref, wout_ref,
     ucar_ref, hcar_ref, st_ref, ohg_ref) = refs[13:]
    t = pl.program_id(1)
    n_t = pl.num_programs(1)

    @pl.when((pl.program_id(0) == 0) & (t == 0))
    def _():
        win_ref[...] = win32_ref[...].astype(BF16)
        wout_ref[...] = wout32_ref[...].astype(BF16)

    @pl.when(t == 0)
    def _():
        ucar_ref[...] = jnp.zeros_like(ucar_ref)
        hcar_ref[...] = jnp.zeros_like(hcar_ref)
        st_ref[...] = jnp.zeros_like(st_ref)

    _mix_tile(x_ref.at[0], m_ref.at[0], o_ref.at[0], t, pre_ref, post_ref, win_ref, cw_ref, cb_ref,
              wa_ref, ba_ref, wx_ref, bx_ref, lam_ref, lbl_ref, hgw_ref, wout_ref,
              ucar_ref, hcar_ref, st_ref, ohg_ref)

    @pl.when(t == n_t - 1)
    def _():
        h_out_ref[0] = hcar_ref[...]
        conv_out_ref[0] = ucar_ref[V7X_SUBLANES - (CONV_WIDTH - 1):V7X_SUBLANES, :]
        for hd in range(HG_HEADS):
            s_out_ref[0, hd] = st_ref[hd].T


def _mix_prompt(x, mod3, p):
    batch, seq_len, _ = x.shape
    n_t = seq_len // MIX_TILE
    tok_spec = pl.BlockSpec((1, MIX_TILE, D_MODEL), lambda b, t: (b, t, 0))
    return pl.pallas_call(
        _mix_prompt_kernel,
        grid=(batch, n_t),
        in_specs=[
            tok_spec,
            pl.BlockSpec((1, 3, D_MODEL), lambda b, t: (b, 0, 0)),
            _const_spec((1, D_MODEL)),
            _const_spec((1, D_MODEL)),
            _const_spec((D_MODEL, D_PROJ)),
            _const_spec((CONV_WIDTH, LRU_WIDTH)),
            _const_spec((1, LRU_WIDTH)),
            _const_spec((LRU_WIDTH, LRU_WIDTH)),
            _const_spec((1, LRU_WIDTH)),
            _const_spec((LRU_WIDTH, LRU_WIDTH)),
            _const_spec((1, LRU_WIDTH)),
            _const_spec((1, LRU_WIDTH)),
            _const_spec(p["lb_logits"].shape),
            _const_spec((1, HG_HEAD_DIM)),
            _const_spec((D_MODEL, D_MODEL)),
        ],
        out_specs=[
            tok_spec,
            pl.BlockSpec((1, 1, LRU_WIDTH), lambda b, t: (b, 0, 0)),
            pl.BlockSpec((1, CONV_WIDTH - 1, LRU_WIDTH), lambda b, t: (b, 0, 0)),
            pl.BlockSpec((1, HG_HEADS, HG_HEAD_DIM, HG_HEAD_DIM), lambda b, t: (b, 0, 0, 0)),
            _const_spec((D_MODEL, D_PROJ)),
            _const_spec((D_MODEL, D_MODEL)),
        ],
        out_shape=[
            jax.ShapeDtypeStruct((batch, seq_len, D_MODEL), F32),
            jax.ShapeDtypeStruct((batch, 1, LRU_WIDTH), F32),
            jax.ShapeDtypeStruct((batch, CONV_WIDTH - 1, LRU_WIDTH), F32),
            jax.ShapeDtypeStruct((batch, HG_HEADS, HG_HEAD_DIM, HG_HEAD_DIM), F32),
            jax.ShapeDtypeStruct((D_MODEL, D_PROJ), BF16),
            jax.ShapeDtypeStruct((D_MODEL, D_MODEL), BF16),
        ],
        scratch_shapes=[
            pltpu.VMEM((V7X_SUBLANES, LRU_WIDTH), F32),
            pltpu.VMEM((1, LRU_WIDTH), F32),
            pltpu.VMEM((HG_HEADS, HG_HEAD_DIM, HG_HEAD_DIM), F32),
            pltpu.VMEM((MIX_TILE, HG_WIDTH), F32),
        ],
        compiler_params=pltpu.CompilerParams(
            dimension_semantics=("arbitrary", "arbitrary"), vmem_limit_bytes=VMEM_LIMIT),
        name="mix_prompt",
    )(x, mod3, p["ln_mix_pre"], p["ln_mix_post"], p["w_in"], p["conv_w"], p["conv_b"],
      p["wa_bd"], p["b_a"], p["wx_bd"], p["b_x"], p["lam"], p["lb_logits"], p["hg_norm_w"],
      p["w_out"])


def _mix_sample_kernel(x_ref, m_ref, pre_ref, post_ref, win_ref, cw_ref, cb_ref,
                       wa_ref, ba_ref, wx_ref, bx_ref, lam_ref, lbl_ref, hgw_ref, wout_ref,
                       h0_ref, conv0_ref, s0_ref,
                       o_ref, h_out_ref, conv_out_ref, s_out_ref,
                       qf_ref, f_ref, k_ref, v_ref, g_ref, qk_ref, olru_ref, ohg_ref, ostage_ref):
    i = pl.program_id(0)
    n_i = pl.num_programs(0)
    nb = SAMPLE_SEQ_BLOCK

    @pl.when(i == 0)
    def _():
        x = x_ref[...]
        h = _rms(x, pre_ref[...]) * (1.0 + m_ref[1]) + m_ref[0]
        proj = _dot(h.astype(BF16), win_ref[...])
        u = proj[:, 0:LRU_WIDTH]
        y_lru = proj[:, LRU_WIDTH:2 * LRU_WIDTH]
        o0 = 2 * LRU_WIDTH
        q = proj[:, o0:o0 + HG_WIDTH]
        f_raw = proj[:, o0 + HG_WIDTH:o0 + 2 * HG_WIDTH]
        v = proj[:, o0 + 2 * HG_WIDTH:o0 + 3 * HG_WIDTH]
        g_ref[...] = proj[:, o0 + 3 * HG_WIDTH:o0 + 4 * HG_WIDTH]

        cw = cw_ref[...]
        u_conv = cb_ref[...]
        for k in range(CONV_WIDTH - 1):
            u_conv = u_conv + conv0_ref[k] * cw[k:k + 1]
            if k > 0:
                conv_out_ref[k - 1] = conv0_ref[k]
        u_conv = u_conv + u * cw[CONV_WIDTH - 1:CONV_WIDTH]
        conv_out_ref[CONV_WIDTH - 2] = u

        a, mult, ig = _lru_gates(u_conv, wa_ref, ba_ref[...], wx_ref, bx_ref[...], lam_ref[...])
        hs = (mult * ig) * u_conv + a * h0_ref[...]
        h_out_ref[...] = hs
        olru_ref[...] = hs * _gelu_tanh(y_lru)

        lb = _lower_bound(lbl_ref[...])
        f = lb + (1.0 - lb) * _sigmoid(f_raw)
        kk = 1.0 - f
        qs = q * (HG_HEAD_DIM ** -0.5)
        f_ref[...] = f
        qf_ref[...] = qs * f
        k_ref[...] = kk
        v_ref[...] = v
        qk = qs * kk
        for hd in range(HG_HEADS):
            l0 = hd * HG_HEAD_DIM
            tot = jnp.sum(qk[:, l0:l0 + HG_HEAD_DIM], axis=-1, keepdims=True)
            qk_ref[:, l0:l0 + HG_HEAD_DIM] = jnp.broadcast_to(tot, (qk.shape[0], HG_HEAD_DIM))

    base = pl.multiple_of(i * nb, nb)
    grp = lambda ref: ref[pl.ds(base, nb), :]
    f_g, k_g, v_g, qf_g, qk_g = grp(f_ref), grp(k_ref), grp(v_ref), grp(qf_ref), grp(qk_ref)
    square = (HG_HEAD_DIM, HG_HEAD_DIM)
    for j in range(nb):
        for hd in range(HG_HEADS):
            l0 = hd * HG_HEAD_DIM
            rowv = lambda z: z[j:j + 1, l0:l0 + HG_HEAD_DIM]
            s_old = s0_ref[j, hd]
            f_col = jnp.broadcast_to(rowv(f_g), square).T
            k_col = jnp.broadcast_to(rowv(k_g), square).T
            v_row = rowv(v_g)
            outer = k_col.astype(BF16).astype(F32) * v_row.astype(BF16).astype(F32)
            s_out_ref[j, hd] = f_col * s_old + outer
            qf8 = jnp.broadcast_to(rowv(qf_g), (V7X_SUBLANES, HG_HEAD_DIM)).astype(BF16)
            o1 = _dot(qf8, s_old.astype(BF16))[0:1]
            ostage_ref[j:j + 1, l0:l0 + HG_HEAD_DIM] = o1 + rowv(qk_g) * v_row
    ohg_ref[pl.ds(base, nb), :] = ostage_ref[...]

    @pl.when(i == n_i - 1)
    def _():
        hgw = hgw_ref[...]
        parts = [olru_ref[...]]
        for hd in range(HG_HEADS):
            l0 = hd * HG_HEAD_DIM
            parts.append(_head_rms_gate(ohg_ref[:, l0:l0 + HG_HEAD_DIM], hgw,
                                        g_ref[:, l0:l0 + HG_HEAD_DIM]))
        mix_in = jnp.concatenate(parts, axis=-1).astype(BF16)
        mix = _dot(mix_in, wout_ref[...])
        o_ref[...] = x_ref[...] + m_ref[2] * _rms(mix, post_ref[...])


def _mix_sample(x, mod9, p, h0, conv0, s0):
    n_seq = x.shape[0]
    nb = SAMPLE_SEQ_BLOCK
    state_spec = pl.BlockSpec((nb, HG_HEADS, HG_HEAD_DIM, HG_HEAD_DIM), lambda i: (i, 0, 0, 0))
    full2 = lambda shape: pl.BlockSpec(shape, lambda i: (0,) * len(shape))
    return pl.pallas_call(
        _mix_sample_kernel,
        grid=(n_seq // nb,),
        in_specs=[
            _const_spec((n_seq, D_MODEL)),
            pl.BlockSpec((3, n_seq, D_MODEL), lambda i: (1, 0, 0), pipeline_mode=pl.Buffered(1)),
            _const_spec((1, D_MODEL)),
            _const_spec((1, D_MODEL)),
            _const_spec((D_MODEL, D_PROJ)),
            _const_spec((CONV_WIDTH, LRU_WIDTH)),
            _const_spec((1, LRU_WIDTH)),
            _const_spec((LRU_WIDTH, LRU_WIDTH)),
            _const_spec((1, LRU_WIDTH)),
            _const_spec((LRU_WIDTH, LRU_WIDTH)),
            _const_spec((1, LRU_WIDTH)),
            _const_spec((1, LRU_WIDTH)),
            _const_spec(p["lb_logits"].shape),
            _const_spec((1, HG_HEAD_DIM)),
            _const_spec((D_MODEL, D_MODEL)),
            _const_spec((n_seq, LRU_WIDTH)),
            _const_spec((CONV_WIDTH - 1, n_seq, LRU_WIDTH)),
            state_spec,
        ],
        out_specs=[
            full2((n_seq, D_MODEL)),
            full2((n_seq, LRU_WIDTH)),
            full2((CONV_WIDTH - 1, n_seq, LRU_WIDTH)),
            state_spec,
        ],
        out_shape=[
            jax.ShapeDtypeStruct((n_seq, D_MODEL), F32),
            jax.ShapeDtypeStruct((n_seq, LRU_WIDTH), F32),
            jax.ShapeDtypeStruct((CONV_WIDTH - 1, n_seq, LRU_WIDTH), F32),
            jax.ShapeDtypeStruct((n_seq, HG_HEADS, HG_HEAD_DIM, HG_HEAD_DIM), F32),
        ],
        scratch_shapes=[pltpu.VMEM((n_seq, HG_WIDTH), F32) for _ in range(8)]
        + [pltpu.VMEM((nb, HG_WIDTH), F32)],
        compiler_params=pltpu.CompilerParams(
            dimension_semantics=("arbitrary",), vmem_limit_bytes=VMEM_LIMIT),
        name="mix_sample",
    )(x, mod9, p["ln_mix_pre"], p["ln_mix_post"], p["w_in"], p["conv_w"], p["conv_b"],
      p["wa_bd"], p["b_a"], p["wx_bd"], p["b_x"], p["lam"], p["lb_logits"], p["hg_norm_w"],
      p["w_out"], h0, conv0, s0)


def _block_diag(w):
    heads, blk, _ = w.shape
    eye = jnp.eye(heads, dtype=w.dtype)
    return (eye[:, None, :, None] * w[:, :, None, :]).reshape(heads * blk, heads * blk)


def kernel(x_prompt, x_sample, c_prompt, c_sample, state_lru_h, state_lru_conv, state_hgrn_S, w_ada, b_ada, ln_ffn1_pre, ln_ffn1_post, ffn1_w_gate, ffn1_w_up, ffn1_w_down, ln_mix_pre, ln_mix_post, w_in, lru_conv_w, lru_conv_b, lru_w_a, lru_b_a, lru_w_x, lru_b_x, lru_lambda, hg_lb_logits, hg_norm_w, w_out, ln_ffn2_pre, ln_ffn2_post, ffn2_w_gate, ffn2_w_up, ffn2_w_down):
    depth = w_ada.shape[0]
    batch, seq_len, _ = x_prompt.shape
    n_seq = x_sample.shape[0]
    assert depth == 1 and x_sample.shape[1] == 1
    assert seq_len % FFN_TILE == 0 and seq_len % MIX_TILE == 0 and n_seq % SAMPLE_SEQ_BLOCK == 0
    assert D_PROJ == 6 * LRU_WIDTH and LRU_WIDTH == HG_WIDTH

    xp = x_prompt.reshape(batch * seq_len, D_MODEL)
    xs = x_sample.reshape(n_seq, D_MODEL)
    ph, pc, pS, sh, sc, sS = [], [], [], [], [], []
    for l in range(depth):
        row = lambda w: w[l].reshape(1, -1)
        mod9 = _ada(c_sample, c_prompt, w_ada[l], b_ada[l])
        mod_p = jnp.transpose(mod9[:, n_seq:], (1, 0, 2))
        p = {
            "ln_mix_pre": row(ln_mix_pre), "ln_mix_post": row(ln_mix_post),
            "w_in": w_in[l], "w_out": w_out[l],
            "conv_w": lru_conv_w[l], "conv_b": row(lru_conv_b),
            "wa_bd": _block_diag(lru_w_a[l]).astype(BF16), "b_a": row(lru_b_a),
            "wx_bd": _block_diag(lru_w_x[l]).astype(BF16), "b_x": row(lru_b_x),
            "lam": row(lru_lambda), "lb_logits": hg_lb_logits,
            "hg_norm_w": row(hg_norm_w),
        }
        f1 = (row(ln_ffn1_pre), row(ln_ffn1_post), ffn1_w_gate[l], ffn1_w_up[l], ffn1_w_down[l])
        f2 = (row(ln_ffn2_pre), row(ln_ffn2_post), ffn2_w_gate[l], ffn2_w_up[l], ffn2_w_down[l])

        xp, xs = _ffn(xp, xs, mod_p[:, 0:3], mod9, 0, *f1, seq_len)
        xp, h_p, c_p, S_p, w_in_bf, w_out_bf = _mix_prompt(
            xp.reshape(batch, seq_len, D_MODEL), mod_p[:, 3:6], p)
        xp = xp.reshape(batch * seq_len, D_MODEL)
        conv0 = jnp.transpose(state_lru_conv[l], (1, 0, 2))
        xs, h_s, c_s, S_s = _mix_sample(xs, mod9, dict(p, w_in=w_in_bf, w_out=w_out_bf),
                                        state_lru_h[l], conv0, state_hgrn_S[l])
        xp, xs = _ffn(xp, xs, mod_p[:, 6:9], mod9, 2, *f2, seq_len)

        ph.append(h_p.reshape(batch, LRU_WIDTH)); pc.append(c_p); pS.append(S_p)
        sh.append(h_s); sc.append(jnp.transpose(c_s, (1, 0, 2))); sS.append(S_s)

    return (xp.reshape(batch, seq_len, D_MODEL), xs.reshape(n_seq, 1, D_MODEL),
            jnp.stack(ph), jnp.stack(pc), jnp.stack(pS), jnp.stack(sh), jnp.stack(sc), jnp.stack(sS))
```

```python
import jax
import jax.numpy as jnp
from jax import lax
from jax.experimental import pallas as pl
from jax.experimental.pallas import tpu as pltpu

F32 = jnp.float32
BF16 = jnp.bfloat16

D_MODEL = 1024
D_FF = 2816
LRU_WIDTH = 512
CONV_WIDTH = 4
LRU_C = 8.0
HG_WIDTH = 512
HG_HEAD_DIM = 128
HG_HEADS = HG_WIDTH // HG_HEAD_DIM
HG_CHUNK = 64
N_MOD = 9
D_PROJ = 2 * LRU_WIDTH + 4 * HG_WIDTH
EPS = 1e-6

V7X_SUBLANES = 8
V7X_VMEM_BYTES = 64 * 1024 * 1024
VMEM_LIMIT = V7X_VMEM_BYTES - 8 * 1024 * 1024

FFN_TILE = 1024
FFN_ROWS = 512
MIX_TILE = 512
MIX_ROWS = 256
HG_DIRECT_ROWS = 256
FFN_WBLOCK = 256
FFN_WSTEPS = D_FF // FFN_WBLOCK
SAMPLE_SEQ_BLOCK = 2 * V7X_SUBLANES

HG_LEVELS = (1, 2, 4, 8, 16, 32)
HG_DIRECT_MIN = 2.0 ** -100


def _rms(x, w):
    return (x * lax.rsqrt(jnp.mean(x * x, axis=-1, keepdims=True) + EPS)) * w


def _sigmoid(x):
    return 1.0 / (1.0 + jnp.exp(-x))


def _silu(x):
    return x * _sigmoid(x)


def _gelu_tanh(x):
    c = 0.7978845608028654
    return x * (0.5 * (1.0 + jnp.tanh(c * (x + 0.044715 * (x * x * x)))))


def _softplus(z):
    return jnp.maximum(z, 0.0) + jnp.log1p(jnp.exp(-jnp.abs(z)))


def _dot(a, b):
    return jnp.dot(a, b, preferred_element_type=F32)


def _dot_nt(a, b):
    return lax.dot_general(a, b, (((1,), (1,)), ((), ())), preferred_element_type=F32)


def _dot_tn(a, b):
    return lax.dot_general(a, b, (((0,), (0,)), ((), ())), preferred_element_type=F32)


def _const_spec(shape):
    nd = len(shape)
    return pl.BlockSpec(shape, lambda *_: (0,) * nd, pipeline_mode=pl.Buffered(1))


def _ada_kernel(cs_ref, cp_ref, w_ref, b_ref, o_ref):
    s = _silu(jnp.concatenate([cs_ref[...], cp_ref[...]], axis=0))
    o_ref[0] = _dot(s.astype(BF16), w_ref[...].astype(BF16)) + b_ref[...]


def _ada(c_sample, c_prompt, w_ada, b_ada):
    rows = c_sample.shape[0] + c_prompt.shape[0]
    return pl.pallas_call(
        _ada_kernel,
        grid=(N_MOD,),
        in_specs=[
            _const_spec(c_sample.shape),
            _const_spec(c_prompt.shape),
            pl.BlockSpec((D_MODEL, D_MODEL), lambda j: (0, j)),
            pl.BlockSpec((1, D_MODEL), lambda j: (0, j)),
        ],
        out_specs=pl.BlockSpec((1, rows, D_MODEL), lambda j: (j, 0, 0)),
        out_shape=jax.ShapeDtypeStruct((N_MOD, rows, D_MODEL), F32),
        compiler_params=pltpu.CompilerParams(
            dimension_semantics=("arbitrary",), vmem_limit_bytes=VMEM_LIMIT),
        name="ada_mod",
    )(c_sample, c_prompt, w_ada, b_ada.reshape(1, N_MOD * D_MODEL))


def _ffn_body(x, shift, scale, gate, pre_w, post_w, wg_ref, wu_ref, wd_ref):
    h = _rms(x, pre_w) * (1.0 + scale) + shift
    hb = h.astype(BF16)
    a = _dot(hb, wg_ref[...])
    u = _dot(hb, wu_ref[...])
    act = (_silu(a) * u).astype(BF16)
    y = _dot(act, wd_ref[...])
    return x + (0.5 * gate) * _rms(y, post_w)


def _ffn_kernel(xp_ref, mp_ref, xs_ref, ms_ref, pre_ref, post_ref, wg32_ref, wu32_ref, wd32_ref,
                op_ref, os_ref, wg_ref, wu_ref, wd_ref):
    s = pl.program_id(0)
    n_steps = pl.num_programs(0)

    for j in range(FFN_WSTEPS):
        @pl.when(s == j)
        def _(j=j):
            c0 = j * FFN_WBLOCK
            wg_ref[:, c0:c0 + FFN_WBLOCK] = wg32_ref[...].astype(BF16)
            wu_ref[:, c0:c0 + FFN_WBLOCK] = wu32_ref[...].astype(BF16)
            wd_ref[c0:c0 + FFN_WBLOCK, :] = wd32_ref[...].astype(BF16)

    @pl.when((s >= FFN_WSTEPS) & (s < n_steps - 1))
    def _():
        m = mp_ref[0]
        for r0 in range(0, FFN_TILE, FFN_ROWS):
            op_ref[r0:r0 + FFN_ROWS, :] = _ffn_body(
                xp_ref[r0:r0 + FFN_ROWS, :], m[0:1], m[1:2], m[2:3],
                pre_ref[...], post_ref[...], wg_ref, wu_ref, wd_ref)

    @pl.when(s == n_steps - 1)
    def _():
        os_ref[...] = _ffn_body(xs_ref[...], ms_ref[0], ms_ref[1], ms_ref[2], pre_ref[...],
                                post_ref[...], wg_ref, wu_ref, wd_ref)


def _ffn(xp, xs, mod_p3, mod9, sub_layer, pre_w, post_w, wg, wu, wd, seq_len):
    n_tok = xp.shape[0]
    n_seq = xs.shape[0]
    n_tiles = n_tok // FFN_TILE
    steps_per_seq = seq_len // FFN_TILE
    tile = lambda s: jnp.clip(s - FFN_WSTEPS, 0, n_tiles - 1)
    wblk = lambda s: jnp.minimum(s, FFN_WSTEPS - 1)
    tok_spec = pl.BlockSpec((FFN_TILE, D_MODEL), lambda s: (tile(s), 0))
    return pl.pallas_call(
        _ffn_kernel,
        grid=(FFN_WSTEPS + n_tiles + 1,),
        in_specs=[
            tok_spec,
            pl.BlockSpec((1, 3, D_MODEL), lambda s: (tile(s) // steps_per_seq, 0, 0)),
            _const_spec((n_seq, D_MODEL)),
            pl.BlockSpec((3, n_seq, D_MODEL), lambda s: (sub_layer, 0, 0), pipeline_mode=pl.Buffered(1)),
            _const_spec((1, D_MODEL)),
            _const_spec((1, D_MODEL)),
            pl.BlockSpec((D_MODEL, FFN_WBLOCK), lambda s: (0, wblk(s))),
            pl.BlockSpec((D_MODEL, FFN_WBLOCK), lambda s: (0, wblk(s))),
            pl.BlockSpec((FFN_WBLOCK, D_MODEL), lambda s: (wblk(s), 0)),
        ],
        out_specs=[tok_spec, pl.BlockSpec((n_seq, D_MODEL), lambda s: (0, 0))],
        out_shape=[jax.ShapeDtypeStruct((n_tok, D_MODEL), F32),
                   jax.ShapeDtypeStruct((n_seq, D_MODEL), F32)],
        scratch_shapes=[pltpu.VMEM((D_MODEL, D_FF), BF16), pltpu.VMEM((D_MODEL, D_FF), BF16),
                        pltpu.VMEM((D_FF, D_MODEL), BF16)],
        compiler_params=pltpu.CompilerParams(
            dimension_semantics=("arbitrary",), vmem_limit_bytes=VMEM_LIMIT),
        name="ffn",
    )(xp, mod_p3, xs, mod9, pre_w, post_w, wg, wu, wd)


def _lower_bound(lb_logits):
    z = lb_logits - jnp.max(lb_logits, axis=0, keepdims=True)
    e = jnp.exp(z)
    return e[0:1] / jnp.sum(e, axis=0, keepdims=True)


def _lru_gates(u_conv, wa_ref, ba, wx_ref, bx, lam):
    ub = u_conv.astype(BF16)
    r = _sigmoid(_dot(ub, wa_ref[...]) + ba)
    ig = _sigmoid(_dot(ub, wx_ref[...]) + bx)
    log_a = (-LRU_C * r) * _softplus(-lam)
    a = jnp.exp(log_a)
    th = jnp.tanh(log_a)
    mult = jnp.sqrt((-2.0 * th) / (1.0 - th))
    return a, mult, ig


def _head_rms_gate(o, norm_w, g):
    return _rms(o, norm_w) * _silu(g)


def _hgrn_chunk_head(scores, q_in, k_out, dec, vb_blk, g_blk, hgw, st_ref, hd, ohg_ref, r0, l0):
    st = st_ref[hd]
    o = _dot_nt(q_in, st.astype(BF16)) + _dot(scores.astype(BF16), vb_blk)
    st_ref[hd] = st * dec + _dot_tn(vb_blk, k_out)
    ohg_ref[r0:r0 + HG_CHUNK, l0:l0 + HG_HEAD_DIM] = _head_rms_gate(o, hgw, g_blk)


def _hgrn_chunks_direct(qs, kk, pfx, tails, vb, g, hgw, st_ref, ohg_ref, r0):
    tb = qs.shape[0]
    n_chunk = len(tails)
    inv = 1.0 / pfx
    q_dec = qs * pfx
    k_dec = kk * inv
    q_in = q_dec.astype(BF16)
    k_in = k_dec.astype(BF16)
    ti = lax.broadcasted_iota(jnp.int32, (tb, tb), 0)
    si = lax.broadcasted_iota(jnp.int32, (tb, tb), 1)
    visible = (ti >= si) & ((ti ^ si) < HG_CHUNK)
    row_chunk = jnp.right_shift(lax.broadcasted_iota(jnp.int32, (tb, HG_HEAD_DIM), 0),
                                HG_CHUNK.bit_length() - 1)
    tail_rows = jnp.concatenate([jnp.broadcast_to(t, (HG_CHUNK, t.shape[-1])) for t in tails], axis=0)
    k_end = k_dec * tail_rows
    for hd in range(HG_HEADS):
        l0 = hd * HG_HEAD_DIM
        head = lambda z: z[:, l0:l0 + HG_HEAD_DIM]
        by_chunk = lambda z: jnp.concatenate(
            [jnp.where(row_chunk == c, head(z), 0.0) for c in range(n_chunk)], axis=-1).astype(BF16)
        scores = jnp.where(visible, _dot_nt(head(q_in), head(k_in)), 0.0)
        grown = _dot_tn(head(vb), by_chunk(k_end))
        st = st_ref[hd]
        states = []
        for c, tail in enumerate(tails):
            states.append(st.astype(BF16))
            st = st * head(tail) + grown[:, c * HG_HEAD_DIM:(c + 1) * HG_HEAD_DIM]
        st_ref[hd] = st
        o = (_dot_nt(by_chunk(q_dec), jnp.concatenate(states, axis=-1))
             + _dot(scores.astype(BF16), head(vb)))
        ohg_ref[r0:r0 + tb, l0:l0 + HG_HEAD_DIM] = _head_rms_gate(o, hgw, head(g))


def _hgrn_chunks_levels(f3, kk3, qs3, vb, g, hgw, st_ref, ohg_ref):
    n_grp = f3.shape[0]
    tb = n_grp * V7X_SUBLANES
    sub = lax.broadcasted_iota(jnp.int32, f3.shape, 1)
    grp = lax.broadcasted_iota(jnp.int32, f3.shape, 0)
    to_mxu = lambda z: z.reshape(tb, z.shape[-1]).astype(BF16)
    q_lv = [to_mxu(qs3)]
    k_lv = [to_mxu(kk3)]
    tot, pre, post = f3, f3, None
    for lvl in HG_LEVELS:
        q_lv.append(to_mxu(qs3 * pre))
        k_lv.append(k_lv[0] if post is None else to_mxu(kk3 * post))
        if lvl < V7X_SUBLANES:
            upper = (sub & lvl) != 0
            below = pltpu.roll(tot, lvl, 1)
            above = pltpu.roll(tot, V7X_SUBLANES - lvl, 1)
        else:
            upper = (grp & (lvl // V7X_SUBLANES)) != 0
            below = jnp.roll(tot, lvl // V7X_SUBLANES, axis=0)
            above = jnp.roll(tot, -(lvl // V7X_SUBLANES), axis=0)
        pre = jnp.where(upper, pre * below, pre)
        post = jnp.where(upper, 1.0, above) if post is None else jnp.where(upper, post, post * above)
        tot = tot * jnp.where(upper, below, above)
    q_in = to_mxu(qs3 * pre)
    k_out = to_mxu(kk3 * post)
    s_decay = tot.reshape(tb, tot.shape[-1])

    ti = lax.broadcasted_iota(jnp.int32, (HG_CHUNK, HG_CHUNK), 0)
    si = lax.broadcasted_iota(jnp.int32, (HG_CHUNK, HG_CHUNK), 1)
    masks = [ti == si]
    tx = ti ^ si
    for lvl in HG_LEVELS:
        masks.append((tx >= lvl) & (tx < 2 * lvl) & ((ti & lvl) != 0))

    for c in range(tb // HG_CHUNK):
        r0 = c * HG_CHUNK
        for hd in range(HG_HEADS):
            l0 = hd * HG_HEAD_DIM
            blk = lambda z: z[r0:r0 + HG_CHUNK, l0:l0 + HG_HEAD_DIM]
            scores = jnp.zeros((HG_CHUNK, HG_CHUNK), F32)
            for ql, kl, msk in zip(q_lv, k_lv, masks):
                scores = jnp.where(msk, _dot_nt(blk(ql), blk(kl)), scores)
            dec = s_decay[r0 + HG_CHUNK - 1:r0 + HG_CHUNK, l0:l0 + HG_HEAD_DIM]
            _hgrn_chunk_head(scores, blk(q_in), blk(k_out), dec, blk(vb), blk(g), hgw,
                             st_ref, hd, ohg_ref, r0, l0)

def _mix_tile(x_ref, m_ref, o_ref, t, pre_ref, post_ref, win_ref, cw_ref, cb_ref, wa_ref, ba_ref,
              wx_ref, bx_ref, lam_ref, lbl_ref, hgw_ref, wout_ref, ucar_ref, hcar_ref, st_ref, ohg_ref):
    tb = MIX_TILE
    x = x_ref[...]
    m = m_ref[...]
    per_block = []
    for r0 in range(0, tb, MIX_ROWS):
        hb = (_rms(x[r0:r0 + MIX_ROWS], pre_ref[...]) * (1.0 + m[1:2]) + m[0:1]).astype(BF16)
        per_block.append([_dot(hb, win_ref[:, c0:c0 + LRU_WIDTH])
                          for c0 in range(0, D_PROJ, LRU_WIDTH)])
    u, y_lru, q, f_raw, v, g = [jnp.concatenate(seg, axis=0) for seg in zip(*per_block)]

    n_grp = tb // V7X_SUBLANES
    grouped = lambda z: z.reshape(n_grp, V7X_SUBLANES, z.shape[-1])
    flat = lambda z: z.reshape(tb, z.shape[-1])
    gshape = (n_grp, V7X_SUBLANES, LRU_WIDTH)
    sub = lax.broadcasted_iota(jnp.int32, gshape, 1)
    grp = lax.broadcasted_iota(jnp.int32, gshape, 0)

    u3 = grouped(u)
    u_all = jnp.concatenate([ucar_ref[...][None], u3], axis=0)
    ucar_ref[...] = u3[n_grp - 1]
    cw = cw_ref[...]
    u_conv = cb_ref[...]
    for k in range(CONV_WIDTH - 1):
        back = CONV_WIDTH - 1 - k
        rot = pltpu.roll(u_all, back, 1)
        u_conv = u_conv + jnp.where(sub >= back, rot[1:], rot[:-1]) * cw[k:k + 1]
    u_conv = flat(u_conv + u3 * cw[CONV_WIDTH - 1:CONV_WIDTH])

    a, mult, ig = _lru_gates(u_conv, wa_ref, ba_ref[...], wx_ref, bx_ref[...], lam_ref[...])
    a3, mult3 = grouped(a), grouped(mult)
    first_row = jnp.where(t == 0, 0, -1)
    mult3 = jnp.where(grp * V7X_SUBLANES + sub == first_row, 1.0, mult3)
    b3 = (mult3 * grouped(ig)) * grouped(u_conv)
    s = 1
    while s < V7X_SUBLANES:
        keep = sub >= s
        b3 = jnp.where(keep, a3 * pltpu.roll(b3, s, 1) + b3, b3)
        a3 = jnp.where(keep, a3 * pltpu.roll(a3, s, 1), a3)
        s *= 2
    carry = hcar_ref[...]
    hs = []
    for gi in range(n_grp):
        h_g = a3[gi] * carry + b3[gi]
        carry = h_g[V7X_SUBLANES - 1:V7X_SUBLANES]
        hs.append(h_g)
    hcar_ref[...] = carry
    o_lru = jnp.concatenate(hs, axis=0) * _gelu_tanh(y_lru)

    lb = _lower_bound(lbl_ref[...])
    f3 = grouped(lb + (1.0 - lb) * _sigmoid(f_raw))
    kk3 = 1.0 - f3
    qs3 = grouped(q * (HG_HEAD_DIM ** -0.5))
    vb = v.astype(BF16)
    hgw = hgw_ref[...]

    pfx = f3
    s = 1
    while s < V7X_SUBLANES:
        pfx = jnp.where(sub >= s, pfx * pltpu.roll(pfx, s, 1), pfx)
        s *= 2
    grp_per_chunk = HG_CHUNK // V7X_SUBLANES
    rows, tails = [], []
    for gi in range(n_grp):
        p_g = pfx[gi] if gi % grp_per_chunk == 0 else pfx[gi] * carry
        carry = p_g[V7X_SUBLANES - 1:V7X_SUBLANES]
        rows.append(p_g)
        if gi % grp_per_chunk == grp_per_chunk - 1:
            tails.append(carry)
    pfx = jnp.concatenate(rows, axis=0)
    direct = jnp.min(jnp.concatenate(tails, axis=0)) >= HG_DIRECT_MIN

    @pl.when(direct)
    def _():
        qs, kk = flat(qs3), flat(kk3)
        per_block = HG_DIRECT_ROWS // HG_CHUNK
        for bi, r0 in enumerate(range(0, tb, HG_DIRECT_ROWS)):
            rows = lambda z: z[r0:r0 + HG_DIRECT_ROWS]
            _hgrn_chunks_direct(rows(qs), rows(kk), rows(pfx), tails[bi * per_block:(bi + 1) * per_block],
                                rows(vb), rows(g), hgw, st_ref, ohg_ref, r0)

    @pl.when(jnp.logical_not(direct))
    def _():
        _hgrn_chunks_levels(f3, kk3, qs3, vb, g, hgw, st_ref, ohg_ref)

    mix_in = jnp.concatenate([o_lru, ohg_ref[...]], axis=-1).astype(BF16)
    for r0 in range(0, tb, MIX_ROWS):
        mix = _dot(mix_in[r0:r0 + MIX_ROWS], wout_ref[...])
        o_ref[r0:r0 + MIX_ROWS, :] = x[r0:r0 + MIX_ROWS] + m[2:3] * _rms(mix, post_ref[...])


def _mix_prompt_kernel(x_ref, m_ref, *refs):
    (pre_ref, post_ref, win32_ref, cw_ref, cb_ref, wa_ref, ba_ref, wx_ref, bx_ref, lam_ref, lbl_ref,
     hgw_ref, wout32_ref) = refs[:13]
    (o_ref, h_out_ref, conv_out_ref, s_out_ref, win_ref, wout_ref,
     ucar_ref, hcar_ref, st_ref, ohg_ref) = refs[13:]
    t = pl.program_id(1)
    n_t = pl.num_programs(1)

    @pl.when((pl.program_id(0) == 0) & (t == 0))
    def _():
        win_ref[...] = win32_ref[...].astype(BF16)
        wout_ref[...] = wout32_ref[...].astype(BF16)

    @pl.when(t == 0)
    def _():
        ucar_ref[...] = jnp.zeros_like(ucar_ref)
        hcar_ref[...] = jnp.zeros_like(hcar_ref)
        st_ref[...] = jnp.zeros_like(st_ref)

    _mix_tile(x_ref.at[0], m_ref.at[0], o_ref.at[0], t, pre_ref, post_ref, win_ref, cw_ref, cb_ref,
              wa_ref, ba_ref, wx_ref, bx_ref, lam_ref, lbl_ref, hgw_ref, wout_ref,
              ucar_ref, hcar_ref, st_ref, ohg_ref)

    @pl.when(t == n_t - 1)
    def _():
        h_out_ref[0] = hcar_ref[...]
        conv_out_ref[0] = ucar_ref[V7X_SUBLANES - (CONV_WIDTH - 1):V7X_SUBLANES, :]
        for hd in range(HG_HEADS):
            s_out_ref[0, hd] = st_ref[hd].T


def _mix_prompt(x, mod3, p):
    batch, seq_len, _ = x.shape
    n_t = seq_len // MIX_TILE
    tok_spec = pl.BlockSpec((1, MIX_TILE, D_MODEL), lambda b, t: (b, t, 0))
    return pl.pallas_call(
        _mix_prompt_kernel,
        grid=(batch, n_t),
        in_specs=[
            tok_spec,
            pl.BlockSpec((1, 3, D_MODEL), lambda b, t: (b, 0, 0)),
            _const_spec((1, D_MODEL)),
            _const_spec((1, D_MODEL)),
            _const_spec((D_MODEL, D_PROJ)),
            _const_spec((CONV_WIDTH, LRU_WIDTH)),
            _const_spec((1, LRU_WIDTH)),
            _const_spec((LRU_WIDTH, LRU_WIDTH)),
            _const_spec((1, LRU_WIDTH)),
            _const_spec((LRU_WIDTH, LRU_WIDTH)),
            _const_spec((1, LRU_WIDTH)),
            _const_spec((1, LRU_WIDTH)),
            _const_spec(p["lb_logits"].shape),
            _const_spec((1, HG_HEAD_DIM)),
            _const_spec((D_MODEL, D_MODEL)),
        ],
        out_specs=[
            tok_spec,
            pl.BlockSpec((1, 1, LRU_WIDTH), lambda b, t: (b, 0, 0)),
            pl.BlockSpec((1, CONV_WIDTH - 1, LRU_WIDTH), lambda b, t: (b, 0, 0)),
            pl.BlockSpec((1, HG_HEADS, HG_HEAD_DIM, HG_HEAD_DIM), lambda b, t: (b, 0, 0, 0)),
            _const_spec((D_MODEL, D_PROJ)),
            _const_spec((D_MODEL, D_MODEL)),
        ],
        out_shape=[
            jax.ShapeDtypeStruct((batch, seq_len, D_MODEL), F32),
            jax.ShapeDtypeStruct((batch, 1, LRU_WIDTH), F32),
            jax.ShapeDtypeStruct((batch, CONV_WIDTH - 1, LRU_WIDTH), F32),
            jax.ShapeDtypeStruct((batch, HG_HEADS, HG_HEAD_DIM, HG_HEAD_DIM), F32),
            jax.ShapeDtypeStruct((D_MODEL, D_PROJ), BF16),
            jax.ShapeDtypeStruct((D_MODEL, D_MODEL), BF16),
        ],
        scratch_shapes=[
            pltpu.VMEM((V7X_SUBLANES, LRU_WIDTH), F32),
            pltpu.VMEM((1, LRU_WIDTH), F32),
            pltpu.VMEM((HG_HEADS, HG_HEAD_DIM, HG_HEAD_DIM), F32),
            pltpu.VMEM((MIX_TILE, HG_WIDTH), F32),
        ],
        compiler_params=pltpu.CompilerParams(
            dimension_semantics=("arbitrary", "arbitrary"), vmem_limit_bytes=VMEM_LIMIT),
        name="mix_prompt",
    )(x, mod3, p["ln_mix_pre"], p["ln_mix_post"], p["w_in"], p["conv_w"], p["conv_b"],
      p["wa_bd"], p["b_a"], p["wx_bd"], p["b_x"], p["lam"], p["lb_logits"], p["hg_norm_w"],
      p["w_out"])


def _mix_sample_kernel(x_ref, m_ref, pre_ref, post_ref, win_ref, cw_ref, cb_ref,
                       wa_ref, ba_ref, wx_ref, bx_ref, lam_ref, lbl_ref, hgw_ref, wout_ref,
                       h0_ref, conv0_ref, s0_ref,
                       o_ref, h_out_ref, conv_out_ref, s_out_ref,
                       qf_ref, f_ref, k_ref, v_ref, g_ref, qk_ref, olru_ref, ohg_ref, ostage_ref):
    i = pl.program_id(0)
    n_i = pl.num_programs(0)
    nb = SAMPLE_SEQ_BLOCK

    @pl.when(i == 0)
    def _():
        x = x_ref[...]
        h = _rms(x, pre_ref[...]) * (1.0 + m_ref[1]) + m_ref[0]
        proj = _dot(h.astype(BF16), win_ref[...])
        u = proj[:, 0:LRU_WIDTH]
        y_lru = proj[:, LRU_WIDTH:2 * LRU_WIDTH]
        o0 = 2 * LRU_WIDTH
        q = proj[:, o0:o0 + HG_WIDTH]
        f_raw = proj[:, o0 + HG_WIDTH:o0 + 2 * HG_WIDTH]
        v = proj[:, o0 + 2 * HG_WIDTH:o0 + 3 * HG_WIDTH]
        g_ref[...] = proj[:, o0 + 3 * HG_WIDTH:o0 + 4 * HG_WIDTH]

        cw = cw_ref[...]
        u_conv = cb_ref[...]
        for k in range(CONV_WIDTH - 1):
            u_conv = u_conv + conv0_ref[k] * cw[k:k + 1]
            if k > 0:
                conv_out_ref[k - 1] = conv0_ref[k]
        u_conv = u_conv + u * cw[CONV_WIDTH - 1:CONV_WIDTH]
        conv_out_ref[CONV_WIDTH - 2] = u

        a, mult, ig = _lru_gates(u_conv, wa_ref, ba_ref[...], wx_ref, bx_ref[...], lam_ref[...])
        hs = (mult * ig) * u_conv + a * h0_ref[...]
        h_out_ref[...] = hs
        olru_ref[...] = hs * _gelu_tanh(y_lru)

        lb = _lower_bound(lbl_ref[...])
        f = lb + (1.0 - lb) * _sigmoid(f_raw)
        kk = 1.0 - f
        qs = q * (HG_HEAD_DIM ** -0.5)
        f_ref[...] = f
        qf_ref[...] = qs * f
        k_ref[...] = kk
        v_ref[...] = v
        qk = qs * kk
        for hd in range(HG_HEADS):
            l0 = hd * HG_HEAD_DIM
            tot = jnp.sum(qk[:, l0:l0 + HG_HEAD_DIM], axis=-1, keepdims=True)
            qk_ref[:, l0:l0 + HG_HEAD_DIM] = jnp.broadcast_to(tot, (qk.shape[0], HG_HEAD_DIM))

    base = pl.multiple_of(i * nb, nb)
    grp = lambda ref: ref[pl.ds(base, nb), :]
    f_g, k_g, v_g, qf_g, qk_g = grp(f_ref), grp(k_ref), grp(v_ref), grp(qf_ref), grp(qk_ref)
    square = (HG_HEAD_DIM, HG_HEAD_DIM)
    for j in range(nb):
        for hd in range(HG_HEADS):
            l0 = hd * HG_HEAD_DIM
            rowv = lambda z: z[j:j + 1, l0:l0 + HG_HEAD_DIM]
            s_old = s0_ref[j, hd]
            f_col = jnp.broadcast_to(rowv(f_g), square).T
            k_col = jnp.broadcast_to(rowv(k_g), square).T
            v_row = rowv(v_g)
            outer = k_col.astype(BF16).astype(F32) * v_row.astype(BF16).astype(F32)
            s_out_ref[j, hd] = f_col * s_old + outer
            qf8 = jnp.broadcast_to(rowv(qf_g), (V7X_SUBLANES, HG_HEAD_DIM)).astype(BF16)
            o1 = _dot(qf8, s_old.astype(BF16))[0:1]
            ostage_ref[j:j + 1, l0:l0 + HG_HEAD_DIM] = o1 + rowv(qk_g) * v_row
    ohg_ref[pl.ds(base, nb), :] = ostage_ref[...]

    @pl.when(i == n_i - 1)
    def _():
        hgw = hgw_ref[...]
        parts = [olru_ref[...]]
        for hd in range(HG_HEADS):
            l0 = hd * HG_HEAD_DIM
            parts.append(_head_rms_gate(ohg_ref[:, l0:l0 + HG_HEAD_DIM], hgw,
                                        g_ref[:, l0:l0 + HG_HEAD_DIM]))
        mix_in = jnp.concatenate(parts, axis=-1).astype(BF16)
        mix = _dot(mix_in, wout_ref[...])
        o_ref[...] = x_ref[...] + m_ref[2] * _rms(mix, post_ref[...])


def _mix_sample(x, mod9, p, h0, conv0, s0):
    n_seq = x.shape[0]
    nb = SAMPLE_SEQ_BLOCK
    state_spec = pl.BlockSpec((nb, HG_HEADS, HG_HEAD_DIM, HG_HEAD_DIM), lambda i: (i, 0, 0, 0))
    full2 = lambda shape: pl.BlockSpec(shape, lambda i: (0,) * len(shape))
    return pl.pallas_call(
        _mix_sample_kernel,
        grid=(n_seq // nb,),
        in_specs=[
            _const_spec((n_seq, D_MODEL)),
            pl.BlockSpec((3, n_seq, D_MODEL), lambda i: (1, 0, 0), pipeline_mode=pl.Buffered(1)),
            _const_spec((1, D_MODEL)),
            _const_spec((1, D_MODEL)),
            _const_spec((D_MODEL, D_PROJ)),
            _const_spec((CONV_WIDTH, LRU_WIDTH)),
            _const_spec((1, LRU_WIDTH)),
            _const_spec((LRU_WIDTH, LRU_WIDTH)),
            _const_spec((1, LRU_WIDTH)),
            _const_spec((LRU_WIDTH, LRU_WIDTH)),
            _const_spec((1, LRU_WIDTH)),
            _const_spec((1, LRU_WIDTH)),
            _const_spec(p["lb_logits"].shape),
            _const_spec((1, HG_HEAD_DIM)),
            _const_spec((D_MODEL, D_MODEL)),
            _const_spec((n_seq, LRU_WIDTH)),
            _const_spec((CONV_WIDTH - 1, n_seq, LRU_WIDTH)),
            state_spec,
        ],
        out_specs=[
            full2((n_seq, D_MODEL)),
            full2((n_seq, LRU_WIDTH)),
            full2((CONV_WIDTH - 1, n_seq, LRU_WIDTH)),
            state_spec,
        ],
        out_shape=[
            jax.ShapeDtypeStruct((n_seq, D_MODEL), F32),
            jax.ShapeDtypeStruct((n_seq, LRU_WIDTH), F32),
            jax.ShapeDtypeStruct((CONV_WIDTH - 1, n_seq, LRU_WIDTH), F32),
            jax.ShapeDtypeStruct((n_seq, HG_HEADS, HG_HEAD_DIM, HG_HEAD_DIM), F32),
        ],
        scratch_shapes=[pltpu.VMEM((n_seq, HG_WIDTH), F32) for _ in range(8)]
        + [pltpu.VMEM((nb, HG_WIDTH), F32)],
        compiler_params=pltpu.CompilerParams(
            dimension_semantics=("arbitrary",), vmem_limit_bytes=VMEM_LIMIT),
        name="mix_sample",
    )(x, mod9, p["ln_mix_pre"], p["ln_mix_post"], p["w_in"], p["conv_w"], p["conv_b"],
      p["wa_bd"], p["b_a"], p["wx_bd"], p["b_x"], p["lam"], p["lb_logits"], p["hg_norm_w"],
      p["w_out"], h0, conv0, s0)


def _block_diag(w):
    heads, blk, _ = w.shape
    eye = jnp.eye(heads, dtype=w.dtype)
    return (eye[:, None, :, None] * w[:, :, None, :]).reshape(heads * blk, heads * blk)


def kernel(x_prompt, x_sample, c_prompt, c_sample, state_lru_h, state_lru_conv, state_hgrn_S, w_ada, b_ada, ln_ffn1_pre, ln_ffn1_post, ffn1_w_gate, ffn1_w_up, ffn1_w_down, ln_mix_pre, ln_mix_post, w_in, lru_conv_w, lru_conv_b, lru_w_a, lru_b_a, lru_w_x, lru_b_x, lru_lambda, hg_lb_logits, hg_norm_w, w_out, ln_ffn2_pre, ln_ffn2_post, ffn2_w_gate, ffn2_w_up, ffn2_w_down):
    depth = w_ada.shape[0]
    batch, seq_len, _ = x_prompt.shape
    n_seq = x_sample.shape[0]
    assert depth == 1 and x_sample.shape[1] == 1
    assert seq_len % FFN_TILE == 0 and seq_len % MIX_TILE == 0 and n_seq % SAMPLE_SEQ_BLOCK == 0
    assert D_PROJ == 6 * LRU_WIDTH and LRU_WIDTH == HG_WIDTH

    xp = x_prompt.reshape(batch * seq_len, D_MODEL)
    xs = x_sample.reshape(n_seq, D_MODEL)
    ph, pc, pS, sh, sc, sS = [], [], [], [], [], []
    for l in range(depth):
        row = lambda w: w[l].reshape(1, -1)
        mod9 = _ada(c_sample, c_prompt, w_ada[l], b_ada[l])
        mod_p = jnp.transpose(mod9[:, n_seq:], (1, 0, 2))
        p = {
            "ln_mix_pre": row(ln_mix_pre), "ln_mix_post": row(ln_mix_post),
            "w_in": w_in[l], "w_out": w_out[l],
            "conv_w": lru_conv_w[l], "conv_b": row(lru_conv_b),
            "wa_bd": _block_diag(lru_w_a[l]).astype(BF16), "b_a": row(lru_b_a),
            "wx_bd": _block_diag(lru_w_x[l]).astype(BF16), "b_x": row(lru_b_x),
            "lam": row(lru_lambda), "lb_logits": hg_lb_logits,
            "hg_norm_w": row(hg_norm_w),
        }
        f1 = (row(ln_ffn1_pre), row(ln_ffn1_post), ffn1_w_gate[l], ffn1_w_up[l], ffn1_w_down[l])
        f2 = (row(ln_ffn2_pre), row(ln_ffn2_post), ffn2_w_gate[l], ffn2_w_up[l], ffn2_w_down[l])

        xp, xs = _ffn(xp, xs, mod_p[:, 0:3], mod9, 0, *f1, seq_len)
        xp, h_p, c_p, S_p, w_in_bf, w_out_bf = _mix_prompt(
            xp.reshape(batch, seq_len, D_MODEL), mod_p[:, 3:6], p)
        xp = xp.reshape(batch * seq_len, D_MODEL)
        conv0 = jnp.transpose(state_lru_conv[l], (1, 0, 2))
        xs, h_s, c_s, S_s = _mix_sample(xs, mod9, dict(p, w_in=w_in_bf, w_out=w_out_bf),
                                        state_lru_h[l], conv0, state_hgrn_S[l])
        xp, xs = _ffn(xp, xs, mod_p[:, 6:9], mod9, 2, *f2, seq_len)

        ph.append(h_p.reshape(batch, LRU_WIDTH)); pc.append(c_p); pS.append(S_p)
        sh.append(h_s); sc.append(jnp.transpose(c_s, (1, 0, 2))); sS.append(S_s)

    return (xp.reshape(batch, seq_len, D_MODEL), xs.reshape(n_seq, 1, D_MODEL),
            jnp.stack(ph), jnp.stack(pc), jnp.stack(pS), jnp.stack(sh), jnp.stack(sc), jnp.stack(sS))
```

```python
import jax
import jax.numpy as jnp
from jax import lax
from jax.experimental import pallas as pl
from jax.experimental.pallas import tpu as pltpu

F32 = jnp.float32
BF16 = jnp.bfloat16

D_MODEL = 1024
D_FF = 2816
LRU_WIDTH = 512
CONV_WIDTH = 4
LRU_C = 8.0
HG_WIDTH = 512
HG_HEAD_DIM = 128
HG_HEADS = HG_WIDTH // HG_HEAD_DIM
HG_CHUNK = 64
N_MOD = 9
D_PROJ = 2 * LRU_WIDTH + 4 * HG_WIDTH
EPS = 1e-6

V7X_SUBLANES = 8
V7X_VMEM_BYTES = 64 * 1024 * 1024
VMEM_LIMIT = V7X_VMEM_BYTES - 8 * 1024 * 1024

FFN_TILE = 1024
FFN_ROWS = 512
MIX_TILE = 512
MIX_ROWS = 256
HG_DIRECT_ROWS = 256
FFN_WBLOCK = 256
FFN_WSTEPS = D_FF // FFN_WBLOCK
SAMPLE_SEQ_BLOCK = 2 * V7X_SUBLANES

HG_LEVELS = (1, 2, 4, 8, 16, 32)
HG_DIRECT_MIN = 2.0 ** -120


def _rms(x, w):
    return (x * lax.rsqrt(jnp.mean(x * x, axis=-1, keepdims=True) + EPS)) * w


def _sigmoid(x):
    return 1.0 / (1.0 + jnp.exp(-x))


def _silu(x):
    return x * _sigmoid(x)


def _gelu_tanh(x):
    c = 0.7978845608028654
    return x * (0.5 * (1.0 + jnp.tanh(c * (x + 0.044715 * (x * x * x)))))


def _softplus(z):
    return jnp.maximum(z, 0.0) + jnp.log1p(jnp.exp(-jnp.abs(z)))


def _dot(a, b):
    return jnp.dot(a, b, preferred_element_type=F32)


def _dot_nt(a, b):
    return lax.dot_general(a, b, (((1,), (1,)), ((), ())), preferred_element_type=F32)


def _dot_tn(a, b):
    return lax.dot_general(a, b, (((0,), (0,)), ((), ())), preferred_element_type=F32)


def _const_spec(shape):
    nd = len(shape)
    return pl.BlockSpec(shape, lambda *_: (0,) * nd, pipeline_mode=pl.Buffered(1))


def _ada_kernel(cs_ref, cp_ref, w_ref, b_ref, o_ref):
    s = _silu(jnp.concatenate([cs_ref[...], cp_ref[...]], axis=0))
    o_ref[0] = _dot(s.astype(BF16), w_ref[...].astype(BF16)) + b_ref[...]


def _ada(c_sample, c_prompt, w_ada, b_ada):
    rows = c_sample.shape[0] + c_prompt.shape[0]
    return pl.pallas_call(
        _ada_kernel,
        grid=(N_MOD,),
        in_specs=[
            _const_spec(c_sample.shape),
            _const_spec(c_prompt.shape),
            pl.BlockSpec((D_MODEL, D_MODEL), lambda j: (0, j)),
            pl.BlockSpec((1, D_MODEL), lambda j: (0, j)),
        ],
        out_specs=pl.BlockSpec((1, rows, D_MODEL), lambda j: (j, 0, 0)),
        out_shape=jax.ShapeDtypeStruct((N_MOD, rows, D_MODEL), F32),
        compiler_params=pltpu.CompilerParams(
            dimension_semantics=("arbitrary",), vmem_limit_bytes=VMEM_LIMIT),
        name="ada_mod",
    )(c_sample, c_prompt, w_ada, b_ada.reshape(1, N_MOD * D_MODEL))


def _ffn_body(x, shift, scale, gate, pre_w, post_w, wg_ref, wu_ref, wd_ref):
    h = _rms(x, pre_w) * (1.0 + scale) + shift
    hb = h.astype(BF16)
    a = _dot(hb, wg_ref[...])
    u = _dot(hb, wu_ref[...])
    act = (_silu(a) * u).astype(BF16)
    y = _dot(act, wd_ref[...])
    return x + (0.5 * gate) * _rms(y, post_w)


def _ffn_kernel(xp_ref, mp_ref, xs_ref, ms_ref, pre_ref, post_ref, wg32_ref, wu32_ref, wd32_ref,
                op_ref, os_ref, wg_ref, wu_ref, wd_ref):
    s = pl.program_id(0)
    n_steps = pl.num_programs(0)

    for j in range(FFN_WSTEPS):
        @pl.when(s == j)
        def _(j=j):
            c0 = j * FFN_WBLOCK
            wg_ref[:, c0:c0 + FFN_WBLOCK] = wg32_ref[...].astype(BF16)
            wu_ref[:, c0:c0 + FFN_WBLOCK] = wu32_ref[...].astype(BF16)
            wd_ref[c0:c0 + FFN_WBLOCK, :] = wd32_ref[...].astype(BF16)

    @pl.when((s >= FFN_WSTEPS) & (s < n_steps - 1))
    def _():
        m = mp_ref[0]
        for r0 in range(0, FFN_TILE, FFN_ROWS):
            op_ref[r0:r0 + FFN_ROWS, :] = _ffn_body(
                xp_ref[r0:r0 + FFN_ROWS, :], m[0:1], m[1:2], m[2:3],
                pre_ref[...], post_ref[...], wg_ref, wu_ref, wd_ref)

    @pl.when(s == n_steps - 1)
    def _():
        os_ref[...] = _ffn_body(xs_ref[...], ms_ref[0], ms_ref[1], ms_ref[2], pre_ref[...],
                                post_ref[...], wg_ref, wu_ref, wd_ref)


def _ffn(xp, xs, mod_p3, mod9, sub_layer, pre_w, post_w, wg, wu, wd, seq_len):
    n_tok = xp.shape[0]
    n_seq = xs.shape[0]
    n_tiles = n_tok // FFN_TILE
    steps_per_seq = seq_len // FFN_TILE
    tile = lambda s: jnp.clip(s - FFN_WSTEPS, 0, n_tiles - 1)
    wblk = lambda s: jnp.minimum(s, FFN_WSTEPS - 1)
    tok_spec = pl.BlockSpec((FFN_TILE, D_MODEL), lambda s: (tile(s), 0))
    return pl.pallas_call(
        _ffn_kernel,
        grid=(FFN_WSTEPS + n_tiles + 1,),
        in_specs=[
            tok_spec,
            pl.BlockSpec((1, 3, D_MODEL), lambda s: (tile(s) // steps_per_seq, 0, 0)),
            _const_spec((n_seq, D_MODEL)),
            pl.BlockSpec((3, n_seq, D_MODEL), lambda s: (sub_layer, 0, 0), pipeline_mode=pl.Buffered(1)),
            _const_spec((1, D_MODEL)),
            _const_spec((1, D_MODEL)),
            pl.BlockSpec((D_MODEL, FFN_WBLOCK), lambda s: (0, wblk(s))),
            pl.BlockSpec((D_MODEL, FFN_WBLOCK), lambda s: (0, wblk(s))),
            pl.BlockSpec((FFN_WBLOCK, D_MODEL), lambda s: (wblk(s), 0)),
        ],
        out_specs=[tok_spec, pl.BlockSpec((n_seq, D_MODEL), lambda s: (0, 0))],
        out_shape=[jax.ShapeDtypeStruct((n_tok, D_MODEL), F32),
                   jax.ShapeDtypeStruct((n_seq, D_MODEL), F32)],
        scratch_shapes=[pltpu.VMEM((D_MODEL, D_FF), BF16), pltpu.VMEM((D_MODEL, D_FF), BF16),
                        pltpu.VMEM((D_FF, D_MODEL), BF16)],
        compiler_params=pltpu.CompilerParams(
            dimension_semantics=("arbitrary",), vmem_limit_bytes=VMEM_LIMIT),
        name="ffn",
    )(xp, mod_p3, xs, mod9, pre_w, post_w, wg, wu, wd)


def _lower_bound(lb_logits):
    z = lb_logits - jnp.max(lb_logits, axis=0, keepdims=True)
    e = jnp.exp(z)
    return e[0:1] / jnp.sum(e, axis=0, keepdims=True)


def _lru_gates(u_conv, wa_ref, ba, wx_ref, bx, lam):
    ub = u_conv.astype(BF16)
    r = _sigmoid(_dot(ub, wa_ref[...]) + ba)
    ig = _sigmoid(_dot(ub, wx_ref[...]) + bx)
    log_a = (-LRU_C * r) * _softplus(-lam)
    a = jnp.exp(log_a)
    th = jnp.tanh(log_a)
    mult = jnp.sqrt((-2.0 * th) / (1.0 - th))
    return a, mult, ig


def _head_rms_gate(o, norm_w, g):
    return _rms(o, norm_w) * _silu(g)


def _hgrn_chunk_head(scores, q_in, k_out, dec, vb_blk, g_blk, hgw, st_ref, hd, ohg_ref, r0, l0):
    st = st_ref[hd]
    o = _dot_nt(q_in, st.astype(BF16)) + _dot(scores.astype(BF16), vb_blk)
    st_ref[hd] = st * dec + _dot_tn(vb_blk, k_out)
    ohg_ref[r0:r0 + HG_CHUNK, l0:l0 + HG_HEAD_DIM] = _head_rms_gate(o, hgw, g_blk)


def _hgrn_chunks_direct(qs, kk, pfx, tails, vb, g, hgw, st_ref, ohg_ref, r0):
    tb = qs.shape[0]
    n_chunk = len(tails)
    inv = 1.0 / pfx
    q_dec = qs * pfx
    k_dec = kk * inv
    q_in = q_dec.astype(BF16)
    k_in = k_dec.astype(BF16)
    ti = lax.broadcasted_iota(jnp.int32, (tb, tb), 0)
    si = lax.broadcasted_iota(jnp.int32, (tb, tb), 1)
    visible = (ti >= si) & ((ti ^ si) < HG_CHUNK)
    row_chunk = jnp.right_shift(lax.broadcasted_iota(jnp.int32, (tb, HG_HEAD_DIM), 0),
                                HG_CHUNK.bit_length() - 1)
    tail_rows = jnp.concatenate([jnp.broadcast_to(t, (HG_CHUNK, t.shape[-1])) for t in tails], axis=0)
    k_end = k_dec * tail_rows
    for hd in range(HG_HEADS):
        l0 = hd * HG_HEAD_DIM
        head = lambda z: z[:, l0:l0 + HG_HEAD_DIM]
        by_chunk = lambda z: jnp.concatenate(
            [jnp.where(row_chunk == c, head(z), 0.0) for c in range(n_chunk)], axis=-1).astype(BF16)
        scores = jnp.where(visible, _dot_nt(head(q_in), head(k_in)), 0.0)
        grown = _dot_tn(head(vb), by_chunk(k_end))
        st = st_ref[hd]
        states = []
        for c, tail in enumerate(tails):
            states.append(st.astype(BF16))
            st = st * head(tail) + grown[:, c * HG_HEAD_DIM:(c + 1) * HG_HEAD_DIM]
        st_ref[hd] = st
        o = (_dot_nt(by_chunk(q_dec), jnp.concatenate(states, axis=-1))
             + _dot(scores.astype(BF16), head(vb)))
        ohg_ref[r0:r0 + tb, l0:l0 + HG_HEAD_DIM] = _head_rms_gate(o, hgw, head(g))


def _hgrn_chunks_levels(f3, kk3, qs3, vb, g, hgw, st_ref, ohg_ref):
    n_grp = f3.shape[0]
    tb = n_grp * V7X_SUBLANES
    sub = lax.broadcasted_iota(jnp.int32, f3.shape, 1)
    grp = lax.broadcasted_iota(jnp.int32, f3.shape, 0)
    to_mxu = lambda z: z.reshape(tb, z.shape[-1]).astype(BF16)
    q_lv = [to_mxu(qs3)]
    k_lv = [to_mxu(kk3)]
    tot, pre, post = f3, f3, None
    for lvl in HG_LEVELS:
        q_lv.append(to_mxu(qs3 * pre))
        k_lv.append(k_lv[0] if post is None else to_mxu(kk3 * post))
        if lvl < V7X_SUBLANES:
            upper = (sub & lvl) != 0
            below = pltpu.roll(tot, lvl, 1)
            above = pltpu.roll(tot, V7X_SUBLANES - lvl, 1)
        else:
            upper = (grp & (lvl // V7X_SUBLANES)) != 0
            below = jnp.roll(tot, lvl // V7X_SUBLANES, axis=0)
            above = jnp.roll(tot, -(lvl // V7X_SUBLANES), axis=0)
        pre = jnp.where(upper, pre * below, pre)
        post = jnp.where(upper, 1.0, above) if post is None else jnp.where(upper, post, post * above)
        tot = tot * jnp.where(upper, below, above)
    q_in = to_mxu(qs3 * pre)
    k_out = to_mxu(kk3 * post)
    s_decay = tot.reshape(tb, tot.shape[-1])

    ti = lax.broadcasted_iota(jnp.int32, (HG_CHUNK, HG_CHUNK), 0)
    si = lax.broadcasted_iota(jnp.int32, (HG_CHUNK, HG_CHUNK), 1)
    masks = [ti == si]
    tx = ti ^ si
    for lvl in HG_LEVELS:
        masks.append((tx >= lvl) & (tx < 2 * lvl) & ((ti & lvl) != 0))

    for c in range(tb // HG_CHUNK):
        r0 = c * HG_CHUNK
        for hd in range(HG_HEADS):
            l0 = hd * HG_HEAD_DIM
            blk = lambda z: z[r0:r0 + HG_CHUNK, l0:l0 + HG_HEAD_DIM]
            scores = jnp.zeros((HG_CHUNK, HG_CHUNK), F32)
            for ql, kl, msk in zip(q_lv, k_lv, masks):
                scores = jnp.where(msk, _dot_nt(blk(ql), blk(kl)), scores)
            dec = s_decay[r0 + HG_CHUNK - 1:r0 + HG_CHUNK, l0:l0 + HG_HEAD_DIM]
            _hgrn_chunk_head(scores, blk(q_in), blk(k_out), dec, blk(vb), blk(g), hgw,
                             st_ref, hd, ohg_ref, r0, l0)

def _mix_tile(x_ref, m_ref, o_ref, t, pre_ref, post_ref, win_ref, cw_ref, cb_ref, wa_ref, ba_ref,
              wx_ref, bx_ref, lam_ref, lbl_ref, hgw_ref, wout_ref, ucar_ref, hcar_ref, st_ref, ohg_ref):
    tb = MIX_TILE
    x = x_ref[...]
    m = m_ref[...]
    per_block = []
    for r0 in range(0, tb, MIX_ROWS):
        hb = (_rms(x[r0:r0 + MIX_ROWS], pre_ref[...]) * (1.0 + m[1:2]) + m[0:1]).astype(BF16)
        per_block.append([_dot(hb, win_ref[:, c0:c0 + LRU_WIDTH])
                          for c0 in range(0, D_PROJ, LRU_WIDTH)])
    u, y_lru, q, f_raw, v, g = [jnp.concatenate(seg, axis=0) for seg in zip(*per_block)]

    n_grp = tb // V7X_SUBLANES
    grouped = lambda z: z.reshape(n_grp, V7X_SUBLANES, z.shape[-1])
    flat = lambda z: z.reshape(tb, z.shape[-1])
    gshape = (n_grp, V7X_SUBLANES, LRU_WIDTH)
    sub = lax.broadcasted_iota(jnp.int32, gshape, 1)
    grp = lax.broadcasted_iota(jnp.int32, gshape, 0)

    u3 = grouped(u)
    u_all = jnp.concatenate([ucar_ref[...][None], u3], axis=0)
    ucar_ref[...] = u3[n_grp - 1]
    cw = cw_ref[...]
    u_conv = cb_ref[...]
    for k in range(CONV_WIDTH - 1):
        back = CONV_WIDTH - 1 - k
        rot = pltpu.roll(u_all, back, 1)
        u_conv = u_conv + jnp.where(sub >= back, rot[1:], rot[:-1]) * cw[k:k + 1]
    u_conv = flat(u_conv + u3 * cw[CONV_WIDTH - 1:CONV_WIDTH])

    a, mult, ig = _lru_gates(u_conv, wa_ref, ba_ref[...], wx_ref, bx_ref[...], lam_ref[...])
    a3, mult3 = grouped(a), grouped(mult)
    first_row = jnp.where(t == 0, 0, -1)
    mult3 = jnp.where(grp * V7X_SUBLANES + sub == first_row, 1.0, mult3)
    b3 = (mult3 * grouped(ig)) * grouped(u_conv)
    s = 1
    while s < V7X_SUBLANES:
        keep = sub >= s
        b3 = jnp.where(keep, a3 * pltpu.roll(b3, s, 1) + b3, b3)
        a3 = jnp.where(keep, a3 * pltpu.roll(a3, s, 1), a3)
        s *= 2
    carry = hcar_ref[...]
    hs = []
    for gi in range(n_grp):
        h_g = a3[gi] * carry + b3[gi]
        carry = h_g[V7X_SUBLANES - 1:V7X_SUBLANES]
        hs.append(h_g)
    hcar_ref[...] = carry
    o_lru = jnp.concatenate(hs, axis=0) * _gelu_tanh(y_lru)

    lb = _lower_bound(lbl_ref[...])
    f3 = grouped(lb + (1.0 - lb) * _sigmoid(f_raw))
    kk3 = 1.0 - f3
    qs3 = grouped(q * (HG_HEAD_DIM ** -0.5))
    vb = v.astype(BF16)
    hgw = hgw_ref[...]

    pfx = f3
    s = 1
    while s < V7X_SUBLANES:
        pfx = jnp.where(sub >= s, pfx * pltpu.roll(pfx, s, 1), pfx)
        s *= 2
    grp_per_chunk = HG_CHUNK // V7X_SUBLANES
    rows, tails = [], []
    for gi in range(n_grp):
        p_g = pfx[gi] if gi % grp_per_chunk == 0 else pfx[gi] * carry
        carry = p_g[V7X_SUBLANES - 1:V7X_SUBLANES]
        rows.append(p_g)
        if gi % grp_per_chunk == grp_per_chunk - 1:
            tails.append(carry)
    pfx = jnp.concatenate(rows, axis=0)
    direct = jnp.min(jnp.concatenate(tails, axis=0)) >= HG_DIRECT_MIN

    @pl.when(direct)
    def _():
        qs, kk = flat(qs3), flat(kk3)
        per_block = HG_DIRECT_ROWS // HG_CHUNK
        for bi, r0 in enumerate(range(0, tb, HG_DIRECT_ROWS)):
            rows = lambda z: z[r0:r0 + HG_DIRECT_ROWS]
            _hgrn_chunks_direct(rows(qs), rows(kk), rows(pfx), tails[bi * per_block:(bi + 1) * per_block],
                                rows(vb), rows(g), hgw, st_ref, ohg_ref, r0)

    @pl.when(jnp.logical_not(direct))
    def _():
        _hgrn_chunks_levels(f3, kk3, qs3, vb, g, hgw, st_ref, ohg_ref)

    mix_in = jnp.concatenate([o_lru, ohg_ref[...]], axis=-1).astype(BF16)
    for r0 in range(0, tb, MIX_ROWS):
        mix = _dot(mix_in[r0:r0 + MIX_ROWS], wout_ref[...])
        o_ref[r0:r0 + MIX_ROWS, :] = x[r0:r0 + MIX_ROWS] + m[2:3] * _rms(mix, post_ref[...])


def _mix_prompt_kernel(x_ref, m_ref, *refs):
    (pre_ref, post_ref, win32_ref, cw_ref, cb_ref, wa_ref, ba_ref, wx_ref, bx_ref, lam_ref, lbl_ref,
     hgw_ref, wout32_ref) = refs[:13]
    (o_ref, h_out_ref, conv_out_ref, s_out_ref, win_ref, wout_ref,
     ucar_ref, hcar_ref, st_ref, ohg_ref) = refs[13:]
    t = pl.program_id(1)
    n_t = pl.num_programs(1)

    @pl.when((pl.program_id(0) == 0) & (t == 0))
    def _():
        win_ref[...] = win32_ref[...].astype(BF16)
        wout_ref[...] = wout32_ref[...].astype(BF16)

    @pl.when(t == 0)
    def _():
        ucar_ref[...] = jnp.zeros_like(ucar_ref)
        hcar_ref[...] = jnp.zeros_like(hcar_ref)
        st_ref[...] = jnp.zeros_like(st_ref)

    _mix_tile(x_ref.at[0], m_ref.at[0], o_ref.at[0], t, pre_ref, post_ref, win_ref, cw_ref, cb_ref,
              wa_ref, ba_ref, wx_ref, bx_ref, lam_ref, lbl_ref, hgw_ref, wout_ref,
              ucar_ref, hcar_ref, st_ref, ohg_ref)

    @pl.when(t == n_t - 1)
    def _():
        h_out_ref[0] = hcar_ref[...]
        conv_out_ref[0] = ucar_ref[V7X_SUBLANES - (CONV_WIDTH - 1):V7X_SUBLANES, :]
        for hd in range(HG_HEADS):
            s_out_ref[0, hd] = st_ref[hd].T


def _mix_prompt(x, mod3, p):
    batch, seq_len, _ = x.shape
    n_t = seq_len // MIX_TILE
    tok_spec = pl.BlockSpec((1, MIX_TILE, D_MODEL), lambda b, t: (b, t, 0))
    return pl.pallas_call(
        _mix_prompt_kernel,
        grid=(batch, n_t),
        in_specs=[
            tok_spec,
            pl.BlockSpec((1, 3, D_MODEL), lambda b, t: (b, 0, 0)),
            _const_spec((1, D_MODEL)),
            _const_spec((1, D_MODEL)),
            _const_spec((D_MODEL, D_PROJ)),
            _const_spec((CONV_WIDTH, LRU_WIDTH)),
            _const_spec((1, LRU_WIDTH)),
            _const_spec((LRU_WIDTH, LRU_WIDTH)),
            _const_spec((1, LRU_WIDTH)),
            _const_spec((LRU_WIDTH, LRU_WIDTH)),
            _const_spec((1, LRU_WIDTH)),
            _const_spec((1, LRU_WIDTH)),
            _const_spec(p["lb_logits"].shape),
            _const_spec((1, HG_HEAD_DIM)),
            _const_spec((D_MODEL, D_MODEL)),
        ],
        out_specs=[
            tok_spec,
            pl.BlockSpec((1, 1, LRU_WIDTH), lambda b, t: (b, 0, 0)),
            pl.BlockSpec((1, CONV_WIDTH - 1, LRU_WIDTH), lambda b, t: (b, 0, 0)),
            pl.BlockSpec((1, HG_HEADS, HG_HEAD_DIM, HG_HEAD_DIM), lambda b, t: (b, 0, 0, 0)),
            _const_spec((D_MODEL, D_PROJ)),
            _const_spec((D_MODEL, D_MODEL)),
        ],
        out_shape=[
            jax.ShapeDtypeStruct((batch, seq_len, D_MODEL), F32),
            jax.ShapeDtypeStruct((batch, 1, LRU_WIDTH), F32),
            jax.ShapeDtypeStruct((batch, CONV_WIDTH - 1, LRU_WIDTH), F32),
            jax.ShapeDtypeStruct((batch, HG_HEADS, HG_HEAD_DIM, HG_HEAD_DIM), F32),
            jax.ShapeDtypeStruct((D_MODEL, D_PROJ), BF16),
            jax.ShapeDtypeStruct((D_MODEL, D_MODEL), BF16),
        ],
        scratch_shapes=[
            pltpu.VMEM((V7X_SUBLANES, LRU_WIDTH), F32),
            pltpu.VMEM((1, LRU_WIDTH), F32),
            pltpu.VMEM((HG_HEADS, HG_HEAD_DIM, HG_HEAD_DIM), F32),
            pltpu.VMEM((MIX_TILE, HG_WIDTH), F32),
        ],
        compiler_params=pltpu.CompilerParams(
            dimension_semantics=("arbitrary", "arbitrary"), vmem_limit_bytes=VMEM_LIMIT),
        name="mix_prompt",
    )(x, mod3, p["ln_mix_pre"], p["ln_mix_post"], p["w_in"], p["conv_w"], p["conv_b"],
      p["wa_bd"], p["b_a"], p["wx_bd"], p["b_x"], p["lam"], p["lb_logits"], p["hg_norm_w"],
      p["w_out"])


def _mix_sample_kernel(x_ref, m_ref, pre_ref, post_ref, win_ref, cw_ref, cb_ref,
                       wa_ref, ba_ref, wx_ref, bx_ref, lam_ref, lbl_ref, hgw_ref, wout_ref,
                       h0_ref, conv0_ref, s0_ref,
                       o_ref, h_out_ref, conv_out_ref, s_out_ref,
                       qf_ref, f_ref, k_ref, v_ref, g_ref, qk_ref, olru_ref, ohg_ref, ostage_ref):
    i = pl.program_id(0)
    n_i = pl.num_programs(0)
    nb = SAMPLE_SEQ_BLOCK

    @pl.when(i == 0)
    def _():
        x = x_ref[...]
        h = _rms(x, pre_ref[...]) * (1.0 + m_ref[1]) + m_ref[0]
        proj = _dot(h.astype(BF16), win_ref[...])
        u = proj[:, 0:LRU_WIDTH]
        y_lru = proj[:, LRU_WIDTH:2 * LRU_WIDTH]
        o0 = 2 * LRU_WIDTH
        q = proj[:, o0:o0 + HG_WIDTH]
        f_raw = proj[:, o0 + HG_WIDTH:o0 + 2 * HG_WIDTH]
        v = proj[:, o0 + 2 * HG_WIDTH:o0 + 3 * HG_WIDTH]
        g_ref[...] = proj[:, o0 + 3 * HG_WIDTH:o0 + 4 * HG_WIDTH]

        cw = cw_ref[...]
        u_conv = cb_ref[...]
        for k in range(CONV_WIDTH - 1):
            u_conv = u_conv + conv0_ref[k] * cw[k:k + 1]
            if k > 0:
                conv_out_ref[k - 1] = conv0_ref[k]
        u_conv = u_conv + u * cw[CONV_WIDTH - 1:CONV_WIDTH]
        conv_out_ref[CONV_WIDTH - 2] = u

        a, mult, ig = _lru_gates(u_conv, wa_ref, ba_ref[...], wx_ref, bx_ref[...], lam_ref[...])
        hs = (mult * ig) * u_conv + a * h0_ref[...]
        h_out_ref[...] = hs
        olru_ref[...] = hs * _gelu_tanh(y_lru)

        lb = _lower_bound(lbl_ref[...])
        f = lb + (1.0 - lb) * _sigmoid(f_raw)
        kk = 1.0 - f
        qs = q * (HG_HEAD_DIM ** -0.5)
        f_ref[...] = f
        qf_ref[...] = qs * f
        k_ref[...] = kk
        v_ref[...] = v
        qk = qs * kk
        for hd in range(HG_HEADS):
            l0 = hd * HG_HEAD_DIM
            tot = jnp.sum(qk[:, l0:l0 + HG_HEAD_DIM], axis=-1, keepdims=True)
            qk_ref[:, l0:l0 + HG_HEAD_DIM] = jnp.broadcast_to(tot, (qk.shape[0], HG_HEAD_DIM))

    base = pl.multiple_of(i * nb, nb)
    grp = lambda ref: ref[pl.ds(base, nb), :]
    f_g, k_g, v_g, qf_g, qk_g = grp(f_ref), grp(k_ref), grp(v_ref), grp(qf_ref), grp(qk_ref)
    square = (HG_HEAD_DIM, HG_HEAD_DIM)
    for j in range(nb):
        for hd in range(HG_HEADS):
            l0 = hd * HG_HEAD_DIM
            rowv = lambda z: z[j:j + 1, l0:l0 + HG_HEAD_DIM]
            s_old = s0_ref[j, hd]
            f_col = jnp.broadcast_to(rowv(f_g), square).T
            k_col = jnp.broadcast_to(rowv(k_g), square).T
            v_row = rowv(v_g)
            outer = k_col.astype(BF16).astype(F32) * v_row.astype(BF16).astype(F32)
            s_out_ref[j, hd] = f_col * s_old + outer
            qf8 = jnp.broadcast_to(rowv(qf_g), (V7X_SUBLANES, HG_HEAD_DIM)).astype(BF16)
            o1 = _dot(qf8, s_old.astype(BF16))[0:1]
            ostage_ref[j:j + 1, l0:l0 + HG_HEAD_DIM] = o1 + rowv(qk_g) * v_row
    ohg_ref[pl.ds(base, nb), :] = ostage_ref[...]

    @pl.when(i == n_i - 1)
    def _():
        hgw = hgw_ref[...]
        parts = [olru_ref[...]]
        for hd in range(HG_HEADS):
            l0 = hd * HG_HEAD_DIM
            parts.append(_head_rms_gate(ohg_ref[:, l0:l0 + HG_HEAD_DIM], hgw,
                                        g_ref[:, l0:l0 + HG_HEAD_DIM]))
        mix_in = jnp.concatenate(parts, axis=-1).astype(BF16)
        mix = _dot(mix_in, wout_ref[...])
        o_ref[...] = x_ref[...] + m_ref[2] * _rms(mix, post_ref[...])


def _mix_sample(x, mod9, p, h0, conv0, s0):
    n_seq = x.shape[0]
    nb = SAMPLE_SEQ_BLOCK
    state_spec = pl.BlockSpec((nb, HG_HEADS, HG_HEAD_DIM, HG_HEAD_DIM), lambda i: (i, 0, 0, 0))
    full2 = lambda shape: pl.BlockSpec(shape, lambda i: (0,) * len(shape))
    return pl.pallas_call(
        _mix_sample_kernel,
        grid=(n_seq // nb,),
        in_specs=[
            _const_spec((n_seq, D_MODEL)),
            pl.BlockSpec((3, n_seq, D_MODEL), lambda i: (1, 0, 0), pipeline_mode=pl.Buffered(1)),
            _const_spec((1, D_MODEL)),
            _const_spec((1, D_MODEL)),
            _const_spec((D_MODEL, D_PROJ)),
            _const_spec((CONV_WIDTH, LRU_WIDTH)),
            _const_spec((1, LRU_WIDTH)),
            _const_spec((LRU_WIDTH, LRU_WIDTH)),
            _const_spec((1, LRU_WIDTH)),
            _const_spec((LRU_WIDTH, LRU_WIDTH)),
            _const_spec((1, LRU_WIDTH)),
            _const_spec((1, LRU_WIDTH)),
            _const_spec(p["lb_logits"].shape),
            _const_spec((1, HG_HEAD_DIM)),
            _const_spec((D_MODEL, D_MODEL)),
            _const_spec((n_seq, LRU_WIDTH)),
            _const_spec((CONV_WIDTH - 1, n_seq, LRU_WIDTH)),
            state_spec,
        ],
        out_specs=[
            full2((n_seq, D_MODEL)),
            full2((n_seq, LRU_WIDTH)),
            full2((CONV_WIDTH - 1, n_seq, LRU_WIDTH)),
            state_spec,
        ],
        out_shape=[
            jax.ShapeDtypeStruct((n_seq, D_MODEL), F32),
            jax.ShapeDtypeStruct((n_seq, LRU_WIDTH), F32),
            jax.ShapeDtypeStruct((CONV_WIDTH - 1, n_seq, LRU_WIDTH), F32),
            jax.ShapeDtypeStruct((n_seq, HG_HEADS, HG_HEAD_DIM, HG_HEAD_DIM), F32),
        ],
        scratch_shapes=[pltpu.VMEM((n_seq, HG_WIDTH), F32) for _ in range(8)]
        + [pltpu.VMEM((nb, HG_WIDTH), F32)],
        compiler_params=pltpu.CompilerParams(
            dimension_semantics=("arbitrary",), vmem_limit_bytes=VMEM_LIMIT),
        name="mix_sample",
    )(x, mod9, p["ln_mix_pre"], p["ln_mix_post"], p["w_in"], p["conv_w"], p["conv_b"],
      p["wa_bd"], p["b_a"], p["wx_bd"], p["b_x"], p["lam"], p["lb_logits"], p["hg_norm_w"],
      p["w_out"], h0, conv0, s0)


def _block_diag(w):
    heads, blk, _ = w.shape
    eye = jnp.eye(heads, dtype=w.dtype)
    return (eye[:, None, :, None] * w[:, :, None, :]).reshape(heads * blk, heads * blk)


def kernel(x_prompt, x_sample, c_prompt, c_sample, state_lru_h, state_lru_conv, state_hgrn_S, w_ada, b_ada, ln_ffn1_pre, ln_ffn1_post, ffn1_w_gate, ffn1_w_up, ffn1_w_down, ln_mix_pre, ln_mix_post, w_in, lru_conv_w, lru_conv_b, lru_w_a, lru_b_a, lru_w_x, lru_b_x, lru_lambda, hg_lb_logits, hg_norm_w, w_out, ln_ffn2_pre, ln_ffn2_post, ffn2_w_gate, ffn2_w_up, ffn2_w_down):
    depth = w_ada.shape[0]
    batch, seq_len, _ = x_prompt.shape
    n_seq = x_sample.shape[0]
    assert depth == 1 and x_sample.shape[1] == 1
    assert seq_len % FFN_TILE == 0 and seq_len % MIX_TILE == 0 and n_seq % SAMPLE_SEQ_BLOCK == 0
    assert D_PROJ == 6 * LRU_WIDTH and LRU_WIDTH == HG_WIDTH

    xp = x_prompt.reshape(batch * seq_len, D_MODEL)
    xs = x_sample.reshape(n_seq, D_MODEL)
    ph, pc, pS, sh, sc, sS = [], [], [], [], [], []
    for l in range(depth):
        row = lambda w: w[l].reshape(1, -1)
        mod9 = _ada(c_sample, c_prompt, w_ada[l], b_ada[l])
        mod_p = jnp.transpose(mod9[:, n_seq:], (1, 0, 2))
        p = {
            "ln_mix_pre": row(ln_mix_pre), "ln_mix_post": row(ln_mix_post),
            "w_in": w_in[l], "w_out": w_out[l],
            "conv_w": lru_conv_w[l], "conv_b": row(lru_conv_b),
            "wa_bd": _block_diag(lru_w_a[l]).astype(BF16), "b_a": row(lru_b_a),
            "wx_bd": _block_diag(lru_w_x[l]).astype(BF16), "b_x": row(lru_b_x),
            "lam": row(lru_lambda), "lb_logits": hg_lb_logits,
            "hg_norm_w": row(hg_norm_w),
        }
        f1 = (row(ln_ffn1_pre), row(ln_ffn1_post), ffn1_w_gate[l], ffn1_w_up[l], ffn1_w_down[l])
        f2 = (row(ln_ffn2_pre), row(ln_ffn2_post), ffn2_w_gate[l], ffn2_w_up[l], ffn2_w_down[l])

        xp, xs = _ffn(xp, xs, mod_p[:, 0:3], mod9, 0, *f1, seq_len)
        xp, h_p, c_p, S_p, w_in_bf, w_out_bf = _mix_prompt(
            xp.reshape(batch, seq_len, D_MODEL), mod_p[:, 3:6], p)
        xp = xp.reshape(batch * seq_len, D_MODEL)
        conv0 = jnp.transpose(state_lru_conv[l], (1, 0, 2))
        xs, h_s, c_s, S_s = _mix_sample(xs, mod9, dict(p, w_in=w_in_bf, w_out=w_out_bf),
                                        state_lru_h[l], conv0, state_hgrn_S[l])
        xp, xs = _ffn(xp, xs, mod_p[:, 6:9], mod9, 2, *f2, seq_len)

        ph.append(h_p.reshape(batch, LRU_WIDTH)); pc.append(c_p); pS.append(S_p)
        sh.append(h_s); sc.append(jnp.transpose(c_s, (1, 0, 2))); sS.append(S_s)

    return (xp.reshape(batch, seq_len, D_MODEL), xs.reshape(n_seq, 1, D_MODEL),
            jnp.stack(ph), jnp.stack(pc), jnp.stack(pS), jnp.stack(sh), jnp.stack(sc), jnp.stack(sS))
```

```python
import jax
import jax.numpy as jnp
from jax import lax
from jax.experimental import pallas as pl
from jax.experimental.pallas import tpu as pltpu

F32 = jnp.float32
BF16 = jnp.bfloat16

D_MODEL = 1024
D_FF = 2816
LRU_WIDTH = 512
CONV_WIDTH = 4
LRU_C = 8.0
HG_WIDTH = 512
HG_HEAD_DIM = 128
HG_HEADS = HG_WIDTH // HG_HEAD_DIM
HG_CHUNK = 64
N_MOD = 9
D_PROJ = 2 * LRU_WIDTH + 4 * HG_WIDTH
EPS = 1e-6

V7X_SUBLANES = 8
V7X_VMEM_BYTES = 64 * 1024 * 1024
VMEM_LIMIT = V7X_VMEM_BYTES - 8 * 1024 * 1024

FFN_TILE = 1024
FFN_ROWS = 512
MIX_TILE = 512
MIX_ROWS = 256
HG_DIRECT_ROWS = 256
FFN_WBLOCK = 256
FFN_WSTEPS = D_FF // FFN_WBLOCK
SAMPLE_SEQ_BLOCK = 2 * V7X_SUBLANES

HG_LEVELS = (1, 2, 4, 8, 16, 32)
HG_DIRECT_MIN = 2.0 ** -120


def _rms(x, w):
    return (x * lax.rsqrt(jnp.mean(x * x, axis=-1, keepdims=True) + EPS)) * w


def _sigmoid(x):
    return 1.0 / (1.0 + jnp.exp(-x))


def _silu(x):
    return x * _sigmoid(x)


def _gelu_tanh(x):
    c = 0.7978845608028654
    return x * (0.5 * (1.0 + jnp.tanh(c * (x + 0.044715 * (x * x * x)))))


def _softplus(z):
    return jnp.maximum(z, 0.0) + jnp.log1p(jnp.exp(-jnp.abs(z)))


def _dot(a, b):
    return jnp.dot(a, b, preferred_element_type=F32)


def _dot_nt(a, b):
    return lax.dot_general(a, b, (((1,), (1,)), ((), ())), preferred_element_type=F32)


def _dot_tn(a, b):
    return lax.dot_general(a, b, (((0,), (0,)), ((), ())), preferred_element_type=F32)


def _const_spec(shape):
    nd = len(shape)
    return pl.BlockSpec(shape, lambda *_: (0,) * nd, pipeline_mode=pl.Buffered(1))


def _ada_kernel(cs_ref, cp_ref, w_ref, b_ref, o_ref):
    s = _silu(jnp.concatenate([cs_ref[...], cp_ref[...]], axis=0))
    o_ref[0] = _dot(s.astype(BF16), w_ref[...].astype(BF16)) + b_ref[...]


def _ada(c_sample, c_prompt, w_ada, b_ada):
    rows = c_sample.shape[0] + c_prompt.shape[0]
    return pl.pallas_call(
        _ada_kernel,
        grid=(N_MOD,),
        in_specs=[
            _const_spec(c_sample.shape),
            _const_spec(c_prompt.shape),
            pl.BlockSpec((D_MODEL, D_MODEL), lambda j: (0, j)),
            pl.BlockSpec((1, D_MODEL), lambda j: (0, j)),
        ],
        out_specs=pl.BlockSpec((1, rows, D_MODEL), lambda j: (j, 0, 0)),
        out_shape=jax.ShapeDtypeStruct((N_MOD, rows, D_MODEL), F32),
        compiler_params=pltpu.CompilerParams(
            dimension_semantics=("arbitrary",), vmem_limit_bytes=VMEM_LIMIT),
        name="ada_mod",
    )(c_sample, c_prompt, w_ada, b_ada.reshape(1, N_MOD * D_MODEL))


def _ffn_body(x, shift, scale, gate, pre_w, post_w, wg_ref, wu_ref, wd_ref):
    h = _rms(x, pre_w) * (1.0 + scale) + shift
    hb = h.astype(BF16)
    a = _dot(hb, wg_ref[...])
    u = _dot(hb, wu_ref[...])
    act = (_silu(a) * u).astype(BF16)
    y = _dot(act, wd_ref[...])
    return x + (0.5 * gate) * _rms(y, post_w)


def _ffn_kernel(xp_ref, mp_ref, xs_ref, ms_ref, pre_ref, post_ref, wg32_ref, wu32_ref, wd32_ref,
                op_ref, os_ref, wg_ref, wu_ref, wd_ref):
    s = pl.program_id(0)
    n_steps = pl.num_programs(0)

    for j in range(FFN_WSTEPS):
        @pl.when(s == j)
        def _(j=j):
            c0 = j * FFN_WBLOCK
            wg_ref[:, c0:c0 + FFN_WBLOCK] = wg32_ref[...].astype(BF16)
            wu_ref[:, c0:c0 + FFN_WBLOCK] = wu32_ref[...].astype(BF16)
            wd_ref[c0:c0 + FFN_WBLOCK, :] = wd32_ref[...].astype(BF16)

    @pl.when((s >= FFN_WSTEPS) & (s < n_steps - 1))
    def _():
        m = mp_ref[0]
        for r0 in range(0, FFN_TILE, FFN_ROWS):
            op_ref[r0:r0 + FFN_ROWS, :] = _ffn_body(
                xp_ref[r0:r0 + FFN_ROWS, :], m[0:1], m[1:2], m[2:3],
                pre_ref[...], post_ref[...], wg_ref, wu_ref, wd_ref)

    @pl.when(s == n_steps - 1)
    def _():
        os_ref[...] = _ffn_body(xs_ref[...], ms_ref[0], ms_ref[1], ms_ref[2], pre_ref[...],
                                post_ref[...], wg_ref, wu_ref, wd_ref)


def _ffn(xp, xs, mod_p3, mod9, sub_layer, pre_w, post_w, wg, wu, wd, seq_len):
    n_tok = xp.shape[0]
    n_seq = xs.shape[0]
    n_tiles = n_tok // FFN_TILE
    steps_per_seq = seq_len // FFN_TILE
    tile = lambda s: jnp.clip(s - FFN_WSTEPS, 0, n_tiles - 1)
    wblk = lambda s: jnp.minimum(s, FFN_WSTEPS - 1)
    tok_spec = pl.BlockSpec((FFN_TILE, D_MODEL), lambda s: (tile(s), 0))
    return pl.pallas_call(
        _ffn_kernel,
        grid=(FFN_WSTEPS + n_tiles + 1,),
        in_specs=[
            tok_spec,
            pl.BlockSpec((1, 3, D_MODEL), lambda s: (tile(s) // steps_per_seq, 0, 0)),
            _const_spec((n_seq, D_MODEL)),
            pl.BlockSpec((3, n_seq, D_MODEL), lambda s: (sub_layer, 0, 0), pipeline_mode=pl.Buffered(1)),
            _const_spec((1, D_MODEL)),
            _const_spec((1, D_MODEL)),
            pl.BlockSpec((D_MODEL, FFN_WBLOCK), lambda s: (0, wblk(s))),
            pl.BlockSpec((D_MODEL, FFN_WBLOCK), lambda s: (0, wblk(s))),
            pl.BlockSpec((FFN_WBLOCK, D_MODEL), lambda s: (wblk(s), 0)),
        ],
        out_specs=[tok_spec, pl.BlockSpec((n_seq, D_MODEL), lambda s: (0, 0))],
        out_shape=[jax.ShapeDtypeStruct((n_tok, D_MODEL), F32),
                   jax.ShapeDtypeStruct((n_seq, D_MODEL), F32)],
        scratch_shapes=[pltpu.VMEM((D_MODEL, D_FF), BF16), pltpu.VMEM((D_MODEL, D_FF), BF16),
                        pltpu.VMEM((D_FF, D_MODEL), BF16)],
        compiler_params=pltpu.CompilerParams(
            dimension_semantics=("arbitrary",), vmem_limit_bytes=VMEM_LIMIT),
        name="ffn",
    )(xp, mod_p3, xs, mod9, pre_w, post_w, wg, wu, wd)


def _lower_bound(lb_logits):
    z = lb_logits - jnp.max(lb_logits, axis=0, keepdims=True)
    e = jnp.exp(z)
    return e[0:1] / jnp.sum(e, axis=0, keepdims=True)


def _lru_gates(u_conv, wa_ref, ba, wx_ref, bx, lam):
    ub = u_conv.astype(BF16)
    r = _sigmoid(_dot(ub, wa_ref[...]) + ba)
    ig = _sigmoid(_dot(ub, wx_ref[...]) + bx)
    log_a = (-LRU_C * r) * _softplus(-lam)
    a = jnp.exp(log_a)
    th = jnp.tanh(log_a)
    mult = jnp.sqrt((-2.0 * th) / (1.0 - th))
    return a, mult, ig


def _head_rms_gate(o, norm_w, g):
    return _rms(o, norm_w) * _silu(g)


def _hgrn_chunk_head(scores, q_in, k_out, dec, vb_blk, g_blk, hgw, st_ref, hd, ohg_ref, r0, l0):
    st = st_ref[hd]
    o = _dot_nt(q_in, st.astype(BF16)) + _dot(scores.astype(BF16), vb_blk)
    st_ref[hd] = st * dec + _dot_tn(vb_blk, k_out)
    ohg_ref[r0:r0 + HG_CHUNK, l0:l0 + HG_HEAD_DIM] = _head_rms_gate(o, hgw, g_blk)


def _hgrn_chunks_direct(qs, kk, pfx, tails, vb, g, hgw, st_ref, ohg_ref, r0):
    tb = qs.shape[0]
    n_chunk = len(tails)
    inv = 1.0 / pfx
    q_dec = qs * pfx
    k_dec = kk * inv
    q_in = q_dec.astype(BF16)
    k_in = k_dec.astype(BF16)
    ti = lax.broadcasted_iota(jnp.int32, (tb, tb), 0)
    si = lax.broadcasted_iota(jnp.int32, (tb, tb), 1)
    visible = (ti >= si) & ((ti ^ si) < HG_CHUNK)
    row_chunk = jnp.right_shift(lax.broadcasted_iota(jnp.int32, (tb, HG_HEAD_DIM), 0),
                                HG_CHUNK.bit_length() - 1)
    tail_rows = jnp.concatenate([jnp.broadcast_to(t, (HG_CHUNK, t.shape[-1])) for t in tails], axis=0)
    k_end = k_dec * tail_rows
    for hd in range(HG_HEADS):
        l0 = hd * HG_HEAD_DIM
        head = lambda z: z[:, l0:l0 + HG_HEAD_DIM]
        by_chunk = lambda z: jnp.concatenate(
            [jnp.where(row_chunk == c, head(z), 0.0) for c in range(n_chunk)], axis=-1).astype(BF16)
        scores = jnp.where(visible, _dot_nt(head(q_in), head(k_in)), 0.0)
        grown = _dot_tn(head(vb), by_chunk(k_end))
        st = st_ref[hd]
        states = []
        for c, tail in enumerate(tails):
            states.append(st.astype(BF16))
            st = st * head(tail) + grown[:, c * HG_HEAD_DIM:(c + 1) * HG_HEAD_DIM]
        st_ref[hd] = st
        o = (_dot_nt(by_chunk(q_dec), jnp.concatenate(states, axis=-1))
             + _dot(scores.astype(BF16), head(vb)))
        ohg_ref[r0:r0 + tb, l0:l0 + HG_HEAD_DIM] = _head_rms_gate(o, hgw, head(g))


def _hgrn_chunks_levels(f3, kk3, qs3, vb, g, hgw, st_ref, ohg_ref):
    n_grp = f3.shape[0]
    tb = n_grp * V7X_SUBLANES
    sub = lax.broadcasted_iota(jnp.int32, f3.shape, 1)
    grp = lax.broadcasted_iota(jnp.int32, f3.shape, 0)
    to_mxu = lambda z: z.reshape(tb, z.shape[-1]).astype(BF16)
    q_lv = [to_mxu(qs3)]
    k_lv = [to_mxu(kk3)]
    tot, pre, post = f3, f3, None
    for lvl in HG_LEVELS:
        q_lv.append(to_mxu(qs3 * pre))
        k_lv.append(k_lv[0] if post is None else to_mxu(kk3 * post))
        if lvl < V7X_SUBLANES:
            upper = (sub & lvl) != 0
            below = pltpu.roll(tot, lvl, 1)
            above = pltpu.roll(tot, V7X_SUBLANES - lvl, 1)
        else:
            upper = (grp & (lvl // V7X_SUBLANES)) != 0
            below = jnp.roll(tot, lvl // V7X_SUBLANES, axis=0)
            above = jnp.roll(tot, -(lvl // V7X_SUBLANES), axis=0)
        pre = jnp.where(upper, pre * below, pre)
        post = jnp.where(upper, 1.0, above) if post is None else jnp.where(upper, post, post * above)
        tot = tot * jnp.where(upper, below, above)
    q_in = to_mxu(qs3 * pre)
    k_out = to_mxu(kk3 * post)
    s_decay = tot.reshape(tb, tot.shape[-1])

    ti = lax.broadcasted_iota(jnp.int32, (HG_CHUNK, HG_CHUNK), 0)
    si = lax.broadcasted_iota(jnp.int32, (HG_CHUNK, HG_CHUNK), 1)
    masks = [ti == si]
    tx = ti ^ si
    for lvl in HG_LEVELS:
        masks.append((tx >= lvl) & (tx < 2 * lvl) & ((ti & lvl) != 0))

    for c in range(tb // HG_CHUNK):
        r0 = c * HG_CHUNK
        for hd in range(HG_HEADS):
            l0 = hd * HG_HEAD_DIM
            blk = lambda z: z[r0:r0 + HG_CHUNK, l0:l0 + HG_HEAD_DIM]
            scores = jnp.zeros((HG_CHUNK, HG_CHUNK), F32)
            for ql, kl, msk in zip(q_lv, k_lv, masks):
                scores = jnp.where(msk, _dot_nt(blk(ql), blk(kl)), scores)
            dec = s_decay[r0 + HG_CHUNK - 1:r0 + HG_CHUNK, l0:l0 + HG_HEAD_DIM]
            _hgrn_chunk_head(scores, blk(q_in), blk(k_out), dec, blk(vb), blk(g), hgw,
                             st_ref, hd, ohg_ref, r0, l0)

def _mix_tile(x_ref, m_ref, o_ref, t, pre_ref, post_ref, win_ref, cw_ref, cb_ref, wa_ref, ba_ref,
              wx_ref, bx_ref, lam_ref, lbl_ref, hgw_ref, wout_ref, ucar_ref, hcar_ref, st_ref, ohg_ref):
    tb = MIX_TILE
    x = x_ref[...]
    m = m_ref[...]
    per_block = []
    for r0 in range(0, tb, MIX_ROWS):
        hb = (_rms(x[r0:r0 + MIX_ROWS], pre_ref[...]) * (1.0 + m[1:2]) + m[0:1]).astype(BF16)
        proj = _dot(hb, win_ref[...])
        per_block.append([proj[:, c0:c0 + LRU_WIDTH] for c0 in range(0, D_PROJ, LRU_WIDTH)])
    u, y_lru, q, f_raw, v, g = [jnp.concatenate(seg, axis=0) for seg in zip(*per_block)]

    n_grp = tb // V7X_SUBLANES
    grouped = lambda z: z.reshape(n_grp, V7X_SUBLANES, z.shape[-1])
    flat = lambda z: z.reshape(tb, z.shape[-1])
    gshape = (n_grp, V7X_SUBLANES, LRU_WIDTH)
    sub = lax.broadcasted_iota(jnp.int32, gshape, 1)
    grp = lax.broadcasted_iota(jnp.int32, gshape, 0)

    u3 = grouped(u)
    u_all = jnp.concatenate([ucar_ref[...][None], u3], axis=0)
    ucar_ref[...] = u3[n_grp - 1]
    cw = cw_ref[...]
    u_conv = cb_ref[...]
    for k in range(CONV_WIDTH - 1):
        back = CONV_WIDTH - 1 - k
        rot = pltpu.roll(u_all, back, 1)
        u_conv = u_conv + jnp.where(sub >= back, rot[1:], rot[:-1]) * cw[k:k + 1]
    u_conv = flat(u_conv + u3 * cw[CONV_WIDTH - 1:CONV_WIDTH])

    a, mult, ig = _lru_gates(u_conv, wa_ref, ba_ref[...], wx_ref, bx_ref[...], lam_ref[...])
    a3, mult3 = grouped(a), grouped(mult)
    first_row = jnp.where(t == 0, 0, -1)
    mult3 = jnp.where(grp * V7X_SUBLANES + sub == first_row, 1.0, mult3)
    b3 = (mult3 * grouped(ig)) * grouped(u_conv)
    s = 1
    while s < V7X_SUBLANES:
        keep = sub >= s
        b3 = jnp.where(keep, a3 * pltpu.roll(b3, s, 1) + b3, b3)
        a3 = jnp.where(keep, a3 * pltpu.roll(a3, s, 1), a3)
        s *= 2
    carry = hcar_ref[...]
    hs = []
    for gi in range(n_grp):
        h_g = a3[gi] * carry + b3[gi]
        carry = h_g[V7X_SUBLANES - 1:V7X_SUBLANES]
        hs.append(h_g)
    hcar_ref[...] = carry
    o_lru = jnp.concatenate(hs, axis=0) * _gelu_tanh(y_lru)

    lb = _lower_bound(lbl_ref[...])
    f3 = grouped(lb + (1.0 - lb) * _sigmoid(f_raw))
    kk3 = 1.0 - f3
    qs3 = grouped(q * (HG_HEAD_DIM ** -0.5))
    vb = v.astype(BF16)
    hgw = hgw_ref[...]

    pfx = f3
    s = 1
    while s < V7X_SUBLANES:
        pfx = jnp.where(sub >= s, pfx * pltpu.roll(pfx, s, 1), pfx)
        s *= 2
    grp_per_chunk = HG_CHUNK // V7X_SUBLANES
    rows, tails = [], []
    for gi in range(n_grp):
        p_g = pfx[gi] if gi % grp_per_chunk == 0 else pfx[gi] * carry
        carry = p_g[V7X_SUBLANES - 1:V7X_SUBLANES]
        rows.append(p_g)
        if gi % grp_per_chunk == grp_per_chunk - 1:
            tails.append(carry)
    pfx = jnp.concatenate(rows, axis=0)
    direct = jnp.min(jnp.concatenate(tails, axis=0)) >= HG_DIRECT_MIN

    @pl.when(direct)
    def _():
        qs, kk = flat(qs3), flat(kk3)
        per_block = HG_DIRECT_ROWS // HG_CHUNK
        for bi, r0 in enumerate(range(0, tb, HG_DIRECT_ROWS)):
            rows = lambda z: z[r0:r0 + HG_DIRECT_ROWS]
            _hgrn_chunks_direct(rows(qs), rows(kk), rows(pfx), tails[bi * per_block:(bi + 1) * per_block],
                                rows(vb), rows(g), hgw, st_ref, ohg_ref, r0)

    @pl.when(jnp.logical_not(direct))
    def _():
        _hgrn_chunks_levels(f3, kk3, qs3, vb, g, hgw, st_ref, ohg_ref)

    mix_in = jnp.concatenate([o_lru, ohg_ref[...]], axis=-1).astype(BF16)
    for r0 in range(0, tb, MIX_ROWS):
        mix = _dot(mix_in[r0:r0 + MIX_ROWS], wout_ref[...])
        o_ref[r0:r0 + MIX_ROWS, :] = x[r0:r0 + MIX_ROWS] + m[2:3] * _rms(mix, post_ref[...])


def _mix_prompt_kernel(x_ref, m_ref, *refs):
    (pre_ref, post_ref, win32_ref, cw_ref, cb_ref, wa_ref, ba_ref, wx_ref, bx_ref, lam_ref, lbl_ref,
     hgw_ref, wout32_ref) = refs[:13]
    (o_ref, h_out_ref, conv_out_ref, s_out_ref, win_ref, wout_ref,
     ucar_ref, hcar_ref, st_ref, ohg_ref) = refs[13:]
    t = pl.program_id(1)
    n_t = pl.num_programs(1)

    @pl.when((pl.program_id(0) == 0) & (t == 0))
    def _():
        win_ref[...] = win32_ref[...].astype(BF16)
        wout_ref[...] = wout32_ref[...].astype(BF16)

    @pl.when(t == 0)
    def _():
        ucar_ref[...] = jnp.zeros_like(ucar_ref)
        hcar_ref[...] = jnp.zeros_like(hcar_ref)
        st_ref[...] = jnp.zeros_like(st_ref)

    _mix_tile(x_ref.at[0], m_ref.at[0], o_ref.at[0], t, pre_ref, post_ref, win_ref, cw_ref, cb_ref,
              wa_ref, ba_ref, wx_ref, bx_ref, lam_ref, lbl_ref, hgw_ref, wout_ref,
              ucar_ref, hcar_ref, st_ref, ohg_ref)

    @pl.when(t == n_t - 1)
    def _():
        h_out_ref[0] = hcar_ref[...]
        conv_out_ref[0] = ucar_ref[V7X_SUBLANES - (CONV_WIDTH - 1):V7X_SUBLANES, :]
        for hd in range(HG_HEADS):
            s_out_ref[0, hd] = st_ref[hd].T


def _mix_prompt(x, mod3, p):
    batch, seq_len, _ = x.shape
    n_t = seq_len // MIX_TILE
    tok_spec = pl.BlockSpec((1, MIX_TILE, D_MODEL), lambda b, t: (b, t, 0))
    return pl.pallas_call(
        _mix_prompt_kernel,
        grid=(batch, n_t),
        in_specs=[
            tok_spec,
            pl.BlockSpec((1, 3, D_MODEL), lambda b, t: (b, 0, 0)),
            _const_spec((1, D_MODEL)),
            _const_spec((1, D_MODEL)),
            _const_spec((D_MODEL, D_PROJ)),
            _const_spec((CONV_WIDTH, LRU_WIDTH)),
            _const_spec((1, LRU_WIDTH)),
            _const_spec((LRU_WIDTH, LRU_WIDTH)),
            _const_spec((1, LRU_WIDTH)),
            _const_spec((LRU_WIDTH, LRU_WIDTH)),
            _const_spec((1, LRU_WIDTH)),
            _const_spec((1, LRU_WIDTH)),
            _const_spec(p["lb_logits"].shape),
            _const_spec((1, HG_HEAD_DIM)),
            _const_spec((D_MODEL, D_MODEL)),
        ],
        out_specs=[
            tok_spec,
            pl.BlockSpec((1, 1, LRU_WIDTH), lambda b, t: (b, 0, 0)),
            pl.BlockSpec((1, CONV_WIDTH - 1, LRU_WIDTH), lambda b, t: (b, 0, 0)),
            pl.BlockSpec((1, HG_HEADS, HG_HEAD_DIM, HG_HEAD_DIM), lambda b, t: (b, 0, 0, 0)),
            _const_spec((D_MODEL, D_PROJ)),
            _const_spec((D_MODEL, D_MODEL)),
        ],
        out_shape=[
            jax.ShapeDtypeStruct((batch, seq_len, D_MODEL), F32),
            jax.ShapeDtypeStruct((batch, 1, LRU_WIDTH), F32),
            jax.ShapeDtypeStruct((batch, CONV_WIDTH - 1, LRU_WIDTH), F32),
            jax.ShapeDtypeStruct((batch, HG_HEADS, HG_HEAD_DIM, HG_HEAD_DIM), F32),
            jax.ShapeDtypeStruct((D_MODEL, D_PROJ), BF16),
            jax.ShapeDtypeStruct((D_MODEL, D_MODEL), BF16),
        ],
        scratch_shapes=[
            pltpu.VMEM((V7X_SUBLANES, LRU_WIDTH), F32),
            pltpu.VMEM((1, LRU_WIDTH), F32),
            pltpu.VMEM((HG_HEADS, HG_HEAD_DIM, HG_HEAD_DIM), F32),
            pltpu.VMEM((MIX_TILE, HG_WIDTH), F32),
        ],
        compiler_params=pltpu.CompilerParams(
            dimension_semantics=("arbitrary", "arbitrary"), vmem_limit_bytes=VMEM_LIMIT),
        name="mix_prompt",
    )(x, mod3, p["ln_mix_pre"], p["ln_mix_post"], p["w_in"], p["conv_w"], p["conv_b"],
      p["wa_bd"], p["b_a"], p["wx_bd"], p["b_x"], p["lam"], p["lb_logits"], p["hg_norm_w"],
      p["w_out"])


def _mix_sample_kernel(x_ref, m_ref, pre_ref, post_ref, win_ref, cw_ref, cb_ref,
                       wa_ref, ba_ref, wx_ref, bx_ref, lam_ref, lbl_ref, hgw_ref, wout_ref,
                       h0_ref, conv0_ref, s0_ref,
                       o_ref, h_out_ref, conv_out_ref, s_out_ref,
                       qf_ref, f_ref, k_ref, v_ref, g_ref, qk_ref, olru_ref, ohg_ref, ostage_ref):
    i = pl.program_id(0)
    n_i = pl.num_programs(0)
    nb = SAMPLE_SEQ_BLOCK

    @pl.when(i == 0)
    def _():
        x = x_ref[...]
        h = _rms(x, pre_ref[...]) * (1.0 + m_ref[1]) + m_ref[0]
        proj = _dot(h.astype(BF16), win_ref[...])
        u = proj[:, 0:LRU_WIDTH]
        y_lru = proj[:, LRU_WIDTH:2 * LRU_WIDTH]
        o0 = 2 * LRU_WIDTH
        q = proj[:, o0:o0 + HG_WIDTH]
        f_raw = proj[:, o0 + HG_WIDTH:o0 + 2 * HG_WIDTH]
        v = proj[:, o0 + 2 * HG_WIDTH:o0 + 3 * HG_WIDTH]
        g_ref[...] = proj[:, o0 + 3 * HG_WIDTH:o0 + 4 * HG_WIDTH]

        cw = cw_ref[...]
        u_conv = cb_ref[...]
        for k in range(CONV_WIDTH - 1):
            u_conv = u_conv + conv0_ref[k] * cw[k:k + 1]
            if k > 0:
                conv_out_ref[k - 1] = conv0_ref[k]
        u_conv = u_conv + u * cw[CONV_WIDTH - 1:CONV_WIDTH]
        conv_out_ref[CONV_WIDTH - 2] = u

        a, mult, ig = _lru_gates(u_conv, wa_ref, ba_ref[...], wx_ref, bx_ref[...], lam_ref[...])
        hs = (mult * ig) * u_conv + a * h0_ref[...]
        h_out_ref[...] = hs
        olru_ref[...] = hs * _gelu_tanh(y_lru)

        lb = _lower_bound(lbl_ref[...])
        f = lb + (1.0 - lb) * _sigmoid(f_raw)
        kk = 1.0 - f
        qs = q * (HG_HEAD_DIM ** -0.5)
        f_ref[...] = f
        qf_ref[...] = qs * f
        k_ref[...] = kk
        v_ref[...] = v
        qk = qs * kk
        for hd in range(HG_HEADS):
            l0 = hd * HG_HEAD_DIM
            tot = jnp.sum(qk[:, l0:l0 + HG_HEAD_DIM], axis=-1, keepdims=True)
            qk_ref[:, l0:l0 + HG_HEAD_DIM] = jnp.broadcast_to(tot, (qk.shape[0], HG_HEAD_DIM))

    base = pl.multiple_of(i * nb, nb)
    grp = lambda ref: ref[pl.ds(base, nb), :]
    f_g, k_g, v_g, qf_g, qk_g = grp(f_ref), grp(k_ref), grp(v_ref), grp(qf_ref), grp(qk_ref)
    square = (HG_HEAD_DIM, HG_HEAD_DIM)
    for j in range(nb):
        for hd in range(HG_HEADS):
            l0 = hd * HG_HEAD_DIM
            rowv = lambda z: z[j:j + 1, l0:l0 + HG_HEAD_DIM]
            s_old = s0_ref[j, hd]
            f_col = jnp.broadcast_to(rowv(f_g), square).T
            k_col = jnp.broadcast_to(rowv(k_g), square).T
            v_row = rowv(v_g)
            outer = k_col.astype(BF16).astype(F32) * v_row.astype(BF16).astype(F32)
            s_out_ref[j, hd] = f_col * s_old + outer
            qf8 = jnp.broadcast_to(rowv(qf_g), (V7X_SUBLANES, HG_HEAD_DIM)).astype(BF16)
            o1 = _dot(qf8, s_old.astype(BF16))[0:1]
            ostage_ref[j:j + 1, l0:l0 + HG_HEAD_DIM] = o1 + rowv(qk_g) * v_row
    ohg_ref[pl.ds(base, nb), :] = ostage_ref[...]

    @pl.when(i == n_i - 1)
    def _():
        hgw = hgw_ref[...]
        parts = [olru_ref[...]]
        for hd in range(HG_HEADS):
            l0 = hd * HG_HEAD_DIM
            parts.append(_head_rms_gate(ohg_ref[:, l0:l0 + HG_HEAD_DIM], hgw,
                                        g_ref[:, l0:l0 + HG_HEAD_DIM]))
        mix_in = jnp.concatenate(parts, axis=-1).astype(BF16)
        mix = _dot(mix_in, wout_ref[...])
        o_ref[...] = x_ref[...] + m_ref[2] * _rms(mix, post_ref[...])


def _mix_sample(x, mod9, p, h0, conv0, s0):
    n_seq = x.shape[0]
    nb = SAMPLE_SEQ_BLOCK
    state_spec = pl.BlockSpec((nb, HG_HEADS, HG_HEAD_DIM, HG_HEAD_DIM), lambda i: (i, 0, 0, 0))
    full2 = lambda shape: pl.BlockSpec(shape, lambda i: (0,) * len(shape))
    return pl.pallas_call(
        _mix_sample_kernel,
        grid=(n_seq // nb,),
        in_specs=[
            _const_spec((n_seq, D_MODEL)),
            pl.BlockSpec((3, n_seq, D_MODEL), lambda i: (1, 0, 0), pipeline_mode=pl.Buffered(1)),
            _const_spec((1, D_MODEL)),
            _const_spec((1, D_MODEL)),
            _const_spec((D_MODEL, D_PROJ)),
            _const_spec((CONV_WIDTH, LRU_WIDTH)),
            _const_spec((1, LRU_WIDTH)),
            _const_spec((LRU_WIDTH, LRU_WIDTH)),
            _const_spec((1, LRU_WIDTH)),
            _const_spec((LRU_WIDTH, LRU_WIDTH)),
            _const_spec((1, LRU_WIDTH)),
            _const_spec((1, LRU_WIDTH)),
            _const_spec(p["lb_logits"].shape),
            _const_spec((1, HG_HEAD_DIM)),
            _const_spec((D_MODEL, D_MODEL)),
            _const_spec((n_seq, LRU_WIDTH)),
            _const_spec((CONV_WIDTH - 1, n_seq, LRU_WIDTH)),
            state_spec,
        ],
        out_specs=[
            full2((n_seq, D_MODEL)),
            full2((n_seq, LRU_WIDTH)),
            full2((CONV_WIDTH - 1, n_seq, LRU_WIDTH)),
            state_spec,
        ],
        out_shape=[
            jax.ShapeDtypeStruct((n_seq, D_MODEL), F32),
            jax.ShapeDtypeStruct((n_seq, LRU_WIDTH), F32),
            jax.ShapeDtypeStruct((CONV_WIDTH - 1, n_seq, LRU_WIDTH), F32),
            jax.ShapeDtypeStruct((n_seq, HG_HEADS, HG_HEAD_DIM, HG_HEAD_DIM), F32),
        ],
        scratch_shapes=[pltpu.VMEM((n_seq, HG_WIDTH), F32) for _ in range(8)]
        + [pltpu.VMEM((nb, HG_WIDTH), F32)],
        compiler_params=pltpu.CompilerParams(
            dimension_semantics=("arbitrary",), vmem_limit_bytes=VMEM_LIMIT),
        name="mix_sample",
    )(x, mod9, p["ln_mix_pre"], p["ln_mix_post"], p["w_in"], p["conv_w"], p["conv_b"],
      p["wa_bd"], p["b_a"], p["wx_bd"], p["b_x"], p["lam"], p["lb_logits"], p["hg_norm_w"],
      p["w_out"], h0, conv0, s0)


def _block_diag(w):
    heads, blk, _ = w.shape
    eye = jnp.eye(heads, dtype=w.dtype)
    return (eye[:, None, :, None] * w[:, :, None, :]).reshape(heads * blk, heads * blk)


def kernel(x_prompt, x_sample, c_prompt, c_sample, state_lru_h, state_lru_conv, state_hgrn_S, w_ada, b_ada, ln_ffn1_pre, ln_ffn1_post, ffn1_w_gate, ffn1_w_up, ffn1_w_down, ln_mix_pre, ln_mix_post, w_in, lru_conv_w, lru_conv_b, lru_w_a, lru_b_a, lru_w_x, lru_b_x, lru_lambda, hg_lb_logits, hg_norm_w, w_out, ln_ffn2_pre, ln_ffn2_post, ffn2_w_gate, ffn2_w_up, ffn2_w_down):
    depth = w_ada.shape[0]
    batch, seq_len, _ = x_prompt.shape
    n_seq = x_sample.shape[0]
    assert depth == 1 and x_sample.shape[1] == 1
    assert seq_len % FFN_TILE == 0 and seq_len % MIX_TILE == 0 and n_seq % SAMPLE_SEQ_BLOCK == 0
    assert D_PROJ == 6 * LRU_WIDTH and LRU_WIDTH == HG_WIDTH

    xp = x_prompt.reshape(batch * seq_len, D_MODEL)
    xs = x_sample.reshape(n_seq, D_MODEL)
    ph, pc, pS, sh, sc, sS = [], [], [], [], [], []
    for l in range(depth):
        row = lambda w: w[l].reshape(1, -1)
        mod9 = _ada(c_sample, c_prompt, w_ada[l], b_ada[l])
        mod_p = jnp.transpose(mod9[:, n_seq:], (1, 0, 2))
        p = {
            "ln_mix_pre": row(ln_mix_pre), "ln_mix_post": row(ln_mix_post),
            "w_in": w_in[l], "w_out": w_out[l],
            "conv_w": lru_conv_w[l], "conv_b": row(lru_conv_b),
            "wa_bd": _block_diag(lru_w_a[l]).astype(BF16), "b_a": row(lru_b_a),
            "wx_bd": _block_diag(lru_w_x[l]).astype(BF16), "b_x": row(lru_b_x),
            "lam": row(lru_lambda), "lb_logits": hg_lb_logits,
            "hg_norm_w": row(hg_norm_w),
        }
        f1 = (row(ln_ffn1_pre), row(ln_ffn1_post), ffn1_w_gate[l], ffn1_w_up[l], ffn1_w_down[l])
        f2 = (row(ln_ffn2_pre), row(ln_ffn2_post), ffn2_w_gate[l], ffn2_w_up[l], ffn2_w_down[l])

        xp, xs = _ffn(xp, xs, mod_p[:, 0:3], mod9, 0, *f1, seq_len)
        xp, h_p, c_p, S_p, w_in_bf, w_out_bf = _mix_prompt(
            xp.reshape(batch, seq_len, D_MODEL), mod_p[:, 3:6], p)
        xp = xp.reshape(batch * seq_len, D_MODEL)
        conv0 = jnp.transpose(state_lru_conv[l], (1, 0, 2))
        xs, h_s, c_s, S_s = _mix_sample(xs, mod9, dict(p, w_in=w_in_bf, w_out=w_out_bf),
                                        state_lru_h[l], conv0, state_hgrn_S[l])
        xp, xs = _ffn(xp, xs, mod_p[:, 6:9], mod9, 2, *f2, seq_len)

        ph.append(h_p.reshape(batch, LRU_WIDTH)); pc.append(c_p); pS.append(S_p)
        sh.append(h_s); sc.append(jnp.transpose(c_s, (1, 0, 2))); sS.append(S_s)

    return (xp.reshape(batch, seq_len, D_MODEL), xs.reshape(n_seq, 1, D_MODEL),
            jnp.stack(ph), jnp.stack(pc), jnp.stack(pS), jnp.stack(sh), jnp.stack(sc), jnp.stack(sS))
```

```python
import jax
import jax.numpy as jnp
from jax import lax
from jax.experimental import pallas as pl
from jax.experimental.pallas import tpu as pltpu

F32 = jnp.float32
BF16 = jnp.bfloat16

D_MODEL = 1024
D_FF = 2816
LRU_WIDTH = 512
CONV_WIDTH = 4
LRU_C = 8.0
HG_WIDTH = 512
HG_HEAD_DIM = 128
HG_HEADS = HG_WIDTH // HG_HEAD_DIM
HG_CHUNK = 64
N_MOD = 9
D_PROJ = 2 * LRU_WIDTH + 4 * HG_WIDTH
EPS = 1e-6

V7X_SUBLANES = 8
V7X_VMEM_BYTES = 64 * 1024 * 1024
VMEM_LIMIT = V7X_VMEM_BYTES - 8 * 1024 * 1024

FFN_TILE = 1024
FFN_ROWS = 512
MIX_TILE = 512
MIX_ROWS = 256
HG_DIRECT_ROWS = 256
FFN_WBLOCK = 256
FFN_WSTEPS = D_FF // FFN_WBLOCK
SAMPLE_SEQ_BLOCK = 2 * V7X_SUBLANES
SAMPLE_IN_SLOTS = 3
SAMPLE_OUT_SLOTS = 2

HG_LEVELS = (1, 2, 4, 8, 16, 32)
HG_DIRECT_MIN = 2.0 ** -120


def _rms(x, w):
    return (x * lax.rsqrt(jnp.mean(x * x, axis=-1, keepdims=True) + EPS)) * w


def _sigmoid(x):
    return 1.0 / (1.0 + jnp.exp(-x))


def _silu(x):
    return x * _sigmoid(x)


def _gelu_tanh(x):
    c = 0.7978845608028654
    return x * (0.5 * (1.0 + jnp.tanh(c * (x + 0.044715 * (x * x * x)))))


def _softplus(z):
    return jnp.maximum(z, 0.0) + jnp.log1p(jnp.exp(-jnp.abs(z)))


def _dot(a, b):
    return jnp.dot(a, b, preferred_element_type=F32)


def _dot_nt(a, b):
    return lax.dot_general(a, b, (((1,), (1,)), ((), ())), preferred_element_type=F32)


def _dot_tn(a, b):
    return lax.dot_general(a, b, (((0,), (0,)), ((), ())), preferred_element_type=F32)


def _const_spec(shape):
    nd = len(shape)
    return pl.BlockSpec(shape, lambda *_: (0,) * nd, pipeline_mode=pl.Buffered(1))


def _ada_kernel(cs_ref, cp_ref, w_ref, b_ref, o_ref):
    s = _silu(jnp.concatenate([cs_ref[...], cp_ref[...]], axis=0))
    o_ref[0] = _dot(s.astype(BF16), w_ref[...].astype(BF16)) + b_ref[...]


def _ada(c_sample, c_prompt, w_ada, b_ada):
    rows = c_sample.shape[0] + c_prompt.shape[0]
    return pl.pallas_call(
        _ada_kernel,
        grid=(N_MOD,),
        in_specs=[
            _const_spec(c_sample.shape),
            _const_spec(c_prompt.shape),
            pl.BlockSpec((D_MODEL, D_MODEL), lambda j: (0, j)),
            pl.BlockSpec((1, D_MODEL), lambda j: (0, j)),
        ],
        out_specs=pl.BlockSpec((1, rows, D_MODEL), lambda j: (j, 0, 0)),
        out_shape=jax.ShapeDtypeStruct((N_MOD, rows, D_MODEL), F32),
        compiler_params=pltpu.CompilerParams(
            dimension_semantics=("arbitrary",), vmem_limit_bytes=VMEM_LIMIT),
        name="ada_mod",
    )(c_sample, c_prompt, w_ada, b_ada.reshape(1, N_MOD * D_MODEL))


def _ffn_body(x, shift, scale, gate, pre_w, post_w, wg_ref, wu_ref, wd_ref):
    h = _rms(x, pre_w) * (1.0 + scale) + shift
    hb = h.astype(BF16)
    a = _dot(hb, wg_ref[...])
    u = _dot(hb, wu_ref[...])
    act = (_silu(a) * u).astype(BF16)
    y = _dot(act, wd_ref[...])
    return x + (0.5 * gate) * _rms(y, post_w)


def _ffn_kernel(xp_ref, mp_ref, xs_ref, ms_ref, pre_ref, post_ref, wg32_ref, wu32_ref, wd32_ref,
                op_ref, os_ref, wg_ref, wu_ref, wd_ref):
    s = pl.program_id(0)
    n_steps = pl.num_programs(0)

    for j in range(FFN_WSTEPS):
        @pl.when(s == j)
        def _(j=j):
            c0 = j * FFN_WBLOCK
            wg_ref[:, c0:c0 + FFN_WBLOCK] = wg32_ref[...].astype(BF16)
            wu_ref[:, c0:c0 + FFN_WBLOCK] = wu32_ref[...].astype(BF16)
            wd_ref[c0:c0 + FFN_WBLOCK, :] = wd32_ref[...].astype(BF16)

    @pl.when((s >= FFN_WSTEPS) & (s < n_steps - 1))
    def _():
        m = mp_ref[0]
        for r0 in range(0, FFN_TILE, FFN_ROWS):
            op_ref[r0:r0 + FFN_ROWS, :] = _ffn_body(
                xp_ref[r0:r0 + FFN_ROWS, :], m[0:1], m[1:2], m[2:3],
                pre_ref[...], post_ref[...], wg_ref, wu_ref, wd_ref)

    @pl.when(s == n_steps - 1)
    def _():
        os_ref[...] = _ffn_body(xs_ref[...], ms_ref[0], ms_ref[1], ms_ref[2], pre_ref[...],
                                post_ref[...], wg_ref, wu_ref, wd_ref)


def _ffn(xp, xs, mod_p3, mod9, sub_layer, pre_w, post_w, wg, wu, wd, seq_len):
    n_tok = xp.shape[0]
    n_seq = xs.shape[0]
    n_tiles = n_tok // FFN_TILE
    steps_per_seq = seq_len // FFN_TILE
    tile = lambda s: jnp.clip(s - FFN_WSTEPS, 0, n_tiles - 1)
    wblk = lambda s: jnp.minimum(s, FFN_WSTEPS - 1)
    tok_spec = pl.BlockSpec((FFN_TILE, D_MODEL), lambda s: (tile(s), 0))
    return pl.pallas_call(
        _ffn_kernel,
        grid=(FFN_WSTEPS + n_tiles + 1,),
        in_specs=[
            tok_spec,
            pl.BlockSpec((1, 3, D_MODEL), lambda s: (tile(s) // steps_per_seq, 0, 0)),
            _const_spec((n_seq, D_MODEL)),
            pl.BlockSpec((3, n_seq, D_MODEL), lambda s: (sub_layer, 0, 0), pipeline_mode=pl.Buffered(1)),
            _const_spec((1, D_MODEL)),
            _const_spec((1, D_MODEL)),
            pl.BlockSpec((D_MODEL, FFN_WBLOCK), lambda s: (0, wblk(s))),
            pl.BlockSpec((D_MODEL, FFN_WBLOCK), lambda s: (0, wblk(s))),
            pl.BlockSpec((FFN_WBLOCK, D_MODEL), lambda s: (wblk(s), 0)),
        ],
        out_specs=[tok_spec, pl.BlockSpec((n_seq, D_MODEL), lambda s: (0, 0))],
        out_shape=[jax.ShapeDtypeStruct((n_tok, D_MODEL), F32),
                   jax.ShapeDtypeStruct((n_seq, D_MODEL), F32)],
        scratch_shapes=[pltpu.VMEM((D_MODEL, D_FF), BF16), pltpu.VMEM((D_MODEL, D_FF), BF16),
                        pltpu.VMEM((D_FF, D_MODEL), BF16)],
        compiler_params=pltpu.CompilerParams(
            dimension_semantics=("arbitrary",), vmem_limit_bytes=VMEM_LIMIT),
        name="ffn",
    )(xp, mod_p3, xs, mod9, pre_w, post_w, wg, wu, wd)


def _lower_bound(lb_logits):
    z = lb_logits - jnp.max(lb_logits, axis=0, keepdims=True)
    e = jnp.exp(z)
    return e[0:1] / jnp.sum(e, axis=0, keepdims=True)


def _lru_gates(u_conv, wa_ref, ba, wx_ref, bx, lam):
    ub = u_conv.astype(BF16)
    r = _sigmoid(_dot(ub, wa_ref[...]) + ba)
    ig = _sigmoid(_dot(ub, wx_ref[...]) + bx)
    log_a = (-LRU_C * r) * _softplus(-lam)
    a = jnp.exp(log_a)
    th = jnp.tanh(log_a)
    mult = jnp.sqrt((-2.0 * th) / (1.0 - th))
    return a, mult, ig


def _head_rms_gate(o, norm_w, g):
    return _rms(o, norm_w) * _silu(g)


def _hgrn_chunk_head(scores, q_in, k_out, dec, vb_blk, g_blk, hgw, st_ref, hd, ohg_ref, r0, l0):
    st = st_ref[hd]
    o = _dot_nt(q_in, st.astype(BF16)) + _dot(scores.astype(BF16), vb_blk)
    st_ref[hd] = st * dec + _dot_tn(vb_blk, k_out)
    ohg_ref[r0:r0 + HG_CHUNK, l0:l0 + HG_HEAD_DIM] = _head_rms_gate(o, hgw, g_blk)


def _hgrn_chunks_direct(qs, kk, pfx, tails, vb, g, hgw, st_ref, ohg_ref, r0):
    tb = qs.shape[0]
    n_chunk = len(tails)
    inv = 1.0 / pfx
    q_dec = qs * pfx
    k_dec = kk * inv
    q_in = q_dec.astype(BF16)
    k_in = k_dec.astype(BF16)
    ti = lax.broadcasted_iota(jnp.int32, (tb, tb), 0)
    si = lax.broadcasted_iota(jnp.int32, (tb, tb), 1)
    visible = (ti >= si) & ((ti ^ si) < HG_CHUNK)
    row_chunk = jnp.right_shift(lax.broadcasted_iota(jnp.int32, (tb, HG_HEAD_DIM), 0),
                                HG_CHUNK.bit_length() - 1)
    tail_rows = jnp.concatenate([jnp.broadcast_to(t, (HG_CHUNK, t.shape[-1])) for t in tails], axis=0)
    k_end = k_dec * tail_rows
    for hd in range(HG_HEADS):
        l0 = hd * HG_HEAD_DIM
        head = lambda z: z[:, l0:l0 + HG_HEAD_DIM]
        by_chunk = lambda z: jnp.concatenate(
            [jnp.where(row_chunk == c, head(z), 0.0) for c in range(n_chunk)], axis=-1).astype(BF16)
        scores = jnp.where(visible, _dot_nt(head(q_in), head(k_in)), 0.0)
        grown = _dot_tn(head(vb), by_chunk(k_end))
        st = st_ref[hd]
        states = []
        for c, tail in enumerate(tails):
            states.append(st.astype(BF16))
            st = st * head(tail) + grown[:, c * HG_HEAD_DIM:(c + 1) * HG_HEAD_DIM]
        st_ref[hd] = st
        o = (_dot_nt(by_chunk(q_dec), jnp.concatenate(states, axis=-1))
             + _dot(scores.astype(BF16), head(vb)))
        ohg_ref[r0:r0 + tb, l0:l0 + HG_HEAD_DIM] = _head_rms_gate(o, hgw, head(g))


def _hgrn_chunks_levels(f3, kk3, qs3, vb, g, hgw, st_ref, ohg_ref):
    n_grp = f3.shape[0]
    tb = n_grp * V7X_SUBLANES
    sub = lax.broadcasted_iota(jnp.int32, f3.shape, 1)
    grp = lax.broadcasted_iota(jnp.int32, f3.shape, 0)
    to_mxu = lambda z: z.reshape(tb, z.shape[-1]).astype(BF16)
    q_lv = [to_mxu(qs3)]
    k_lv = [to_mxu(kk3)]
    tot, pre, post = f3, f3, None
    for lvl in HG_LEVELS:
        q_lv.append(to_mxu(qs3 * pre))
        k_lv.append(k_lv[0] if post is None else to_mxu(kk3 * post))
        if lvl < V7X_SUBLANES:
            upper = (sub & lvl) != 0
            below = pltpu.roll(tot, lvl, 1)
            above = pltpu.roll(tot, V7X_SUBLANES - lvl, 1)
        else:
            upper = (grp & (lvl // V7X_SUBLANES)) != 0
            below = jnp.roll(tot, lvl // V7X_SUBLANES, axis=0)
            above = jnp.roll(tot, -(lvl // V7X_SUBLANES), axis=0)
        pre = jnp.where(upper, pre * below, pre)
        post = jnp.where(upper, 1.0, above) if post is None else jnp.where(upper, post, post * above)
        tot = tot * jnp.where(upper, below, above)
    q_in = to_mxu(qs3 * pre)
    k_out = to_mxu(kk3 * post)
    s_decay = tot.reshape(tb, tot.shape[-1])

    ti = lax.broadcasted_iota(jnp.int32, (HG_CHUNK, HG_CHUNK), 0)
    si = lax.broadcasted_iota(jnp.int32, (HG_CHUNK, HG_CHUNK), 1)
    masks = [ti == si]
    tx = ti ^ si
    for lvl in HG_LEVELS:
        masks.append((tx >= lvl) & (tx < 2 * lvl) & ((ti & lvl) != 0))

    for c in range(tb // HG_CHUNK):
        r0 = c * HG_CHUNK
        for hd in range(HG_HEADS):
            l0 = hd * HG_HEAD_DIM
            blk = lambda z: z[r0:r0 + HG_CHUNK, l0:l0 + HG_HEAD_DIM]
            scores = jnp.zeros((HG_CHUNK, HG_CHUNK), F32)
            for ql, kl, msk in zip(q_lv, k_lv, masks):
                scores = jnp.where(msk, _dot_nt(blk(ql), blk(kl)), scores)
            dec = s_decay[r0 + HG_CHUNK - 1:r0 + HG_CHUNK, l0:l0 + HG_HEAD_DIM]
            _hgrn_chunk_head(scores, blk(q_in), blk(k_out), dec, blk(vb), blk(g), hgw,
                             st_ref, hd, ohg_ref, r0, l0)

def _mix_tile(x_ref, m_ref, o_ref, t, pre_ref, post_ref, win_ref, cw_ref, cb_ref, wa_ref, ba_ref,
              wx_ref, bx_ref, lam_ref, lbl_ref, hgw_ref, wout_ref, ucar_ref, hcar_ref, st_ref, ohg_ref):
    tb = MIX_TILE
    x = x_ref[...]
    m = m_ref[...]
    per_block = []
    for r0 in range(0, tb, MIX_ROWS):
        hb = (_rms(x[r0:r0 + MIX_ROWS], pre_ref[...]) * (1.0 + m[1:2]) + m[0:1]).astype(BF16)
        proj = _dot(hb, win_ref[...])
        per_block.append([proj[:, c0:c0 + LRU_WIDTH] for c0 in range(0, D_PROJ, LRU_WIDTH)])
    u, y_lru, q, f_raw, v, g = [jnp.concatenate(seg, axis=0) for seg in zip(*per_block)]

    n_grp = tb // V7X_SUBLANES
    grouped = lambda z: z.reshape(n_grp, V7X_SUBLANES, z.shape[-1])
    flat = lambda z: z.reshape(tb, z.shape[-1])
    gshape = (n_grp, V7X_SUBLANES, LRU_WIDTH)
    sub = lax.broadcasted_iota(jnp.int32, gshape, 1)
    grp = lax.broadcasted_iota(jnp.int32, gshape, 0)

    u3 = grouped(u)
    u_all = jnp.concatenate([ucar_ref[...][None], u3], axis=0)
    ucar_ref[...] = u3[n_grp - 1]
    cw = cw_ref[...]
    u_conv = cb_ref[...]
    for k in range(CONV_WIDTH - 1):
        back = CONV_WIDTH - 1 - k
        rot = pltpu.roll(u_all, back, 1)
        u_conv = u_conv + jnp.where(sub >= back, rot[1:], rot[:-1]) * cw[k:k + 1]
    u_conv = flat(u_conv + u3 * cw[CONV_WIDTH - 1:CONV_WIDTH])

    a, mult, ig = _lru_gates(u_conv, wa_ref, ba_ref[...], wx_ref, bx_ref[...], lam_ref[...])
    a3, mult3 = grouped(a), grouped(mult)
    first_row = jnp.where(t == 0, 0, -1)
    mult3 = jnp.where(grp * V7X_SUBLANES + sub == first_row, 1.0, mult3)
    b3 = (mult3 * grouped(ig)) * grouped(u_conv)
    s = 1
    while s < V7X_SUBLANES:
        keep = sub >= s
        b3 = jnp.where(keep, a3 * pltpu.roll(b3, s, 1) + b3, b3)
        a3 = jnp.where(keep, a3 * pltpu.roll(a3, s, 1), a3)
        s *= 2
    carry = hcar_ref[...]
    hs = []
    for gi in range(n_grp):
        h_g = a3[gi] * carry + b3[gi]
        carry = h_g[V7X_SUBLANES - 1:V7X_SUBLANES]
        hs.append(h_g)
    hcar_ref[...] = carry
    o_lru = jnp.concatenate(hs, axis=0) * _gelu_tanh(y_lru)

    lb = _lower_bound(lbl_ref[...])
    f3 = grouped(lb + (1.0 - lb) * _sigmoid(f_raw))
    kk3 = 1.0 - f3
    qs3 = grouped(q * (HG_HEAD_DIM ** -0.5))
    vb = v.astype(BF16)
    hgw = hgw_ref[...]

    pfx = f3
    s = 1
    while s < V7X_SUBLANES:
        pfx = jnp.where(sub >= s, pfx * pltpu.roll(pfx, s, 1), pfx)
        s *= 2
    grp_per_chunk = HG_CHUNK // V7X_SUBLANES
    rows, tails = [], []
    for gi in range(n_grp):
        p_g = pfx[gi] if gi % grp_per_chunk == 0 else pfx[gi] * carry
        carry = p_g[V7X_SUBLANES - 1:V7X_SUBLANES]
        rows.append(p_g)
        if gi % grp_per_chunk == grp_per_chunk - 1:
            tails.append(carry)
    pfx = jnp.concatenate(rows, axis=0)
    direct = jnp.min(jnp.concatenate(tails, axis=0)) >= HG_DIRECT_MIN

    @pl.when(direct)
    def _():
        qs, kk = flat(qs3), flat(kk3)
        per_block = HG_DIRECT_ROWS // HG_CHUNK
        for bi, r0 in enumerate(range(0, tb, HG_DIRECT_ROWS)):
            rows = lambda z: z[r0:r0 + HG_DIRECT_ROWS]
            _hgrn_chunks_direct(rows(qs), rows(kk), rows(pfx), tails[bi * per_block:(bi + 1) * per_block],
                                rows(vb), rows(g), hgw, st_ref, ohg_ref, r0)

    @pl.when(jnp.logical_not(direct))
    def _():
        _hgrn_chunks_levels(f3, kk3, qs3, vb, g, hgw, st_ref, ohg_ref)

    mix_in = jnp.concatenate([o_lru, ohg_ref[...]], axis=-1).astype(BF16)
    for r0 in range(0, tb, MIX_ROWS):
        mix = _dot(mix_in[r0:r0 + MIX_ROWS], wout_ref[...])
        o_ref[r0:r0 + MIX_ROWS, :] = x[r0:r0 + MIX_ROWS] + m[2:3] * _rms(mix, post_ref[...])


def _mix_prompt_kernel(x_ref, m_ref, *refs):
    (pre_ref, post_ref, win32_ref, cw_ref, cb_ref, wa_ref, ba_ref, wx_ref, bx_ref, lam_ref, lbl_ref,
     hgw_ref, wout32_ref) = refs[:13]
    (o_ref, h_out_ref, conv_out_ref, s_out_ref, win_ref, wout_ref,
     ucar_ref, hcar_ref, st_ref, ohg_ref) = refs[13:]
    t = pl.program_id(1)
    n_t = pl.num_programs(1)

    @pl.when((pl.program_id(0) == 0) & (t == 0))
    def _():
        win_ref[...] = win32_ref[...].astype(BF16)
        wout_ref[...] = wout32_ref[...].astype(BF16)

    @pl.when(t == 0)
    def _():
        ucar_ref[...] = jnp.zeros_like(ucar_ref)
        hcar_ref[...] = jnp.zeros_like(hcar_ref)
        st_ref[...] = jnp.zeros_like(st_ref)

    _mix_tile(x_ref.at[0], m_ref.at[0], o_ref.at[0], t, pre_ref, post_ref, win_ref, cw_ref, cb_ref,
              wa_ref, ba_ref, wx_ref, bx_ref, lam_ref, lbl_ref, hgw_ref, wout_ref,
              ucar_ref, hcar_ref, st_ref, ohg_ref)

    @pl.when(t == n_t - 1)
    def _():
        h_out_ref[0] = hcar_ref[...]
        conv_out_ref[0] = ucar_ref[V7X_SUBLANES - (CONV_WIDTH - 1):V7X_SUBLANES, :]
        for hd in range(HG_HEADS):
            s_out_ref[0, hd] = st_ref[hd].T


def _mix_prompt(x, mod3, p):
    batch, seq_len, _ = x.shape
    n_t = seq_len // MIX_TILE
    tok_spec = pl.BlockSpec((1, MIX_TILE, D_MODEL), lambda b, t: (b, t, 0))
    return pl.pallas_call(
        _mix_prompt_kernel,
        grid=(batch, n_t),
        in_specs=[
            tok_spec,
            pl.BlockSpec((1, 3, D_MODEL), lambda b, t: (b, 0, 0)),
            _const_spec((1, D_MODEL)),
            _const_spec((1, D_MODEL)),
            _const_spec((D_MODEL, D_PROJ)),
            _const_spec((CONV_WIDTH, LRU_WIDTH)),
            _const_spec((1, LRU_WIDTH)),
            _const_spec((LRU_WIDTH, LRU_WIDTH)),
            _const_spec((1, LRU_WIDTH)),
            _const_spec((LRU_WIDTH, LRU_WIDTH)),
            _const_spec((1, LRU_WIDTH)),
            _const_spec((1, LRU_WIDTH)),
            _const_spec(p["lb_logits"].shape),
            _const_spec((1, HG_HEAD_DIM)),
            _const_spec((D_MODEL, D_MODEL)),
        ],
        out_specs=[
            tok_spec,
            pl.BlockSpec((1, 1, LRU_WIDTH), lambda b, t: (b, 0, 0)),
            pl.BlockSpec((1, CONV_WIDTH - 1, LRU_WIDTH), lambda b, t: (b, 0, 0)),
            pl.BlockSpec((1, HG_HEADS, HG_HEAD_DIM, HG_HEAD_DIM), lambda b, t: (b, 0, 0, 0)),
            _const_spec((D_MODEL, D_PROJ)),
            _const_spec((D_MODEL, D_MODEL)),
        ],
        out_shape=[
            jax.ShapeDtypeStruct((batch, seq_len, D_MODEL), F32),
            jax.ShapeDtypeStruct((batch, 1, LRU_WIDTH), F32),
            jax.ShapeDtypeStruct((batch, CONV_WIDTH - 1, LRU_WIDTH), F32),
            jax.ShapeDtypeStruct((batch, HG_HEADS, HG_HEAD_DIM, HG_HEAD_DIM), F32),
            jax.ShapeDtypeStruct((D_MODEL, D_PROJ), BF16),
            jax.ShapeDtypeStruct((D_MODEL, D_MODEL), BF16),
        ],
        scratch_shapes=[
            pltpu.VMEM((V7X_SUBLANES, LRU_WIDTH), F32),
            pltpu.VMEM((1, LRU_WIDTH), F32),
            pltpu.VMEM((HG_HEADS, HG_HEAD_DIM, HG_HEAD_DIM), F32),
            pltpu.VMEM((MIX_TILE, HG_WIDTH), F32),
        ],
        compiler_params=pltpu.CompilerParams(
            dimension_semantics=("arbitrary", "arbitrary"), vmem_limit_bytes=VMEM_LIMIT),
        name="mix_prompt",
    )(x, mod3, p["ln_mix_pre"], p["ln_mix_post"], p["w_in"], p["conv_w"], p["conv_b"],
      p["wa_bd"], p["b_a"], p["wx_bd"], p["b_x"], p["lam"], p["lb_logits"], p["hg_norm_w"],
      p["w_out"])


def _mix_sample_kernel(x_ref, m_ref, pre_ref, post_ref, win_ref, cw_ref, cb_ref,
                       wa_ref, ba_ref, wx_ref, bx_ref, lam_ref, lbl_ref, hgw_ref, wout_ref,
                       h0_ref, conv0_ref, s0_hbm,
                       o_ref, h_out_ref, conv_out_ref, s_out_hbm,
                       qf_ref, f_ref, k_ref, v_ref, g_ref, qk_ref, olru_ref, ohg_ref, ostage_ref,
                       sin_ref, sout_ref, in_sem, out_sem):
    i = pl.program_id(0)
    n_i = pl.num_programs(0)
    nb = SAMPLE_SEQ_BLOCK

    def read_block(b):
        slot = b % SAMPLE_IN_SLOTS
        return pltpu.make_async_copy(s0_hbm.at[pl.ds(b * nb, nb)], sin_ref.at[slot], in_sem.at[slot])

    def write_block(b):
        slot = b % SAMPLE_OUT_SLOTS
        return pltpu.make_async_copy(sout_ref.at[slot], s_out_hbm.at[pl.ds(b * nb, nb)],
                                     out_sem.at[slot])

    @pl.when(i == 0)
    def _():
        for b in range(SAMPLE_IN_SLOTS - 1):
            read_block(b).start()

    @pl.when(i + SAMPLE_IN_SLOTS - 1 < n_i)
    def _():
        read_block(i + SAMPLE_IN_SLOTS - 1).start()

    @pl.when(i == 0)
    def _():
        x = x_ref[...]
        h = _rms(x, pre_ref[...]) * (1.0 + m_ref[1]) + m_ref[0]
        proj = _dot(h.astype(BF16), win_ref[...])
        u = proj[:, 0:LRU_WIDTH]
        y_lru = proj[:, LRU_WIDTH:2 * LRU_WIDTH]
        o0 = 2 * LRU_WIDTH
        q = proj[:, o0:o0 + HG_WIDTH]
        f_raw = proj[:, o0 + HG_WIDTH:o0 + 2 * HG_WIDTH]
        v = proj[:, o0 + 2 * HG_WIDTH:o0 + 3 * HG_WIDTH]
        g_ref[...] = proj[:, o0 + 3 * HG_WIDTH:o0 + 4 * HG_WIDTH]

        cw = cw_ref[...]
        u_conv = cb_ref[...]
        for k in range(CONV_WIDTH - 1):
            u_conv = u_conv + conv0_ref[k] * cw[k:k + 1]
            if k > 0:
                conv_out_ref[k - 1] = conv0_ref[k]
        u_conv = u_conv + u * cw[CONV_WIDTH - 1:CONV_WIDTH]
        conv_out_ref[CONV_WIDTH - 2] = u

        a, mult, ig = _lru_gates(u_conv, wa_ref, ba_ref[...], wx_ref, bx_ref[...], lam_ref[...])
        hs = (mult * ig) * u_conv + a * h0_ref[...]
        h_out_ref[...] = hs
        olru_ref[...] = hs * _gelu_tanh(y_lru)

        lb = _lower_bound(lbl_ref[...])
        f = lb + (1.0 - lb) * _sigmoid(f_raw)
        kk = 1.0 - f
        qs = q * (HG_HEAD_DIM ** -0.5)
        f_ref[...] = f
        qf_ref[...] = qs * f
        k_ref[...] = kk
        v_ref[...] = v
        qk = qs * kk
        for hd in range(HG_HEADS):
            l0 = hd * HG_HEAD_DIM
            tot = jnp.sum(qk[:, l0:l0 + HG_HEAD_DIM], axis=-1, keepdims=True)
            qk_ref[:, l0:l0 + HG_HEAD_DIM] = jnp.broadcast_to(tot, (qk.shape[0], HG_HEAD_DIM))

    base = pl.multiple_of(i * nb, nb)
    grp = lambda ref: ref[pl.ds(base, nb), :]
    f_g, k_g, v_g, qf_g, qk_g = grp(f_ref), grp(k_ref), grp(v_ref), grp(qf_ref), grp(qk_ref)
    square = (HG_HEAD_DIM, HG_HEAD_DIM)
    read_block(i).wait()

    @pl.when(i >= SAMPLE_OUT_SLOTS)
    def _():
        write_block(i - SAMPLE_OUT_SLOTS).wait()

    s0_ref = sin_ref.at[i % SAMPLE_IN_SLOTS]
    s_out_ref = sout_ref.at[i % SAMPLE_OUT_SLOTS]
    for j in range(nb):
        for hd in range(HG_HEADS):
            l0 = hd * HG_HEAD_DIM
            rowv = lambda z: z[j:j + 1, l0:l0 + HG_HEAD_DIM]
            s_old = s0_ref[j, hd]
            f_col = jnp.broadcast_to(rowv(f_g), square).T
            k_col = jnp.broadcast_to(rowv(k_g), square).T
            v_row = rowv(v_g)
            outer = k_col.astype(BF16).astype(F32) * v_row.astype(BF16).astype(F32)
            s_out_ref[j, hd] = f_col * s_old + outer
            qf8 = jnp.broadcast_to(rowv(qf_g), (V7X_SUBLANES, HG_HEAD_DIM)).astype(BF16)
            o1 = _dot(qf8, s_old.astype(BF16))[0:1]
            ostage_ref[j:j + 1, l0:l0 + HG_HEAD_DIM] = o1 + rowv(qk_g) * v_row
    ohg_ref[pl.ds(base, nb), :] = ostage_ref[...]
    write_block(i).start()

    @pl.when(i == n_i - 1)
    def _():
        for back in reversed(range(SAMPLE_OUT_SLOTS)):
            write_block(i - back).wait()
        hgw = hgw_ref[...]
        parts = [olru_ref[...]]
        for hd in range(HG_HEADS):
            l0 = hd * HG_HEAD_DIM
            parts.append(_head_rms_gate(ohg_ref[:, l0:l0 + HG_HEAD_DIM], hgw,
                                        g_ref[:, l0:l0 + HG_HEAD_DIM]))
        mix_in = jnp.concatenate(parts, axis=-1).astype(BF16)
        mix = _dot(mix_in, wout_ref[...])
        o_ref[...] = x_ref[...] + m_ref[2] * _rms(mix, post_ref[...])


def _mix_sample(x, mod9, p, h0, conv0, s0):
    n_seq = x.shape[0]
    nb = SAMPLE_SEQ_BLOCK
    assert n_seq // nb >= SAMPLE_IN_SLOTS
    state_spec = pl.BlockSpec(memory_space=pl.ANY)
    state_block = (nb, HG_HEADS, HG_HEAD_DIM, HG_HEAD_DIM)
    full2 = lambda shape: pl.BlockSpec(shape, lambda i: (0,) * len(shape))
    return pl.pallas_call(
        _mix_sample_kernel,
        grid=(n_seq // nb,),
        in_specs=[
            _const_spec((n_seq, D_MODEL)),
            pl.BlockSpec((3, n_seq, D_MODEL), lambda i: (1, 0, 0), pipeline_mode=pl.Buffered(1)),
            _const_spec((1, D_MODEL)),
            _const_spec((1, D_MODEL)),
            _const_spec((D_MODEL, D_PROJ)),
            _const_spec((CONV_WIDTH, LRU_WIDTH)),
            _const_spec((1, LRU_WIDTH)),
            _const_spec((LRU_WIDTH, LRU_WIDTH)),
            _const_spec((1, LRU_WIDTH)),
            _const_spec((LRU_WIDTH, LRU_WIDTH)),
            _const_spec((1, LRU_WIDTH)),
            _const_spec((1, LRU_WIDTH)),
            _const_spec(p["lb_logits"].shape),
            _const_spec((1, HG_HEAD_DIM)),
            _const_spec((D_MODEL, D_MODEL)),
            _const_spec((n_seq, LRU_WIDTH)),
            _const_spec((CONV_WIDTH - 1, n_seq, LRU_WIDTH)),
            state_spec,
        ],
        out_specs=[
            full2((n_seq, D_MODEL)),
            full2((n_seq, LRU_WIDTH)),
            full2((CONV_WIDTH - 1, n_seq, LRU_WIDTH)),
            state_spec,
        ],
        out_shape=[
            jax.ShapeDtypeStruct((n_seq, D_MODEL), F32),
            jax.ShapeDtypeStruct((n_seq, LRU_WIDTH), F32),
            jax.ShapeDtypeStruct((CONV_WIDTH - 1, n_seq, LRU_WIDTH), F32),
            jax.ShapeDtypeStruct((n_seq, HG_HEADS, HG_HEAD_DIM, HG_HEAD_DIM), F32),
        ],
        scratch_shapes=[pltpu.VMEM((n_seq, HG_WIDTH), F32) for _ in range(8)]
        + [pltpu.VMEM((nb, HG_WIDTH), F32),
           pltpu.VMEM((SAMPLE_IN_SLOTS,) + state_block, F32),
           pltpu.VMEM((SAMPLE_OUT_SLOTS,) + state_block, F32),
           pltpu.SemaphoreType.DMA((SAMPLE_IN_SLOTS,)),
           pltpu.SemaphoreType.DMA((SAMPLE_OUT_SLOTS,))],
        compiler_params=pltpu.CompilerParams(
            dimension_semantics=("arbitrary",), vmem_limit_bytes=VMEM_LIMIT),
        name="mix_sample",
    )(x, mod9, p["ln_mix_pre"], p["ln_mix_post"], p["w_in"], p["conv_w"], p["conv_b"],
      p["wa_bd"], p["b_a"], p["wx_bd"], p["b_x"], p["lam"], p["lb_logits"], p["hg_norm_w"],
      p["w_out"], h0, conv0, s0)


def _block_diag(w):
    heads, blk, _ = w.shape
    eye = jnp.eye(heads, dtype=w.dtype)
    return (eye[:, None, :, None] * w[:, :, None, :]).reshape(heads * blk, heads * blk)


def kernel(x_prompt, x_sample, c_prompt, c_sample, state_lru_h, state_lru_conv, state_hgrn_S, w_ada, b_ada, ln_ffn1_pre, ln_ffn1_post, ffn1_w_gate, ffn1_w_up, ffn1_w_down, ln_mix_pre, ln_mix_post, w_in, lru_conv_w, lru_conv_b, lru_w_a, lru_b_a, lru_w_x, lru_b_x, lru_lambda, hg_lb_logits, hg_norm_w, w_out, ln_ffn2_pre, ln_ffn2_post, ffn2_w_gate, ffn2_w_up, ffn2_w_down):
    depth = w_ada.shape[0]
    batch, seq_len, _ = x_prompt.shape
    n_seq = x_sample.shape[0]
    assert depth == 1 and x_sample.shape[1] == 1
    assert seq_len % FFN_TILE == 0 and seq_len % MIX_TILE == 0 and n_seq % SAMPLE_SEQ_BLOCK == 0
    assert D_PROJ == 6 * LRU_WIDTH and LRU_WIDTH == HG_WIDTH

    xp = x_prompt.reshape(batch * seq_len, D_MODEL)
    xs = x_sample.reshape(n_seq, D_MODEL)
    ph, pc, pS, sh, sc, sS = [], [], [], [], [], []
    for l in range(depth):
        row = lambda w: w[l].reshape(1, -1)
        mod9 = _ada(c_sample, c_prompt, w_ada[l], b_ada[l])
        mod_p = jnp.transpose(mod9[:, n_seq:], (1, 0, 2))
        p = {
            "ln_mix_pre": row(ln_mix_pre), "ln_mix_post": row(ln_mix_post),
            "w_in": w_in[l], "w_out": w_out[l],
            "conv_w": lru_conv_w[l], "conv_b": row(lru_conv_b),
            "wa_bd": _block_diag(lru_w_a[l]).astype(BF16), "b_a": row(lru_b_a),
            "wx_bd": _block_diag(lru_w_x[l]).astype(BF16), "b_x": row(lru_b_x),
            "lam": row(lru_lambda), "lb_logits": hg_lb_logits,
            "hg_norm_w": row(hg_norm_w),
        }
        f1 = (row(ln_ffn1_pre), row(ln_ffn1_post), ffn1_w_gate[l], ffn1_w_up[l], ffn1_w_down[l])
        f2 = (row(ln_ffn2_pre), row(ln_ffn2_post), ffn2_w_gate[l], ffn2_w_up[l], ffn2_w_down[l])

        xp, xs = _ffn(xp, xs, mod_p[:, 0:3], mod9, 0, *f1, seq_len)
        xp, h_p, c_p, S_p, w_in_bf, w_out_bf = _mix_prompt(
            xp.reshape(batch, seq_len, D_MODEL), mod_p[:, 3:6], p)
        xp = xp.reshape(batch * seq_len, D_MODEL)
        conv0 = jnp.transpose(state_lru_conv[l], (1, 0, 2))
        xs, h_s, c_s, S_s = _mix_sample(xs, mod9, dict(p, w_in=w_in_bf, w_out=w_out_bf),
                                        state_lru_h[l], conv0, state_hgrn_S[l])
        xp, xs = _ffn(xp, xs, mod_p[:, 6:9], mod9, 2, *f2, seq_len)

        ph.append(h_p.reshape(batch, LRU_WIDTH)); pc.append(c_p); pS.append(S_p)
        sh.append(h_s); sc.append(jnp.transpose(c_s, (1, 0, 2))); sS.append(S_s)

    return (xp.reshape(batch, seq_len, D_MODEL), xs.reshape(n_seq, 1, D_MODEL),
            jnp.stack(ph), jnp.stack(pc), jnp.stack(pS), jnp.stack(sh), jnp.stack(sc), jnp.stack(sS))
```

```python
import jax
import jax.numpy as jnp
from jax import lax
from jax.experimental import pallas as pl
from jax.experimental.pallas import tpu as pltpu

F32 = jnp.float32
BF16 = jnp.bfloat16

D_MODEL = 1024
D_FF = 2816
LRU_WIDTH = 512
CONV_WIDTH = 4
LRU_C = 8.0
HG_WIDTH = 512
HG_HEAD_DIM = 128
HG_HEADS = HG_WIDTH // HG_HEAD_DIM
HG_CHUNK = 64
N_MOD = 9
D_PROJ = 2 * LRU_WIDTH + 4 * HG_WIDTH
EPS = 1e-6

V7X_SUBLANES = 8
V7X_VMEM_BYTES = 64 * 1024 * 1024
VMEM_LIMIT = V7X_VMEM_BYTES - 8 * 1024 * 1024

FFN_TILE = 1024
FFN_ROWS = 512
MIX_TILE = 512
MIX_ROWS = 256
HG_DIRECT_ROWS = 256
FFN_WBLOCK = 256
FFN_WSTEPS = D_FF // FFN_WBLOCK
FFN_WSLOTS = 3
SAMPLE_SEQ_BLOCK = 2 * V7X_SUBLANES
SAMPLE_IN_SLOTS = 3
SAMPLE_OUT_SLOTS = 2

HG_LEVELS = (1, 2, 4, 8, 16, 32)
HG_DIRECT_MIN = 2.0 ** -120


def _rms(x, w):
    return (x * lax.rsqrt(jnp.mean(x * x, axis=-1, keepdims=True) + EPS)) * w


def _sigmoid(x):
    return 1.0 / (1.0 + jnp.exp(-x))


def _silu(x):
    return x * _sigmoid(x)


def _gelu_tanh(x):
    c = 0.7978845608028654
    return x * (0.5 * (1.0 + jnp.tanh(c * (x + 0.044715 * (x * x * x)))))


def _softplus(z):
    return jnp.maximum(z, 0.0) + jnp.log1p(jnp.exp(-jnp.abs(z)))


def _dot(a, b):
    return jnp.dot(a, b, preferred_element_type=F32)


def _dot_nt(a, b):
    return lax.dot_general(a, b, (((1,), (1,)), ((), ())), preferred_element_type=F32)


def _dot_tn(a, b):
    return lax.dot_general(a, b, (((0,), (0,)), ((), ())), preferred_element_type=F32)


def _const_spec(shape):
    nd = len(shape)
    return pl.BlockSpec(shape, lambda *_: (0,) * nd, pipeline_mode=pl.Buffered(1))


def _ada_kernel(cs_ref, cp_ref, w_ref, b_ref, o_ref):
    s = _silu(jnp.concatenate([cs_ref[...], cp_ref[...]], axis=0))
    o_ref[0] = _dot(s.astype(BF16), w_ref[...].astype(BF16)) + b_ref[...]


def _ada(c_sample, c_prompt, w_ada, b_ada):
    rows = c_sample.shape[0] + c_prompt.shape[0]
    return pl.pallas_call(
        _ada_kernel,
        grid=(N_MOD,),
        in_specs=[
            _const_spec(c_sample.shape),
            _const_spec(c_prompt.shape),
            pl.BlockSpec((D_MODEL, D_MODEL), lambda j: (0, j)),
            pl.BlockSpec((1, D_MODEL), lambda j: (0, j)),
        ],
        out_specs=pl.BlockSpec((1, rows, D_MODEL), lambda j: (j, 0, 0)),
        out_shape=jax.ShapeDtypeStruct((N_MOD, rows, D_MODEL), F32),
        compiler_params=pltpu.CompilerParams(
            dimension_semantics=("arbitrary",), vmem_limit_bytes=VMEM_LIMIT),
        name="ada_mod",
    )(c_sample, c_prompt, w_ada, b_ada.reshape(1, N_MOD * D_MODEL))


def _ffn_body(x, shift, scale, gate, pre_w, post_w, wg_ref, wu_ref, wd_ref):
    h = _rms(x, pre_w) * (1.0 + scale) + shift
    hb = h.astype(BF16)
    a = _dot(hb, wg_ref[...])
    u = _dot(hb, wu_ref[...])
    act = (_silu(a) * u).astype(BF16)
    y = _dot(act, wd_ref[...])
    return x + (0.5 * gate) * _rms(y, post_w)


def _ffn_kernel(xp_ref, mp_ref, xs_ref, ms_ref, pre_ref, post_ref, wg32_ref, wu32_ref, wd32_ref,
                op_ref, os_ref, wg_ref, wu_ref, wd_ref, gring_ref, uring_ref, dring_ref, wsem):
    s = pl.program_id(0)
    n_steps = pl.num_programs(0)

    def weight_reads(j):
        slot, c0 = j % FFN_WSLOTS, j * FFN_WBLOCK
        return (pltpu.make_async_copy(wg32_ref.at[:, c0:c0 + FFN_WBLOCK], gring_ref.at[slot], wsem.at[0, slot]),
                pltpu.make_async_copy(wu32_ref.at[:, c0:c0 + FFN_WBLOCK], uring_ref.at[slot], wsem.at[1, slot]),
                pltpu.make_async_copy(wd32_ref.at[c0:c0 + FFN_WBLOCK, :], dring_ref.at[slot], wsem.at[2, slot]))

    for j in range(FFN_WSTEPS):
        @pl.when(s == j)
        def _(j=j):
            ahead = range(FFN_WSLOTS) if j == 0 else [j + FFN_WSLOTS - 1]
            for jj in ahead:
                if jj < FFN_WSTEPS:
                    for copy in weight_reads(jj):
                        copy.start()
            for copy in weight_reads(j):
                copy.wait()
            slot, c0 = j % FFN_WSLOTS, j * FFN_WBLOCK
            wg_ref[:, c0:c0 + FFN_WBLOCK] = gring_ref[slot].astype(BF16)
            wu_ref[:, c0:c0 + FFN_WBLOCK] = uring_ref[slot].astype(BF16)
            wd_ref[c0:c0 + FFN_WBLOCK, :] = dring_ref[slot].astype(BF16)

    @pl.when((s >= FFN_WSTEPS) & (s < n_steps - 1))
    def _():
        m = mp_ref[0]
        for r0 in range(0, FFN_TILE, FFN_ROWS):
            op_ref[r0:r0 + FFN_ROWS, :] = _ffn_body(
                xp_ref[r0:r0 + FFN_ROWS, :], m[0:1], m[1:2], m[2:3],
                pre_ref[...], post_ref[...], wg_ref, wu_ref, wd_ref)

    @pl.when(s == n_steps - 1)
    def _():
        os_ref[...] = _ffn_body(xs_ref[...], ms_ref[0], ms_ref[1], ms_ref[2], pre_ref[...],
                                post_ref[...], wg_ref, wu_ref, wd_ref)


def _ffn(xp, xs, mod_p3, mod9, sub_layer, pre_w, post_w, wg, wu, wd, seq_len):
    n_tok = xp.shape[0]
    n_seq = xs.shape[0]
    n_tiles = n_tok // FFN_TILE
    steps_per_seq = seq_len // FFN_TILE
    tile = lambda s: jnp.clip(s - FFN_WSTEPS, 0, n_tiles - 1)
    tok_spec = pl.BlockSpec((FFN_TILE, D_MODEL), lambda s: (tile(s), 0))
    return pl.pallas_call(
        _ffn_kernel,
        grid=(FFN_WSTEPS + n_tiles + 1,),
        in_specs=[
            tok_spec,
            pl.BlockSpec((1, 3, D_MODEL), lambda s: (tile(s) // steps_per_seq, 0, 0)),
            _const_spec((n_seq, D_MODEL)),
            pl.BlockSpec((3, n_seq, D_MODEL), lambda s: (sub_layer, 0, 0), pipeline_mode=pl.Buffered(1)),
            _const_spec((1, D_MODEL)),
            _const_spec((1, D_MODEL)),
            pl.BlockSpec(memory_space=pl.ANY),
            pl.BlockSpec(memory_space=pl.ANY),
            pl.BlockSpec(memory_space=pl.ANY),
        ],
        out_specs=[tok_spec, pl.BlockSpec((n_seq, D_MODEL), lambda s: (0, 0))],
        out_shape=[jax.ShapeDtypeStruct((n_tok, D_MODEL), F32),
                   jax.ShapeDtypeStruct((n_seq, D_MODEL), F32)],
        scratch_shapes=[pltpu.VMEM((D_MODEL, D_FF), BF16), pltpu.VMEM((D_MODEL, D_FF), BF16),
                        pltpu.VMEM((D_FF, D_MODEL), BF16),
                        pltpu.VMEM((FFN_WSLOTS, D_MODEL, FFN_WBLOCK), F32),
                        pltpu.VMEM((FFN_WSLOTS, D_MODEL, FFN_WBLOCK), F32),
                        pltpu.VMEM((FFN_WSLOTS, FFN_WBLOCK, D_MODEL), F32),
                        pltpu.SemaphoreType.DMA((3, FFN_WSLOTS))],
        compiler_params=pltpu.CompilerParams(
            dimension_semantics=("arbitrary",), vmem_limit_bytes=VMEM_LIMIT),
        name="ffn",
    )(xp, mod_p3, xs, mod9, pre_w, post_w, wg, wu, wd)


def _lower_bound(lb_logits):
    z = lb_logits - jnp.max(lb_logits, axis=0, keepdims=True)
    e = jnp.exp(z)
    return e[0:1] / jnp.sum(e, axis=0, keepdims=True)


def _lru_gates(u_conv, wa_ref, ba, wx_ref, bx, lam):
    ub = u_conv.astype(BF16)
    r = _sigmoid(_dot(ub, wa_ref[...]) + ba)
    ig = _sigmoid(_dot(ub, wx_ref[...]) + bx)
    log_a = (-LRU_C * r) * _softplus(-lam)
    a = jnp.exp(log_a)
    th = jnp.tanh(log_a)
    mult = jnp.sqrt((-2.0 * th) / (1.0 - th))
    return a, mult, ig


def _head_rms_gate(o, norm_w, g):
    return _rms(o, norm_w) * _silu(g)


def _hgrn_chunk_head(scores, q_in, k_out, dec, vb_blk, g_blk, hgw, st_ref, hd, ohg_ref, r0, l0):
    st = st_ref[hd]
    o = _dot_nt(q_in, st.astype(BF16)) + _dot(scores.astype(BF16), vb_blk)
    st_ref[hd] = st * dec + _dot_tn(vb_blk, k_out)
    ohg_ref[r0:r0 + HG_CHUNK, l0:l0 + HG_HEAD_DIM] = _head_rms_gate(o, hgw, g_blk)


def _hgrn_chunks_direct(qs, kk, pfx, tails, vb, g, hgw, st_ref, ohg_ref, r0):
    tb = qs.shape[0]
    n_chunk = len(tails)
    inv = 1.0 / pfx
    q_dec = qs * pfx
    k_dec = kk * inv
    q_in = q_dec.astype(BF16)
    k_in = k_dec.astype(BF16)
    ti = lax.broadcasted_iota(jnp.int32, (tb, tb), 0)
    si = lax.broadcasted_iota(jnp.int32, (tb, tb), 1)
    visible = (ti >= si) & ((ti ^ si) < HG_CHUNK)
    row_chunk = jnp.right_shift(lax.broadcasted_iota(jnp.int32, (tb, HG_HEAD_DIM), 0),
                                HG_CHUNK.bit_length() - 1)
    tail_rows = jnp.concatenate([jnp.broadcast_to(t, (HG_CHUNK, t.shape[-1])) for t in tails], axis=0)
    k_end = k_dec * tail_rows
    for hd in range(HG_HEADS):
        l0 = hd * HG_HEAD_DIM
        head = lambda z: z[:, l0:l0 + HG_HEAD_DIM]
        by_chunk = lambda z: jnp.concatenate(
            [jnp.where(row_chunk == c, head(z), 0.0) for c in range(n_chunk)], axis=-1).astype(BF16)
        scores = jnp.where(visible, _dot_nt(head(q_in), head(k_in)), 0.0)
        grown = _dot_tn(head(vb), by_chunk(k_end))
        st = st_ref[hd]
        states = []
        for c, tail in enumerate(tails):
            states.append(st.astype(BF16))
            st = st * head(tail) + grown[:, c * HG_HEAD_DIM:(c + 1) * HG_HEAD_DIM]
        st_ref[hd] = st
        o = (_dot_nt(by_chunk(q_dec), jnp.concatenate(states, axis=-1))
             + _dot(scores.astype(BF16), head(vb)))
        ohg_ref[r0:r0 + tb, l0:l0 + HG_HEAD_DIM] = _head_rms_gate(o, hgw, head(g))


def _hgrn_chunks_levels(f3, kk3, qs3, vb, g, hgw, st_ref, ohg_ref):
    n_grp = f3.shape[0]
    tb = n_grp * V7X_SUBLANES
    sub = lax.broadcasted_iota(jnp.int32, f3.shape, 1)
    grp = lax.broadcasted_iota(jnp.int32, f3.shape, 0)
    to_mxu = lambda z: z.reshape(tb, z.shape[-1]).astype(BF16)
    q_lv = [to_mxu(qs3)]
    k_lv = [to_mxu(kk3)]
    tot, pre, post = f3, f3, None
    for lvl in HG_LEVELS:
        q_lv.append(to_mxu(qs3 * pre))
        k_lv.append(k_lv[0] if post is None else to_mxu(kk3 * post))
        if lvl < V7X_SUBLANES:
            upper = (sub & lvl) != 0
            below = pltpu.roll(tot, lvl, 1)
            above = pltpu.roll(tot, V7X_SUBLANES - lvl, 1)
        else:
            upper = (grp & (lvl // V7X_SUBLANES)) != 0
            below = jnp.roll(tot, lvl // V7X_SUBLANES, axis=0)
            above = jnp.roll(tot, -(lvl // V7X_SUBLANES), axis=0)
        pre = jnp.where(upper, pre * below, pre)
        post = jnp.where(upper, 1.0, above) if post is None else jnp.where(upper, post, post * above)
        tot = tot * jnp.where(upper, below, above)
    q_in = to_mxu(qs3 * pre)
    k_out = to_mxu(kk3 * post)
    s_decay = tot.reshape(tb, tot.shape[-1])

    ti = lax.broadcasted_iota(jnp.int32, (HG_CHUNK, HG_CHUNK), 0)
    si = lax.broadcasted_iota(jnp.int32, (HG_CHUNK, HG_CHUNK), 1)
    masks = [ti == si]
    tx = ti ^ si
    for lvl in HG_LEVELS:
        masks.append((tx >= lvl) & (tx < 2 * lvl) & ((ti & lvl) != 0))

    for c in range(tb // HG_CHUNK):
        r0 = c * HG_CHUNK
        for hd in range(HG_HEADS):
            l0 = hd * HG_HEAD_DIM
            blk = lambda z: z[r0:r0 + HG_CHUNK, l0:l0 + HG_HEAD_DIM]
            scores = jnp.zeros((HG_CHUNK, HG_CHUNK), F32)
            for ql, kl, msk in zip(q_lv, k_lv, masks):
                scores = jnp.where(msk, _dot_nt(blk(ql), blk(kl)), scores)
            dec = s_decay[r0 + HG_CHUNK - 1:r0 + HG_CHUNK, l0:l0 + HG_HEAD_DIM]
            _hgrn_chunk_head(scores, blk(q_in), blk(k_out), dec, blk(vb), blk(g), hgw,
                             st_ref, hd, ohg_ref, r0, l0)

def _mix_tile(x_ref, m_ref, o_ref, t, pre_ref, post_ref, win_ref, cw_ref, cb_ref, wa_ref, ba_ref,
              wx_ref, bx_ref, lam_ref, lbl_ref, hgw_ref, wout_ref, ucar_ref, hcar_ref, st_ref, ohg_ref):
    tb = MIX_TILE
    x = x_ref[...]
    m = m_ref[...]
    per_block = []
    for r0 in range(0, tb, MIX_ROWS):
        hb = (_rms(x[r0:r0 + MIX_ROWS], pre_ref[...]) * (1.0 + m[1:2]) + m[0:1]).astype(BF16)
        proj = _dot(hb, win_ref[...])
        per_block.append([proj[:, c0:c0 + LRU_WIDTH] for c0 in range(0, D_PROJ, LRU_WIDTH)])
    u, y_lru, q, f_raw, v, g = [jnp.concatenate(seg, axis=0) for seg in zip(*per_block)]

    n_grp = tb // V7X_SUBLANES
    grouped = lambda z: z.reshape(n_grp, V7X_SUBLANES, z.shape[-1])
    flat = lambda z: z.reshape(tb, z.shape[-1])
    gshape = (n_grp, V7X_SUBLANES, LRU_WIDTH)
    sub = lax.broadcasted_iota(jnp.int32, gshape, 1)
    grp = lax.broadcasted_iota(jnp.int32, gshape, 0)

    u3 = grouped(u)
    u_all = jnp.concatenate([ucar_ref[...][None], u3], axis=0)
    ucar_ref[...] = u3[n_grp - 1]
    cw = cw_ref[...]
    u_conv = cb_ref[...]
    for k in range(CONV_WIDTH - 1):
        back = CONV_WIDTH - 1 - k
        rot = pltpu.roll(u_all, back, 1)
        u_conv = u_conv + jnp.where(sub >= back, rot[1:], rot[:-1]) * cw[k:k + 1]
    u_conv = flat(u_conv + u3 * cw[CONV_WIDTH - 1:CONV_WIDTH])

    a, mult, ig = _lru_gates(u_conv, wa_ref, ba_ref[...], wx_ref, bx_ref[...], lam_ref[...])
    a3, mult3 = grouped(a), grouped(mult)
    first_row = jnp.where(t == 0, 0, -1)
    mult3 = jnp.where(grp * V7X_SUBLANES + sub == first_row, 1.0, mult3)
    b3 = (mult3 * grouped(ig)) * grouped(u_conv)
    s = 1
    while s < V7X_SUBLANES:
        keep = sub >= s
        b3 = jnp.where(keep, a3 * pltpu.roll(b3, s, 1) + b3, b3)
        a3 = jnp.where(keep, a3 * pltpu.roll(a3, s, 1), a3)
        s *= 2
    carry = hcar_ref[...]
    hs = []
    for gi in range(n_grp):
        h_g = a3[gi] * carry + b3[gi]
        carry = h_g[V7X_SUBLANES - 1:V7X_SUBLANES]
        hs.append(h_g)
    hcar_ref[...] = carry
    o_lru = jnp.concatenate(hs, axis=0) * _gelu_tanh(y_lru)

    lb = _lower_bound(lbl_ref[...])
    f3 = grouped(lb + (1.0 - lb) * _sigmoid(f_raw))
    kk3 = 1.0 - f3
    qs3 = grouped(q * (HG_HEAD_DIM ** -0.5))
    vb = v.astype(BF16)
    hgw = hgw_ref[...]

    pfx = f3
    s = 1
    while s < V7X_SUBLANES:
        pfx = jnp.where(sub >= s, pfx * pltpu.roll(pfx, s, 1), pfx)
        s *= 2
    grp_per_chunk = HG_CHUNK // V7X_SUBLANES
    rows, tails = [], []
    for gi in range(n_grp):
        p_g = pfx[gi] if gi % grp_per_chunk == 0 else pfx[gi] * carry
        carry = p_g[V7X_SUBLANES - 1:V7X_SUBLANES]
        rows.append(p_g)
        if gi % grp_per_chunk == grp_per_chunk - 1:
            tails.append(carry)
    pfx = jnp.concatenate(rows, axis=0)
    direct = jnp.min(jnp.concatenate(tails, axis=0)) >= HG_DIRECT_MIN

    @pl.when(direct)
    def _():
        qs, kk = flat(qs3), flat(kk3)
        per_block = HG_DIRECT_ROWS // HG_CHUNK
        for bi, r0 in enumerate(range(0, tb, HG_DIRECT_ROWS)):
            rows = lambda z: z[r0:r0 + HG_DIRECT_ROWS]
            _hgrn_chunks_direct(rows(qs), rows(kk), rows(pfx), tails[bi * per_block:(bi + 1) * per_block],
                                rows(vb), rows(g), hgw, st_ref, ohg_ref, r0)

    @pl.when(jnp.logical_not(direct))
    def _():
        _hgrn_chunks_levels(f3, kk3, qs3, vb, g, hgw, st_ref, ohg_ref)

    mix_in = jnp.concatenate([o_lru, ohg_ref[...]], axis=-1).astype(BF16)
    for r0 in range(0, tb, MIX_ROWS):
        mix = _dot(mix_in[r0:r0 + MIX_ROWS], wout_ref[...])
        o_ref[r0:r0 + MIX_ROWS, :] = x[r0:r0 + MIX_ROWS] + m[2:3] * _rms(mix, post_ref[...])


def _mix_prompt_kernel(x_ref, m_ref, *refs):
    (pre_ref, post_ref, win32_ref, cw_ref, cb_ref, wa_ref, ba_ref, wx_ref, bx_ref, lam_ref, lbl_ref,
     hgw_ref, wout32_ref) = refs[:13]
    (o_ref, h_out_ref, conv_out_ref, s_out_ref, win_ref, wout_ref,
     ucar_ref, hcar_ref, st_ref, ohg_ref) = refs[13:]
    t = pl.program_id(1)
    n_t = pl.num_programs(1)

    @pl.when((pl.program_id(0) == 0) & (t == 0))
    def _():
        win_ref[...] = win32_ref[...].astype(BF16)
        wout_ref[...] = wout32_ref[...].astype(BF16)

    @pl.when(t == 0)
    def _():
        ucar_ref[...] = jnp.zeros_like(ucar_ref)
        hcar_ref[...] = jnp.zeros_like(hcar_ref)
        st_ref[...] = jnp.zeros_like(st_ref)

    _mix_tile(x_ref.at[0], m_ref.at[0], o_ref.at[0], t, pre_ref, post_ref, win_ref, cw_ref, cb_ref,
              wa_ref, ba_ref, wx_ref, bx_ref, lam_ref, lbl_ref, hgw_ref, wout_ref,
              ucar_ref, hcar_ref, st_ref, ohg_ref)

    @pl.when(t == n_t - 1)
    def _():
        h_out_ref[0] = hcar_ref[...]
        conv_out_ref[0] = ucar_ref[V7X_SUBLANES - (CONV_WIDTH - 1):V7X_SUBLANES, :]
        for hd in range(HG_HEADS):
            s_out_ref[0, hd] = st_ref[hd].T


def _mix_prompt(x, mod3, p):
    batch, seq_len, _ = x.shape
    n_t = seq_len // MIX_TILE
    tok_spec = pl.BlockSpec((1, MIX_TILE, D_MODEL), lambda b, t: (b, t, 0))
    return pl.pallas_call(
        _mix_prompt_kernel,
        grid=(batch, n_t),
        in_specs=[
            tok_spec,
            pl.BlockSpec((1, 3, D_MODEL), lambda b, t: (b, 0, 0)),
            _const_spec((1, D_MODEL)),
            _const_spec((1, D_MODEL)),
            _const_spec((D_MODEL, D_PROJ)),
            _const_spec((CONV_WIDTH, LRU_WIDTH)),
            _const_spec((1, LRU_WIDTH)),
            _const_spec((LRU_WIDTH, LRU_WIDTH)),
            _const_spec((1, LRU_WIDTH)),
            _const_spec((LRU_WIDTH, LRU_WIDTH)),
            _const_spec((1, LRU_WIDTH)),
            _const_spec((1, LRU_WIDTH)),
            _const_spec(p["lb_logits"].shape),
            _const_spec((1, HG_HEAD_DIM)),
            _const_spec((D_MODEL, D_MODEL)),
        ],
        out_specs=[
            tok_spec,
            pl.BlockSpec((1, 1, LRU_WIDTH), lambda b, t: (b, 0, 0)),
            pl.BlockSpec((1, CONV_WIDTH - 1, LRU_WIDTH), lambda b, t: (b, 0, 0)),
            pl.BlockSpec((1, HG_HEADS, HG_HEAD_DIM, HG_HEAD_DIM), lambda b, t: (b, 0, 0, 0)),
            _const_spec((D_MODEL, D_PROJ)),
            _const_spec((D_MODEL, D_MODEL)),
        ],
        out_shape=[
            jax.ShapeDtypeStruct((batch, seq_len, D_MODEL), F32),
            jax.ShapeDtypeStruct((batch, 1, LRU_WIDTH), F32),
            jax.ShapeDtypeStruct((batch, CONV_WIDTH - 1, LRU_WIDTH), F32),
            jax.ShapeDtypeStruct((batch, HG_HEADS, HG_HEAD_DIM, HG_HEAD_DIM), F32),
            jax.ShapeDtypeStruct((D_MODEL, D_PROJ), BF16),
            jax.ShapeDtypeStruct((D_MODEL, D_MODEL), BF16),
        ],
        scratch_shapes=[
            pltpu.VMEM((V7X_SUBLANES, LRU_WIDTH), F32),
            pltpu.VMEM((1, LRU_WIDTH), F32),
            pltpu.VMEM((HG_HEADS, HG_HEAD_DIM, HG_HEAD_DIM), F32),
            pltpu.VMEM((MIX_TILE, HG_WIDTH), F32),
        ],
        compiler_params=pltpu.CompilerParams(
            dimension_semantics=("arbitrary", "arbitrary"), vmem_limit_bytes=VMEM_LIMIT),
        name="mix_prompt",
    )(x, mod3, p["ln_mix_pre"], p["ln_mix_post"], p["w_in"], p["conv_w"], p["conv_b"],
      p["wa_bd"], p["b_a"], p["wx_bd"], p["b_x"], p["lam"], p["lb_logits"], p["hg_norm_w"],
      p["w_out"])


def _mix_sample_kernel(x_ref, m_ref, pre_ref, post_ref, win_ref, cw_ref, cb_ref,
                       wa_ref, ba_ref, wx_ref, bx_ref, lam_ref, lbl_ref, hgw_ref, wout_ref,
                       h0_ref, conv0_ref, s0_hbm,
                       o_ref, h_out_ref, conv_out_ref, s_out_hbm,
                       qf_ref, f_ref, k_ref, v_ref, g_ref, qk_ref, olru_ref, ohg_ref, ostage_ref,
                       sin_ref, sout_ref, in_sem, out_sem):
    i = pl.program_id(0)
    n_i = pl.num_programs(0)
    nb = SAMPLE_SEQ_BLOCK

    def read_block(b):
        slot = b % SAMPLE_IN_SLOTS
        return pltpu.make_async_copy(s0_hbm.at[pl.ds(b * nb, nb)], sin_ref.at[slot], in_sem.at[slot])

    def write_block(b):
        slot = b % SAMPLE_OUT_SLOTS
        return pltpu.make_async_copy(sout_ref.at[slot], s_out_hbm.at[pl.ds(b * nb, nb)],
                                     out_sem.at[slot])

    @pl.when(i == 0)
    def _():
        for b in range(SAMPLE_IN_SLOTS - 1):
            read_block(b).start()

    @pl.when(i + SAMPLE_IN_SLOTS - 1 < n_i)
    def _():
        read_block(i + SAMPLE_IN_SLOTS - 1).start()

    @pl.when(i == 0)
    def _():
        x = x_ref[...]
        h = _rms(x, pre_ref[...]) * (1.0 + m_ref[1]) + m_ref[0]
        proj = _dot(h.astype(BF16), win_ref[...])
        u = proj[:, 0:LRU_WIDTH]
        y_lru = proj[:, LRU_WIDTH:2 * LRU_WIDTH]
        o0 = 2 * LRU_WIDTH
        q = proj[:, o0:o0 + HG_WIDTH]
        f_raw = proj[:, o0 + HG_WIDTH:o0 + 2 * HG_WIDTH]
        v = proj[:, o0 + 2 * HG_WIDTH:o0 + 3 * HG_WIDTH]
        g_ref[...] = proj[:, o0 + 3 * HG_WIDTH:o0 + 4 * HG_WIDTH]

        cw = cw_ref[...]
        u_conv = cb_ref[...]
        for k in range(CONV_WIDTH - 1):
            u_conv = u_conv + conv0_ref[k] * cw[k:k + 1]
            if k > 0:
                conv_out_ref[k - 1] = conv0_ref[k]
        u_conv = u_conv + u * cw[CONV_WIDTH - 1:CONV_WIDTH]
        conv_out_ref[CONV_WIDTH - 2] = u

        a, mult, ig = _lru_gates(u_conv, wa_ref, ba_ref[...], wx_ref, bx_ref[...], lam_ref[...])
        hs = (mult * ig) * u_conv + a * h0_ref[...]
        h_out_ref[...] = hs
        olru_ref[...] = hs * _gelu_tanh(y_lru)

        lb = _lower_bound(lbl_ref[...])
        f = lb + (1.0 - lb) * _sigmoid(f_raw)
        kk = 1.0 - f
        qs = q * (HG_HEAD_DIM ** -0.5)
        f_ref[...] = f
        qf_ref[...] = qs * f
        k_ref[...] = kk
        v_ref[...] = v
        qk = qs * kk
        for hd in range(HG_HEADS):
            l0 = hd * HG_HEAD_DIM
            tot = jnp.sum(qk[:, l0:l0 + HG_HEAD_DIM], axis=-1, keepdims=True)
            qk_ref[:, l0:l0 + HG_HEAD_DIM] = jnp.broadcast_to(tot, (qk.shape[0], HG_HEAD_DIM))

    base = pl.multiple_of(i * nb, nb)
    grp = lambda ref: ref[pl.ds(base, nb), :]
    f_g, k_g, v_g, qf_g, qk_g = grp(f_ref), grp(k_ref), grp(v_ref), grp(qf_ref), grp(qk_ref)
    square = (HG_HEAD_DIM, HG_HEAD_DIM)
    read_block(i).wait()

    @pl.when(i >= SAMPLE_OUT_SLOTS)
    def _():
        write_block(i - SAMPLE_OUT_SLOTS).wait()

    s0_ref = sin_ref.at[i % SAMPLE_IN_SLOTS]
    s_out_ref = sout_ref.at[i % SAMPLE_OUT_SLOTS]
    for j in range(nb):
        for hd in range(HG_HEADS):
            l0 = hd * HG_HEAD_DIM
            rowv = lambda z: z[j:j + 1, l0:l0 + HG_HEAD_DIM]
            s_old = s0_ref[j, hd]
            f_col = jnp.broadcast_to(rowv(f_g), square).T
            k_col = jnp.broadcast_to(rowv(k_g), square).T
            v_row = rowv(v_g)
            outer = k_col.astype(BF16).astype(F32) * v_row.astype(BF16).astype(F32)
            s_out_ref[j, hd] = f_col * s_old + outer
            qf8 = jnp.broadcast_to(rowv(qf_g), (V7X_SUBLANES, HG_HEAD_DIM)).astype(BF16)
            o1 = _dot(qf8, s_old.astype(BF16))[0:1]
            ostage_ref[j:j + 1, l0:l0 + HG_HEAD_DIM] = o1 + rowv(qk_g) * v_row
    ohg_ref[pl.ds(base, nb), :] = ostage_ref[...]
    write_block(i).start()

    @pl.when(i == n_i - 1)
    def _():
        for back in reversed(range(SAMPLE_OUT_SLOTS)):
            write_block(i - back).wait()
        hgw = hgw_ref[...]
        parts = [olru_ref[...]]
        for hd in range(HG_HEADS):
            l0 = hd * HG_HEAD_DIM
            parts.append(_head_rms_gate(ohg_ref[:, l0:l0 + HG_HEAD_DIM], hgw,
                                        g_ref[:, l0:l0 + HG_HEAD_DIM]))
        mix_in = jnp.concatenate(parts, axis=-1).astype(BF16)
        mix = _dot(mix_in, wout_ref[...])
        o_ref[...] = x_ref[...] + m_ref[2] * _rms(mix, post_ref[...])


def _mix_sample(x, mod9, p, h0, conv0, s0):
    n_seq = x.shape[0]
    nb = SAMPLE_SEQ_BLOCK
    assert n_seq // nb >= SAMPLE_IN_SLOTS
    state_spec = pl.BlockSpec(memory_space=pl.ANY)
    state_block = (nb, HG_HEADS, HG_HEAD_DIM, HG_HEAD_DIM)
    full2 = lambda shape: pl.BlockSpec(shape, lambda i: (0,) * len(shape))
    return pl.pallas_call(
        _mix_sample_kernel,
        grid=(n_seq // nb,),
        in_specs=[
            _const_spec((n_seq, D_MODEL)),
            pl.BlockSpec((3, n_seq, D_MODEL), lambda i: (1, 0, 0), pipeline_mode=pl.Buffered(1)),
            _const_spec((1, D_MODEL)),
            _const_spec((1, D_MODEL)),
            _const_spec((D_MODEL, D_PROJ)),
            _const_spec((CONV_WIDTH, LRU_WIDTH)),
            _const_spec((1, LRU_WIDTH)),
            _const_spec((LRU_WIDTH, LRU_WIDTH)),
            _const_spec((1, LRU_WIDTH)),
            _const_spec((LRU_WIDTH, LRU_WIDTH)),
            _const_spec((1, LRU_WIDTH)),
            _const_spec((1, LRU_WIDTH)),
            _const_spec(p["lb_logits"].shape),
            _const_spec((1, HG_HEAD_DIM)),
            _const_spec((D_MODEL, D_MODEL)),
            _const_spec((n_seq, LRU_WIDTH)),
            _const_spec((CONV_WIDTH - 1, n_seq, LRU_WIDTH)),
            state_spec,
        ],
        out_specs=[
            full2((n_seq, D_MODEL)),
            full2((n_seq, LRU_WIDTH)),
            full2((CONV_WIDTH - 1, n_seq, LRU_WIDTH)),
            state_spec,
        ],
        out_shape=[
            jax.ShapeDtypeStruct((n_seq, D_MODEL), F32),
            jax.ShapeDtypeStruct((n_seq, LRU_WIDTH), F32),
            jax.ShapeDtypeStruct((CONV_WIDTH - 1, n_seq, LRU_WIDTH), F32),
            jax.ShapeDtypeStruct((n_seq, HG_HEADS, HG_HEAD_DIM, HG_HEAD_DIM), F32),
        ],
        scratch_shapes=[pltpu.VMEM((n_seq, HG_WIDTH), F32) for _ in range(8)]
        + [pltpu.VMEM((nb, HG_WIDTH), F32),
           pltpu.VMEM((SAMPLE_IN_SLOTS,) + state_block, F32),
           pltpu.VMEM((SAMPLE_OUT_SLOTS,) + state_block, F32),
           pltpu.SemaphoreType.DMA((SAMPLE_IN_SLOTS,)),
           pltpu.SemaphoreType.DMA((SAMPLE_OUT_SLOTS,))],
        compiler_params=pltpu.CompilerParams(
            dimension_semantics=("arbitrary",), vmem_limit_bytes=VMEM_LIMIT),
        name="mix_sample",
    )(x, mod9, p["ln_mix_pre"], p["ln_mix_post"], p["w_in"], p["conv_w"], p["conv_b"],
      p["wa_bd"], p["b_a"], p["wx_bd"], p["b_x"], p["lam"], p["lb_logits"], p["hg_norm_w"],
      p["w_out"], h0, conv0, s0)


def _block_diag(w):
    heads, blk, _ = w.shape
    eye = jnp.eye(heads, dtype=w.dtype)
    return (eye[:, None, :, None] * w[:, :, None, :]).reshape(heads * blk, heads * blk)


def kernel(x_prompt, x_sample, c_prompt, c_sample, state_lru_h, state_lru_conv, state_hgrn_S, w_ada, b_ada, ln_ffn1_pre, ln_ffn1_post, ffn1_w_gate, ffn1_w_up, ffn1_w_down, ln_mix_pre, ln_mix_post, w_in, lru_conv_w, lru_conv_b, lru_w_a, lru_b_a, lru_w_x, lru_b_x, lru_lambda, hg_lb_logits, hg_norm_w, w_out, ln_ffn2_pre, ln_ffn2_post, ffn2_w_gate, ffn2_w_up, ffn2_w_down):
    depth = w_ada.shape[0]
    batch, seq_len, _ = x_prompt.shape
    n_seq = x_sample.shape[0]
    assert depth == 1 and x_sample.shape[1] == 1
    assert seq_len % FFN_TILE == 0 and seq_len % MIX_TILE == 0 and n_seq % SAMPLE_SEQ_BLOCK == 0
    assert D_PROJ == 6 * LRU_WIDTH and LRU_WIDTH == HG_WIDTH

    xp = x_prompt.reshape(batch * seq_len, D_MODEL)
    xs = x_sample.reshape(n_seq, D_MODEL)
    ph, pc, pS, sh, sc, sS = [], [], [], [], [], []
    for l in range(depth):
        row = lambda w: w[l].reshape(1, -1)
        mod9 = _ada(c_sample, c_prompt, w_ada[l], b_ada[l])
        mod_p = jnp.transpose(mod9[:, n_seq:], (1, 0, 2))
        p = {
            "ln_mix_pre": row(ln_mix_pre), "ln_mix_post": row(ln_mix_post),
            "w_in": w_in[l], "w_out": w_out[l],
            "conv_w": lru_conv_w[l], "conv_b": row(lru_conv_b),
            "wa_bd": _block_diag(lru_w_a[l]).astype(BF16), "b_a": row(lru_b_a),
            "wx_bd": _block_diag(lru_w_x[l]).astype(BF16), "b_x": row(lru_b_x),
            "lam": row(lru_lambda), "lb_logits": hg_lb_logits,
            "hg_norm_w": row(hg_norm_w),
        }
        f1 = (row(ln_ffn1_pre), row(ln_ffn1_post), ffn1_w_gate[l], ffn1_w_up[l], ffn1_w_down[l])
        f2 = (row(ln_ffn2_pre), row(ln_ffn2_post), ffn2_w_gate[l], ffn2_w_up[l], ffn2_w_down[l])

        xp, xs = _ffn(xp, xs, mod_p[:, 0:3], mod9, 0, *f1, seq_len)
        xp, h_p, c_p, S_p, w_in_bf, w_out_bf = _mix_prompt(
            xp.reshape(batch, seq_len, D_MODEL), mod_p[:, 3:6], p)
        xp = xp.reshape(batch * seq_len, D_MODEL)
        conv0 = jnp.transpose(state_lru_conv[l], (1, 0, 2))
        xs, h_s, c_s, S_s = _mix_sample(xs, mod9, dict(p, w_in=w_in_bf, w_out=w_out_bf),
                                        state_lru_h[l], conv0, state_hgrn_S[l])
        xp, xs = _ffn(xp, xs, mod_p[:, 6:9], mod9, 2, *f2, seq_len)

        ph.append(h_p.reshape(batch, LRU_WIDTH)); pc.append(c_p); pS.append(S_p)
        sh.append(h_s); sc.append(jnp.transpose(c_s, (1, 0, 2))); sS.append(S_s)

    return (xp.reshape(batch, seq_len, D_MODEL), xs.reshape(n_seq, 1, D_MODEL),
            jnp.stack(ph), jnp.stack(pc), jnp.stack(pS), jnp.stack(sh), jnp.stack(sc), jnp.stack(sS))
```

```python
import jax
import jax.numpy as jnp
from jax import lax
from jax.experimental import pallas as pl
from jax.experimental.pallas import tpu as pltpu

F32 = jnp.float32
BF16 = jnp.bfloat16

D_MODEL = 1024
D_FF = 2816
LRU_WIDTH = 512
CONV_WIDTH = 4
LRU_C = 8.0
HG_WIDTH = 512
HG_HEAD_DIM = 128
HG_HEADS = HG_WIDTH // HG_HEAD_DIM
HG_CHUNK = 64
N_MOD = 9
D_PROJ = 2 * LRU_WIDTH + 4 * HG_WIDTH
EPS = 1e-6

V7X_SUBLANES = 8
V7X_VMEM_BYTES = 64 * 1024 * 1024
VMEM_LIMIT = V7X_VMEM_BYTES - 8 * 1024 * 1024

FFN_TILE = 1024
FFN_ROWS = 512
MIX_TILE = 512
MIX_ROWS = 256
HG_DIRECT_ROWS = 256
FFN_WBLOCK = 256
FFN_WSTEPS = D_FF // FFN_WBLOCK
MIX_WROWS = 256
FFN_WSLOTS = 3
SAMPLE_SEQ_BLOCK = 2 * V7X_SUBLANES
SAMPLE_IN_SLOTS = 3
SAMPLE_OUT_SLOTS = 2

HG_LEVELS = (1, 2, 4, 8, 16, 32)
HG_DIRECT_MIN = 2.0 ** -120


def _rms(x, w):
    return (x * lax.rsqrt(jnp.mean(x * x, axis=-1, keepdims=True) + EPS)) * w


def _sigmoid(x):
    return 1.0 / (1.0 + jnp.exp(-x))


def _silu(x):
    return x * _sigmoid(x)


def _gelu_tanh(x):
    c = 0.7978845608028654
    return x * (0.5 * (1.0 + jnp.tanh(c * (x + 0.044715 * (x * x * x)))))


def _softplus(z):
    return jnp.maximum(z, 0.0) + jnp.log1p(jnp.exp(-jnp.abs(z)))


def _dot(a, b):
    return jnp.dot(a, b, preferred_element_type=F32)


def _dot_nt(a, b):
    return lax.dot_general(a, b, (((1,), (1,)), ((), ())), preferred_element_type=F32)


def _dot_tn(a, b):
    return lax.dot_general(a, b, (((0,), (0,)), ((), ())), preferred_element_type=F32)


def _const_spec(shape):
    nd = len(shape)
    return pl.BlockSpec(shape, lambda *_: (0,) * nd, pipeline_mode=pl.Buffered(1))


def _ada_kernel(cs_ref, cp_ref, w_ref, b_ref, o_ref):
    s = _silu(jnp.concatenate([cs_ref[...], cp_ref[...]], axis=0))
    o_ref[0] = _dot(s.astype(BF16), w_ref[...].astype(BF16)) + b_ref[...]


def _ada(c_sample, c_prompt, w_ada, b_ada):
    rows = c_sample.shape[0] + c_prompt.shape[0]
    return pl.pallas_call(
        _ada_kernel,
        grid=(N_MOD,),
        in_specs=[
            _const_spec(c_sample.shape),
            _const_spec(c_prompt.shape),
            pl.BlockSpec((D_MODEL, D_MODEL), lambda j: (0, j)),
            pl.BlockSpec((1, D_MODEL), lambda j: (0, j)),
        ],
        out_specs=pl.BlockSpec((1, rows, D_MODEL), lambda j: (j, 0, 0)),
        out_shape=jax.ShapeDtypeStruct((N_MOD, rows, D_MODEL), F32),
        compiler_params=pltpu.CompilerParams(
            dimension_semantics=("arbitrary",), vmem_limit_bytes=VMEM_LIMIT),
        name="ada_mod",
    )(c_sample, c_prompt, w_ada, b_ada.reshape(1, N_MOD * D_MODEL))


def _ffn_body(x, shift, scale, gate, pre_w, post_w, wg_ref, wu_ref, wd_ref):
    h = _rms(x, pre_w) * (1.0 + scale) + shift
    hb = h.astype(BF16)
    a = _dot(hb, wg_ref[...])
    u = _dot(hb, wu_ref[...])
    act = (_silu(a) * u).astype(BF16)
    y = _dot(act, wd_ref[...])
    return x + (0.5 * gate) * _rms(y, post_w)


def _ffn_kernel(xp_ref, mp_ref, xs_ref, ms_ref, pre_ref, post_ref, wg32_ref, wu32_ref, wd32_ref,
                op_ref, os_ref, wg_ref, wu_ref, wd_ref, gring_ref, uring_ref, dring_ref, wsem):
    s = pl.program_id(0)
    n_steps = pl.num_programs(0)

    def weight_reads(j):
        slot, c0 = j % FFN_WSLOTS, j * FFN_WBLOCK
        return (pltpu.make_async_copy(wg32_ref.at[:, c0:c0 + FFN_WBLOCK], gring_ref.at[slot], wsem.at[0, slot]),
                pltpu.make_async_copy(wu32_ref.at[:, c0:c0 + FFN_WBLOCK], uring_ref.at[slot], wsem.at[1, slot]),
                pltpu.make_async_copy(wd32_ref.at[c0:c0 + FFN_WBLOCK, :], dring_ref.at[slot], wsem.at[2, slot]))

    for j in range(FFN_WSTEPS):
        @pl.when(s == j)
        def _(j=j):
            ahead = range(FFN_WSLOTS) if j == 0 else [j + FFN_WSLOTS - 1]
            for jj in ahead:
                if jj < FFN_WSTEPS:
                    for copy in weight_reads(jj):
                        copy.start()
            for copy in weight_reads(j):
                copy.wait()
            slot, c0 = j % FFN_WSLOTS, j * FFN_WBLOCK
            wg_ref[:, c0:c0 + FFN_WBLOCK] = gring_ref[slot].astype(BF16)
            wu_ref[:, c0:c0 + FFN_WBLOCK] = uring_ref[slot].astype(BF16)
            wd_ref[c0:c0 + FFN_WBLOCK, :] = dring_ref[slot].astype(BF16)

    @pl.when((s >= FFN_WSTEPS) & (s < n_steps - 1))
    def _():
        m = mp_ref[0]
        for r0 in range(0, FFN_TILE, FFN_ROWS):
            op_ref[r0:r0 + FFN_ROWS, :] = _ffn_body(
                xp_ref[r0:r0 + FFN_ROWS, :], m[0:1], m[1:2], m[2:3],
                pre_ref[...], post_ref[...], wg_ref, wu_ref, wd_ref)

    @pl.when(s == n_steps - 1)
    def _():
        os_ref[...] = _ffn_body(xs_ref[...], ms_ref[0], ms_ref[1], ms_ref[2], pre_ref[...],
                                post_ref[...], wg_ref, wu_ref, wd_ref)


def _ffn(xp, xs, mod_p3, mod9, sub_layer, pre_w, post_w, wg, wu, wd, seq_len):
    n_tok = xp.shape[0]
    n_seq = xs.shape[0]
    n_tiles = n_tok // FFN_TILE
    steps_per_seq = seq_len // FFN_TILE
    tile = lambda s: jnp.clip(s - FFN_WSTEPS, 0, n_tiles - 1)
    tok_spec = pl.BlockSpec((FFN_TILE, D_MODEL), lambda s: (tile(s), 0))
    return pl.pallas_call(
        _ffn_kernel,
        grid=(FFN_WSTEPS + n_tiles + 1,),
        in_specs=[
            tok_spec,
            pl.BlockSpec((1, 3, D_MODEL), lambda s: (tile(s) // steps_per_seq, 0, 0)),
            _const_spec((n_seq, D_MODEL)),
            pl.BlockSpec((3, n_seq, D_MODEL), lambda s: (sub_layer, 0, 0), pipeline_mode=pl.Buffered(1)),
            _const_spec((1, D_MODEL)),
            _const_spec((1, D_MODEL)),
            pl.BlockSpec(memory_space=pl.ANY),
            pl.BlockSpec(memory_space=pl.ANY),
            pl.BlockSpec(memory_space=pl.ANY),
        ],
        out_specs=[tok_spec, pl.BlockSpec((n_seq, D_MODEL), lambda s: (0, 0))],
        out_shape=[jax.ShapeDtypeStruct((n_tok, D_MODEL), F32),
                   jax.ShapeDtypeStruct((n_seq, D_MODEL), F32)],
        scratch_shapes=[pltpu.VMEM((D_MODEL, D_FF), BF16), pltpu.VMEM((D_MODEL, D_FF), BF16),
                        pltpu.VMEM((D_FF, D_MODEL), BF16),
                        pltpu.VMEM((FFN_WSLOTS, D_MODEL, FFN_WBLOCK), F32),
                        pltpu.VMEM((FFN_WSLOTS, D_MODEL, FFN_WBLOCK), F32),
                        pltpu.VMEM((FFN_WSLOTS, FFN_WBLOCK, D_MODEL), F32),
                        pltpu.SemaphoreType.DMA((3, FFN_WSLOTS))],
        compiler_params=pltpu.CompilerParams(
            dimension_semantics=("arbitrary",), vmem_limit_bytes=VMEM_LIMIT),
        name="ffn",
    )(xp, mod_p3, xs, mod9, pre_w, post_w, wg, wu, wd)


def _lower_bound(lb_logits):
    z = lb_logits - jnp.max(lb_logits, axis=0, keepdims=True)
    e = jnp.exp(z)
    return e[0:1] / jnp.sum(e, axis=0, keepdims=True)


def _lru_gates(u_conv, wa_ref, ba, wx_ref, bx, lam):
    ub = u_conv.astype(BF16)
    r = _sigmoid(_dot(ub, wa_ref[...]) + ba)
    ig = _sigmoid(_dot(ub, wx_ref[...]) + bx)
    log_a = (-LRU_C * r) * _softplus(-lam)
    a = jnp.exp(log_a)
    th = jnp.tanh(log_a)
    mult = jnp.sqrt((-2.0 * th) / (1.0 - th))
    return a, mult, ig


def _head_rms_gate(o, norm_w, g):
    return _rms(o, norm_w) * _silu(g)


def _hgrn_chunk_head(scores, q_in, k_out, dec, vb_blk, g_blk, hgw, st_ref, hd, ohg_ref, r0, l0):
    st = st_ref[hd]
    o = _dot_nt(q_in, st.astype(BF16)) + _dot(scores.astype(BF16), vb_blk)
    st_ref[hd] = st * dec + _dot_tn(vb_blk, k_out)
    ohg_ref[r0:r0 + HG_CHUNK, l0:l0 + HG_HEAD_DIM] = _head_rms_gate(o, hgw, g_blk)


def _hgrn_chunks_direct(qs, kk, pfx, tails, vb, g, hgw, st_ref, ohg_ref, r0):
    tb = qs.shape[0]
    n_chunk = len(tails)
    inv = 1.0 / pfx
    q_dec = qs * pfx
    k_dec = kk * inv
    q_in = q_dec.astype(BF16)
    k_in = k_dec.astype(BF16)
    ti = lax.broadcasted_iota(jnp.int32, (tb, tb), 0)
    si = lax.broadcasted_iota(jnp.int32, (tb, tb), 1)
    visible = (ti >= si) & ((ti ^ si) < HG_CHUNK)
    row_chunk = jnp.right_shift(lax.broadcasted_iota(jnp.int32, (tb, HG_HEAD_DIM), 0),
                                HG_CHUNK.bit_length() - 1)
    tail_rows = jnp.concatenate([jnp.broadcast_to(t, (HG_CHUNK, t.shape[-1])) for t in tails], axis=0)
    k_end = k_dec * tail_rows
    for hd in range(HG_HEADS):
        l0 = hd * HG_HEAD_DIM
        head = lambda z: z[:, l0:l0 + HG_HEAD_DIM]
        by_chunk = lambda z: jnp.concatenate(
            [jnp.where(row_chunk == c, head(z), 0.0) for c in range(n_chunk)], axis=-1).astype(BF16)
        scores = jnp.where(visible, _dot_nt(head(q_in), head(k_in)), 0.0)
        grown = _dot_tn(head(vb), by_chunk(k_end))
        st = st_ref[hd]
        states = []
        for c, tail in enumerate(tails):
            states.append(st.astype(BF16))
            st = st * head(tail) + grown[:, c * HG_HEAD_DIM:(c + 1) * HG_HEAD_DIM]
        st_ref[hd] = st
        o = (_dot_nt(by_chunk(q_dec), jnp.concatenate(states, axis=-1))
             + _dot(scores.astype(BF16), head(vb)))
        ohg_ref[r0:r0 + tb, l0:l0 + HG_HEAD_DIM] = _head_rms_gate(o, hgw, head(g))


def _hgrn_chunks_levels(f3, kk3, qs3, vb, g, hgw, st_ref, ohg_ref):
    n_grp = f3.shape[0]
    tb = n_grp * V7X_SUBLANES
    sub = lax.broadcasted_iota(jnp.int32, f3.shape, 1)
    grp = lax.broadcasted_iota(jnp.int32, f3.shape, 0)
    to_mxu = lambda z: z.reshape(tb, z.shape[-1]).astype(BF16)
    q_lv = [to_mxu(qs3)]
    k_lv = [to_mxu(kk3)]
    tot, pre, post = f3, f3, None
    for lvl in HG_LEVELS:
        q_lv.append(to_mxu(qs3 * pre))
        k_lv.append(k_lv[0] if post is None else to_mxu(kk3 * post))
        if lvl < V7X_SUBLANES:
            upper = (sub & lvl) != 0
            below = pltpu.roll(tot, lvl, 1)
            above = pltpu.roll(tot, V7X_SUBLANES - lvl, 1)
        else:
            upper = (grp & (lvl // V7X_SUBLANES)) != 0
            below = jnp.roll(tot, lvl // V7X_SUBLANES, axis=0)
            above = jnp.roll(tot, -(lvl // V7X_SUBLANES), axis=0)
        pre = jnp.where(upper, pre * below, pre)
        post = jnp.where(upper, 1.0, above) if post is None else jnp.where(upper, post, post * above)
        tot = tot * jnp.where(upper, below, above)
    q_in = to_mxu(qs3 * pre)
    k_out = to_mxu(kk3 * post)
    s_decay = tot.reshape(tb, tot.shape[-1])

    ti = lax.broadcasted_iota(jnp.int32, (HG_CHUNK, HG_CHUNK), 0)
    si = lax.broadcasted_iota(jnp.int32, (HG_CHUNK, HG_CHUNK), 1)
    masks = [ti == si]
    tx = ti ^ si
    for lvl in HG_LEVELS:
        masks.append((tx >= lvl) & (tx < 2 * lvl) & ((ti & lvl) != 0))

    for c in range(tb // HG_CHUNK):
        r0 = c * HG_CHUNK
        for hd in range(HG_HEADS):
            l0 = hd * HG_HEAD_DIM
            blk = lambda z: z[r0:r0 + HG_CHUNK, l0:l0 + HG_HEAD_DIM]
            scores = jnp.zeros((HG_CHUNK, HG_CHUNK), F32)
            for ql, kl, msk in zip(q_lv, k_lv, masks):
                scores = jnp.where(msk, _dot_nt(blk(ql), blk(kl)), scores)
            dec = s_decay[r0 + HG_CHUNK - 1:r0 + HG_CHUNK, l0:l0 + HG_HEAD_DIM]
            _hgrn_chunk_head(scores, blk(q_in), blk(k_out), dec, blk(vb), blk(g), hgw,
                             st_ref, hd, ohg_ref, r0, l0)

def _mix_tile(x_ref, m_ref, o_ref, t, pre_ref, post_ref, win_ref, cw_ref, cb_ref, wa_ref, ba_ref,
              wx_ref, bx_ref, lam_ref, lbl_ref, hgw_ref, wout_ref, ucar_ref, hcar_ref, st_ref, ohg_ref):
    tb = MIX_TILE
    x = x_ref[...]
    m = m_ref[...]
    per_block = []
    for r0 in range(0, tb, MIX_ROWS):
        hb = (_rms(x[r0:r0 + MIX_ROWS], pre_ref[...]) * (1.0 + m[1:2]) + m[0:1]).astype(BF16)
        proj = _dot(hb, win_ref[...])
        per_block.append([proj[:, c0:c0 + LRU_WIDTH] for c0 in range(0, D_PROJ, LRU_WIDTH)])
    u, y_lru, q, f_raw, v, g = [jnp.concatenate(seg, axis=0) for seg in zip(*per_block)]

    n_grp = tb // V7X_SUBLANES
    grouped = lambda z: z.reshape(n_grp, V7X_SUBLANES, z.shape[-1])
    flat = lambda z: z.reshape(tb, z.shape[-1])
    gshape = (n_grp, V7X_SUBLANES, LRU_WIDTH)
    sub = lax.broadcasted_iota(jnp.int32, gshape, 1)
    grp = lax.broadcasted_iota(jnp.int32, gshape, 0)

    u3 = grouped(u)
    u_all = jnp.concatenate([ucar_ref[...][None], u3], axis=0)
    ucar_ref[...] = u3[n_grp - 1]
    cw = cw_ref[...]
    u_conv = cb_ref[...]
    for k in range(CONV_WIDTH - 1):
        back = CONV_WIDTH - 1 - k
        rot = pltpu.roll(u_all, back, 1)
        u_conv = u_conv + jnp.where(sub >= back, rot[1:], rot[:-1]) * cw[k:k + 1]
    u_conv = flat(u_conv + u3 * cw[CONV_WIDTH - 1:CONV_WIDTH])

    a, mult, ig = _lru_gates(u_conv, wa_ref, ba_ref[...], wx_ref, bx_ref[...], lam_ref[...])
    a3, mult3 = grouped(a), grouped(mult)
    first_row = jnp.where(t == 0, 0, -1)
    mult3 = jnp.where(grp * V7X_SUBLANES + sub == first_row, 1.0, mult3)
    b3 = (mult3 * grouped(ig)) * grouped(u_conv)
    s = 1
    while s < V7X_SUBLANES:
        keep = sub >= s
        b3 = jnp.where(keep, a3 * pltpu.roll(b3, s, 1) + b3, b3)
        a3 = jnp.where(keep, a3 * pltpu.roll(a3, s, 1), a3)
        s *= 2
    carry = hcar_ref[...]
    hs = []
    for gi in range(n_grp):
        h_g = a3[gi] * carry + b3[gi]
        carry = h_g[V7X_SUBLANES - 1:V7X_SUBLANES]
        hs.append(h_g)
    hcar_ref[...] = carry
    o_lru = jnp.concatenate(hs, axis=0) * _gelu_tanh(y_lru)

    lb = _lower_bound(lbl_ref[...])
    f3 = grouped(lb + (1.0 - lb) * _sigmoid(f_raw))
    kk3 = 1.0 - f3
    qs3 = grouped(q * (HG_HEAD_DIM ** -0.5))
    vb = v.astype(BF16)
    hgw = hgw_ref[...]

    pfx = f3
    s = 1
    while s < V7X_SUBLANES:
        pfx = jnp.where(sub >= s, pfx * pltpu.roll(pfx, s, 1), pfx)
        s *= 2
    grp_per_chunk = HG_CHUNK // V7X_SUBLANES
    rows, tails = [], []
    for gi in range(n_grp):
        p_g = pfx[gi] if gi % grp_per_chunk == 0 else pfx[gi] * carry
        carry = p_g[V7X_SUBLANES - 1:V7X_SUBLANES]
        rows.append(p_g)
        if gi % grp_per_chunk == grp_per_chunk - 1:
            tails.append(carry)
    pfx = jnp.concatenate(rows, axis=0)
    direct = jnp.min(jnp.concatenate(tails, axis=0)) >= HG_DIRECT_MIN

    @pl.when(direct)
    def _():
        qs, kk = flat(qs3), flat(kk3)
        per_block = HG_DIRECT_ROWS // HG_CHUNK
        for bi, r0 in enumerate(range(0, tb, HG_DIRECT_ROWS)):
            rows = lambda z: z[r0:r0 + HG_DIRECT_ROWS]
            _hgrn_chunks_direct(rows(qs), rows(kk), rows(pfx), tails[bi * per_block:(bi + 1) * per_block],
                                rows(vb), rows(g), hgw, st_ref, ohg_ref, r0)

    @pl.when(jnp.logical_not(direct))
    def _():
        _hgrn_chunks_levels(f3, kk3, qs3, vb, g, hgw, st_ref, ohg_ref)

    mix_in = jnp.concatenate([o_lru, ohg_ref[...]], axis=-1).astype(BF16)
    for r0 in range(0, tb, MIX_ROWS):
        mix = _dot(mix_in[r0:r0 + MIX_ROWS], wout_ref[...])
        o_ref[r0:r0 + MIX_ROWS, :] = x[r0:r0 + MIX_ROWS] + m[2:3] * _rms(mix, post_ref[...])


def _mix_prompt_kernel(x_ref, m_ref, *refs):
    (pre_ref, post_ref, win32_hbm, cw_ref, cb_ref, wa_ref, ba_ref, wx_ref, bx_ref, lam_ref, lbl_ref,
     hgw_ref, wout32_hbm) = refs[:13]
    (o_ref, h_out_ref, conv_out_ref, s_out_ref, win_hbm, wout_hbm,
     ucar_ref, hcar_ref, st_ref, ohg_ref, win_ref, wout_ref, win32_ref, wout32_ref,
     rsem, wsem) = refs[13:]
    t = pl.program_id(1)
    n_t = pl.num_programs(1)
    first = (pl.program_id(0) == 0) & (t == 0)
    last = (pl.program_id(0) == pl.num_programs(0) - 1) & (t == n_t - 1)
    n_wblk = D_MODEL // MIX_WROWS

    def weight_read(k):
        src, dst = (win32_hbm, win32_ref) if k < n_wblk else (wout32_hbm, wout32_ref)
        rows = pl.ds((k % n_wblk) * MIX_WROWS, MIX_WROWS)
        return pltpu.make_async_copy(src.at[rows], dst.at[rows], rsem.at[k])

    def weight_writes():
        return (pltpu.make_async_copy(win_ref, win_hbm, wsem.at[0]),
                pltpu.make_async_copy(wout_ref, wout_hbm, wsem.at[1]))

    @pl.when(first)
    def _():
        for k in range(2 * n_wblk):
            weight_read(k).start()
        for k in range(2 * n_wblk):
            weight_read(k).wait()
            r0 = (k % n_wblk) * MIX_WROWS
            if k < n_wblk:
                win_ref[r0:r0 + MIX_WROWS, :] = win32_ref[r0:r0 + MIX_WROWS, :].astype(BF16)
            else:
                wout_ref[r0:r0 + MIX_WROWS, :] = wout32_ref[r0:r0 + MIX_WROWS, :].astype(BF16)
        for copy in weight_writes():
            copy.start()

    @pl.when(t == 0)
    def _():
        ucar_ref[...] = jnp.zeros_like(ucar_ref)
        hcar_ref[...] = jnp.zeros_like(hcar_ref)
        st_ref[...] = jnp.zeros_like(st_ref)

    _mix_tile(x_ref.at[0], m_ref.at[0], o_ref.at[0], t, pre_ref, post_ref, win_ref, cw_ref, cb_ref,
              wa_ref, ba_ref, wx_ref, bx_ref, lam_ref, lbl_ref, hgw_ref, wout_ref,
              ucar_ref, hcar_ref, st_ref, ohg_ref)

    @pl.when(t == n_t - 1)
    def _():
        h_out_ref[0] = hcar_ref[...]
        conv_out_ref[0] = ucar_ref[V7X_SUBLANES - (CONV_WIDTH - 1):V7X_SUBLANES, :]
        for hd in range(HG_HEADS):
            s_out_ref[0, hd] = st_ref[hd].T

    @pl.when(last)
    def _():
        for copy in weight_writes():
            copy.wait()


def _mix_prompt(x, mod3, p):
    batch, seq_len, _ = x.shape
    n_t = seq_len // MIX_TILE
    tok_spec = pl.BlockSpec((1, MIX_TILE, D_MODEL), lambda b, t: (b, t, 0))
    return pl.pallas_call(
        _mix_prompt_kernel,
        grid=(batch, n_t),
        in_specs=[
            tok_spec,
            pl.BlockSpec((1, 3, D_MODEL), lambda b, t: (b, 0, 0)),
            _const_spec((1, D_MODEL)),
            _const_spec((1, D_MODEL)),
            pl.BlockSpec(memory_space=pl.ANY),
            _const_spec((CONV_WIDTH, LRU_WIDTH)),
            _const_spec((1, LRU_WIDTH)),
            _const_spec((LRU_WIDTH, LRU_WIDTH)),
            _const_spec((1, LRU_WIDTH)),
            _const_spec((LRU_WIDTH, LRU_WIDTH)),
            _const_spec((1, LRU_WIDTH)),
            _const_spec((1, LRU_WIDTH)),
            _const_spec(p["lb_logits"].shape),
            _const_spec((1, HG_HEAD_DIM)),
            pl.BlockSpec(memory_space=pl.ANY),
        ],
        out_specs=[
            tok_spec,
            pl.BlockSpec((1, 1, LRU_WIDTH), lambda b, t: (b, 0, 0)),
            pl.BlockSpec((1, CONV_WIDTH - 1, LRU_WIDTH), lambda b, t: (b, 0, 0)),
            pl.BlockSpec((1, HG_HEADS, HG_HEAD_DIM, HG_HEAD_DIM), lambda b, t: (b, 0, 0, 0)),
            pl.BlockSpec(memory_space=pl.ANY),
            pl.BlockSpec(memory_space=pl.ANY),
        ],
        out_shape=[
            jax.ShapeDtypeStruct((batch, seq_len, D_MODEL), F32),
            jax.ShapeDtypeStruct((batch, 1, LRU_WIDTH), F32),
            jax.ShapeDtypeStruct((batch, CONV_WIDTH - 1, LRU_WIDTH), F32),
            jax.ShapeDtypeStruct((batch, HG_HEADS, HG_HEAD_DIM, HG_HEAD_DIM), F32),
            jax.ShapeDtypeStruct((D_MODEL, D_PROJ), BF16),
            jax.ShapeDtypeStruct((D_MODEL, D_MODEL), BF16),
        ],
        scratch_shapes=[
            pltpu.VMEM((V7X_SUBLANES, LRU_WIDTH), F32),
            pltpu.VMEM((1, LRU_WIDTH), F32),
            pltpu.VMEM((HG_HEADS, HG_HEAD_DIM, HG_HEAD_DIM), F32),
            pltpu.VMEM((MIX_TILE, HG_WIDTH), F32),
            pltpu.VMEM((D_MODEL, D_PROJ), BF16),
            pltpu.VMEM((D_MODEL, D_MODEL), BF16),
            pltpu.VMEM((D_MODEL, D_PROJ), F32),
            pltpu.VMEM((D_MODEL, D_MODEL), F32),
            pltpu.SemaphoreType.DMA((2 * (D_MODEL // MIX_WROWS),)),
            pltpu.SemaphoreType.DMA((2,)),
        ],
        compiler_params=pltpu.CompilerParams(
            dimension_semantics=("arbitrary", "arbitrary"), vmem_limit_bytes=VMEM_LIMIT),
        name="mix_prompt",
    )(x, mod3, p["ln_mix_pre"], p["ln_mix_post"], p["w_in"], p["conv_w"], p["conv_b"],
      p["wa_bd"], p["b_a"], p["wx_bd"], p["b_x"], p["lam"], p["lb_logits"], p["hg_norm_w"],
      p["w_out"])


def _mix_sample_kernel(x_ref, m_ref, pre_ref, post_ref, win_ref, cw_ref, cb_ref,
                       wa_ref, ba_ref, wx_ref, bx_ref, lam_ref, lbl_ref, hgw_ref, wout_ref,
                       h0_ref, conv0_ref, s0_hbm,
                       o_ref, h_out_ref, conv_out_ref, s_out_hbm,
                       qf_ref, f_ref, k_ref, v_ref, g_ref, qk_ref, olru_ref, ohg_ref, ostage_ref,
                       sin_ref, sout_ref, in_sem, out_sem):
    i = pl.program_id(0)
    n_i = pl.num_programs(0)
    nb = SAMPLE_SEQ_BLOCK

    def read_block(b):
        slot = b % SAMPLE_IN_SLOTS
        return pltpu.make_async_copy(s0_hbm.at[pl.ds(b * nb, nb)], sin_ref.at[slot], in_sem.at[slot])

    def write_block(b):
        slot = b % SAMPLE_OUT_SLOTS
        return pltpu.make_async_copy(sout_ref.at[slot], s_out_hbm.at[pl.ds(b * nb, nb)],
                                     out_sem.at[slot])

    @pl.when(i == 0)
    def _():
        for b in range(SAMPLE_IN_SLOTS - 1):
            read_block(b).start()

    @pl.when(i + SAMPLE_IN_SLOTS - 1 < n_i)
    def _():
        read_block(i + SAMPLE_IN_SLOTS - 1).start()

    @pl.when(i == 0)
    def _():
        x = x_ref[...]
        h = _rms(x, pre_ref[...]) * (1.0 + m_ref[1]) + m_ref[0]
        proj = _dot(h.astype(BF16), win_ref[...])
        u = proj[:, 0:LRU_WIDTH]
        y_lru = proj[:, LRU_WIDTH:2 * LRU_WIDTH]
        o0 = 2 * LRU_WIDTH
        q = proj[:, o0:o0 + HG_WIDTH]
        f_raw = proj[:, o0 + HG_WIDTH:o0 + 2 * HG_WIDTH]
        v = proj[:, o0 + 2 * HG_WIDTH:o0 + 3 * HG_WIDTH]
        g_ref[...] = proj[:, o0 + 3 * HG_WIDTH:o0 + 4 * HG_WIDTH]

        cw = cw_ref[...]
        u_conv = cb_ref[...]
        for k in range(CONV_WIDTH - 1):
            u_conv = u_conv + conv0_ref[k] * cw[k:k + 1]
            if k > 0:
                conv_out_ref[k - 1] = conv0_ref[k]
        u_conv = u_conv + u * cw[CONV_WIDTH - 1:CONV_WIDTH]
        conv_out_ref[CONV_WIDTH - 2] = u

        a, mult, ig = _lru_gates(u_conv, wa_ref, ba_ref[...], wx_ref, bx_ref[...], lam_ref[...])
        hs = (mult * ig) * u_conv + a * h0_ref[...]
        h_out_ref[...] = hs
        olru_ref[...] = hs * _gelu_tanh(y_lru)

        lb = _lower_bound(lbl_ref[...])
        f = lb + (1.0 - lb) * _sigmoid(f_raw)
        kk = 1.0 - f
        qs = q * (HG_HEAD_DIM ** -0.5)
        f_ref[...] = f
        qf_ref[...] = qs * f
        k_ref[...] = kk
        v_ref[...] = v
        qk = qs * kk
        for hd in range(HG_HEADS):
            l0 = hd * HG_HEAD_DIM
            tot = jnp.sum(qk[:, l0:l0 + HG_HEAD_DIM], axis=-1, keepdims=True)
            qk_ref[:, l0:l0 + HG_HEAD_DIM] = jnp.broadcast_to(tot, (qk.shape[0], HG_HEAD_DIM))

    base = pl.multiple_of(i * nb, nb)
    grp = lambda ref: ref[pl.ds(base, nb), :]
    f_g, k_g, v_g, qf_g, qk_g = grp(f_ref), grp(k_ref), grp(v_ref), grp(qf_ref), grp(qk_ref)
    square = (HG_HEAD_DIM, HG_HEAD_DIM)
    read_block(i).wait()

    @pl.when(i >= SAMPLE_OUT_SLOTS)
    def _():
        write_block(i - SAMPLE_OUT_SLOTS).wait()

    s0_ref = sin_ref.at[i % SAMPLE_IN_SLOTS]
    s_out_ref = sout_ref.at[i % SAMPLE_OUT_SLOTS]
    for j in range(nb):
        for hd in range(HG_HEADS):
            l0 = hd * HG_HEAD_DIM
            rowv = lambda z: z[j:j + 1, l0:l0 + HG_HEAD_DIM]
            s_old = s0_ref[j, hd]
            f_col = jnp.broadcast_to(rowv(f_g), square).T
            k_col = jnp.broadcast_to(rowv(k_g), square).T
            v_row = rowv(v_g)
            outer = k_col.astype(BF16).astype(F32) * v_row.astype(BF16).astype(F32)
            s_out_ref[j, hd] = f_col * s_old + outer
            qf8 = jnp.broadcast_to(rowv(qf_g), (V7X_SUBLANES, HG_HEAD_DIM)).astype(BF16)
            o1 = _dot(qf8, s_old.astype(BF16))[0:1]
            ostage_ref[j:j + 1, l0:l0 + HG_HEAD_DIM] = o1 + rowv(qk_g) * v_row
    ohg_ref[pl.ds(base, nb), :] = ostage_ref[...]
    write_block(i).start()

    @pl.when(i == n_i - 1)
    def _():
        for back in reversed(range(SAMPLE_OUT_SLOTS)):
            write_block(i - back).wait()
        hgw = hgw_ref[...]
        parts = [olru_ref[...]]
        for hd in range(HG_HEADS):
            l0 = hd * HG_HEAD_DIM
            parts.append(_head_rms_gate(ohg_ref[:, l0:l0 + HG_HEAD_DIM], hgw,
                                        g_ref[:, l0:l0 + HG_HEAD_DIM]))
        mix_in = jnp.concatenate(parts, axis=-1).astype(BF16)
        mix = _dot(mix_in, wout_ref[...])
        o_ref[...] = x_ref[...] + m_ref[2] * _rms(mix, post_ref[...])


def _mix_sample(x, mod9, p, h0, conv0, s0):
    n_seq = x.shape[0]
    nb = SAMPLE_SEQ_BLOCK
    assert n_seq // nb >= SAMPLE_IN_SLOTS
    state_spec = pl.BlockSpec(memory_space=pl.ANY)
    state_block = (nb, HG_HEADS, HG_HEAD_DIM, HG_HEAD_DIM)
    full2 = lambda shape: pl.BlockSpec(shape, lambda i: (0,) * len(shape))
    return pl.pallas_call(
        _mix_sample_kernel,
        grid=(n_seq // nb,),
        in_specs=[
            _const_spec((n_seq, D_MODEL)),
            pl.BlockSpec((3, n_seq, D_MODEL), lambda i: (1, 0, 0), pipeline_mode=pl.Buffered(1)),
            _const_spec((1, D_MODEL)),
            _const_spec((1, D_MODEL)),
            _const_spec((D_MODEL, D_PROJ)),
            _const_spec((CONV_WIDTH, LRU_WIDTH)),
            _const_spec((1, LRU_WIDTH)),
            _const_spec((LRU_WIDTH, LRU_WIDTH)),
            _const_spec((1, LRU_WIDTH)),
            _const_spec((LRU_WIDTH, LRU_WIDTH)),
            _const_spec((1, LRU_WIDTH)),
            _const_spec((1, LRU_WIDTH)),
            _const_spec(p["lb_logits"].shape),
            _const_spec((1, HG_HEAD_DIM)),
            _const_spec((D_MODEL, D_MODEL)),
            _const_spec((n_seq, LRU_WIDTH)),
            _const_spec((CONV_WIDTH - 1, n_seq, LRU_WIDTH)),
            state_spec,
        ],
        out_specs=[
            full2((n_seq, D_MODEL)),
            full2((n_seq, LRU_WIDTH)),
            full2((CONV_WIDTH - 1, n_seq, LRU_WIDTH)),
            state_spec,
        ],
        out_shape=[
            jax.ShapeDtypeStruct((n_seq, D_MODEL), F32),
            jax.ShapeDtypeStruct((n_seq, LRU_WIDTH), F32),
            jax.ShapeDtypeStruct((CONV_WIDTH - 1, n_seq, LRU_WIDTH), F32),
            jax.ShapeDtypeStruct((n_seq, HG_HEADS, HG_HEAD_DIM, HG_HEAD_DIM), F32),
        ],
        scratch_shapes=[pltpu.VMEM((n_seq, HG_WIDTH), F32) for _ in range(8)]
        + [pltpu.VMEM((nb, HG_WIDTH), F32),
           pltpu.VMEM((SAMPLE_IN_SLOTS,) + state_block, F32),
           pltpu.VMEM((SAMPLE_OUT_SLOTS,) + state_block, F32),
           pltpu.SemaphoreType.DMA((SAMPLE_IN_SLOTS,)),
           pltpu.SemaphoreType.DMA((SAMPLE_OUT_SLOTS,))],
        compiler_params=pltpu.CompilerParams(
            dimension_semantics=("arbitrary",), vmem_limit_bytes=VMEM_LIMIT),
        name="mix_sample",
    )(x, mod9, p["ln_mix_pre"], p["ln_mix_post"], p["w_in"], p["conv_w"], p["conv_b"],
      p["wa_bd"], p["b_a"], p["wx_bd"], p["b_x"], p["lam"], p["lb_logits"], p["hg_norm_w"],
      p["w_out"], h0, conv0, s0)


def _block_diag(w):
    heads, blk, _ = w.shape
    eye = jnp.eye(heads, dtype=w.dtype)
    return (eye[:, None, :, None] * w[:, :, None, :]).reshape(heads * blk, heads * blk)


def kernel(x_prompt, x_sample, c_prompt, c_sample, state_lru_h, state_lru_conv, state_hgrn_S, w_ada, b_ada, ln_ffn1_pre, ln_ffn1_post, ffn1_w_gate, ffn1_w_up, ffn1_w_down, ln_mix_pre, ln_mix_post, w_in, lru_conv_w, lru_conv_b, lru_w_a, lru_b_a, lru_w_x, lru_b_x, lru_lambda, hg_lb_logits, hg_norm_w, w_out, ln_ffn2_pre, ln_ffn2_post, ffn2_w_gate, ffn2_w_up, ffn2_w_down):
    depth = w_ada.shape[0]
    batch, seq_len, _ = x_prompt.shape
    n_seq = x_sample.shape[0]
    assert depth == 1 and x_sample.shape[1] == 1
    assert seq_len % FFN_TILE == 0 and seq_len % MIX_TILE == 0 and n_seq % SAMPLE_SEQ_BLOCK == 0
    assert D_PROJ == 6 * LRU_WIDTH and LRU_WIDTH == HG_WIDTH

    xp = x_prompt.reshape(batch * seq_len, D_MODEL)
    xs = x_sample.reshape(n_seq, D_MODEL)
    ph, pc, pS, sh, sc, sS = [], [], [], [], [], []
    for l in range(depth):
        row = lambda w: w[l].reshape(1, -1)
        mod9 = _ada(c_sample, c_prompt, w_ada[l], b_ada[l])
        mod_p = jnp.transpose(mod9[:, n_seq:], (1, 0, 2))
        p = {
            "ln_mix_pre": row(ln_mix_pre), "ln_mix_post": row(ln_mix_post),
            "w_in": w_in[l], "w_out": w_out[l],
            "conv_w": lru_conv_w[l], "conv_b": row(lru_conv_b),
            "wa_bd": _block_diag(lru_w_a[l]).astype(BF16), "b_a": row(lru_b_a),
            "wx_bd": _block_diag(lru_w_x[l]).astype(BF16), "b_x": row(lru_b_x),
            "lam": row(lru_lambda), "lb_logits": hg_lb_logits,
            "hg_norm_w": row(hg_norm_w),
        }
        f1 = (row(ln_ffn1_pre), row(ln_ffn1_post), ffn1_w_gate[l], ffn1_w_up[l], ffn1_w_down[l])
        f2 = (row(ln_ffn2_pre), row(ln_ffn2_post), ffn2_w_gate[l], ffn2_w_up[l], ffn2_w_down[l])

        xp, xs = _ffn(xp, xs, mod_p[:, 0:3], mod9, 0, *f1, seq_len)
        xp, h_p, c_p, S_p, w_in_bf, w_out_bf = _mix_prompt(
            xp.reshape(batch, seq_len, D_MODEL), mod_p[:, 3:6], p)
        xp = xp.reshape(batch * seq_len, D_MODEL)
        conv0 = jnp.transpose(state_lru_conv[l], (1, 0, 2))
        xs, h_s, c_s, S_s = _mix_sample(xs, mod9, dict(p, w_in=w_in_bf, w_out=w_out_bf),
                                        state_lru_h[l], conv0, state_hgrn_S[l])
        xp, xs = _ffn(xp, xs, mod_p[:, 6:9], mod9, 2, *f2, seq_len)

        ph.append(h_p.reshape(batch, LRU_WIDTH)); pc.append(c_p); pS.append(S_p)
        sh.append(h_s); sc.append(jnp.transpose(c_s, (1, 0, 2))); sS.append(S_s)

    return (xp.reshape(batch, seq_len, D_MODEL), xs.reshape(n_seq, 1, D_MODEL),
            jnp.stack(ph), jnp.stack(pc), jnp.stack(pS), jnp.stack(sh), jnp.stack(sc), jnp.stack(sS))
```

```python
import jax
import jax.numpy as jnp
from jax import lax
from jax.experimental import pallas as pl
from jax.experimental.pallas import tpu as pltpu

F32 = jnp.float32
BF16 = jnp.bfloat16

D_MODEL = 1024
D_FF = 2816
LRU_WIDTH = 512
CONV_WIDTH = 4
LRU_C = 8.0
HG_WIDTH = 512
HG_HEAD_DIM = 128
HG_HEADS = HG_WIDTH // HG_HEAD_DIM
HG_CHUNK = 64
N_MOD = 9
D_PROJ = 2 * LRU_WIDTH + 4 * HG_WIDTH
EPS = 1e-6

V7X_SUBLANES = 8
V7X_VMEM_BYTES = 64 * 1024 * 1024
VMEM_LIMIT = V7X_VMEM_BYTES - 8 * 1024 * 1024

FFN_TILE = 1024
FFN_ROWS = 512
MIX_TILE = 512
MIX_ROWS = 256
HG_DIRECT_ROWS = 256
FFN_WBLOCK = 256
FFN_WSTEPS = D_FF // FFN_WBLOCK
ADA_SLOTS = 3
MIX_WROWS = 256
FFN_WSLOTS = 3
SAMPLE_SEQ_BLOCK = 2 * V7X_SUBLANES
SAMPLE_IN_SLOTS = 3
SAMPLE_OUT_SLOTS = 2

HG_LEVELS = (1, 2, 4, 8, 16, 32)
HG_DIRECT_MIN = 2.0 ** -120


def _rms(x, w):
    return (x * lax.rsqrt(jnp.mean(x * x, axis=-1, keepdims=True) + EPS)) * w


def _sigmoid(x):
    return 1.0 / (1.0 + jnp.exp(-x))


def _silu(x):
    return x * _sigmoid(x)


def _gelu_tanh(x):
    c = 0.7978845608028654
    return x * (0.5 * (1.0 + jnp.tanh(c * (x + 0.044715 * (x * x * x)))))


def _softplus(z):
    return jnp.maximum(z, 0.0) + jnp.log1p(jnp.exp(-jnp.abs(z)))


def _dot(a, b):
    return jnp.dot(a, b, preferred_element_type=F32)


def _dot_nt(a, b):
    return lax.dot_general(a, b, (((1,), (1,)), ((), ())), preferred_element_type=F32)


def _dot_tn(a, b):
    return lax.dot_general(a, b, (((0,), (0,)), ((), ())), preferred_element_type=F32)


def _const_spec(shape):
    nd = len(shape)
    return pl.BlockSpec(shape, lambda *_: (0,) * nd, pipeline_mode=pl.Buffered(1))


def _ada_kernel(cs_ref, cp_ref, w_hbm, b_ref, o_ref, ring_ref, sem):
    j = pl.program_id(0)
    n_j = pl.num_programs(0)

    def w_read(k):
        col = k * D_MODEL
        if not isinstance(k, int):
            col = pl.multiple_of(col, D_MODEL)
        slot = k % ADA_SLOTS
        return pltpu.make_async_copy(w_hbm.at[:, pl.ds(col, D_MODEL)], ring_ref.at[slot], sem.at[slot])

    @pl.when(j == 0)
    def _():
        for k in range(ADA_SLOTS - 1):
            w_read(k).start()

    @pl.when(j + ADA_SLOTS - 1 < n_j)
    def _():
        w_read(j + ADA_SLOTS - 1).start()

    s = _silu(jnp.concatenate([cs_ref[...], cp_ref[...]], axis=0))
    w_read(j).wait()
    o_ref[0] = _dot(s.astype(BF16), ring_ref[j % ADA_SLOTS].astype(BF16)) + b_ref[...]


def _ada(c_sample, c_prompt, w_ada, b_ada):
    rows = c_sample.shape[0] + c_prompt.shape[0]
    return pl.pallas_call(
        _ada_kernel,
        grid=(N_MOD,),
        in_specs=[
            _const_spec(c_sample.shape),
            _const_spec(c_prompt.shape),
            pl.BlockSpec(memory_space=pl.ANY),
            pl.BlockSpec((1, D_MODEL), lambda j: (0, j)),
        ],
        out_specs=pl.BlockSpec((1, rows, D_MODEL), lambda j: (j, 0, 0)),
        out_shape=jax.ShapeDtypeStruct((N_MOD, rows, D_MODEL), F32),
        scratch_shapes=[pltpu.VMEM((ADA_SLOTS, D_MODEL, D_MODEL), F32),
                        pltpu.SemaphoreType.DMA((ADA_SLOTS,))],
        compiler_params=pltpu.CompilerParams(
            dimension_semantics=("arbitrary",), vmem_limit_bytes=VMEM_LIMIT),
        name="ada_mod",
    )(c_sample, c_prompt, w_ada, b_ada.reshape(1, N_MOD * D_MODEL))


def _ffn_body(x, shift, scale, gate, pre_w, post_w, wg_ref, wu_ref, wd_ref):
    h = _rms(x, pre_w) * (1.0 + scale) + shift
    hb = h.astype(BF16)
    a = _dot(hb, wg_ref[...])
    u = _dot(hb, wu_ref[...])
    act = (_silu(a) * u).astype(BF16)
    y = _dot(act, wd_ref[...])
    return x + (0.5 * gate) * _rms(y, post_w)


def _ffn_kernel(xp_ref, mp_ref, xs_ref, ms_ref, pre_ref, post_ref, wg32_ref, wu32_ref, wd32_ref,
                op_ref, os_ref, wg_ref, wu_ref, wd_ref, gring_ref, uring_ref, dring_ref, wsem):
    s = pl.program_id(0)
    n_steps = pl.num_programs(0)

    def weight_reads(j):
        slot, c0 = j % FFN_WSLOTS, j * FFN_WBLOCK
        return (pltpu.make_async_copy(wg32_ref.at[:, c0:c0 + FFN_WBLOCK], gring_ref.at[slot], wsem.at[0, slot]),
                pltpu.make_async_copy(wu32_ref.at[:, c0:c0 + FFN_WBLOCK], uring_ref.at[slot], wsem.at[1, slot]),
                pltpu.make_async_copy(wd32_ref.at[c0:c0 + FFN_WBLOCK, :], dring_ref.at[slot], wsem.at[2, slot]))

    for j in range(FFN_WSTEPS):
        @pl.when(s == j)
        def _(j=j):
            ahead = range(FFN_WSLOTS) if j == 0 else [j + FFN_WSLOTS - 1]
            for jj in ahead:
                if jj < FFN_WSTEPS:
                    for copy in weight_reads(jj):
                        copy.start()
            for copy in weight_reads(j):
                copy.wait()
            slot, c0 = j % FFN_WSLOTS, j * FFN_WBLOCK
            wg_ref[:, c0:c0 + FFN_WBLOCK] = gring_ref[slot].astype(BF16)
            wu_ref[:, c0:c0 + FFN_WBLOCK] = uring_ref[slot].astype(BF16)
            wd_ref[c0:c0 + FFN_WBLOCK, :] = dring_ref[slot].astype(BF16)

    @pl.when((s >= FFN_WSTEPS) & (s < n_steps - 1))
    def _():
        m = mp_ref[0]
        for r0 in range(0, FFN_TILE, FFN_ROWS):
            op_ref[r0:r0 + FFN_ROWS, :] = _ffn_body(
                xp_ref[r0:r0 + FFN_ROWS, :], m[0:1], m[1:2], m[2:3],
                pre_ref[...], post_ref[...], wg_ref, wu_ref, wd_ref)

    @pl.when(s == n_steps - 1)
    def _():
        os_ref[...] = _ffn_body(xs_ref[...], ms_ref[0], ms_ref[1], ms_ref[2], pre_ref[...],
                                post_ref[...], wg_ref, wu_ref, wd_ref)


def _ffn(xp, xs, mod_p3, mod9, sub_layer, pre_w, post_w, wg, wu, wd, seq_len):
    n_tok = xp.shape[0]
    n_seq = xs.shape[0]
    n_tiles = n_tok // FFN_TILE
    steps_per_seq = seq_len // FFN_TILE
    tile = lambda s: jnp.clip(s - FFN_WSTEPS, 0, n_tiles - 1)
    tok_spec = pl.BlockSpec((FFN_TILE, D_MODEL), lambda s: (tile(s), 0))
    return pl.pallas_call(
        _ffn_kernel,
        grid=(FFN_WSTEPS + n_tiles + 1,),
        in_specs=[
            tok_spec,
            pl.BlockSpec((1, 3, D_MODEL), lambda s: (tile(s) // steps_per_seq, 0, 0)),
            _const_spec((n_seq, D_MODEL)),
            pl.BlockSpec((3, n_seq, D_MODEL), lambda s: (sub_layer, 0, 0), pipeline_mode=pl.Buffered(1)),
            _const_spec((1, D_MODEL)),
            _const_spec((1, D_MODEL)),
            pl.BlockSpec(memory_space=pl.ANY),
            pl.BlockSpec(memory_space=pl.ANY),
            pl.BlockSpec(memory_space=pl.ANY),
        ],
        out_specs=[tok_spec, pl.BlockSpec((n_seq, D_MODEL), lambda s: (0, 0))],
        out_shape=[jax.ShapeDtypeStruct((n_tok, D_MODEL), F32),
                   jax.ShapeDtypeStruct((n_seq, D_MODEL), F32)],
        scratch_shapes=[pltpu.VMEM((D_MODEL, D_FF), BF16), pltpu.VMEM((D_MODEL, D_FF), BF16),
                        pltpu.VMEM((D_FF, D_MODEL), BF16),
                        pltpu.VMEM((FFN_WSLOTS, D_MODEL, FFN_WBLOCK), F32),
                        pltpu.VMEM((FFN_WSLOTS, D_MODEL, FFN_WBLOCK), F32),
                        pltpu.VMEM((FFN_WSLOTS, FFN_WBLOCK, D_MODEL), F32),
                        pltpu.SemaphoreType.DMA((3, FFN_WSLOTS))],
        compiler_params=pltpu.CompilerParams(
            dimension_semantics=("arbitrary",), vmem_limit_bytes=VMEM_LIMIT),
        name="ffn",
    )(xp, mod_p3, xs, mod9, pre_w, post_w, wg, wu, wd)


def _lower_bound(lb_logits):
    z = lb_logits - jnp.max(lb_logits, axis=0, keepdims=True)
    e = jnp.exp(z)
    return e[0:1] / jnp.sum(e, axis=0, keepdims=True)


def _lru_gates(u_conv, wa_ref, ba, wx_ref, bx, lam):
    ub = u_conv.astype(BF16)
    r = _sigmoid(_dot(ub, wa_ref[...]) + ba)
    ig = _sigmoid(_dot(ub, wx_ref[...]) + bx)
    log_a = (-LRU_C * r) * _softplus(-lam)
    a = jnp.exp(log_a)
    th = jnp.tanh(log_a)
    mult = jnp.sqrt((-2.0 * th) / (1.0 - th))
    return a, mult, ig


def _head_rms_gate(o, norm_w, g):
    return _rms(o, norm_w) * _silu(g)


def _hgrn_chunk_head(scores, q_in, k_out, dec, vb_blk, g_blk, hgw, st_ref, hd, ohg_ref, r0, l0):
    st = st_ref[hd]
    o = _dot_nt(q_in, st.astype(BF16)) + _dot(scores.astype(BF16), vb_blk)
    st_ref[hd] = st * dec + _dot_tn(vb_blk, k_out)
    ohg_ref[r0:r0 + HG_CHUNK, l0:l0 + HG_HEAD_DIM] = _head_rms_gate(o, hgw, g_blk)


def _hgrn_chunks_direct(qs, kk, pfx, tails, vb, g, hgw, st_ref, ohg_ref, r0):
    tb = qs.shape[0]
    n_chunk = len(tails)
    inv = 1.0 / pfx
    q_dec = qs * pfx
    k_dec = kk * inv
    q_in = q_dec.astype(BF16)
    k_in = k_dec.astype(BF16)
    ti = lax.broadcasted_iota(jnp.int32, (tb, tb), 0)
    si = lax.broadcasted_iota(jnp.int32, (tb, tb), 1)
    visible = (ti >= si) & ((ti ^ si) < HG_CHUNK)
    row_chunk = jnp.right_shift(lax.broadcasted_iota(jnp.int32, (tb, HG_HEAD_DIM), 0),
                                HG_CHUNK.bit_length() - 1)
    tail_rows = jnp.concatenate([jnp.broadcast_to(t, (HG_CHUNK, t.shape[-1])) for t in tails], axis=0)
    k_end = k_dec * tail_rows
    for hd in range(HG_HEADS):
        l0 = hd * HG_HEAD_DIM
        head = lambda z: z[:, l0:l0 + HG_HEAD_DIM]
        by_chunk = lambda z: jnp.concatenate(
            [jnp.where(row_chunk == c, head(z), 0.0) for c in range(n_chunk)], axis=-1).astype(BF16)
        scores = jnp.where(visible, _dot_nt(head(q_in), head(k_in)), 0.0)
        grown = _dot_tn(head(vb), by_chunk(k_end))
        st = st_ref[hd]
        states = []
        for c, tail in enumerate(tails):
            states.append(st.astype(BF16))
            st = st * head(tail) + grown[:, c * HG_HEAD_DIM:(c + 1) * HG_HEAD_DIM]
        st_ref[hd] = st
        o = (_dot_nt(by_chunk(q_dec), jnp.concatenate(states, axis=-1))
             + _dot(scores.astype(BF16), head(vb)))
        ohg_ref[r0:r0 + tb, l0:l0 + HG_HEAD_DIM] = _head_rms_gate(o, hgw, head(g))


def _hgrn_chunks_levels(f3, kk3, qs3, vb, g, hgw, st_ref, ohg_ref):
    n_grp = f3.shape[0]
    tb = n_grp * V7X_SUBLANES
    sub = lax.broadcasted_iota(jnp.int32, f3.shape, 1)
    grp = lax.broadcasted_iota(jnp.int32, f3.shape, 0)
    to_mxu = lambda z: z.reshape(tb, z.shape[-1]).astype(BF16)
    q_lv = [to_mxu(qs3)]
    k_lv = [to_mxu(kk3)]
    tot, pre, post = f3, f3, None
    for lvl in HG_LEVELS:
        q_lv.append(to_mxu(qs3 * pre))
        k_lv.append(k_lv[0] if post is None else to_mxu(kk3 * post))
        if lvl < V7X_SUBLANES:
            upper = (sub & lvl) != 0
            below = pltpu.roll(tot, lvl, 1)
            above = pltpu.roll(tot, V7X_SUBLANES - lvl, 1)
        else:
            upper = (grp & (lvl // V7X_SUBLANES)) != 0
            below = jnp.roll(tot, lvl // V7X_SUBLANES, axis=0)
            above = jnp.roll(tot, -(lvl // V7X_SUBLANES), axis=0)
        pre = jnp.where(upper, pre * below, pre)
        post = jnp.where(upper, 1.0, above) if post is None else jnp.where(upper, post, post * above)
        tot = tot * jnp.where(upper, below, above)
    q_in = to_mxu(qs3 * pre)
    k_out = to_mxu(kk3 * post)
    s_decay = tot.reshape(tb, tot.shape[-1])

    ti = lax.broadcasted_iota(jnp.int32, (HG_CHUNK, HG_CHUNK), 0)
    si = lax.broadcasted_iota(jnp.int32, (HG_CHUNK, HG_CHUNK), 1)
    masks = [ti == si]
    tx = ti ^ si
    for lvl in HG_LEVELS:
        masks.append((tx >= lvl) & (tx < 2 * lvl) & ((ti & lvl) != 0))

    for c in range(tb // HG_CHUNK):
        r0 = c * HG_CHUNK
        for hd in range(HG_HEADS):
            l0 = hd * HG_HEAD_DIM
            blk = lambda z: z[r0:r0 + HG_CHUNK, l0:l0 + HG_HEAD_DIM]
            scores = jnp.zeros((HG_CHUNK, HG_CHUNK), F32)
            for ql, kl, msk in zip(q_lv, k_lv, masks):
                scores = jnp.where(msk, _dot_nt(blk(ql), blk(kl)), scores)
            dec = s_decay[r0 + HG_CHUNK - 1:r0 + HG_CHUNK, l0:l0 + HG_HEAD_DIM]
            _hgrn_chunk_head(scores, blk(q_in), blk(k_out), dec, blk(vb), blk(g), hgw,
                             st_ref, hd, ohg_ref, r0, l0)

def _mix_tile(x_ref, m_ref, o_ref, t, pre_ref, post_ref, win_ref, cw_ref, cb_ref, wa_ref, ba_ref,
              wx_ref, bx_ref, lam_ref, lbl_ref, hgw_ref, wout_ref, ucar_ref, hcar_ref, st_ref, ohg_ref):
    tb = MIX_TILE
    x = x_ref[...]
    m = m_ref[...]
    per_block = []
    for r0 in range(0, tb, MIX_ROWS):
        hb = (_rms(x[r0:r0 + MIX_ROWS], pre_ref[...]) * (1.0 + m[1:2]) + m[0:1]).astype(BF16)
        proj = _dot(hb, win_ref[...])
        per_block.append([proj[:, c0:c0 + LRU_WIDTH] for c0 in range(0, D_PROJ, LRU_WIDTH)])
    u, y_lru, q, f_raw, v, g = [jnp.concatenate(seg, axis=0) for seg in zip(*per_block)]

    n_grp = tb // V7X_SUBLANES
    grouped = lambda z: z.reshape(n_grp, V7X_SUBLANES, z.shape[-1])
    flat = lambda z: z.reshape(tb, z.shape[-1])
    gshape = (n_grp, V7X_SUBLANES, LRU_WIDTH)
    sub = lax.broadcasted_iota(jnp.int32, gshape, 1)
    grp = lax.broadcasted_iota(jnp.int32, gshape, 0)

    u3 = grouped(u)
    u_all = jnp.concatenate([ucar_ref[...][None], u3], axis=0)
    ucar_ref[...] = u3[n_grp - 1]
    cw = cw_ref[...]
    u_conv = cb_ref[...]
    for k in range(CONV_WIDTH - 1):
        back = CONV_WIDTH - 1 - k
        rot = pltpu.roll(u_all, back, 1)
        u_conv = u_conv + jnp.where(sub >= back, rot[1:], rot[:-1]) * cw[k:k + 1]
    u_conv = flat(u_conv + u3 * cw[CONV_WIDTH - 1:CONV_WIDTH])

    a, mult, ig = _lru_gates(u_conv, wa_ref, ba_ref[...], wx_ref, bx_ref[...], lam_ref[...])
    a3, mult3 = grouped(a), grouped(mult)
    first_row = jnp.where(t == 0, 0, -1)
    mult3 = jnp.where(grp * V7X_SUBLANES + sub == first_row, 1.0, mult3)
    b3 = (mult3 * grouped(ig)) * grouped(u_conv)
    s = 1
    while s < V7X_SUBLANES:
        keep = sub >= s
        b3 = jnp.where(keep, a3 * pltpu.roll(b3, s, 1) + b3, b3)
        a3 = jnp.where(keep, a3 * pltpu.roll(a3, s, 1), a3)
        s *= 2
    carry = hcar_ref[...]
    hs = []
    for gi in range(n_grp):
        h_g = a3[gi] * carry + b3[gi]
        carry = h_g[V7X_SUBLANES - 1:V7X_SUBLANES]
        hs.append(h_g)
    hcar_ref[...] = carry
    o_lru = jnp.concatenate(hs, axis=0) * _gelu_tanh(y_lru)

    lb = _lower_bound(lbl_ref[...])
    f3 = grouped(lb + (1.0 - lb) * _sigmoid(f_raw))
    kk3 = 1.0 - f3
    qs3 = grouped(q * (HG_HEAD_DIM ** -0.5))
    vb = v.astype(BF16)
    hgw = hgw_ref[...]

    pfx = f3
    s = 1
    while s < V7X_SUBLANES:
        pfx = jnp.where(sub >= s, pfx * pltpu.roll(pfx, s, 1), pfx)
        s *= 2
    grp_per_chunk = HG_CHUNK // V7X_SUBLANES
    rows, tails = [], []
    for gi in range(n_grp):
        p_g = pfx[gi] if gi % grp_per_chunk == 0 else pfx[gi] * carry
        carry = p_g[V7X_SUBLANES - 1:V7X_SUBLANES]
        rows.append(p_g)
        if gi % grp_per_chunk == grp_per_chunk - 1:
            tails.append(carry)
    pfx = jnp.concatenate(rows, axis=0)
    direct = jnp.min(jnp.concatenate(tails, axis=0)) >= HG_DIRECT_MIN

    @pl.when(direct)
    def _():
        qs, kk = flat(qs3), flat(kk3)
        per_block = HG_DIRECT_ROWS // HG_CHUNK
        for bi, r0 in enumerate(range(0, tb, HG_DIRECT_ROWS)):
            rows = lambda z: z[r0:r0 + HG_DIRECT_ROWS]
            _hgrn_chunks_direct(rows(qs), rows(kk), rows(pfx), tails[bi * per_block:(bi + 1) * per_block],
                                rows(vb), rows(g), hgw, st_ref, ohg_ref, r0)

    @pl.when(jnp.logical_not(direct))
    def _():
        _hgrn_chunks_levels(f3, kk3, qs3, vb, g, hgw, st_ref, ohg_ref)

    mix_in = jnp.concatenate([o_lru, ohg_ref[...]], axis=-1).astype(BF16)
    for r0 in range(0, tb, MIX_ROWS):
        mix = _dot(mix_in[r0:r0 + MIX_ROWS], wout_ref[...])
        o_ref[r0:r0 + MIX_ROWS, :] = x[r0:r0 + MIX_ROWS] + m[2:3] * _rms(mix, post_ref[...])


def _mix_prompt_kernel(x_ref, m_ref, *refs):
    (pre_ref, post_ref, win32_hbm, cw_ref, cb_ref, wa_ref, ba_ref, wx_ref, bx_ref, lam_ref, lbl_ref,
     hgw_ref, wout32_hbm) = refs[:13]
    (o_ref, h_out_ref, conv_out_ref, s_out_ref, win_hbm, wout_hbm,
     ucar_ref, hcar_ref, st_ref, ohg_ref, win_ref, wout_ref, win32_ref, wout32_ref,
     rsem, wsem) = refs[13:]
    t = pl.program_id(1)
    n_t = pl.num_programs(1)
    first = (pl.program_id(0) == 0) & (t == 0)
    last = (pl.program_id(0) == pl.num_programs(0) - 1) & (t == n_t - 1)
    n_wblk = D_MODEL // MIX_WROWS

    def weight_read(k):
        src, dst = (win32_hbm, win32_ref) if k < n_wblk else (wout32_hbm, wout32_ref)
        rows = pl.ds((k % n_wblk) * MIX_WROWS, MIX_WROWS)
        return pltpu.make_async_copy(src.at[rows], dst.at[rows], rsem.at[k])

    def weight_writes():
        return (pltpu.make_async_copy(win_ref, win_hbm, wsem.at[0]),
                pltpu.make_async_copy(wout_ref, wout_hbm, wsem.at[1]))

    @pl.when(first)
    def _():
        for k in range(2 * n_wblk):
            weight_read(k).start()
        for k in range(2 * n_wblk):
            weight_read(k).wait()
            r0 = (k % n_wblk) * MIX_WROWS
            if k < n_wblk:
                win_ref[r0:r0 + MIX_WROWS, :] = win32_ref[r0:r0 + MIX_WROWS, :].astype(BF16)
            else:
                wout_ref[r0:r0 + MIX_WROWS, :] = wout32_ref[r0:r0 + MIX_WROWS, :].astype(BF16)
        for copy in weight_writes():
            copy.start()

    @pl.when(t == 0)
    def _():
        ucar_ref[...] = jnp.zeros_like(ucar_ref)
        hcar_ref[...] = jnp.zeros_like(hcar_ref)
        st_ref[...] = jnp.zeros_like(st_ref)

    _mix_tile(x_ref.at[0], m_ref.at[0], o_ref.at[0], t, pre_ref, post_ref, win_ref, cw_ref, cb_ref,
              wa_ref, ba_ref, wx_ref, bx_ref, lam_ref, lbl_ref, hgw_ref, wout_ref,
              ucar_ref, hcar_ref, st_ref, ohg_ref)

    @pl.when(t == n_t - 1)
    def _():
        h_out_ref[0] = hcar_ref[...]
        conv_out_ref[0] = ucar_ref[V7X_SUBLANES - (CONV_WIDTH - 1):V7X_SUBLANES, :]
        for hd in range(HG_HEADS):
            s_out_ref[0, hd] = st_ref[hd].T

    @pl.when(last)
    def _():
        for copy in weight_writes():
            copy.wait()


def _mix_prompt(x, mod3, p):
    batch, seq_len, _ = x.shape
    n_t = seq_len // MIX_TILE
    tok_spec = pl.BlockSpec((1, MIX_TILE, D_MODEL), lambda b, t: (b, t, 0))
    return pl.pallas_call(
        _mix_prompt_kernel,
        grid=(batch, n_t),
        in_specs=[
            tok_spec,
            pl.BlockSpec((1, 3, D_MODEL), lambda b, t: (b, 0, 0)),
            _const_spec((1, D_MODEL)),
            _const_spec((1, D_MODEL)),
            pl.BlockSpec(memory_space=pl.ANY),
            _const_spec((CONV_WIDTH, LRU_WIDTH)),
            _const_spec((1, LRU_WIDTH)),
            _const_spec((LRU_WIDTH, LRU_WIDTH)),
            _const_spec((1, LRU_WIDTH)),
            _const_spec((LRU_WIDTH, LRU_WIDTH)),
            _const_spec((1, LRU_WIDTH)),
            _const_spec((1, LRU_WIDTH)),
            _const_spec(p["lb_logits"].shape),
            _const_spec((1, HG_HEAD_DIM)),
            pl.BlockSpec(memory_space=pl.ANY),
        ],
        out_specs=[
            tok_spec,
            pl.BlockSpec((1, 1, LRU_WIDTH), lambda b, t: (b, 0, 0)),
            pl.BlockSpec((1, CONV_WIDTH - 1, LRU_WIDTH), lambda b, t: (b, 0, 0)),
            pl.BlockSpec((1, HG_HEADS, HG_HEAD_DIM, HG_HEAD_DIM), lambda b, t: (b, 0, 0, 0)),
            pl.BlockSpec(memory_space=pl.ANY),
            pl.BlockSpec(memory_space=pl.ANY),
        ],
        out_shape=[
            jax.ShapeDtypeStruct((batch, seq_len, D_MODEL), F32),
            jax.ShapeDtypeStruct((batch, 1, LRU_WIDTH), F32),
            jax.ShapeDtypeStruct((batch, CONV_WIDTH - 1, LRU_WIDTH), F32),
            jax.ShapeDtypeStruct((batch, HG_HEADS, HG_HEAD_DIM, HG_HEAD_DIM), F32),
            jax.ShapeDtypeStruct((D_MODEL, D_PROJ), BF16),
            jax.ShapeDtypeStruct((D_MODEL, D_MODEL), BF16),
        ],
        scratch_shapes=[
            pltpu.VMEM((V7X_SUBLANES, LRU_WIDTH), F32),
            pltpu.VMEM((1, LRU_WIDTH), F32),
            pltpu.VMEM((HG_HEADS, HG_HEAD_DIM, HG_HEAD_DIM), F32),
            pltpu.VMEM((MIX_TILE, HG_WIDTH), F32),
            pltpu.VMEM((D_MODEL, D_PROJ), BF16),
            pltpu.VMEM((D_MODEL, D_MODEL), BF16),
            pltpu.VMEM((D_MODEL, D_PROJ), F32),
            pltpu.VMEM((D_MODEL, D_MODEL), F32),
            pltpu.SemaphoreType.DMA((2 * (D_MODEL // MIX_WROWS),)),
            pltpu.SemaphoreType.DMA((2,)),
        ],
        compiler_params=pltpu.CompilerParams(
            dimension_semantics=("arbitrary", "arbitrary"), vmem_limit_bytes=VMEM_LIMIT),
        name="mix_prompt",
    )(x, mod3, p["ln_mix_pre"], p["ln_mix_post"], p["w_in"], p["conv_w"], p["conv_b"],
      p["wa_bd"], p["b_a"], p["wx_bd"], p["b_x"], p["lam"], p["lb_logits"], p["hg_norm_w"],
      p["w_out"])


def _mix_sample_kernel(x_ref, m_ref, pre_ref, post_ref, win_ref, cw_ref, cb_ref,
                       wa_ref, ba_ref, wx_ref, bx_ref, lam_ref, lbl_ref, hgw_ref, wout_ref,
                       h0_ref, conv0_ref, s0_hbm,
                       o_ref, h_out_ref, conv_out_ref, s_out_hbm,
                       qf_ref, f_ref, k_ref, v_ref, g_ref, qk_ref, olru_ref, ohg_ref, ostage_ref,
                       sin_ref, sout_ref, in_sem, out_sem):
    i = pl.program_id(0)
    n_i = pl.num_programs(0)
    nb = SAMPLE_SEQ_BLOCK

    def read_block(b):
        slot = b % SAMPLE_IN_SLOTS
        return pltpu.make_async_copy(s0_hbm.at[pl.ds(b * nb, nb)], sin_ref.at[slot], in_sem.at[slot])

    def write_block(b):
        slot = b % SAMPLE_OUT_SLOTS
        return pltpu.make_async_copy(sout_ref.at[slot], s_out_hbm.at[pl.ds(b * nb, nb)],
                                     out_sem.at[slot])

    @pl.when(i == 0)
    def _():
        for b in range(SAMPLE_IN_SLOTS - 1):
            read_block(b).start()

    @pl.when(i + SAMPLE_IN_SLOTS - 1 < n_i)
    def _():
        read_block(i + SAMPLE_IN_SLOTS - 1).start()

    @pl.when(i == 0)
    def _():
        x = x_ref[...]
        h = _rms(x, pre_ref[...]) * (1.0 + m_ref[1]) + m_ref[0]
        proj = _dot(h.astype(BF16), win_ref[...])
        u = proj[:, 0:LRU_WIDTH]
        y_lru = proj[:, LRU_WIDTH:2 * LRU_WIDTH]
        o0 = 2 * LRU_WIDTH
        q = proj[:, o0:o0 + HG_WIDTH]
        f_raw = proj[:, o0 + HG_WIDTH:o0 + 2 * HG_WIDTH]
        v = proj[:, o0 + 2 * HG_WIDTH:o0 + 3 * HG_WIDTH]
        g_ref[...] = proj[:, o0 + 3 * HG_WIDTH:o0 + 4 * HG_WIDTH]

        cw = cw_ref[...]
        u_conv = cb_ref[...]
        for k in range(CONV_WIDTH - 1):
            u_conv = u_conv + conv0_ref[k] * cw[k:k + 1]
            if k > 0:
                conv_out_ref[k - 1] = conv0_ref[k]
        u_conv = u_conv + u * cw[CONV_WIDTH - 1:CONV_WIDTH]
        conv_out_ref[CONV_WIDTH - 2] = u

        a, mult, ig = _lru_gates(u_conv, wa_ref, ba_ref[...], wx_ref, bx_ref[...], lam_ref[...])
        hs = (mult * ig) * u_conv + a * h0_ref[...]
        h_out_ref[...] = hs
        olru_ref[...] = hs * _gelu_tanh(y_lru)

        lb = _lower_bound(lbl_ref[...])
        f = lb + (1.0 - lb) * _sigmoid(f_raw)
        kk = 1.0 - f
        qs = q * (HG_HEAD_DIM ** -0.5)
        f_ref[...] = f
        qf_ref[...] = qs * f
        k_ref[...] = kk
        v_ref[...] = v
        qk = qs * kk
        for hd in range(HG_HEADS):
            l0 = hd * HG_HEAD_DIM
            tot = jnp.sum(qk[:, l0:l0 + HG_HEAD_DIM], axis=-1, keepdims=True)
            qk_ref[:, l0:l0 + HG_HEAD_DIM] = jnp.broadcast_to(tot, (qk.shape[0], HG_HEAD_DIM))

    base = pl.multiple_of(i * nb, nb)
    grp = lambda ref: ref[pl.ds(base, nb), :]
    f_g, k_g, v_g, qf_g, qk_g = grp(f_ref), grp(k_ref), grp(v_ref), grp(qf_ref), grp(qk_ref)
    square = (HG_HEAD_DIM, HG_HEAD_DIM)
    read_block(i).wait()

    @pl.when(i >= SAMPLE_OUT_SLOTS)
    def _():
        write_block(i - SAMPLE_OUT_SLOTS).wait()

    s0_ref = sin_ref.at[i % SAMPLE_IN_SLOTS]
    s_out_ref = sout_ref.at[i % SAMPLE_OUT_SLOTS]
    for j in range(nb):
        for hd in range(HG_HEADS):
            l0 = hd * HG_HEAD_DIM
            rowv = lambda z: z[j:j + 1, l0:l0 + HG_HEAD_DIM]
            s_old = s0_ref[j, hd]
            f_col = jnp.broadcast_to(rowv(f_g), square).T
            k_col = jnp.broadcast_to(rowv(k_g), square).T
            v_row = rowv(v_g)
            outer = k_col.astype(BF16).astype(F32) * v_row.astype(BF16).astype(F32)
            s_out_ref[j, hd] = f_col * s_old + outer
            qf8 = jnp.broadcast_to(rowv(qf_g), (V7X_SUBLANES, HG_HEAD_DIM)).astype(BF16)
            o1 = _dot(qf8, s_old.astype(BF16))[0:1]
            ostage_ref[j:j + 1, l0:l0 + HG_HEAD_DIM] = o1 + rowv(qk_g) * v_row
    ohg_ref[pl.ds(base, nb), :] = ostage_ref[...]
    write_block(i).start()

    @pl.when(i == n_i - 1)
    def _():
        for back in reversed(range(SAMPLE_OUT_SLOTS)):
            write_block(i - back).wait()
        hgw = hgw_ref[...]
        parts = [olru_ref[...]]
        for hd in range(HG_HEADS):
            l0 = hd * HG_HEAD_DIM
            parts.append(_head_rms_gate(ohg_ref[:, l0:l0 + HG_HEAD_DIM], hgw,
                                        g_ref[:, l0:l0 + HG_HEAD_DIM]))
        mix_in = jnp.concatenate(parts, axis=-1).astype(BF16)
        mix = _dot(mix_in, wout_ref[...])
        o_ref[...] = x_ref[...] + m_ref[2] * _rms(mix, post_ref[...])


def _mix_sample(x, mod9, p, h0, conv0, s0):
    n_seq = x.shape[0]
    nb = SAMPLE_SEQ_BLOCK
    assert n_seq // nb >= SAMPLE_IN_SLOTS
    state_spec = pl.BlockSpec(memory_space=pl.ANY)
    state_block = (nb, HG_HEADS, HG_HEAD_DIM, HG_HEAD_DIM)
    full2 = lambda shape: pl.BlockSpec(shape, lambda i: (0,) * len(shape))
    return pl.pallas_call(
        _mix_sample_kernel,
        grid=(n_seq // nb,),
        in_specs=[
            _const_spec((n_seq, D_MODEL)),
            pl.BlockSpec((3, n_seq, D_MODEL), lambda i: (1, 0, 0), pipeline_mode=pl.Buffered(1)),
            _const_spec((1, D_MODEL)),
            _const_spec((1, D_MODEL)),
            _const_spec((D_MODEL, D_PROJ)),
            _const_spec((CONV_WIDTH, LRU_WIDTH)),
            _const_spec((1, LRU_WIDTH)),
            _const_spec((LRU_WIDTH, LRU_WIDTH)),
            _const_spec((1, LRU_WIDTH)),
            _const_spec((LRU_WIDTH, LRU_WIDTH)),
            _const_spec((1, LRU_WIDTH)),
            _const_spec((1, LRU_WIDTH)),
            _const_spec(p["lb_logits"].shape),
            _const_spec((1, HG_HEAD_DIM)),
            _const_spec((D_MODEL, D_MODEL)),
            _const_spec((n_seq, LRU_WIDTH)),
            _const_spec((CONV_WIDTH - 1, n_seq, LRU_WIDTH)),
            state_spec,
        ],
        out_specs=[
            full2((n_seq, D_MODEL)),
            full2((n_seq, LRU_WIDTH)),
            full2((CONV_WIDTH - 1, n_seq, LRU_WIDTH)),
            state_spec,
        ],
        out_shape=[
            jax.ShapeDtypeStruct((n_seq, D_MODEL), F32),
            jax.ShapeDtypeStruct((n_seq, LRU_WIDTH), F32),
            jax.ShapeDtypeStruct((CONV_WIDTH - 1, n_seq, LRU_WIDTH), F32),
            jax.ShapeDtypeStruct((n_seq, HG_HEADS, HG_HEAD_DIM, HG_HEAD_DIM), F32),
        ],
        scratch_shapes=[pltpu.VMEM((n_seq, HG_WIDTH), F32) for _ in range(8)]
        + [pltpu.VMEM((nb, HG_WIDTH), F32),
           pltpu.VMEM((SAMPLE_IN_SLOTS,) + state_block, F32),
           pltpu.VMEM((SAMPLE_OUT_SLOTS,) + state_block, F32),
           pltpu.SemaphoreType.DMA((SAMPLE_IN_SLOTS,)),
           pltpu.SemaphoreType.DMA((SAMPLE_OUT_SLOTS,))],
        compiler_params=pltpu.CompilerParams(
            dimension_semantics=("arbitrary",), vmem_limit_bytes=VMEM_LIMIT),
        name="mix_sample",
    )(x, mod9, p["ln_mix_pre"], p["ln_mix_post"], p["w_in"], p["conv_w"], p["conv_b"],
      p["wa_bd"], p["b_a"], p["wx_bd"], p["b_x"], p["lam"], p["lb_logits"], p["hg_norm_w"],
      p["w_out"], h0, conv0, s0)


def _block_diag(w):
    heads, blk, _ = w.shape
    eye = jnp.eye(heads, dtype=w.dtype)
    return (eye[:, None, :, None] * w[:, :, None, :]).reshape(heads * blk, heads * blk)


def kernel(x_prompt, x_sample, c_prompt, c_sample, state_lru_h, state_lru_conv, state_hgrn_S, w_ada, b_ada, ln_ffn1_pre, ln_ffn1_post, ffn1_w_gate, ffn1_w_up, ffn1_w_down, ln_mix_pre, ln_mix_post, w_in, lru_conv_w, lru_conv_b, lru_w_a, lru_b_a, lru_w_x, lru_b_x, lru_lambda, hg_lb_logits, hg_norm_w, w_out, ln_ffn2_pre, ln_ffn2_post, ffn2_w_gate, ffn2_w_up, ffn2_w_down):
    depth = w_ada.shape[0]
    batch, seq_len, _ = x_prompt.shape
    n_seq = x_sample.shape[0]
    assert depth == 1 and x_sample.shape[1] == 1
    assert seq_len % FFN_TILE == 0 and seq_len % MIX_TILE == 0 and n_seq % SAMPLE_SEQ_BLOCK == 0
    assert D_PROJ == 6 * LRU_WIDTH and LRU_WIDTH == HG_WIDTH

    xp = x_prompt.reshape(batch * seq_len, D_MODEL)
    xs = x_sample.reshape(n_seq, D_MODEL)
    ph, pc, pS, sh, sc, sS = [], [], [], [], [], []
    for l in range(depth):
        row = lambda w: w[l].reshape(1, -1)
        mod9 = _ada(c_sample, c_prompt, w_ada[l], b_ada[l])
        mod_p = jnp.transpose(mod9[:, n_seq:], (1, 0, 2))
        p = {
            "ln_mix_pre": row(ln_mix_pre), "ln_mix_post": row(ln_mix_post),
            "w_in": w_in[l], "w_out": w_out[l],
            "conv_w": lru_conv_w[l], "conv_b": row(lru_conv_b),
            "wa_bd": _block_diag(lru_w_a[l]).astype(BF16), "b_a": row(lru_b_a),
            "wx_bd": _block_diag(lru_w_x[l]).astype(BF16), "b_x": row(lru_b_x),
            "lam": row(lru_lambda), "lb_logits": hg_lb_logits,
            "hg_norm_w": row(hg_norm_w),
        }
        f1 = (row(ln_ffn1_pre), row(ln_ffn1_post), ffn1_w_gate[l], ffn1_w_up[l], ffn1_w_down[l])
        f2 = (row(ln_ffn2_pre), row(ln_ffn2_post), ffn2_w_gate[l], ffn2_w_up[l], ffn2_w_down[l])

        xp, xs = _ffn(xp, xs, mod_p[:, 0:3], mod9, 0, *f1, seq_len)
        xp, h_p, c_p, S_p, w_in_bf, w_out_bf = _mix_prompt(
            xp.reshape(batch, seq_len, D_MODEL), mod_p[:, 3:6], p)
        xp = xp.reshape(batch * seq_len, D_MODEL)
        conv0 = jnp.transpose(state_lru_conv[l], (1, 0, 2))
        xs, h_s, c_s, S_s = _mix_sample(xs, mod9, dict(p, w_in=w_in_bf, w_out=w_out_bf),
                                        state_lru_h[l], conv0, state_hgrn_S[l])
        xp, xs = _ffn(xp, xs, mod_p[:, 6:9], mod9, 2, *f2, seq_len)

        ph.append(h_p.reshape(batch, LRU_WIDTH)); pc.append(c_p); pS.append(S_p)
        sh.append(h_s); sc.append(jnp.transpose(c_s, (1, 0, 2))); sS.append(S_s)

    return (xp.reshape(batch, seq_len, D_MODEL), xs.reshape(n_seq, 1, D_MODEL),
            jnp.stack(ph), jnp.stack(pc), jnp.stack(pS), jnp.stack(sh), jnp.stack(sc), jnp.stack(sS))
```
